```python
import math
import jax
import jax.numpy as jnp
from jax import lax
import numpy as np

D_MODEL = 1024
BATCH = 8
SEQ = 8192
DEPTH = 2

GRID_W = 64
CTX_LEN = 256
Q_BLOCK = 128
NORM_EPS = 1e-6

ATTN_HEADS = 8
ATTN_KV_HEADS = 2
ATTN_HEAD_DIM = 64
ATTN_AXIS_FREQS = ATTN_HEAD_DIM // 4
ROPE_THETA = 10000.0

SSD_HEADS = 8
SSD_HEAD_DIM = 64
SSD_D_INNER = SSD_HEADS * SSD_HEAD_DIM
SSD_GROUPS = 2
SSD_STATE = 128
SSD_CONV_K = 3
SSD_CONV_DIM = SSD_D_INNER + 2 * SSD_GROUPS * SSD_STATE
SSD_CHUNK = 128

RET_HEADS = 4
RET_DK = 128
RET_DV = 128
RET_CHUNK = 128

N_BRANCH = 3
BRANCH_W = 512
MLP_HIDDEN = 4 * D_MODEL

IN_SPLITS = (ATTN_HEADS * ATTN_HEAD_DIM, ATTN_KV_HEADS * ATTN_HEAD_DIM, ATTN_KV_HEADS * ATTN_HEAD_DIM,
             SSD_D_INNER, SSD_CONV_DIM, 2 * SSD_HEADS,
             RET_HEADS * RET_DK, RET_HEADS * RET_DK, RET_HEADS * RET_DV, RET_HEADS * RET_DV,
             N_BRANCH * D_MODEL)
IN_DIM = sum(IN_SPLITS)

kernel_name = 'hybrid_attn_ssd_retention_prefix_dit'


def rms_norm(x, w):
    xf = x.astype(jnp.float32)
    y = xf * lax.rsqrt(jnp.mean(xf * xf, axis=-1, keepdims=True) + NORM_EPS)
    return (y * w.astype(jnp.float32)).astype(x.dtype)


def modulate(x, shift, scale):
    return x * (1 + scale[:, None, :]) + shift[:, None, :]


def split_cols(p):
    out = []
    off = 0
    for size in IN_SPLITS:
        out.append(p[..., off:off + size])
        off += size
    return out


def flip(t):
    return jnp.flip(t, axis=1)


def rope_apply(x, cos, sin):
    half = x.shape[-1] // 2
    x1 = x[..., :half]
    x2 = x[..., half:]
    cs = cos[:, None, :].astype(x.dtype)
    sn = sin[:, None, :].astype(x.dtype)
    return jnp.concatenate([x1 * cs - x2 * sn, x1 * sn + x2 * cs], axis=-1)


def axial_angles(rows):
    row = jnp.repeat(jnp.arange(rows, dtype=jnp.float32), GRID_W)
    col = jnp.tile(jnp.arange(GRID_W, dtype=jnp.float32), rows)
    inv = ROPE_THETA ** (-jnp.arange(ATTN_AXIS_FREQS, dtype=jnp.float32) / ATTN_AXIS_FREQS)
    ang = jnp.concatenate([row[:, None] * inv, col[:, None] * inv], axis=-1)
    return jnp.cos(ang), jnp.sin(ang)


def seq_angles(start, n):
    pos = jnp.arange(n, dtype=jnp.float32) + start
    inv = ROPE_THETA ** (-jnp.linspace(0.0, 1.0, RET_DK // 2, dtype=jnp.float32))
    ang = pos[:, None] * inv
    return jnp.cos(ang), jnp.sin(ang)


def dwconv_centred(x, w, bias):
    y = lax.conv_general_dilated(
        x, w[:, None, :].astype(x.dtype), window_strides=(1,),
        padding=[(SSD_CONV_K // 2, SSD_CONV_K // 2)],
        dimension_numbers=('NWC', 'WIO', 'NWC'), feature_group_count=x.shape[-1])
    return y + bias.astype(x.dtype)


def gqa_attend(q, k, v):
    b, lq, h, hd = q.shape
    kvh = k.shape[2]
    grp = h // kvh
    qb = q.reshape(b, lq // Q_BLOCK, Q_BLOCK, kvh, grp, hd).transpose(1, 0, 2, 3, 4, 5)
    scale = hd ** -0.5

    def block(qblk):
        s = jnp.einsum('bqkgd,bskd->bkgqs', qblk, k, preferred_element_type=jnp.float32) * scale
        p = jax.nn.softmax(s, axis=-1).astype(v.dtype)
        return jnp.einsum('bkgqs,bskd->bqkgd', p, v)

    o = lax.map(block, qb)
    return o.transpose(1, 0, 2, 3, 4, 5).reshape(b, lq, h * hd)


def chunk_scan(init, states, decay):
    def step(s, inp):
        st, dc = inp
        return s * dc + st, s
    final, prev = lax.scan(step, init, (states, decay))
    return final, prev


def ssd_scan(xh, dt, a_neg, bm, cm, init, return_y):
    b, l, nh, hp = xh.shape
    ng, ns = bm.shape[2], bm.shape[3]
    r = nh // ng
    L = SSD_CHUNK
    nc = l // L
    dtf = dt.astype(jnp.float32)
    xd = (xh.astype(jnp.float32) * dtf[..., None]).reshape(b, nc, L, ng, r, hp)
    a = (dtf * a_neg).reshape(b, nc, L, ng, r).transpose(0, 3, 4, 1, 2)
    bc = bm.astype(jnp.float32).reshape(b, nc, L, ng, ns)
    a_cum = jnp.cumsum(a, axis=-1)
    decay_states = jnp.exp(a_cum[..., -1:] - a_cum)
    states = jnp.einsum('bclgn,bgrcl,bclgrp->cbgrpn', bc, decay_states, xd)
    chunk_decay = jnp.exp(a_cum[..., -1]).transpose(3, 0, 1, 2)[..., None, None]
    final, prev = chunk_scan(init, states, chunk_decay)
    if not return_y:
        return None, final
    cc = cm.astype(jnp.float32).reshape(b, nc, L, ng, ns)
    seg = a_cum[..., :, None] - a_cum[..., None, :]
    causal = jnp.tril(jnp.ones((L, L), dtype=bool))
    lmat = jnp.where(causal, jnp.exp(jnp.where(causal, seg, 0.0)), 0.0)
    cb = jnp.einsum('bclgn,bcsgn->bcgls', cc, bc)
    y_diag = jnp.einsum('bcgls,bgrcls,bcsgrp->bclgrp', cb, lmat, xd)
    y_off = jnp.einsum('bclgn,cbgrpn,bgrcl->bclgrp', cc, prev, jnp.exp(a_cum))
    return (y_diag + y_off).reshape(b, l, nh, hp), final


def retention_scan(q, k, v, lg, init, return_y):
    b, l, nh, dk = k.shape
    dv = v.shape[-1]
    L = RET_CHUNK
    nc = l // L
    kc = k.astype(jnp.float32).reshape(b, nc, L, nh, dk)
    vc = v.astype(jnp.float32).reshape(b, nc, L, nh, dv)
    pos = jnp.arange(L, dtype=jnp.float32)
    k_decay = jnp.exp((L - 1 - pos)[:, None] * lg)
    states = jnp.einsum('bcshk,sh,bcshv->cbhkv', kc, k_decay, vc)
    chunk_decay = jnp.broadcast_to(jnp.exp(L * lg)[None, None, :, None, None], (nc, 1, nh, 1, 1))
    final, prev = chunk_scan(init, states, chunk_decay)
    if not return_y:
        return None, final
    qc = q.astype(jnp.float32).reshape(b, nc, L, nh, dk)
    diff = pos[:, None] - pos[None, :]
    dmat = jnp.where(diff[None] >= 0, jnp.exp(jnp.maximum(diff, 0.0)[None] * lg[:, None, None]), 0.0)
    s = jnp.einsum('bclhk,bcshk->bchls', qc, kc) * dmat
    y_in = jnp.einsum('bchls,bcshv->bclhv', s, vc)
    q_decay = jnp.exp((pos + 1)[:, None] * lg)
    y_x = jnp.einsum('bclhk,cbhkv,lh->bclhv', qc, prev, q_decay)
    return (y_in + y_x).reshape(b, l, nh, dv), final


def attn_q(aq, q_norm, rope):
    b, l, _ = aq.shape
    q = rms_norm(aq.reshape(b, l, ATTN_HEADS, ATTN_HEAD_DIM), q_norm)
    return q if rope is None else rope_apply(q, *rope)


def attn_kv(ak, av, k_norm, rope):
    b, l, _ = ak.shape
    k = rms_norm(ak.reshape(b, l, ATTN_KV_HEADS, ATTN_HEAD_DIM), k_norm)
    if rope is not None:
        k = rope_apply(k, *rope)
    return k, av.reshape(b, l, ATTN_KV_HEADS, ATTN_HEAD_DIM)


def ssd_inputs(xbc_raw, dt_raw, conv_w, conv_b, dt_bias):
    b, l, _ = xbc_raw.shape
    gn = SSD_GROUPS * SSD_STATE
    xbc = jax.nn.silu(dwconv_centred(xbc_raw, conv_w, conv_b))
    xs = xbc[..., :SSD_D_INNER].reshape(b, l, SSD_HEADS, SSD_HEAD_DIM)
    bm = xbc[..., SSD_D_INNER:SSD_D_INNER + gn].reshape(b, l, SSD_GROUPS, SSD_STATE)
    cm = xbc[..., SSD_D_INNER + gn:].reshape(b, l, SSD_GROUPS, SSD_STATE)
    dt = jax.nn.softplus(dt_raw.astype(jnp.float32).reshape(b, l, 2, SSD_HEADS) + dt_bias.astype(jnp.float32))
    return xs, bm, cm, dt[:, :, 0], dt[:, :, 1]


def ssd_finish(y, xh, z, d_skip, norm_w):
    b, l, nh, hp = xh.shape
    y = y + d_skip.astype(jnp.float32)[:, None] * xh.astype(jnp.float32)
    y = y.reshape(b, l, nh * hp) * jax.nn.silu(z.astype(jnp.float32))
    return rms_norm(y, norm_w).astype(z.dtype)


def ret_q(rq, rope):
    b, l, _ = rq.shape
    return rope_apply(rq.reshape(b, l, RET_HEADS, RET_DK), *rope)


def ret_kv(rk, rv, rope):
    b, l, _ = rk.shape
    k = rope_apply(rk.reshape(b, l, RET_HEADS, RET_DK), *rope) * (RET_DK ** -0.5)
    return k, rv.reshape(b, l, RET_HEADS, RET_DV)


def ret_finish(y, g, gn_w):
    b, l, nh, dv = y.shape
    mu = jnp.mean(y, axis=-1, keepdims=True)
    yc = y - mu
    var = jnp.mean(yc * yc, axis=-1, keepdims=True)
    yn = (yc * lax.rsqrt(var + NORM_EPS)).reshape(b, l, nh * dv) * gn_w.astype(jnp.float32)
    return (yn * jax.nn.silu(g.astype(jnp.float32))).astype(g.dtype)


def merge_branches(br_attn, br_ssd, br_ret, gate_logits, w_branch, w_out):
    b, l, _ = gate_logits.shape
    gates = jax.nn.sigmoid(gate_logits.reshape(b, l, N_BRANCH, D_MODEL))
    merged = (gates[:, :, 0] * (br_attn @ w_branch[0])
              + gates[:, :, 1] * (br_ssd @ w_branch[1])
              + gates[:, :, 2] * (br_ret @ w_branch[2]))
    return merged @ w_out


def sq_relu_mlp(x, w1, w2):
    h = jax.nn.relu(x @ w1)
    return (h * h) @ w2


def hybrid_mixer(u_lat, u_ctx, w_in, q_norm, k_norm, conv_w, conv_b, dt_bias, a_log, d_skip,
                 ssd_norm_w, ret_log_decay, ret_gn_w, w_branch, w_out,
                 rope_lat, ret_rope_ctx, ret_rope_lat, need_ctx):
    b = u_lat.shape[0]
    pl = split_cols(u_lat @ w_in)
    pc = split_cols(u_ctx @ w_in)

    k_c, v_c = attn_kv(pc[1], pc[2], k_norm, None)
    k_l, v_l = attn_kv(pl[1], pl[2], k_norm, rope_lat)
    q_l = attn_q(pl[0], q_norm, rope_lat)
    attn_l = gqa_attend(q_l, jnp.concatenate([k_c, k_l], axis=1), jnp.concatenate([v_c, v_l], axis=1))

    x_c, b_c, c_c, dtf_c, dtb_c = ssd_inputs(pc[4], pc[5], conv_w, conv_b, dt_bias)
    x_l, b_l, c_l, dtf_l, dtb_l = ssd_inputs(pl[4], pl[5], conv_w, conv_b, dt_bias)
    a_neg = -jnp.exp(a_log.astype(jnp.float32))
    s_init = jnp.zeros((b, SSD_GROUPS, SSD_HEADS // SSD_GROUPS, SSD_HEAD_DIM, SSD_STATE), jnp.float32)
    yc_f, st_f = ssd_scan(x_c, dtf_c, a_neg[0], b_c, c_c, s_init, need_ctx)
    yc_b, st_b = ssd_scan(flip(x_c), flip(dtb_c), a_neg[1], flip(b_c), flip(c_c), s_init, need_ctx)
    yl_f, _ = ssd_scan(x_l, dtf_l, a_neg[0], b_l, c_l, st_f, True)
    yl_b, _ = ssd_scan(flip(x_l), flip(dtb_l), a_neg[1], flip(b_l), flip(c_l), st_b, True)
    ssd_l = ssd_finish(yl_f + flip(yl_b), x_l, pl[3], d_skip, ssd_norm_w)

    lg = -jnp.exp(ret_log_decay.astype(jnp.float32))
    r_init = jnp.zeros((b, RET_HEADS, RET_DK, RET_DV), jnp.float32)
    rk_c, rv_c = ret_kv(pc[7], pc[8], ret_rope_ctx)
    rq_c = ret_q(pc[6], ret_rope_ctx) if need_ctx else None
    rk_l, rv_l = ret_kv(pl[7], pl[8], ret_rope_lat)
    rq_l = ret_q(pl[6], ret_rope_lat)
    rc_f, rs_f = retention_scan(rq_c, rk_c, rv_c, lg[0], r_init, need_ctx)
    rc_b, rs_b = retention_scan(None if rq_c is None else flip(rq_c), flip(rk_c), flip(rv_c), lg[1], r_init, need_ctx)
    rl_f, _ = retention_scan(rq_l, rk_l, rv_l, lg[0], rs_f, True)
    rl_b, _ = retention_scan(flip(rq_l), flip(rk_l), flip(rv_l), lg[1], rs_b, True)
    ret_l = ret_finish(rl_f + flip(rl_b), pl[9], ret_gn_w)

    out_l = merge_branches(attn_l, ssd_l, ret_l, pl[10], w_branch, w_out)
    if not need_ctx:
        return out_l, None

    attn_c = gqa_attend(attn_q(pc[0], q_norm, None), k_c, v_c)
    ssd_c = ssd_finish(yc_f + flip(yc_b), x_c, pc[3], d_skip, ssd_norm_w)
    ret_c = ret_finish(rc_f + flip(rc_b), pc[9], ret_gn_w)
    out_c = merge_branches(attn_c, ssd_c, ret_c, pc[10], w_branch, w_out)
    return out_l, out_c


def _fwd_setup_inputs(seed: int = 0) -> dict:
    key = jax.random.key(seed)
    ks = jax.random.split(key, 24)
    f32 = jnp.float32

    def nrm(k, shape, scale):
        return jax.random.normal(k, shape, f32) * scale

    dt = jnp.exp(jax.random.uniform(ks[13], (DEPTH, 2, SSD_HEADS), f32, math.log(1e-3), math.log(1e-1)))
    return {
        'x': nrm(ks[0], (BATCH, SEQ, D_MODEL), 1.0),
        'c': nrm(ks[1], (BATCH, D_MODEL), 1.0),
        'ctx': nrm(ks[2], (BATCH, CTX_LEN, D_MODEL), 1.0),
        'c_ctx': nrm(ks[3], (D_MODEL,), 1.0),
        'w_mod': nrm(ks[4], (DEPTH, D_MODEL, 6 * D_MODEL), 0.5 * D_MODEL ** -0.5),
        'b_mod': nrm(ks[5], (DEPTH, 6 * D_MODEL), 0.01),
        'norm1_w': 1.0 + nrm(ks[6], (DEPTH, D_MODEL), 0.02),
        'norm2_w': 1.0 + nrm(ks[7], (DEPTH, D_MODEL), 0.02),
        'w_in': nrm(ks[8], (DEPTH, D_MODEL, IN_DIM), D_MODEL ** -0.5),
        'attn_q_norm': 1.0 + nrm(ks[9], (DEPTH, ATTN_HEAD_DIM), 0.02),
        'attn_k_norm': 1.0 + nrm(ks[10], (DEPTH, ATTN_HEAD_DIM), 0.02),
        'ssd_conv_w': nrm(ks[11], (DEPTH, SSD_CONV_K, SSD_CONV_DIM), SSD_CONV_K ** -0.5),
        'ssd_conv_b': nrm(ks[12], (DEPTH, SSD_CONV_DIM), 0.01),
        'ssd_dt_bias': dt + jnp.log(-jnp.expm1(-dt)),
        'ssd_a_log': jnp.log(jax.random.uniform(ks[14], (DEPTH, 2, SSD_HEADS), f32, 1.0, 16.0)),
        'ssd_d': 1.0 + nrm(ks[15], (DEPTH, SSD_HEADS), 0.1),
        'ssd_norm_w': 1.0 + nrm(ks[16], (DEPTH, SSD_D_INNER), 0.02),
        'ret_log_decay': (-5.0 - jnp.arange(RET_HEADS, dtype=f32)) * math.log(2.0)
                         + nrm(ks[17], (DEPTH, 2, RET_HEADS), 0.1),
        'ret_gn_w': 1.0 + nrm(ks[18], (DEPTH, RET_HEADS * RET_DV), 0.02),
        'w_branch': nrm(ks[19], (DEPTH, N_BRANCH, BRANCH_W, D_MODEL), BRANCH_W ** -0.5),
        'w_out': nrm(ks[20], (DEPTH, D_MODEL, D_MODEL), D_MODEL ** -0.5),
        'w_mlp1': nrm(ks[21], (DEPTH, D_MODEL, MLP_HIDDEN), D_MODEL ** -0.5),
        'w_mlp2': nrm(ks[22], (DEPTH, MLP_HIDDEN, D_MODEL), MLP_HIDDEN ** -0.5),
        'final_norm_w': 1.0 + nrm(ks[23], (D_MODEL,), 0.02),
    }


def _fwd_reference(x, c, ctx, c_ctx, w_mod, b_mod, norm1_w, norm2_w, w_in, attn_q_norm, attn_k_norm,
              ssd_conv_w, ssd_conv_b, ssd_dt_bias, ssd_a_log, ssd_d, ssd_norm_w, ret_log_decay,
              ret_gn_w, w_branch, w_out, w_mlp1, w_mlp2, final_norm_w):
    n = x.shape[1]
    m = ctx.shape[1]
    ROWS = n // GRID_W
    rope_lat = axial_angles(ROWS)
    ret_rope_ctx = seq_angles(0, m)
    ret_rope_lat = seq_angles(m, n)
    h_lat, h_ctx = x, ctx
    for layer in range(DEPTH):
        need_ctx = layer < DEPTH - 1
        mod_lat = jnp.split(jax.nn.silu(c) @ w_mod[layer] + b_mod[layer], 6, axis=-1)
        mod_ctx = jnp.split(jax.nn.silu(c_ctx)[None, :] @ w_mod[layer] + b_mod[layer], 6, axis=-1)
        u_lat = modulate(rms_norm(h_lat, norm1_w[layer]), mod_lat[0], mod_lat[1])
        u_ctx = modulate(rms_norm(h_ctx, norm1_w[layer]), mod_ctx[0], mod_ctx[1])
        mix_lat, mix_ctx = hybrid_mixer(
            u_lat, u_ctx, w_in[layer], attn_q_norm[layer], attn_k_norm[layer],
            ssd_conv_w[layer], ssd_conv_b[layer], ssd_dt_bias[layer], ssd_a_log[layer], ssd_d[layer],
            ssd_norm_w[layer], ret_log_decay[layer], ret_gn_w[layer], w_branch[layer], w_out[layer],
            rope_lat, ret_rope_ctx, ret_rope_lat, need_ctx)
        h_lat = h_lat + mod_lat[2][:, None, :] * mix_lat
        v_lat = modulate(rms_norm(h_lat, norm2_w[layer]), mod_lat[3], mod_lat[4])
        h_lat = h_lat + mod_lat[5][:, None, :] * sq_relu_mlp(v_lat, w_mlp1[layer], w_mlp2[layer])
        if need_ctx:
            h_ctx = h_ctx + mod_ctx[2][:, None, :] * mix_ctx
            v_ctx = modulate(rms_norm(h_ctx, norm2_w[layer]), mod_ctx[3], mod_ctx[4])
            h_ctx = h_ctx + mod_ctx[5][:, None, :] * sq_relu_mlp(v_ctx, w_mlp1[layer], w_mlp2[layer])
    return rms_norm(h_lat, final_norm_w)


import jax as _jax
import jax.numpy as _jnp

TWIN_FORMAT = 'train_step'
FWD_PARAMS = ['x', 'c', 'ctx', 'c_ctx', 'w_mod', 'b_mod', 'norm1_w', 'norm2_w', 'w_in', 'attn_q_norm', 'attn_k_norm', 'ssd_conv_w', 'ssd_conv_b', 'ssd_dt_bias', 'ssd_a_log', 'ssd_d', 'ssd_norm_w', 'ret_log_decay', 'ret_gn_w', 'w_branch', 'w_out', 'w_mlp1', 'w_mlp2', 'final_norm_w']
TWIN_WEIGHTS = ['c_ctx', 'w_mod', 'b_mod', 'norm1_w', 'norm2_w', 'w_in', 'attn_q_norm', 'attn_k_norm', 'ssd_conv_w', 'ssd_conv_b', 'ssd_dt_bias', 'ssd_a_log', 'ssd_d', 'ssd_norm_w', 'ret_log_decay', 'ret_gn_w', 'w_branch', 'w_out', 'w_mlp1', 'w_mlp2', 'final_norm_w']
TWIN_DIFF_INPUT = 'x'
TWIN_INPUTS = ['x', 'c', 'ctx', 'c_ctx', 'w_mod', 'b_mod', 'norm1_w', 'norm2_w', 'w_in', 'attn_q_norm', 'attn_k_norm', 'ssd_conv_w', 'ssd_conv_b', 'ssd_dt_bias', 'ssd_a_log', 'ssd_d', 'ssd_norm_w', 'ret_log_decay', 'ret_gn_w', 'w_branch', 'w_out', 'w_mlp1', 'w_mlp2', 'final_norm_w', 'loss_target', 'm_c_ctx', 'm_w_mod', 'm_b_mod', 'm_norm1_w', 'm_norm2_w', 'm_w_in', 'm_attn_q_norm', 'm_attn_k_norm', 'm_ssd_conv_w', 'm_ssd_conv_b', 'm_ssd_dt_bias', 'm_ssd_a_log', 'm_ssd_d', 'm_ssd_norm_w', 'm_ret_log_decay', 'm_ret_gn_w', 'm_w_branch', 'm_w_out', 'm_w_mlp1', 'm_w_mlp2', 'm_final_norm_w', 'v_c_ctx', 'v_w_mod', 'v_b_mod', 'v_norm1_w', 'v_norm2_w', 'v_w_in', 'v_attn_q_norm', 'v_attn_k_norm', 'v_ssd_conv_w', 'v_ssd_conv_b', 'v_ssd_dt_bias', 'v_ssd_a_log', 'v_ssd_d', 'v_ssd_norm_w', 'v_ret_log_decay', 'v_ret_gn_w', 'v_w_branch', 'v_w_out', 'v_w_mlp1', 'v_w_mlp2', 'v_final_norm_w']
TWIN_OUTPUTS = ['loss', 'grad_x', 'grad_c_ctx', 'grad_w_mod', 'grad_b_mod', 'grad_norm1_w', 'grad_norm2_w', 'grad_w_in', 'grad_attn_q_norm', 'grad_attn_k_norm', 'grad_ssd_conv_w', 'grad_ssd_conv_b', 'grad_ssd_dt_bias', 'grad_ssd_a_log', 'grad_ssd_d', 'grad_ssd_norm_w', 'grad_ret_log_decay', 'grad_ret_gn_w', 'grad_w_branch', 'grad_w_out', 'grad_w_mlp1', 'grad_w_mlp2', 'grad_final_norm_w', 'delta_c_ctx', 'delta_w_mod', 'delta_b_mod', 'delta_norm1_w', 'delta_norm2_w', 'delta_w_in', 'delta_attn_q_norm', 'delta_attn_k_norm', 'delta_ssd_conv_w', 'delta_ssd_conv_b', 'delta_ssd_dt_bias', 'delta_ssd_a_log', 'delta_ssd_d', 'delta_ssd_norm_w', 'delta_ret_log_decay', 'delta_ret_gn_w', 'delta_w_branch', 'delta_w_out', 'delta_w_mlp1', 'delta_w_mlp2', 'delta_final_norm_w', 'new_m_c_ctx', 'new_m_w_mod', 'new_m_b_mod', 'new_m_norm1_w', 'new_m_norm2_w', 'new_m_w_in', 'new_m_attn_q_norm', 'new_m_attn_k_norm', 'new_m_ssd_conv_w', 'new_m_ssd_conv_b', 'new_m_ssd_dt_bias', 'new_m_ssd_a_log', 'new_m_ssd_d', 'new_m_ssd_norm_w', 'new_m_ret_log_decay', 'new_m_ret_gn_w', 'new_m_w_branch', 'new_m_w_out', 'new_m_w_mlp1', 'new_m_w_mlp2', 'new_m_final_norm_w', 'new_v_c_ctx', 'new_v_w_mod', 'new_v_b_mod', 'new_v_norm1_w', 'new_v_norm2_w', 'new_v_w_in', 'new_v_attn_q_norm', 'new_v_attn_k_norm', 'new_v_ssd_conv_w', 'new_v_ssd_conv_b', 'new_v_ssd_dt_bias', 'new_v_ssd_a_log', 'new_v_ssd_d', 'new_v_ssd_norm_w', 'new_v_ret_log_decay', 'new_v_ret_gn_w', 'new_v_w_branch', 'new_v_w_out', 'new_v_w_mlp1', 'new_v_w_mlp2', 'new_v_final_norm_w']
TWIN_LEAF_KINDS = {'loss': 'loss', 'grad_x': 'grad_x', 'grad_c_ctx': 'grad_w', 'grad_w_mod': 'grad_w', 'grad_b_mod': 'grad_w', 'grad_norm1_w': 'grad_w', 'grad_norm2_w': 'grad_w', 'grad_w_in': 'grad_w', 'grad_attn_q_norm': 'grad_w', 'grad_attn_k_norm': 'grad_w', 'grad_ssd_conv_w': 'grad_w', 'grad_ssd_conv_b': 'grad_w', 'grad_ssd_dt_bias': 'grad_w', 'grad_ssd_a_log': 'grad_w', 'grad_ssd_d': 'grad_w', 'grad_ssd_norm_w': 'grad_w', 'grad_ret_log_decay': 'grad_w', 'grad_ret_gn_w': 'grad_w', 'grad_w_branch': 'grad_w', 'grad_w_out': 'grad_w', 'grad_w_mlp1': 'grad_w', 'grad_w_mlp2': 'grad_w', 'grad_final_norm_w': 'grad_w', 'delta_c_ctx': 'delta_w', 'delta_w_mod': 'delta_w', 'delta_b_mod': 'delta_w', 'delta_norm1_w': 'delta_w', 'delta_norm2_w': 'delta_w', 'delta_w_in': 'delta_w', 'delta_attn_q_norm': 'delta_w', 'delta_attn_k_norm': 'delta_w', 'delta_ssd_conv_w': 'delta_w', 'delta_ssd_conv_b': 'delta_w', 'delta_ssd_dt_bias': 'delta_w', 'delta_ssd_a_log': 'delta_w', 'delta_ssd_d': 'delta_w', 'delta_ssd_norm_w': 'delta_w', 'delta_ret_log_decay': 'delta_w', 'delta_ret_gn_w': 'delta_w', 'delta_w_branch': 'delta_w', 'delta_w_out': 'delta_w', 'delta_w_mlp1': 'delta_w', 'delta_w_mlp2': 'delta_w', 'delta_final_norm_w': 'delta_w', 'new_m_c_ctx': 'new_m', 'new_m_w_mod': 'new_m', 'new_m_b_mod': 'new_m', 'new_m_norm1_w': 'new_m', 'new_m_norm2_w': 'new_m', 'new_m_w_in': 'new_m', 'new_m_attn_q_norm': 'new_m', 'new_m_attn_k_norm': 'new_m', 'new_m_ssd_conv_w': 'new_m', 'new_m_ssd_conv_b': 'new_m', 'new_m_ssd_dt_bias': 'new_m', 'new_m_ssd_a_log': 'new_m', 'new_m_ssd_d': 'new_m', 'new_m_ssd_norm_w': 'new_m', 'new_m_ret_log_decay': 'new_m', 'new_m_ret_gn_w': 'new_m', 'new_m_w_branch': 'new_m', 'new_m_w_out': 'new_m', 'new_m_w_mlp1': 'new_m', 'new_m_w_mlp2': 'new_m', 'new_m_final_norm_w': 'new_m', 'new_v_c_ctx': 'new_v', 'new_v_w_mod': 'new_v', 'new_v_b_mod': 'new_v', 'new_v_norm1_w': 'new_v', 'new_v_norm2_w': 'new_v', 'new_v_w_in': 'new_v', 'new_v_attn_q_norm': 'new_v', 'new_v_attn_k_norm': 'new_v', 'new_v_ssd_conv_w': 'new_v', 'new_v_ssd_conv_b': 'new_v', 'new_v_ssd_dt_bias': 'new_v', 'new_v_ssd_a_log': 'new_v', 'new_v_ssd_d': 'new_v', 'new_v_ssd_norm_w': 'new_v', 'new_v_ret_log_decay': 'new_v', 'new_v_ret_gn_w': 'new_v', 'new_v_w_branch': 'new_v', 'new_v_w_out': 'new_v', 'new_v_w_mlp1': 'new_v', 'new_v_w_mlp2': 'new_v', 'new_v_final_norm_w': 'new_v'}


def _forward(args):
    return _fwd_reference(*[args[k] for k in FWD_PARAMS])


def _output_shape():
    def fwd():
        inp = _fwd_setup_inputs(0)
        return _fwd_reference(*[inp[k] for k in FWD_PARAMS])
    out = _jax.eval_shape(fwd)
    return out.shape, out.dtype

N_MICROBATCH = 1
ADAM_LR = 0.001
ADAM_B1 = 0.9
ADAM_B2 = 0.999
ADAM_EPS = 1e-08
ADAM_WD = 0.01
ADAM_STEP = 10
PER_EXAMPLE_BATCH_AXIS = {'x': 0, 'c': 0, 'ctx': 0, 'loss_target': 0}
SHARED_INPUTS = []
_WEIGHT_DTYPES = {'c_ctx': _jnp.float32, 'w_mod': _jnp.float32, 'b_mod': _jnp.float32, 'norm1_w': _jnp.float32, 'norm2_w': _jnp.float32, 'w_in': _jnp.float32, 'attn_q_norm': _jnp.float32, 'attn_k_norm': _jnp.float32, 'ssd_conv_w': _jnp.float32, 'ssd_conv_b': _jnp.float32, 'ssd_dt_bias': _jnp.float32, 'ssd_a_log': _jnp.float32, 'ssd_d': _jnp.float32, 'ssd_norm_w': _jnp.float32, 'ret_log_decay': _jnp.float32, 'ret_gn_w': _jnp.float32, 'w_branch': _jnp.float32, 'w_out': _jnp.float32, 'w_mlp1': _jnp.float32, 'w_mlp2': _jnp.float32, 'final_norm_w': _jnp.float32}
MOMENT_SCALE = {'c_ctx': 2.264378e-02, 'w_mod': 1.161210e-01, 'b_mod': 2.016309e-01, 'norm1_w': 7.863733e-02, 'norm2_w': 1.084982e-01, 'w_in': 3.208794e-02, 'attn_q_norm': 1.436613e-02, 'attn_k_norm': 1.463505e-02, 'ssd_conv_w': 4.589635e-02, 'ssd_conv_b': 5.894703e-02, 'ssd_dt_bias': 1.440843e-01, 'ssd_a_log': 1.654499e-01, 'ssd_d': 1.882692e-01, 'ssd_norm_w': 6.174726e-02, 'ret_log_decay': 1.599808e-01, 'ret_gn_w': 3.974370e-02, 'w_branch': 2.899625e-02, 'w_out': 5.063601e-02, 'w_mlp1': 5.657339e-02, 'w_mlp2': 1.040339e-01, 'final_norm_w': 6.440719e+01}


def _to_microbatches(a, axis):
    t = _jnp.moveaxis(a, axis, 0)
    t = t.reshape((N_MICROBATCH, t.shape[0] // N_MICROBATCH) + t.shape[1:])
    return _jnp.moveaxis(t, 1, axis + 1)


def setup_inputs(seed: int = 0) -> dict:
    inp = _fwd_setup_inputs(seed)
    key = _jax.random.fold_in(_jax.random.key(seed), 7919)
    shape, _ = _output_shape()
    out = dict(inp)
    out["loss_target"] = _jax.random.normal(_jax.random.fold_in(key, 0), shape, _jnp.float32)
    for i, name in enumerate(TWIN_WEIGHTS):
        w = inp[name].astype(_jnp.float32)
        if MOMENT_SCALE is None:
            s = _jnp.sqrt(_jnp.mean(_jnp.square(w)) + 1e-30)
        else:
            s = MOMENT_SCALE[name]
        km, kv = _jax.random.split(_jax.random.fold_in(key, i + 1))
        out[name] = w
        out["m_" + name] = s * _jax.random.normal(km, w.shape, _jnp.float32)
        out["v_" + name] = (s * s) * _jax.random.uniform(kv, w.shape, _jnp.float32, 0.5, 1.5)
    if N_MICROBATCH > 1:
        for name, axis in PER_EXAMPLE_BATCH_AXIS.items():
            out[name] = _to_microbatches(out[name], axis)
    return {'x': out['x'], 'c': out['c'], 'ctx': out['ctx'], 'c_ctx': out['c_ctx'], 'w_mod': out['w_mod'], 'b_mod': out['b_mod'], 'norm1_w': out['norm1_w'], 'norm2_w': out['norm2_w'], 'w_in': out['w_in'], 'attn_q_norm': out['attn_q_norm'], 'attn_k_norm': out['attn_k_norm'], 'ssd_conv_w': out['ssd_conv_w'], 'ssd_conv_b': out['ssd_conv_b'], 'ssd_dt_bias': out['ssd_dt_bias'], 'ssd_a_log': out['ssd_a_log'], 'ssd_d': out['ssd_d'], 'ssd_norm_w': out['ssd_norm_w'], 'ret_log_decay': out['ret_log_decay'], 'ret_gn_w': out['ret_gn_w'], 'w_branch': out['w_branch'], 'w_out': out['w_out'], 'w_mlp1': out['w_mlp1'], 'w_mlp2': out['w_mlp2'], 'final_norm_w': out['final_norm_w'], 'loss_target': out['loss_target'], 'm_c_ctx': out['m_c_ctx'], 'm_w_mod': out['m_w_mod'], 'm_b_mod': out['m_b_mod'], 'm_norm1_w': out['m_norm1_w'], 'm_norm2_w': out['m_norm2_w'], 'm_w_in': out['m_w_in'], 'm_attn_q_norm': out['m_attn_q_norm'], 'm_attn_k_norm': out['m_attn_k_norm'], 'm_ssd_conv_w': out['m_ssd_conv_w'], 'm_ssd_conv_b': out['m_ssd_conv_b'], 'm_ssd_dt_bias': out['m_ssd_dt_bias'], 'm_ssd_a_log': out['m_ssd_a_log'], 'm_ssd_d': out['m_ssd_d'], 'm_ssd_norm_w': out['m_ssd_norm_w'], 'm_ret_log_decay': out['m_ret_log_decay'], 'm_ret_gn_w': out['m_ret_gn_w'], 'm_w_branch': out['m_w_branch'], 'm_w_out': out['m_w_out'], 'm_w_mlp1': out['m_w_mlp1'], 'm_w_mlp2': out['m_w_mlp2'], 'm_final_norm_w': out['m_final_norm_w'], 'v_c_ctx': out['v_c_ctx'], 'v_w_mod': out['v_w_mod'], 'v_b_mod': out['v_b_mod'], 'v_norm1_w': out['v_norm1_w'], 'v_norm2_w': out['v_norm2_w'], 'v_w_in': out['v_w_in'], 'v_attn_q_norm': out['v_attn_q_norm'], 'v_attn_k_norm': out['v_attn_k_norm'], 'v_ssd_conv_w': out['v_ssd_conv_w'], 'v_ssd_conv_b': out['v_ssd_conv_b'], 'v_ssd_dt_bias': out['v_ssd_dt_bias'], 'v_ssd_a_log': out['v_ssd_a_log'], 'v_ssd_d': out['v_ssd_d'], 'v_ssd_norm_w': out['v_ssd_norm_w'], 'v_ret_log_decay': out['v_ret_log_decay'], 'v_ret_gn_w': out['v_ret_gn_w'], 'v_w_branch': out['v_w_branch'], 'v_w_out': out['v_w_out'], 'v_w_mlp1': out['v_w_mlp1'], 'v_w_mlp2': out['v_w_mlp2'], 'v_final_norm_w': out['v_final_norm_w']}


def _loss(weights, diff, rest, loss_target):
    with _jax.named_scope("forward"):
        args = {**rest, TWIN_DIFF_INPUT: diff, **{k: w.astype(_WEIGHT_DTYPES[k]) for k, w in weights.items()}}
        y = _forward(args)
    with _jax.named_scope("loss_head"):
        err = _jnp.square(y.astype(_jnp.float32) - loss_target)
        return 0.5 * _jnp.sum(_jnp.mean(err, axis=-1)) if err.ndim else 0.5 * err


def _adamw(w, g, m, v):
    m = ADAM_B1 * m + (1.0 - ADAM_B1) * g
    v = ADAM_B2 * v + (1.0 - ADAM_B2) * _jnp.square(g)
    m_hat = m / (1.0 - ADAM_B1 ** ADAM_STEP)
    v_hat = v / (1.0 - ADAM_B2 ** ADAM_STEP)
    delta = -ADAM_LR * (m_hat / (_jnp.sqrt(v_hat) + ADAM_EPS) + ADAM_WD * w)
    return delta, m, v


def reference(x, c, ctx, c_ctx, w_mod, b_mod, norm1_w, norm2_w, w_in, attn_q_norm, attn_k_norm, ssd_conv_w, ssd_conv_b, ssd_dt_bias, ssd_a_log, ssd_d, ssd_norm_w, ret_log_decay, ret_gn_w, w_branch, w_out, w_mlp1, w_mlp2, final_norm_w, loss_target, m_c_ctx, m_w_mod, m_b_mod, m_norm1_w, m_norm2_w, m_w_in, m_attn_q_norm, m_attn_k_norm, m_ssd_conv_w, m_ssd_conv_b, m_ssd_dt_bias, m_ssd_a_log, m_ssd_d, m_ssd_norm_w, m_ret_log_decay, m_ret_gn_w, m_w_branch, m_w_out, m_w_mlp1, m_w_mlp2, m_final_norm_w, v_c_ctx, v_w_mod, v_b_mod, v_norm1_w, v_norm2_w, v_w_in, v_attn_q_norm, v_attn_k_norm, v_ssd_conv_w, v_ssd_conv_b, v_ssd_dt_bias, v_ssd_a_log, v_ssd_d, v_ssd_norm_w, v_ret_log_decay, v_ret_gn_w, v_w_branch, v_w_out, v_w_mlp1, v_w_mlp2, v_final_norm_w):
    given = dict(x=x, c=c, ctx=ctx, c_ctx=c_ctx, w_mod=w_mod, b_mod=b_mod, norm1_w=norm1_w, norm2_w=norm2_w, w_in=w_in, attn_q_norm=attn_q_norm, attn_k_norm=attn_k_norm, ssd_conv_w=ssd_conv_w, ssd_conv_b=ssd_conv_b, ssd_dt_bias=ssd_dt_bias, ssd_a_log=ssd_a_log, ssd_d=ssd_d, ssd_norm_w=ssd_norm_w, ret_log_decay=ret_log_decay, ret_gn_w=ret_gn_w, w_branch=w_branch, w_out=w_out, w_mlp1=w_mlp1, w_mlp2=w_mlp2, final_norm_w=final_norm_w, loss_target=loss_target, m_c_ctx=m_c_ctx, m_w_mod=m_w_mod, m_b_mod=m_b_mod, m_norm1_w=m_norm1_w, m_norm2_w=m_norm2_w, m_w_in=m_w_in, m_attn_q_norm=m_attn_q_norm, m_attn_k_norm=m_attn_k_norm, m_ssd_conv_w=m_ssd_conv_w, m_ssd_conv_b=m_ssd_conv_b, m_ssd_dt_bias=m_ssd_dt_bias, m_ssd_a_log=m_ssd_a_log, m_ssd_d=m_ssd_d, m_ssd_norm_w=m_ssd_norm_w, m_ret_log_decay=m_ret_log_decay, m_ret_gn_w=m_ret_gn_w, m_w_branch=m_w_branch, m_w_out=m_w_out, m_w_mlp1=m_w_mlp1, m_w_mlp2=m_w_mlp2, m_final_norm_w=m_final_norm_w, v_c_ctx=v_c_ctx, v_w_mod=v_w_mod, v_b_mod=v_b_mod, v_norm1_w=v_norm1_w, v_norm2_w=v_norm2_w, v_w_in=v_w_in, v_attn_q_norm=v_attn_q_norm, v_attn_k_norm=v_attn_k_norm, v_ssd_conv_w=v_ssd_conv_w, v_ssd_conv_b=v_ssd_conv_b, v_ssd_dt_bias=v_ssd_dt_bias, v_ssd_a_log=v_ssd_a_log, v_ssd_d=v_ssd_d, v_ssd_norm_w=v_ssd_norm_w, v_ret_log_decay=v_ret_log_decay, v_ret_gn_w=v_ret_gn_w, v_w_branch=v_w_branch, v_w_out=v_w_out, v_w_mlp1=v_w_mlp1, v_w_mlp2=v_w_mlp2, v_final_norm_w=v_final_norm_w)
    weights = {n: given[n] for n in TWIN_WEIGHTS}
    shared = {n: given[n] for n in SHARED_INPUTS}
    per_example = {n: given[n] for n in ['x', 'c', 'ctx']}
    grad_fn = _jax.value_and_grad(_loss, argnums=(0, 1))

    def one_microbatch(ex, loss_target):
        ex = dict(ex)
        diff = ex.pop(TWIN_DIFF_INPUT)
        return grad_fn(weights, diff, {**shared, **ex}, loss_target)

    if N_MICROBATCH == 1:
        loss, (grad_w, grad_x) = one_microbatch(per_example, given["loss_target"])
    else:
        def body(carry, xs):
            loss_sum, grad_sum = carry
            l_k, (gw_k, gx_k) = one_microbatch(xs[0], xs[1])
            with _jax.named_scope("update"):
                return (loss_sum + l_k, _jax.tree.map(_jnp.add, grad_sum, gw_k)), gx_k

        init = (_jnp.zeros((), _jnp.float32), _jax.tree.map(_jnp.zeros_like, weights))
        (loss, grad_w), grad_x = _jax.lax.scan(body, init, (per_example, given["loss_target"]))
    with _jax.named_scope("update"):
        delta_w, new_m, new_v = {}, {}, {}
        for n in TWIN_WEIGHTS:
            delta_w[n], new_m[n], new_v[n] = _adamw(weights[n], grad_w[n], given["m_" + n], given["v_" + n])
    return (loss, grad_x, *[grad_w[n] for n in TWIN_WEIGHTS], *[delta_w[n] for n in TWIN_WEIGHTS],
            *[new_m[n] for n in TWIN_WEIGHTS], *[new_v[n] for n in TWIN_WEIGHTS])
```

```python
import functools
import math

import jax
import jax.numpy as jnp
from jax import lax
from jax.experimental import pallas as pl
from jax.experimental.pallas import tpu as pltpu

F32 = jnp.float32
MXU_DTYPE = jnp.bfloat16
VMEM_LIMIT_BYTES = 48 * 1024 * 1024
LANES = 128
N_DEV = 8
MESH_AXES = ("x", "y", "c")

D_MODEL = 1024
GRID_W = 64
NORM_EPS = 1e-6
ROPE_THETA = 10000.0
ATTN_HEADS, ATTN_KV_HEADS, ATTN_HEAD_DIM = 8, 2, 64
ATTN_GROUP = ATTN_HEADS // ATTN_KV_HEADS
SSD_HEADS, SSD_HEAD_DIM, SSD_GROUPS, SSD_STATE = 8, 64, 2, 128
SSD_D_INNER = SSD_HEADS * SSD_HEAD_DIM
RET_HEADS, RET_DK, RET_DV = 4, 128, 128
SCAN_CHUNK = 128
N_BRANCH = 3
DEPTH = 2

IN_SPLITS = (512, 128, 128, 512, 1024, 16, 512, 512, 512, 512, 3072)
IN_DIM = sum(IN_SPLITS)
DT_COLS = 16
DT_PAD = LANES - DT_COLS
TAIL_PAD = 128
IN_DIM_PADDED = IN_DIM + DT_PAD + TAIL_PAD
DT_END = sum(IN_SPLITS[:6])

ADAM_LR, ADAM_B1, ADAM_B2, ADAM_EPS, ADAM_WD, ADAM_STEP = 0.001, 0.9, 0.999, 1e-08, 0.01, 10


def _tile(dim, prefs):
    for p in prefs:
        if dim % p == 0:
            return p
    return dim


def _params(*sem):
    return pltpu.CompilerParams(dimension_semantics=sem, vmem_limit_bytes=VMEM_LIMIT_BYTES)


def _mm(a, b, *, ta=False, tb=False, name):
    if ta:
        kdim, m = a.shape
    else:
        m, kdim = a.shape
    if tb:
        n, kdim_b = b.shape
    else:
        kdim_b, n = b.shape
    assert kdim == kdim_b, (a.shape, b.shape, ta, tb)
    tm = _tile(m, (768, 512, 256, 128))
    tn = _tile(n, (512, 384, 256, 128))
    tk = _tile(kdim, (1024, 768, 512, 256, 128))
    dims = (((0 if ta else 1,), (1 if tb else 0,)), ((), ()))

    def body(a_ref, b_ref, o_ref):
        part = lax.dot_general(a_ref[...].astype(MXU_DTYPE), b_ref[...].astype(MXU_DTYPE), dims,
                               preferred_element_type=F32)

        @pl.when(pl.program_id(2) == 0)
        def _():
            o_ref[...] = part

        @pl.when(pl.program_id(2) > 0)
        def _():
            o_ref[...] += part

    a_spec = pl.BlockSpec((tk, tm), lambda i, j, k: (k, i)) if ta else pl.BlockSpec((tm, tk), lambda i, j, k: (i, k))
    b_spec = pl.BlockSpec((tn, tk), lambda i, j, k: (j, k)) if tb else pl.BlockSpec((tk, tn), lambda i, j, k: (k, j))
    return pl.pallas_call(
        body, name=name,
        grid=(m // tm, n // tn, kdim // tk),
        in_specs=[a_spec, b_spec],
        out_specs=pl.BlockSpec((tm, tn), lambda i, j, k: (i, j)),
        out_shape=jax.ShapeDtypeStruct((m, n), F32),
        compiler_params=_params("parallel", "parallel", "arbitrary"),
    )(a, b)


@jax.custom_vjp
def matmul(a, b):
    return _mm(a, b, name="mm_fwd")


def _matmul_fwd(a, b):
    return _mm(a, b, name="mm_fwd"), (a, b)


def _matmul_bwd(res, g):
    a, b = res
    return _mm(g, b, tb=True, name="mm_dx"), _mm(a, g, ta=True, name="mm_dw")


matmul.defvjp(_matmul_fwd, _matmul_bwd)


NEG_BIG = -1e30


def _attn_tiles(t, ctx_len):
    tq = _tile(ctx_len, (256, 128))
    assert t % tq == 0 and ctx_len % tq == 0
    tk = _tile(t, (768, 512, 256, 128))
    return tq, tk


def _scores(q_ref, k_ref, ki, tk, ctx_len, masked):
    scale = ATTN_HEAD_DIM ** -0.5
    q = (q_ref[0] * scale).reshape(-1, ATTN_HEAD_DIM).astype(MXU_DTYPE)
    s = lax.dot_general(q, k_ref[0].astype(MXU_DTYPE), (((1,), (1,)), ((), ())), preferred_element_type=F32)
    if masked:
        col = ki * tk + lax.broadcasted_iota(jnp.int32, s.shape, 1)
        s = jnp.where(col < ctx_len, s, NEG_BIG)
    return q, s


def _attn_cases(qi, ki, tq, tk, ctx_len, compute):
    ctx_q = (qi + 1) * tq <= ctx_len

    @pl.when(jnp.logical_not(ctx_q))
    def _():
        compute(False)

    @pl.when(jnp.logical_and(ctx_q, ki * tk < ctx_len))
    def _():
        compute(True)


def _attn_fwd_call(q, k, v, ctx_len):
    kvh, grp, t, hd = q.shape
    tq, tk = _attn_tiles(t, ctx_len)
    nkb = t // tk
    rows = grp * tq

    def body(q_ref, k_ref, v_ref, o_ref, lse_ref, m_sc, l_sc, acc_sc):
        qi, ki = pl.program_id(1), pl.program_id(2)

        @pl.when(ki == 0)
        def _():
            m_sc[...] = jnp.full(m_sc.shape, NEG_BIG, F32)
            l_sc[...] = jnp.zeros(l_sc.shape, F32)
            acc_sc[...] = jnp.zeros(acc_sc.shape, F32)

        def compute(masked):
            _, s = _scores(q_ref, k_ref, ki, tk, ctx_len, masked)
            m_prev = m_sc[...]
            m_new = jnp.maximum(m_prev, jnp.max(s, axis=1, keepdims=True))
            alpha = jnp.exp(m_prev - m_new)
            p = jnp.exp(s - m_new)
            l_sc[...] = alpha * l_sc[...] + jnp.sum(p, axis=1, keepdims=True)
            acc_sc[...] = alpha * acc_sc[...] + jnp.dot(p.astype(MXU_DTYPE), v_ref[0].astype(MXU_DTYPE),
                                                        preferred_element_type=F32)
            m_sc[...] = m_new

        _attn_cases(qi, ki, tq, tk, ctx_len, compute)

        @pl.when(ki == nkb - 1)
        def _():
            o_ref[0] = (acc_sc[...] / l_sc[...]).reshape(grp, tq, hd)
            lse_ref[0] = (m_sc[...] + jnp.log(l_sc[...])).reshape(grp, tq, 1)

    return pl.pallas_call(
        body, name="attn_fwd",
        grid=(kvh, t // tq, nkb),
        in_specs=[pl.BlockSpec((1, grp, tq, hd), lambda h, i, j: (h, 0, i, 0)),
                  pl.BlockSpec((1, tk, hd), lambda h, i, j: (h, j, 0)),
                  pl.BlockSpec((1, tk, hd), lambda h, i, j: (h, j, 0))],
        out_specs=[pl.BlockSpec((1, grp, tq, hd), lambda h, i, j: (h, 0, i, 0)),
                   pl.BlockSpec((1, grp, tq, 1), lambda h, i, j: (h, 0, i, 0))],
        out_shape=[jax.ShapeDtypeStruct(q.shape, F32), jax.ShapeDtypeStruct((kvh, grp, t, 1), F32)],
        scratch_shapes=[pltpu.VMEM((rows, 1), F32), pltpu.VMEM((rows, 1), F32), pltpu.VMEM((rows, hd), F32)],
        compiler_params=_params("parallel", "parallel", "arbitrary"),
    )(q, k, v)


def _attn_probs(q_ref, k_ref, v_ref, o_ref, do_ref, lse_ref, ki, tk, ctx_len, masked):
    q, s = _scores(q_ref, k_ref, ki, tk, ctx_len, masked)
    hd = ATTN_HEAD_DIM
    do = do_ref[0].reshape(-1, hd)
    delta = jnp.sum(do * o_ref[0].reshape(-1, hd), axis=1, keepdims=True)
    p = jnp.exp(s - lse_ref[0].reshape(-1, 1))
    do = do.astype(MXU_DTYPE)
    dp = lax.dot_general(do, v_ref[0].astype(MXU_DTYPE), (((1,), (1,)), ((), ())), preferred_element_type=F32)
    ds = p * (dp - delta)
    return q, do, p, ds


def _attn_dq_call(q, k, v, o, lse, do, ctx_len):
    kvh, grp, t, hd = q.shape
    tq, tk = _attn_tiles(t, ctx_len)
    nkb = t // tk

    def body(q_ref, k_ref, v_ref, o_ref, lse_ref, do_ref, dq_ref, acc_sc):
        qi, ki = pl.program_id(1), pl.program_id(2)

        @pl.when(ki == 0)
        def _():
            acc_sc[...] = jnp.zeros(acc_sc.shape, F32)

        def compute(masked):
            _, _, _, ds = _attn_probs(q_ref, k_ref, v_ref, o_ref, do_ref, lse_ref, ki, tk, ctx_len, masked)
            acc_sc[...] += jnp.dot(ds.astype(MXU_DTYPE), k_ref[0].astype(MXU_DTYPE), preferred_element_type=F32)

        _attn_cases(qi, ki, tq, tk, ctx_len, compute)

        @pl.when(ki == nkb - 1)
        def _():
            dq_ref[0] = (acc_sc[...] * (hd ** -0.5)).reshape(grp, tq, hd)

    qspec = pl.BlockSpec((1, grp, tq, hd), lambda h, i, j: (h, 0, i, 0))
    kspec = pl.BlockSpec((1, tk, hd), lambda h, i, j: (h, j, 0))
    return pl.pallas_call(
        body, name="attn_dq",
        grid=(kvh, t // tq, nkb),
        in_specs=[qspec, kspec, kspec, qspec, pl.BlockSpec((1, grp, tq, 1), lambda h, i, j: (h, 0, i, 0)), qspec],
        out_specs=qspec,
        out_shape=jax.ShapeDtypeStruct(q.shape, F32),
        scratch_shapes=[pltpu.VMEM((grp * tq, hd), F32)],
        compiler_params=_params("parallel", "parallel", "arbitrary"),
    )(q, k, v, o, lse, do)


def _attn_dkv_call(q, k, v, o, lse, do, ctx_len):
    kvh, grp, t, hd = q.shape
    tq, tk = _attn_tiles(t, ctx_len)
    nqb = t // tq

    def body(q_ref, k_ref, v_ref, o_ref, lse_ref, do_ref, dk_ref, dv_ref, dk_sc, dv_sc):
        ki, qi = pl.program_id(1), pl.program_id(2)

        @pl.when(qi == 0)
        def _():
            dk_sc[...] = jnp.zeros(dk_sc.shape, F32)
            dv_sc[...] = jnp.zeros(dv_sc.shape, F32)

        def compute(masked):
            qs, do, p, ds = _attn_probs(q_ref, k_ref, v_ref, o_ref, do_ref, lse_ref, ki, tk, ctx_len, masked)
            tn = (((0,), (0,)), ((), ()))
            dv_sc[...] += lax.dot_general(p.astype(MXU_DTYPE), do, tn, preferred_element_type=F32)
            dk_sc[...] += lax.dot_general(ds.astype(MXU_DTYPE), qs, tn, preferred_element_type=F32)

        _attn_cases(qi, ki, tq, tk, ctx_len, compute)

        @pl.when(qi == nqb - 1)
        def _():
            dk_ref[0] = dk_sc[...]
            dv_ref[0] = dv_sc[...]

    qspec = pl.BlockSpec((1, grp, tq, hd), lambda h, j, i: (h, 0, i, 0))
    kspec = pl.BlockSpec((1, tk, hd), lambda h, j, i: (h, j, 0))
    return pl.pallas_call(
        body, name="attn_dkv",
        grid=(kvh, t // tk, nqb),
        in_specs=[qspec, kspec, kspec, qspec, pl.BlockSpec((1, grp, tq, 1), lambda h, j, i: (h, 0, i, 0)), qspec],
        out_specs=[kspec, kspec],
        out_shape=[jax.ShapeDtypeStruct(k.shape, F32), jax.ShapeDtypeStruct(v.shape, F32)],
        scratch_shapes=[pltpu.VMEM((tk, hd), F32), pltpu.VMEM((tk, hd), F32)],
        compiler_params=_params("parallel", "parallel", "arbitrary"),
    )(q, k, v, o, lse, do)


@functools.partial(jax.custom_vjp, nondiff_argnums=(3,))
def attention(q, k, v, ctx_len):
    return _attn_fwd_call(q, k, v, ctx_len)[0]


def _attention_fwd(q, k, v, ctx_len):
    o, lse = _attn_fwd_call(q, k, v, ctx_len)
    return o, (q, k, v, o, lse)


def _attention_bwd(ctx_len, res, do):
    q, k, v, o, lse = res
    dq = _attn_dq_call(q, k, v, o, lse, do, ctx_len)
    dk, dv = _attn_dkv_call(q, k, v, o, lse, do, ctx_len)
    return dq, dk, dv


attention.defvjp(_attention_fwd, _attention_bwd)


def _chunk_order(step, n_chunks, n_ctx_chunks, reverse):
    if not reverse:
        return step
    return jnp.where(step < n_ctx_chunks, n_ctx_chunks - 1 - step, n_chunks + n_ctx_chunks - 1 - step)


def _scan_masks(chunk, reverse):
    row = lax.broadcasted_iota(jnp.int32, (chunk, chunk), 0)
    col = lax.broadcasted_iota(jnp.int32, (chunk, chunk), 1)
    vis = (col >= row) if reverse else (col <= row)
    vis_t = (row >= col) if reverse else (row <= col)
    return vis, vis.astype(F32), vis_t.astype(F32)


def _cum_decay(a_col, a_row, vis_f):
    hi = lax.Precision.HIGHEST
    cum_col = jnp.dot(vis_f, a_col, precision=hi, preferred_element_type=F32)
    cum_row = lax.dot_general(a_row, vis_f, (((1,), (1,)), ((), ())), precision=hi, preferred_element_type=F32)
    total = jnp.sum(a_col, axis=0, keepdims=True)
    return cum_col, cum_row, total


def _bf(x):
    return x.astype(MXU_DTYPE)


def _dot(a, b):
    return jnp.dot(_bf(a), _bf(b), preferred_element_type=F32)


def _dot_nt(a, b):
    return lax.dot_general(_bf(a), _bf(b), (((1,), (1,)), ((), ())), preferred_element_type=F32)


def _dot_tn(a, b):
    return lax.dot_general(_bf(a), _bf(b), (((0,), (0,)), ((), ())), preferred_element_type=F32)


def _scan_specs(chunk, n_chunks, n_ctx_chunks, reverse, backward, widths):
    def order(i):
        step = (n_chunks - 1 - i) if backward else i
        return _chunk_order(step, n_chunks, n_ctx_chunks, reverse)

    return [pl.BlockSpec((chunk, w), lambda i: (order(i), 0)) for w in widths], order


def _scan_fwd_call(q, k, v, a_col, a_row, *, groups, per_group, dk, dv, ctx_len, reverse):
    t = q.shape[0]
    chunk = SCAN_CHUNK
    n_chunks, n_ctx = t // chunk, ctx_len // chunk
    heads = groups * per_group
    (q_spec, k_spec, v_spec, acol_spec), order = _scan_specs(
        chunk, n_chunks, n_ctx, reverse, False, (groups * dk, groups * dk, heads * dv, LANES))

    def body(q_ref, k_ref, v_ref, acol_ref, arow_ref, y_ref, st_ref, s_sc):
        @pl.when(pl.program_id(0) == 0)
        def _():
            s_sc[...] = jnp.zeros(s_sc.shape, F32)

        st_ref[0] = s_sc[...]
        vis, vis_f, _ = _scan_masks(chunk, reverse)
        cum_col, cum_row, total = _cum_decay(acol_ref[...], arow_ref[...], vis_f)
        for g in range(groups):
            qg = q_ref[:, g * dk:(g + 1) * dk]
            kg = k_ref[:, g * dk:(g + 1) * dk]
            qk = _dot_nt(qg, kg)
            for r in range(per_group):
                h = g * per_group + r
                ccol = cum_col[:, h:h + 1]
                decay = jnp.exp(jnp.where(vis, ccol - cum_row[h:h + 1, :], NEG_BIG))
                vh = v_ref[:, h * dv:(h + 1) * dv]
                s_in = s_sc[h]
                y = _dot(qk * decay, vh) + jnp.exp(ccol) * _dot(qg, s_in)
                y_ref[:, h * dv:(h + 1) * dv] = y
                tot = total[:, h:h + 1]
                s_sc[h] = jnp.exp(tot) * s_in + _dot_tn(kg * jnp.exp(tot - ccol), vh)

    return pl.pallas_call(
        body, name="scan_fwd",
        grid=(n_chunks,),
        in_specs=[q_spec, k_spec, v_spec, acol_spec, pl.BlockSpec((8, chunk), lambda i: (0, order(i)))],
        out_specs=[v_spec, pl.BlockSpec((1, heads, dk, dv), lambda i: (order(i), 0, 0, 0))],
        out_shape=[jax.ShapeDtypeStruct(v.shape, F32), jax.ShapeDtypeStruct((n_chunks, heads, dk, dv), F32)],
        scratch_shapes=[pltpu.VMEM((heads, dk, dv), F32)],
        compiler_params=_params("arbitrary"),
    )(q, k, v, a_col, a_row)


def _scan_bwd_call(q, k, v, a_col, a_row, states, dy, *, groups, per_group, dk, dv, ctx_len, reverse):
    t = q.shape[0]
    chunk = SCAN_CHUNK
    n_chunks, n_ctx = t // chunk, ctx_len // chunk
    heads = groups * per_group
    (q_spec, k_spec, v_spec, acol_spec), order = _scan_specs(
        chunk, n_chunks, n_ctx, reverse, True, (groups * dk, groups * dk, heads * dv, LANES))
    arow_spec = pl.BlockSpec((8, chunk), lambda i: (0, order(i)))
    last = 0 if reverse else chunk - 1

    def body(q_ref, k_ref, v_ref, acol_ref, arow_ref, st_ref, dy_ref, dq_ref, dk_ref, dv_ref, da_ref, dat_ref,
             ds_sc):
        @pl.when(pl.program_id(0) == 0)
        def _():
            ds_sc[...] = jnp.zeros(ds_sc.shape, F32)

        vis, vis_f, vis_tf = _scan_masks(chunk, reverse)
        cum_col, cum_row, total = _cum_decay(acol_ref[...], arow_ref[...], vis_f)
        lane = lax.broadcasted_iota(jnp.int32, (chunk, LANES), 1)
        row = lax.broadcasted_iota(jnp.int32, (chunk, LANES), 0)
        sub = lax.broadcasted_iota(jnp.int32, (8, chunk), 0)
        dcum = jnp.zeros((chunk, LANES), F32)
        dcum_t = jnp.zeros((8, chunk), F32)
        for g in range(groups):
            qg = q_ref[:, g * dk:(g + 1) * dk]
            kg = k_ref[:, g * dk:(g + 1) * dk]
            qk = _dot_nt(qg, kg)
            dq_g = jnp.zeros((chunk, dk), F32)
            dk_g = jnp.zeros((chunk, dk), F32)
            for r in range(per_group):
                h = g * per_group + r
                ccol = cum_col[:, h:h + 1]
                decay = jnp.exp(jnp.where(vis, ccol - cum_row[h:h + 1, :], NEG_BIG))
                vh = v_ref[:, h * dv:(h + 1) * dv]
                dyh = dy_ref[:, h * dv:(h + 1) * dv]
                s_in = st_ref[0, h]
                ds_out = ds_sc[h]
                tot = total[:, h:h + 1]
                e_in = jnp.exp(ccol)
                e_out = jnp.exp(tot - ccol)
                e_tot = jnp.exp(tot)
                k_out = kg * e_out
                dv_ref[:, h * dv:(h + 1) * dv] = _dot_tn(qk * decay, dyh) + _dot(k_out, ds_out)
                dqk = _dot_nt(dyh, vh) * decay
                dq_in = e_in * _dot_nt(dyh, s_in)
                dk_out = e_out * _dot_nt(vh, ds_out)
                dq_h = _dot(dqk, kg) + dq_in
                dk_h = _dot_tn(dqk, qg) + dk_out
                s_out = e_tot * s_in + _dot_tn(k_out, vh)
                edge = jnp.sum(jnp.sum(s_out * ds_out, axis=1, keepdims=True), axis=0, keepdims=True)
                w_seg = dqk * qk
                dcum_h = (jnp.sum(w_seg, axis=1, keepdims=True) + jnp.sum(dq_in * qg, axis=1, keepdims=True)
                          - jnp.sum(dk_out * kg, axis=1, keepdims=True))
                dcum = jnp.where(lane == h, dcum_h + jnp.where(row == last, edge, 0.0), dcum)
                dcum_t = jnp.where(sub == h, -jnp.sum(w_seg, axis=0, keepdims=True), dcum_t)
                ds_sc[h] = e_tot * ds_out + _dot_tn(qg, e_in * dyh)
                dq_g = dq_g + dq_h
                dk_g = dk_g + dk_h
            dq_ref[:, g * dk:(g + 1) * dk] = dq_g
            dk_ref[:, g * dk:(g + 1) * dk] = dk_g
        hi = lax.Precision.HIGHEST
        da_ref[...] = jnp.dot(vis_tf, dcum, precision=hi, preferred_element_type=F32)
        dat_ref[...] = jnp.dot(dcum_t, vis_f, precision=hi, preferred_element_type=F32)

    return pl.pallas_call(
        body, name="scan_bwd",
        grid=(n_chunks,),
        in_specs=[q_spec, k_spec, v_spec, acol_spec, arow_spec,
                  pl.BlockSpec((1, heads, dk, dv), lambda i: (order(i), 0, 0, 0)), v_spec],
        out_specs=[q_spec, k_spec, v_spec, acol_spec, arow_spec],
        out_shape=[jax.ShapeDtypeStruct(q.shape, F32), jax.ShapeDtypeStruct(k.shape, F32),
                   jax.ShapeDtypeStruct(v.shape, F32), jax.ShapeDtypeStruct((t, LANES), F32),
                   jax.ShapeDtypeStruct((8, t), F32)],
        scratch_shapes=[pltpu.VMEM((heads, dk, dv), F32)],
        compiler_params=_params("arbitrary"),
    )(q, k, v, a_col, a_row, states, dy)


def _decay_layouts(a):
    t, heads = a.shape
    a_col = jnp.pad(a, ((0, 0), (0, LANES - heads)))
    a_row = jnp.pad(a.T, ((0, 8 - heads), (0, 0)))
    return a_col, a_row


@functools.partial(jax.custom_vjp, nondiff_argnums=(4,))
def linear_scan(q, k, v, a, cfg):
    a_col, a_row = _decay_layouts(a)
    return _scan_fwd_call(q, k, v, a_col, a_row, **dict(cfg))[0]


def _linear_scan_fwd(q, k, v, a, cfg):
    a_col, a_row = _decay_layouts(a)
    y, states = _scan_fwd_call(q, k, v, a_col, a_row, **dict(cfg))
    return y, (q, k, v, a, states)


def _linear_scan_bwd(cfg, res, dy):
    q, k, v, a, states = res
    a_col, a_row = _decay_layouts(a)
    dq, dk, dv, da, da_t = _scan_bwd_call(q, k, v, a_col, a_row, states, dy, **dict(cfg))
    heads = a.shape[1]
    return dq, dk, dv, da[:, :heads] + da_t[:heads].T


linear_scan.defvjp(_linear_scan_fwd, _linear_scan_bwd)


def _scan_cfg(groups, per_group, dk, dv, ctx_len, reverse):
    return (("groups", groups), ("per_group", per_group), ("dk", dk), ("dv", dv), ("ctx_len", ctx_len),
            ("reverse", reverse))


def _rms_norm(x, w):
    return x * lax.rsqrt(jnp.mean(x * x, axis=-1, keepdims=True) + NORM_EPS) * w


def _rope(x, cos, sin):
    half = x.shape[-1] // 2
    x1, x2 = x[..., :half], x[..., half:]
    cs, sn = cos[:, None, :], sin[:, None, :]
    return jnp.concatenate([x1 * cs - x2 * sn, x1 * sn + x2 * cs], axis=-1)


def _axial_tables(n_lat, n_ctx):
    freqs = ATTN_HEAD_DIM // 4
    rows = n_lat // GRID_W
    row = jnp.repeat(jnp.arange(rows, dtype=F32), GRID_W)
    col = jnp.tile(jnp.arange(GRID_W, dtype=F32), rows)
    inv = ROPE_THETA ** (-jnp.arange(freqs, dtype=F32) / freqs)
    ang = jnp.concatenate([row[:, None] * inv, col[:, None] * inv], axis=-1)
    cos = jnp.concatenate([jnp.ones((n_ctx, 2 * freqs), F32), jnp.cos(ang)], axis=0)
    sin = jnp.concatenate([jnp.zeros((n_ctx, 2 * freqs), F32), jnp.sin(ang)], axis=0)
    return cos, sin


def _seq_tables(t):
    pos = jnp.arange(t, dtype=F32)
    inv = ROPE_THETA ** (-jnp.linspace(0.0, 1.0, RET_DK // 2, dtype=F32))
    ang = pos[:, None] * inv
    return jnp.cos(ang), jnp.sin(ang)


def _pad_w_in(w_in):
    d = w_in.shape[0]
    return jnp.concatenate([w_in[:, :DT_END], jnp.zeros((d, DT_PAD), w_in.dtype), w_in[:, DT_END:],
                            jnp.zeros((d, TAIL_PAD), w_in.dtype)], axis=1)


def _split_proj(p):
    widths = list(IN_SPLITS)
    widths[5] = LANES
    out, off = [], 0
    for w in widths:
        out.append(p[:, off:off + w])
        off += w
    out[5] = out[5][:, :DT_COLS]
    return out


def _mixer(u, w, layer, n_ctx, tables):
    t = u.shape[0]
    attn_rope, ret_rope, seg_first, seg_last = tables
    aq, ak, av, z, xbc_raw, dt_raw, rq, rk, rv, rg, gate_logits = _split_proj(matmul(u, _pad_w_in(w["w_in"][layer])))

    q = _rope(_rms_norm(aq.reshape(t, ATTN_HEADS, ATTN_HEAD_DIM), w["attn_q_norm"][layer]), *attn_rope)
    k = _rope(_rms_norm(ak.reshape(t, ATTN_KV_HEADS, ATTN_HEAD_DIM), w["attn_k_norm"][layer]), *attn_rope)
    q4 = q.reshape(t, ATTN_KV_HEADS, ATTN_GROUP, ATTN_HEAD_DIM).transpose(1, 2, 0, 3)
    k3 = k.transpose(1, 0, 2)
    v3 = av.reshape(t, ATTN_KV_HEADS, ATTN_HEAD_DIM).transpose(1, 0, 2)
    o4 = attention(q4, k3, v3, n_ctx)
    br_attn = o4.transpose(2, 0, 1, 3).reshape(t, ATTN_HEADS * ATTN_HEAD_DIM)

    cw, cb = w["ssd_conv_w"][layer], w["ssd_conv_b"][layer]
    zero_row = jnp.zeros((1, xbc_raw.shape[1]), F32)
    prev = jnp.concatenate([zero_row, xbc_raw[:-1]], axis=0) * (1.0 - seg_first)
    nxt = jnp.concatenate([xbc_raw[1:], zero_row], axis=0) * (1.0 - seg_last)
    xbc = jax.nn.silu(prev * cw[0] + xbc_raw * cw[1] + nxt * cw[2] + cb)
    gn = SSD_GROUPS * SSD_STATE
    xs = xbc[:, :SSD_D_INNER]
    bm = xbc[:, SSD_D_INNER:SSD_D_INNER + gn]
    cm = xbc[:, SSD_D_INNER + gn:]
    dt = jax.nn.softplus(dt_raw.reshape(t, 2, SSD_HEADS) + w["ssd_dt_bias"][layer])
    a_neg = -jnp.exp(w["ssd_a_log"][layer])
    xs_h = xs.reshape(t, SSD_HEADS, SSD_HEAD_DIM)
    y_ssd = jnp.zeros((t, SSD_D_INNER), F32)
    for d, reverse in ((0, False), (1, True)):
        dtd = dt[:, d]
        cfg = _scan_cfg(SSD_GROUPS, SSD_HEADS // SSD_GROUPS, SSD_STATE, SSD_HEAD_DIM, n_ctx, reverse)
        y_ssd = y_ssd + linear_scan(cm, bm, (xs_h * dtd[:, :, None]).reshape(t, SSD_D_INNER), dtd * a_neg[d], cfg)
    y_ssd = y_ssd.reshape(t, SSD_HEADS, SSD_HEAD_DIM) + w["ssd_d"][layer][:, None] * xs_h
    br_ssd = _rms_norm(y_ssd.reshape(t, SSD_D_INNER) * jax.nn.silu(z), w["ssd_norm_w"][layer])

    lg = -jnp.exp(w["ret_log_decay"][layer])
    rq_r = _rope(rq.reshape(t, RET_HEADS, RET_DK), *ret_rope).reshape(t, RET_HEADS * RET_DK)
    rk_r = (_rope(rk.reshape(t, RET_HEADS, RET_DK), *ret_rope) * (RET_DK ** -0.5)).reshape(t, RET_HEADS * RET_DK)
    y_ret = jnp.zeros((t, RET_HEADS * RET_DV), F32)
    for d, reverse in ((0, False), (1, True)):
        cfg = _scan_cfg(RET_HEADS, 1, RET_DK, RET_DV, n_ctx, reverse)
        y_ret = y_ret + linear_scan(rq_r, rk_r, rv, jnp.broadcast_to(lg[d][None, :], (t, RET_HEADS)), cfg)
    y4 = y_ret.reshape(t, RET_HEADS, RET_DV)
    yc = y4 - jnp.mean(y4, axis=-1, keepdims=True)
    yn = yc * lax.rsqrt(jnp.mean(yc * yc, axis=-1, keepdims=True) + NORM_EPS)
    br_ret = yn.reshape(t, RET_HEADS * RET_DV) * w["ret_gn_w"][layer] * jax.nn.silu(rg)

    gates = jax.nn.sigmoid(gate_logits.reshape(t, N_BRANCH, D_MODEL))
    merged = (gates[:, 0] * matmul(br_attn, w["w_branch"][layer, 0])
              + gates[:, 1] * matmul(br_ssd, w["w_branch"][layer, 1])
              + gates[:, 2] * matmul(br_ret, w["w_branch"][layer, 2]))
    return matmul(merged, w["w_out"][layer])


def _local_loss(w, x, c, ctx, target):
    n, m = x.shape[0], ctx.shape[0]
    t = n + m
    is_lat = (jnp.arange(t) >= m)[:, None]
    pos = jnp.arange(t)[:, None]
    seg_first = ((pos == 0) | (pos == m)).astype(F32)
    seg_last = ((pos == m - 1) | (pos == t - 1)).astype(F32)
    tables = (_axial_tables(n, m), _seq_tables(t), seg_first, seg_last)
    h = jnp.concatenate([ctx, x], axis=0)
    cond = jax.nn.silu(jnp.stack([c, w["c_ctx"]], axis=0))
    cond8 = jnp.concatenate([cond, jnp.zeros((6, D_MODEL), F32)], axis=0)
    for layer in range(DEPTH):
        mod = (matmul(cond8, w["w_mod"][layer])[:2] + w["b_mod"][layer]).reshape(2, 6, D_MODEL)

        def sel(j, mod=mod):
            return jnp.where(is_lat, mod[0, j][None, :], mod[1, j][None, :])

        u = _rms_norm(h, w["norm1_w"][layer]) * (1 + sel(1)) + sel(0)
        h = h + sel(2) * _mixer(u, w, layer, m, tables)
        v = _rms_norm(h, w["norm2_w"][layer]) * (1 + sel(4)) + sel(3)
        hid = jax.nn.relu(matmul(v, w["w_mlp1"][layer]))
        h = h + sel(5) * matmul(hid * hid, w["w_mlp2"][layer])
    y = _rms_norm(h[m:], w["final_norm_w"])
    return 0.5 * jnp.sum(jnp.mean(jnp.square(y - target), axis=-1))


def _coords():
    return lax.axis_index("x"), lax.axis_index("y"), lax.axis_index("c")


def _all_gather(block, name):
    rows, lanes = block.shape

    def body(x_ref, out_ref, send_sems, recv_sems, local_sem):
        x, y, c = _coords()
        me, sibling = (x, y, c), (x, y, 1 - c)
        chips = [(1 - x, y), (x, 1 - y), (1 - x, 1 - y)]

        def slot(px, py, pc):
            return out_ref.at[4 * px + 2 * py + pc]

        def copy(k, blk, to, src=None):
            return pltpu.make_async_remote_copy(
                src_ref=slot(*blk) if src is None else src, dst_ref=slot(*blk),
                send_sem=send_sems.at[k], recv_sem=recv_sems.at[k],
                device_id=to, device_id_type=pl.DeviceIdType.MESH)

        mine = pltpu.make_async_copy(x_ref, slot(*me), local_sem)
        mine.start()
        first = [copy(0, me, sibling, src=x_ref)]
        first += [copy(1 + j, me, (*chip, c), src=x_ref) for j, chip in enumerate(chips)]
        for cp in first:
            cp.start()
        passed = [copy(4 + j, (*chip, c), sibling) for j, chip in enumerate(chips)]
        for j, chip in enumerate(chips):
            copy(1 + j, (*chip, c), me).wait_recv()
            passed[j].start()
        copy(0, sibling, me).wait_recv()
        for j, chip in enumerate(chips):
            copy(4 + j, (*chip, 1 - c), me).wait_recv()
        for cp in first + passed:
            cp.wait_send()
        mine.wait()

    return pl.pallas_call(
        body, name=name,
        out_shape=jax.ShapeDtypeStruct((N_DEV, rows, lanes), block.dtype),
        in_specs=[pl.BlockSpec(memory_space=pl.ANY)],
        out_specs=pl.BlockSpec(memory_space=pl.ANY),
        scratch_shapes=[pltpu.SemaphoreType.DMA((7,)), pltpu.SemaphoreType.DMA((7,)), pltpu.SemaphoreType.DMA],
    )(block)


def _all_to_all(blocks, name):
    _, rows, lanes = blocks.shape

    def body(g_ref, out_ref, send_sems, recv_sems, local_sem):
        x, y, c = _coords()
        me = 4 * x + 2 * y + c
        mine = pltpu.make_async_copy(g_ref.at[me], out_ref.at[me], local_sem)
        mine.start()
        copies = []
        for k in range(1, N_DEV):
            bx, by, bc = (k >> 2) & 1, (k >> 1) & 1, k & 1
            px, py, pc = (1 - x if bx else x), (1 - y if by else y), (1 - c if bc else c)
            peer = 4 * px + 2 * py + pc
            copies.append(pltpu.make_async_remote_copy(
                src_ref=g_ref.at[peer], dst_ref=out_ref.at[me],
                send_sem=send_sems.at[k - 1], recv_sem=recv_sems.at[k - 1],
                device_id=(px, py, pc), device_id_type=pl.DeviceIdType.MESH))
        for cp in copies:
            cp.start()
        for cp in copies:
            cp.wait_recv()
        for cp in copies:
            cp.wait_send()
        mine.wait()

    return pl.pallas_call(
        body, name=name,
        out_shape=jax.ShapeDtypeStruct(blocks.shape, blocks.dtype),
        in_specs=[pl.BlockSpec(memory_space=pl.ANY)],
        out_specs=pl.BlockSpec(memory_space=pl.ANY),
        scratch_shapes=[pltpu.SemaphoreType.DMA((7,)), pltpu.SemaphoreType.DMA((7,)), pltpu.SemaphoreType.DMA],
    )(blocks)


def _sum_adamw(g8, w, m, v, name):
    rows = w.shape[0]
    tr = _tile(rows, (1024, 512, 256, 128, 64, 32, 16, 8))

    def body(g_ref, w_ref, m_ref, v_ref, go_ref, d_ref, mo_ref, vo_ref):
        g = g_ref[0]
        for s in range(1, N_DEV):
            g = g + g_ref[s]
        m_new = ADAM_B1 * m_ref[...] + (1.0 - ADAM_B1) * g
        v_new = ADAM_B2 * v_ref[...] + (1.0 - ADAM_B2) * (g * g)
        m_hat = m_new / (1.0 - ADAM_B1 ** ADAM_STEP)
        v_hat = v_new / (1.0 - ADAM_B2 ** ADAM_STEP)
        go_ref[...] = g
        d_ref[...] = -ADAM_LR * (m_hat / (jnp.sqrt(v_hat) + ADAM_EPS) + ADAM_WD * w_ref[...])
        mo_ref[...] = m_new
        vo_ref[...] = v_new

    spec = pl.BlockSpec((tr, LANES), lambda i: (i, 0))
    shape = jax.ShapeDtypeStruct((rows, LANES), F32)
    return pl.pallas_call(
        body, name=name,
        grid=(rows // tr,),
        in_specs=[pl.BlockSpec((N_DEV, tr, LANES), lambda i: (0, i, 0)), spec, spec, spec],
        out_specs=[spec, spec, spec, spec],
        out_shape=[shape, shape, shape, shape],
        compiler_params=_params("parallel"),
    )(g8, w, m, v)


BIG = (("w_mod", 2), ("w_in", 2), ("w_branch", 3), ("w_out", 1), ("w_mlp1", 2), ("w_mlp2", 1))
SMALL = ("c_ctx", "b_mod", "norm1_w", "norm2_w", "attn_q_norm", "attn_k_norm", "ssd_conv_b", "ssd_dt_bias",
         "ssd_a_log", "ssd_d", "ssd_norm_w", "ret_log_decay", "ret_gn_w", "final_norm_w")
CONV_AXIS = 2
ORDER = ("c_ctx", "w_mod", "b_mod", "norm1_w", "norm2_w", "w_in", "attn_q_norm", "attn_k_norm", "ssd_conv_w",
         "ssd_conv_b", "ssd_dt_bias", "ssd_a_log", "ssd_d", "ssd_norm_w", "ret_log_decay", "ret_gn_w", "w_branch",
         "w_out", "w_mlp1", "w_mlp2", "final_norm_w")


def _pack(arrays, row_multiple, dtype=F32):
    flat = jnp.concatenate([a.astype(dtype) for a in arrays], axis=-1)
    n = flat.shape[-1]
    per = LANES * row_multiple
    padded = -(-n // per) * per
    flat = jnp.pad(flat, [(0, 0)] * (flat.ndim - 1) + [(0, padded - n)])
    return flat.reshape(flat.shape[:-1] + (padded // LANES, LANES))


def _unpack(slab, shapes):
    flat = slab.reshape(slab.shape[:-2] + (-1,))
    out, off = [], 0
    for shp in shapes:
        size = math.prod(shp)
        out.append(flat[..., off:off + size].reshape(flat.shape[:-1] + tuple(shp)))
        off += size
    return out


def _to_shards(full, axis):
    shp = full.shape
    split = full.reshape(shp[:axis] + (N_DEV, shp[axis] // N_DEV) + shp[axis + 1:])
    return jnp.moveaxis(split, axis, 0).reshape(N_DEV, -1)


def _from_shards(shards, axis):
    moved = jnp.moveaxis(shards, 0, axis)
    shp = moved.shape
    return moved.reshape(shp[:axis] + (shp[axis] * shp[axis + 1],) + shp[axis + 2:])


def kernel(x, c, ctx, c_ctx, w_mod, b_mod, norm1_w, norm2_w, w_in, attn_q_norm, attn_k_norm, ssd_conv_w, ssd_conv_b, ssd_dt_bias, ssd_a_log, ssd_d, ssd_norm_w, ret_log_decay, ret_gn_w, w_branch, w_out, w_mlp1, w_mlp2, final_norm_w, loss_target, m_c_ctx, m_w_mod, m_b_mod, m_norm1_w, m_norm2_w, m_w_in, m_attn_q_norm, m_attn_k_norm, m_ssd_conv_w, m_ssd_conv_b, m_ssd_dt_bias, m_ssd_a_log, m_ssd_d, m_ssd_norm_w, m_ret_log_decay, m_ret_gn_w, m_w_branch, m_w_out, m_w_mlp1, m_w_mlp2, m_final_norm_w, v_c_ctx, v_w_mod, v_b_mod, v_norm1_w, v_norm2_w, v_w_in, v_attn_q_norm, v_attn_k_norm, v_ssd_conv_w, v_ssd_conv_b, v_ssd_dt_bias, v_ssd_a_log, v_ssd_d, v_ssd_norm_w, v_ret_log_decay, v_ret_gn_w, v_w_branch, v_w_out, v_w_mlp1, v_w_mlp2, v_final_norm_w):
    args = dict(locals())
    weights = {n: args[n] for n in ORDER}
    mom1 = {n: args["m_" + n] for n in ORDER}
    mom2 = {n: args["v_" + n] for n in ORDER}
    me = 4 * lax.axis_index("x") + 2 * lax.axis_index("y") + lax.axis_index("c")

    big_names = [n for n, _ in BIG]
    shard_shapes = [weights[n].shape for n in big_names]
    gathered = _all_gather(_pack([weights[n].reshape(-1) for n in big_names], 16, jnp.bfloat16), "gather_weights")
    full = {}
    for (name, axis), shards in zip(BIG, _unpack(gathered, shard_shapes)):
        full[name] = _from_shards(shards, axis).astype(F32)
    conv_shape = ssd_conv_w.shape
    conv_all = _all_gather(_pack([ssd_conv_w.reshape(-1)], 8), "gather_conv")
    full["ssd_conv_w"] = _from_shards(_unpack(conv_all, [conv_shape])[0], CONV_AXIS)
    for n in SMALL:
        full[n] = weights[n]

    loss, (gw, gx) = jax.value_and_grad(_local_loss, argnums=(0, 1))(full, x[0], c[0], ctx[0], loss_target[0])
    loss = lax.psum(loss, MESH_AXES)

    g_send = _pack([_to_shards(gw[n], axis) for n, axis in BIG], 1024)
    g_recv = _all_to_all(g_send, "scatter_grads")
    slabs = [_pack([d[n].reshape(-1) for n in big_names], 1024) for d in (weights, mom1, mom2)]
    big_out = [_unpack(s, shard_shapes) for s in _sum_adamw(g_recv, *slabs, "adamw_big")]

    conv_full_shape = full["ssd_conv_w"].shape
    small_shapes = [weights[n].shape for n in SMALL]
    partial = _pack([gw[n].reshape(-1) for n in SMALL] + [gw["ssd_conv_w"].reshape(-1)], 8)
    parts = _unpack(_all_gather(partial, "gather_small_grads"), small_shapes + [conv_full_shape])
    conv_part = lax.dynamic_slice_in_dim(parts[-1], me * conv_shape[CONV_AXIS], conv_shape[CONV_AXIS], CONV_AXIS + 1)
    small_names = list(SMALL) + ["ssd_conv_w"]
    g8_small = _pack([p.reshape(N_DEV, -1) for p in parts[:-1]] + [conv_part.reshape(N_DEV, -1)], 8)
    slabs = [_pack([d[n].reshape(-1) for n in small_names], 8) for d in (weights, mom1, mom2)]
    small_out = [_unpack(s, small_shapes + [conv_shape]) for s in _sum_adamw(g8_small, *slabs, "adamw_small")]

    result = {}
    for kind, big_k, small_k in zip(("grad", "delta", "new_m", "new_v"), big_out, small_out):
        for n, arr in zip(big_names, big_k):
            result[kind, n] = arr
        for n, arr in zip(small_names, small_k):
            result[kind, n] = arr
    outs = [loss, gx[None]]
    for kind in ("grad", "delta", "new_m", "new_v"):
        outs += [result[kind, n] for n in ORDER]
    return tuple(outs)
```

```python
import functools
import math

import jax
import jax.numpy as jnp
from jax import lax
from jax.experimental import pallas as pl
from jax.experimental.pallas import tpu as pltpu

F32 = jnp.float32
MXU_DTYPE = jnp.bfloat16
VMEM_LIMIT_BYTES = 48 * 1024 * 1024
LANES = 128
N_DEV = 8
MESH_AXES = ("x", "y", "c")

D_MODEL = 1024
GRID_W = 64
NORM_EPS = 1e-6
ROPE_THETA = 10000.0
ATTN_HEADS, ATTN_KV_HEADS, ATTN_HEAD_DIM = 8, 2, 64
ATTN_GROUP = ATTN_HEADS // ATTN_KV_HEADS
SSD_HEADS, SSD_HEAD_DIM, SSD_GROUPS, SSD_STATE = 8, 64, 2, 128
SSD_D_INNER = SSD_HEADS * SSD_HEAD_DIM
RET_HEADS, RET_DK, RET_DV = 4, 128, 128
SCAN_CHUNK = 128
N_BRANCH = 3
DEPTH = 2

IN_SPLITS = (512, 128, 128, 512, 1024, 16, 512, 512, 512, 512, 3072)
IN_DIM = sum(IN_SPLITS)
DT_COLS = 16
DT_PAD = LANES - DT_COLS
TAIL_PAD = 128
IN_DIM_PADDED = IN_DIM + DT_PAD + TAIL_PAD
DT_END = sum(IN_SPLITS[:6])

ADAM_LR, ADAM_B1, ADAM_B2, ADAM_EPS, ADAM_WD, ADAM_STEP = 0.001, 0.9, 0.999, 1e-08, 0.01, 10


def _tile(dim, prefs):
    for p in prefs:
        if dim % p == 0:
            return p
    return dim


def _params(*sem):
    return pltpu.CompilerParams(dimension_semantics=sem, vmem_limit_bytes=VMEM_LIMIT_BYTES)


def _mm(a, b, *, ta=False, tb=False, name):
    if ta:
        kdim, m = a.shape
    else:
        m, kdim = a.shape
    if tb:
        n, kdim_b = b.shape
    else:
        kdim_b, n = b.shape
    assert kdim == kdim_b, (a.shape, b.shape, ta, tb)
    tm = _tile(m, (768, 512, 256, 128))
    tn = _tile(n, (512, 384, 256, 128))
    tk = _tile(kdim, (1024, 768, 512, 256, 128))
    dims = (((0 if ta else 1,), (1 if tb else 0,)), ((), ()))

    def body(a_ref, b_ref, o_ref):
        part = lax.dot_general(a_ref[...].astype(MXU_DTYPE), b_ref[...].astype(MXU_DTYPE), dims,
                               preferred_element_type=F32)

        @pl.when(pl.program_id(2) == 0)
        def _():
            o_ref[...] = part

        @pl.when(pl.program_id(2) > 0)
        def _():
            o_ref[...] += part

    a_spec = pl.BlockSpec((tk, tm), lambda i, j, k: (k, i)) if ta else pl.BlockSpec((tm, tk), lambda i, j, k: (i, k))
    b_spec = pl.BlockSpec((tn, tk), lambda i, j, k: (j, k)) if tb else pl.BlockSpec((tk, tn), lambda i, j, k: (k, j))
    return pl.pallas_call(
        body, name=name,
        grid=(m // tm, n // tn, kdim // tk),
        in_specs=[a_spec, b_spec],
        out_specs=pl.BlockSpec((tm, tn), lambda i, j, k: (i, j)),
        out_shape=jax.ShapeDtypeStruct((m, n), F32),
        compiler_params=_params("parallel", "parallel", "arbitrary"),
    )(a, b)


@jax.custom_vjp
def matmul(a, b):
    return _mm(a, b, name="mm_fwd")


def _matmul_fwd(a, b):
    return _mm(a, b, name="mm_fwd"), (a, b)


def _matmul_bwd(res, g):
    a, b = res
    return _mm(g, b, tb=True, name="mm_dx"), _mm(a, g, ta=True, name="mm_dw")


matmul.defvjp(_matmul_fwd, _matmul_bwd)


def _bf(x):
    return x.astype(MXU_DTYPE)


def _dot(a, b):
    return jnp.dot(_bf(a), _bf(b), preferred_element_type=F32)


def _dot_nt(a, b):
    return lax.dot_general(_bf(a), _bf(b), (((1,), (1,)), ((), ())), preferred_element_type=F32)


def _dot_tn(a, b):
    return lax.dot_general(_bf(a), _bf(b), (((0,), (0,)), ((), ())), preferred_element_type=F32)


NEG_BIG = -1e30


def _attn_tiles(t, ctx_len):
    tq = _tile(ctx_len, (256, 128))
    assert t % tq == 0 and ctx_len % tq == 0
    tk = _tile(t, (768, 512, 256, 128))
    return tq, tk


def _head_scores(q_ref, k_bf, g, ki, tk, ctx_len, masked):
    q = (q_ref[0, g] * (ATTN_HEAD_DIM ** -0.5)).astype(MXU_DTYPE)
    s = _dot_nt(q, k_bf)
    if masked:
        col = ki * tk + lax.broadcasted_iota(jnp.int32, s.shape, 1)
        s = jnp.where(col < ctx_len, s, NEG_BIG)
    return q, s


def _attn_cases(qi, ki, tq, tk, ctx_len, compute):
    ctx_q = (qi + 1) * tq <= ctx_len

    @pl.when(jnp.logical_not(ctx_q))
    def _():
        compute(False)

    @pl.when(jnp.logical_and(ctx_q, ki * tk < ctx_len))
    def _():
        compute(True)


def _attn_fwd_call(q, k, v, ctx_len):
    kvh, grp, t, hd = q.shape
    tq, tk = _attn_tiles(t, ctx_len)
    nkb = t // tk

    def body(q_ref, k_ref, v_ref, o_ref, lse_ref, m_sc, l_sc, acc_sc):
        qi, ki = pl.program_id(1), pl.program_id(2)

        @pl.when(ki == 0)
        def _():
            m_sc[...] = jnp.full(m_sc.shape, NEG_BIG, F32)
            l_sc[...] = jnp.zeros(l_sc.shape, F32)
            acc_sc[...] = jnp.zeros(acc_sc.shape, F32)

        def compute(masked):
            k_bf, v_bf = _bf(k_ref[0]), _bf(v_ref[0])
            for g in range(grp):
                _, s = _head_scores(q_ref, k_bf, g, ki, tk, ctx_len, masked)
                m_prev = m_sc[g]
                m_new = jnp.maximum(m_prev, jnp.max(s, axis=1, keepdims=True))
                alpha = jnp.exp(m_prev - m_new)
                p = jnp.exp(s - m_new)
                l_sc[g] = alpha * l_sc[g] + jnp.sum(p, axis=1, keepdims=True)
                acc_sc[g] = alpha * acc_sc[g] + _dot(p, v_bf)
                m_sc[g] = m_new

        _attn_cases(qi, ki, tq, tk, ctx_len, compute)

        @pl.when(ki == nkb - 1)
        def _():
            o_ref[0] = acc_sc[...] / l_sc[...]
            lse_ref[0] = m_sc[...] + jnp.log(l_sc[...])

    return pl.pallas_call(
        body, name="attn_fwd",
        grid=(kvh, t // tq, nkb),
        in_specs=[pl.BlockSpec((1, grp, tq, hd), lambda h, i, j: (h, 0, i, 0)),
                  pl.BlockSpec((1, tk, hd), lambda h, i, j: (h, j, 0)),
                  pl.BlockSpec((1, tk, hd), lambda h, i, j: (h, j, 0))],
        out_specs=[pl.BlockSpec((1, grp, tq, hd), lambda h, i, j: (h, 0, i, 0)),
                   pl.BlockSpec((1, grp, tq, 1), lambda h, i, j: (h, 0, i, 0))],
        out_shape=[jax.ShapeDtypeStruct(q.shape, F32), jax.ShapeDtypeStruct((kvh, grp, t, 1), F32)],
        scratch_shapes=[pltpu.VMEM((grp, tq, 1), F32), pltpu.VMEM((grp, tq, 1), F32), pltpu.VMEM((grp, tq, hd), F32)],
        compiler_params=_params("parallel", "parallel", "arbitrary"),
    )(q, k, v)


def _head_probs(q_ref, k_bf, v_bf, o_ref, do_ref, lse_ref, g, ki, tk, ctx_len, masked):
    q, s = _head_scores(q_ref, k_bf, g, ki, tk, ctx_len, masked)
    do = do_ref[0, g]
    delta = jnp.sum(do * o_ref[0, g], axis=1, keepdims=True)
    p = jnp.exp(s - lse_ref[0, g])
    do = _bf(do)
    ds = p * (_dot_nt(do, v_bf) - delta)
    return q, do, p, ds


def _attn_bwd_call(q, k, v, o, lse, do, ctx_len):
    kvh, grp, t, hd = q.shape
    tq, tk = _attn_tiles(t, ctx_len)
    nqb, nkb = t // tq, t // tk

    def body(q_ref, k_ref, v_ref, o_ref, lse_ref, do_ref, dq_hbm, dk_ref, dv_ref, dq_sc, dk_sc, dv_sc, dq_out,
             dq_sem):
        hi, ki, qi = pl.program_id(0), pl.program_id(1), pl.program_id(2)
        rows = pl.ds(pl.multiple_of(qi * tq, tq), tq)

        @pl.when(ki == 0)
        def _():
            dq_sc[:, rows, :] = jnp.zeros((grp, tq, hd), F32)

        @pl.when(qi == 0)
        def _():
            dk_sc[...] = jnp.zeros(dk_sc.shape, F32)
            dv_sc[...] = jnp.zeros(dv_sc.shape, F32)

        def compute(masked):
            k_bf, v_bf = _bf(k_ref[0]), _bf(v_ref[0])
            dk_part = jnp.zeros(dk_sc.shape, F32)
            dv_part = jnp.zeros(dv_sc.shape, F32)
            for g in range(grp):
                qs, dob, p, ds = _head_probs(q_ref, k_bf, v_bf, o_ref, do_ref, lse_ref, g, ki, tk, ctx_len, masked)
                dv_part = dv_part + _dot_tn(p, dob)
                dk_part = dk_part + _dot_tn(ds, qs)
                dq_sc[g, rows, :] += _dot(ds, k_bf)
            dk_sc[...] += dk_part
            dv_sc[...] += dv_part

        _attn_cases(qi, ki, tq, tk, ctx_len, compute)

        @pl.when(ki == nkb - 1)
        def _():
            dq_out[...] = dq_sc[:, rows, :] * (hd ** -0.5)
            done = pltpu.make_async_copy(dq_out, dq_hbm.at[hi, :, rows, :], dq_sem)
            done.start()
            done.wait()

        @pl.when(qi == nqb - 1)
        def _():
            dk_ref[0] = dk_sc[...]
            dv_ref[0] = dv_sc[...]

    qspec = pl.BlockSpec((1, grp, tq, hd), lambda h, j, i: (h, 0, i, 0))
    kspec = pl.BlockSpec((1, tk, hd), lambda h, j, i: (h, j, 0))
    return pl.pallas_call(
        body, name="attn_bwd",
        grid=(kvh, t // tk, nqb),
        in_specs=[qspec, kspec, kspec, qspec, pl.BlockSpec((1, grp, tq, 1), lambda h, j, i: (h, 0, i, 0)), qspec],
        out_specs=[pl.BlockSpec(memory_space=pl.ANY), kspec, kspec],
        out_shape=[jax.ShapeDtypeStruct(q.shape, F32), jax.ShapeDtypeStruct(k.shape, F32),
                   jax.ShapeDtypeStruct(v.shape, F32)],
        scratch_shapes=[pltpu.VMEM((grp, t, hd), F32), pltpu.VMEM((tk, hd), F32), pltpu.VMEM((tk, hd), F32),
                        pltpu.VMEM((grp, tq, hd), F32), pltpu.SemaphoreType.DMA],
        compiler_params=_params("arbitrary", "arbitrary", "arbitrary"),
    )(q, k, v, o, lse, do)


@functools.partial(jax.custom_vjp, nondiff_argnums=(3,))
def attention(q, k, v, ctx_len):
    return _attn_fwd_call(q, k, v, ctx_len)[0]


def _attention_fwd(q, k, v, ctx_len):
    o, lse = _attn_fwd_call(q, k, v, ctx_len)
    return o, (q, k, v, o, lse)


def _attention_bwd(ctx_len, res, do):
    q, k, v, o, lse = res
    return tuple(_attn_bwd_call(q, k, v, o, lse, do, ctx_len))


attention.defvjp(_attention_fwd, _attention_bwd)


def _chunk_order(step, n_chunks, n_ctx_chunks, reverse):
    if not reverse:
        return step
    return jnp.where(step < n_ctx_chunks, n_ctx_chunks - 1 - step, n_chunks + n_ctx_chunks - 1 - step)


def _scan_masks(chunk, reverse):
    row = lax.broadcasted_iota(jnp.int32, (chunk, chunk), 0)
    col = lax.broadcasted_iota(jnp.int32, (chunk, chunk), 1)
    vis = (col >= row) if reverse else (col <= row)
    vis_t = (row >= col) if reverse else (row <= col)
    return vis, vis.astype(F32), vis_t.astype(F32)


def _cum_decay(a_col, a_row, vis_f):
    hi = lax.Precision.HIGHEST
    cum_col = jnp.dot(vis_f, a_col, precision=hi, preferred_element_type=F32)
    cum_row = lax.dot_general(a_row, vis_f, (((1,), (1,)), ((), ())), precision=hi, preferred_element_type=F32)
    total = jnp.sum(a_col, axis=0, keepdims=True)
    return cum_col, cum_row, total


def _scan_specs(chunk, n_chunks, n_ctx_chunks, reverse, backward, widths):
    def order(i):
        step = (n_chunks - 1 - i) if backward else i
        return _chunk_order(step, n_chunks, n_ctx_chunks, reverse)

    return [pl.BlockSpec((chunk, w), lambda i: (order(i), 0)) for w in widths], order


def _scan_fwd_call(q, k, v, a_col, a_row, *, groups, per_group, dk, dv, ctx_len, reverse):
    t = q.shape[0]
    chunk = SCAN_CHUNK
    n_chunks, n_ctx = t // chunk, ctx_len // chunk
    heads = groups * per_group
    (q_spec, k_spec, v_spec, acol_spec), order = _scan_specs(
        chunk, n_chunks, n_ctx, reverse, False, (groups * dk, groups * dk, heads * dv, LANES))

    def body(q_ref, k_ref, v_ref, acol_ref, arow_ref, y_ref, st_ref, s_sc):
        @pl.when(pl.program_id(0) == 0)
        def _():
            s_sc[...] = jnp.zeros(s_sc.shape, F32)

        st_ref[0] = s_sc[...]
        vis, vis_f, _ = _scan_masks(chunk, reverse)
        cum_col, cum_row, total = _cum_decay(acol_ref[...], arow_ref[...], vis_f)
        for g in range(groups):
            qg = q_ref[:, g * dk:(g + 1) * dk]
            kg = k_ref[:, g * dk:(g + 1) * dk]
            qk = _dot_nt(qg, kg)
            for r in range(per_group):
                h = g * per_group + r
                ccol = cum_col[:, h:h + 1]
                decay = jnp.exp(jnp.where(vis, ccol - cum_row[h:h + 1, :], NEG_BIG))
                vh = v_ref[:, h * dv:(h + 1) * dv]
                s_in = s_sc[h]
                y = _dot(qk * decay, vh) + jnp.exp(ccol) * _dot(qg, s_in)
                y_ref[:, h * dv:(h + 1) * dv] = y
                tot = total[:, h:h + 1]
                s_sc[h] = jnp.exp(tot) * s_in + _dot_tn(kg * jnp.exp(tot - ccol), vh)

    return pl.pallas_call(
        body, name="scan_fwd",
        grid=(n_chunks,),
        in_specs=[q_spec, k_spec, v_spec, acol_spec, pl.BlockSpec((8, chunk), lambda i: (0, order(i)))],
        out_specs=[v_spec, pl.BlockSpec((1, heads, dk, dv), lambda i: (order(i), 0, 0, 0))],
        out_shape=[jax.ShapeDtypeStruct(v.shape, F32), jax.ShapeDtypeStruct((n_chunks, heads, dk, dv), F32)],
        scratch_shapes=[pltpu.VMEM((heads, dk, dv), F32)],
        compiler_params=_params("arbitrary"),
    )(q, k, v, a_col, a_row)


def _scan_bwd_call(q, k, v, a_col, a_row, states, dy, *, groups, per_group, dk, dv, ctx_len, reverse):
    t = q.shape[0]
    chunk = SCAN_CHUNK
    n_chunks, n_ctx = t // chunk, ctx_len // chunk
    heads = groups * per_group
    (q_spec, k_spec, v_spec, acol_spec), order = _scan_specs(
        chunk, n_chunks, n_ctx, reverse, True, (groups * dk, groups * dk, heads * dv, LANES))
    arow_spec = pl.BlockSpec((8, chunk), lambda i: (0, order(i)))
    last = 0 if reverse else chunk - 1

    def body(q_ref, k_ref, v_ref, acol_ref, arow_ref, st_ref, dy_ref, dq_ref, dk_ref, dv_ref, da_ref, dat_ref,
             ds_sc):
        @pl.when(pl.program_id(0) == 0)
        def _():
            ds_sc[...] = jnp.zeros(ds_sc.shape, F32)

        vis, vis_f, vis_tf = _scan_masks(chunk, reverse)
        cum_col, cum_row, total = _cum_decay(acol_ref[...], arow_ref[...], vis_f)
        lane = lax.broadcasted_iota(jnp.int32, (chunk, LANES), 1)
        row = lax.broadcasted_iota(jnp.int32, (chunk, LANES), 0)
        sub = lax.broadcasted_iota(jnp.int32, (8, chunk), 0)
        dcum = jnp.zeros((chunk, LANES), F32)
        dcum_t = jnp.zeros((8, chunk), F32)
        for g in range(groups):
            qg = q_ref[:, g * dk:(g + 1) * dk]
            kg = k_ref[:, g * dk:(g + 1) * dk]
            qk = _dot_nt(qg, kg)
            dq_g = jnp.zeros((chunk, dk), F32)
            dk_g = jnp.zeros((chunk, dk), F32)
            for r in range(per_group):
                h = g * per_group + r
                ccol = cum_col[:, h:h + 1]
                decay = jnp.exp(jnp.where(vis, ccol - cum_row[h:h + 1, :], NEG_BIG))
                vh = v_ref[:, h * dv:(h + 1) * dv]
                dyh = dy_ref[:, h * dv:(h + 1) * dv]
                s_in = st_ref[0, h]
                ds_out = ds_sc[h]
                tot = total[:, h:h + 1]
                e_in = jnp.exp(ccol)
                e_out = jnp.exp(tot - ccol)
                e_tot = jnp.exp(tot)
                k_out = kg * e_out
                dv_ref[:, h * dv:(h + 1) * dv] = _dot_tn(qk * decay, dyh) + _dot(k_out, ds_out)
                dqk = _dot_nt(dyh, vh) * decay
                dq_in = e_in * _dot_nt(dyh, s_in)
                dk_out = e_out * _dot_nt(vh, ds_out)
                dq_h = _dot(dqk, kg) + dq_in
                dk_h = _dot_tn(dqk, qg) + dk_out
                s_out = e_tot * s_in + _dot_tn(k_out, vh)
                edge = jnp.sum(jnp.sum(s_out * ds_out, axis=1, keepdims=True), axis=0, keepdims=True)
                w_seg = dqk * qk
                dcum_h = (jnp.sum(w_seg, axis=1, keepdims=True) + jnp.sum(dq_in * qg, axis=1, keepdims=True)
                          - jnp.sum(dk_out * kg, axis=1, keepdims=True))
                dcum = jnp.where(lane == h, dcum_h + jnp.where(row == last, edge, 0.0), dcum)
                dcum_t = jnp.where(sub == h, -jnp.sum(w_seg, axis=0, keepdims=True), dcum_t)
                ds_sc[h] = e_tot * ds_out + _dot_tn(qg, e_in * dyh)
                dq_g = dq_g + dq_h
                dk_g = dk_g + dk_h
            dq_ref[:, g * dk:(g + 1) * dk] = dq_g
            dk_ref[:, g * dk:(g + 1) * dk] = dk_g
        hi = lax.Precision.HIGHEST
        da_ref[...] = jnp.dot(vis_tf, dcum, precision=hi, preferred_element_type=F32)
        dat_ref[...] = jnp.dot(dcum_t, vis_f, precision=hi, preferred_element_type=F32)

    return pl.pallas_call(
        body, name="scan_bwd",
        grid=(n_chunks,),
        in_specs=[q_spec, k_spec, v_spec, acol_spec, arow_spec,
                  pl.BlockSpec((1, heads, dk, dv), lambda i: (order(i), 0, 0, 0)), v_spec],
        out_specs=[q_spec, k_spec, v_spec, acol_spec, arow_spec],
        out_shape=[jax.ShapeDtypeStruct(q.shape, F32), jax.ShapeDtypeStruct(k.shape, F32),
                   jax.ShapeDtypeStruct(v.shape, F32), jax.ShapeDtypeStruct((t, LANES), F32),
                   jax.ShapeDtypeStruct((8, t), F32)],
        scratch_shapes=[pltpu.VMEM((heads, dk, dv), F32)],
        compiler_params=_params("arbitrary"),
    )(q, k, v, a_col, a_row, states, dy)


def _decay_layouts(a):
    t, heads = a.shape
    a_col = jnp.pad(a, ((0, 0), (0, LANES - heads)))
    a_row = jnp.pad(a.T, ((0, 8 - heads), (0, 0)))
    return a_col, a_row


@functools.partial(jax.custom_vjp, nondiff_argnums=(4,))
def linear_scan(q, k, v, a, cfg):
    a_col, a_row = _decay_layouts(a)
    return _scan_fwd_call(q, k, v, a_col, a_row, **dict(cfg))[0]


def _linear_scan_fwd(q, k, v, a, cfg):
    a_col, a_row = _decay_layouts(a)
    y, states = _scan_fwd_call(q, k, v, a_col, a_row, **dict(cfg))
    return y, (q, k, v, a, states)


def _linear_scan_bwd(cfg, res, dy):
    q, k, v, a, states = res
    a_col, a_row = _decay_layouts(a)
    dq, dk, dv, da, da_t = _scan_bwd_call(q, k, v, a_col, a_row, states, dy, **dict(cfg))
    heads = a.shape[1]
    return dq, dk, dv, da[:, :heads] + da_t[:heads].T


linear_scan.defvjp(_linear_scan_fwd, _linear_scan_bwd)


def _scan_cfg(groups, per_group, dk, dv, ctx_len, reverse):
    return (("groups", groups), ("per_group", per_group), ("dk", dk), ("dv", dv), ("ctx_len", ctx_len),
            ("reverse", reverse))


def _rms_norm(x, w):
    return x * lax.rsqrt(jnp.mean(x * x, axis=-1, keepdims=True) + NORM_EPS) * w


def _rope(x, cos, sin):
    half = x.shape[-1] // 2
    x1, x2 = x[..., :half], x[..., half:]
    cs, sn = cos[:, None, :], sin[:, None, :]
    return jnp.concatenate([x1 * cs - x2 * sn, x1 * sn + x2 * cs], axis=-1)


def _axial_tables(n_lat, n_ctx):
    freqs = ATTN_HEAD_DIM // 4
    rows = n_lat // GRID_W
    row = jnp.repeat(jnp.arange(rows, dtype=F32), GRID_W)
    col = jnp.tile(jnp.arange(GRID_W, dtype=F32), rows)
    inv = ROPE_THETA ** (-jnp.arange(freqs, dtype=F32) / freqs)
    ang = jnp.concatenate([row[:, None] * inv, col[:, None] * inv], axis=-1)
    cos = jnp.concatenate([jnp.ones((n_ctx, 2 * freqs), F32), jnp.cos(ang)], axis=0)
    sin = jnp.concatenate([jnp.zeros((n_ctx, 2 * freqs), F32), jnp.sin(ang)], axis=0)
    return cos, sin


def _seq_tables(t):
    pos = jnp.arange(t, dtype=F32)
    inv = ROPE_THETA ** (-jnp.linspace(0.0, 1.0, RET_DK // 2, dtype=F32))
    ang = pos[:, None] * inv
    return jnp.cos(ang), jnp.sin(ang)


def _pad_w_in(w_in):
    d = w_in.shape[0]
    return jnp.concatenate([w_in[:, :DT_END], jnp.zeros((d, DT_PAD), w_in.dtype), w_in[:, DT_END:],
                            jnp.zeros((d, TAIL_PAD), w_in.dtype)], axis=1)


def _split_proj(p):
    widths = list(IN_SPLITS)
    widths[5] = LANES
    out, off = [], 0
    for w in widths:
        out.append(p[:, off:off + w])
        off += w
    out[5] = out[5][:, :DT_COLS]
    return out


def _mixer(u, w, layer, n_ctx, tables):
    t = u.shape[0]
    attn_rope, ret_rope, seg_first, seg_last = tables
    aq, ak, av, z, xbc_raw, dt_raw, rq, rk, rv, rg, gate_logits = _split_proj(matmul(u, _pad_w_in(w["w_in"][layer])))

    q = _rope(_rms_norm(aq.reshape(t, ATTN_HEADS, ATTN_HEAD_DIM), w["attn_q_norm"][layer]), *attn_rope)
    k = _rope(_rms_norm(ak.reshape(t, ATTN_KV_HEADS, ATTN_HEAD_DIM), w["attn_k_norm"][layer]), *attn_rope)
    q4 = q.reshape(t, ATTN_KV_HEADS, ATTN_GROUP, ATTN_HEAD_DIM).transpose(1, 2, 0, 3)
    k3 = k.transpose(1, 0, 2)
    v3 = av.reshape(t, ATTN_KV_HEADS, ATTN_HEAD_DIM).transpose(1, 0, 2)
    o4 = attention(q4, k3, v3, n_ctx)
    br_attn = o4.transpose(2, 0, 1, 3).reshape(t, ATTN_HEADS * ATTN_HEAD_DIM)

    cw, cb = w["ssd_conv_w"][layer], w["ssd_conv_b"][layer]
    zero_row = jnp.zeros((1, xbc_raw.shape[1]), F32)
    prev = jnp.concatenate([zero_row, xbc_raw[:-1]], axis=0) * (1.0 - seg_first)
    nxt = jnp.concatenate([xbc_raw[1:], zero_row], axis=0) * (1.0 - seg_last)
    xbc = jax.nn.silu(prev * cw[0] + xbc_raw * cw[1] + nxt * cw[2] + cb)
    gn = SSD_GROUPS * SSD_STATE
    xs = xbc[:, :SSD_D_INNER]
    bm = xbc[:, SSD_D_INNER:SSD_D_INNER + gn]
    cm = xbc[:, SSD_D_INNER + gn:]
    dt = jax.nn.softplus(dt_raw.reshape(t, 2, SSD_HEADS) + w["ssd_dt_bias"][layer])
    a_neg = -jnp.exp(w["ssd_a_log"][layer])
    xs_h = xs.reshape(t, SSD_HEADS, SSD_HEAD_DIM)
    y_ssd = jnp.zeros((t, SSD_D_INNER), F32)
    for d, reverse in ((0, False), (1, True)):
        dtd = dt[:, d]
        cfg = _scan_cfg(SSD_GROUPS, SSD_HEADS // SSD_GROUPS, SSD_STATE, SSD_HEAD_DIM, n_ctx, reverse)
        y_ssd = y_ssd + linear_scan(cm, bm, (xs_h * dtd[:, :, None]).reshape(t, SSD_D_INNER), dtd * a_neg[d], cfg)
    y_ssd = y_ssd.reshape(t, SSD_HEADS, SSD_HEAD_DIM) + w["ssd_d"][layer][:, None] * xs_h
    br_ssd = _rms_norm(y_ssd.reshape(t, SSD_D_INNER) * jax.nn.silu(z), w["ssd_norm_w"][layer])

    lg = -jnp.exp(w["ret_log_decay"][layer])
    rq_r = _rope(rq.reshape(t, RET_HEADS, RET_DK), *ret_rope).reshape(t, RET_HEADS * RET_DK)
    rk_r = (_rope(rk.reshape(t, RET_HEADS, RET_DK), *ret_rope) * (RET_DK ** -0.5)).reshape(t, RET_HEADS * RET_DK)
    y_ret = jnp.zeros((t, RET_HEADS * RET_DV), F32)
    for d, reverse in ((0, False), (1, True)):
        cfg = _scan_cfg(RET_HEADS, 1, RET_DK, RET_DV, n_ctx, reverse)
        y_ret = y_ret + linear_scan(rq_r, rk_r, rv, jnp.broadcast_to(lg[d][None, :], (t, RET_HEADS)), cfg)
    y4 = y_ret.reshape(t, RET_HEADS, RET_DV)
    yc = y4 - jnp.mean(y4, axis=-1, keepdims=True)
    yn = yc * lax.rsqrt(jnp.mean(yc * yc, axis=-1, keepdims=True) + NORM_EPS)
    br_ret = yn.reshape(t, RET_HEADS * RET_DV) * w["ret_gn_w"][layer] * jax.nn.silu(rg)

    gates = jax.nn.sigmoid(gate_logits.reshape(t, N_BRANCH, D_MODEL))
    merged = (gates[:, 0] * matmul(br_attn, w["w_branch"][layer, 0])
              + gates[:, 1] * matmul(br_ssd, w["w_branch"][layer, 1])
              + gates[:, 2] * matmul(br_ret, w["w_branch"][layer, 2]))
    return matmul(merged, w["w_out"][layer])


def _local_loss(w, x, c, ctx, target):
    n, m = x.shape[0], ctx.shape[0]
    t = n + m
    is_lat = (jnp.arange(t) >= m)[:, None]
    pos = jnp.arange(t)[:, None]
    seg_first = ((pos == 0) | (pos == m)).astype(F32)
    seg_last = ((pos == m - 1) | (pos == t - 1)).astype(F32)
    tables = (_axial_tables(n, m), _seq_tables(t), seg_first, seg_last)
    h = jnp.concatenate([ctx, x], axis=0)
    cond = jax.nn.silu(jnp.stack([c, w["c_ctx"]], axis=0))
    cond8 = jnp.concatenate([cond, jnp.zeros((6, D_MODEL), F32)], axis=0)
    for layer in range(DEPTH):
        mod = (matmul(cond8, w["w_mod"][layer])[:2] + w["b_mod"][layer]).reshape(2, 6, D_MODEL)

        def sel(j, mod=mod):
            return jnp.where(is_lat, mod[0, j][None, :], mod[1, j][None, :])

        u = _rms_norm(h, w["norm1_w"][layer]) * (1 + sel(1)) + sel(0)
        h = h + sel(2) * _mixer(u, w, layer, m, tables)
        v = _rms_norm(h, w["norm2_w"][layer]) * (1 + sel(4)) + sel(3)
        hid = jax.nn.relu(matmul(v, w["w_mlp1"][layer]))
        h = h + sel(5) * matmul(hid * hid, w["w_mlp2"][layer])
    y = _rms_norm(h[m:], w["final_norm_w"])
    return 0.5 * jnp.sum(jnp.mean(jnp.square(y - target), axis=-1))


def _coords():
    return lax.axis_index("x"), lax.axis_index("y"), lax.axis_index("c")


def _all_gather(block, name):
    rows, lanes = block.shape

    def body(x_ref, out_ref, send_sems, recv_sems, local_sem):
        x, y, c = _coords()
        me, sibling = (x, y, c), (x, y, 1 - c)
        chips = [(1 - x, y), (x, 1 - y), (1 - x, 1 - y)]

        def slot(px, py, pc):
            return out_ref.at[4 * px + 2 * py + pc]

        def copy(k, blk, to, src=None):
            return pltpu.make_async_remote_copy(
                src_ref=slot(*blk) if src is None else src, dst_ref=slot(*blk),
                send_sem=send_sems.at[k], recv_sem=recv_sems.at[k],
                device_id=to, device_id_type=pl.DeviceIdType.MESH)

        mine = pltpu.make_async_copy(x_ref, slot(*me), local_sem)
        mine.start()
        first = [copy(0, me, sibling, src=x_ref)]
        first += [copy(1 + j, me, (*chip, c), src=x_ref) for j, chip in enumerate(chips)]
        for cp in first:
            cp.start()
        passed = [copy(4 + j, (*chip, c), sibling) for j, chip in enumerate(chips)]
        for j, chip in enumerate(chips):
            copy(1 + j, (*chip, c), me).wait_recv()
            passed[j].start()
        copy(0, sibling, me).wait_recv()
        for j, chip in enumerate(chips):
            copy(4 + j, (*chip, 1 - c), me).wait_recv()
        for cp in first + passed:
            cp.wait_send()
        mine.wait()

    return pl.pallas_call(
        body, name=name,
        out_shape=jax.ShapeDtypeStruct((N_DEV, rows, lanes), block.dtype),
        in_specs=[pl.BlockSpec(memory_space=pl.ANY)],
        out_specs=pl.BlockSpec(memory_space=pl.ANY),
        scratch_shapes=[pltpu.SemaphoreType.DMA((7,)), pltpu.SemaphoreType.DMA((7,)), pltpu.SemaphoreType.DMA],
    )(block)


def _all_to_all(blocks, name):
    _, rows, lanes = blocks.shape

    def body(g_ref, out_ref, send_sems, recv_sems, local_sem):
        x, y, c = _coords()
        me = 4 * x + 2 * y + c
        mine = pltpu.make_async_copy(g_ref.at[me], out_ref.at[me], local_sem)
        mine.start()
        copies = []
        for k in range(1, N_DEV):
            bx, by, bc = (k >> 2) & 1, (k >> 1) & 1, k & 1
            px, py, pc = (1 - x if bx else x), (1 - y if by else y), (1 - c if bc else c)
            peer = 4 * px + 2 * py + pc
            copies.append(pltpu.make_async_remote_copy(
                src_ref=g_ref.at[peer], dst_ref=out_ref.at[me],
                send_sem=send_sems.at[k - 1], recv_sem=recv_sems.at[k - 1],
                device_id=(px, py, pc), device_id_type=pl.DeviceIdType.MESH))
        for cp in copies:
            cp.start()
        for cp in copies:
            cp.wait_recv()
        for cp in copies:
            cp.wait_send()
        mine.wait()

    return pl.pallas_call(
        body, name=name,
        out_shape=jax.ShapeDtypeStruct(blocks.shape, blocks.dtype),
        in_specs=[pl.BlockSpec(memory_space=pl.ANY)],
        out_specs=pl.BlockSpec(memory_space=pl.ANY),
        scratch_shapes=[pltpu.SemaphoreType.DMA((7,)), pltpu.SemaphoreType.DMA((7,)), pltpu.SemaphoreType.DMA],
    )(blocks)


def _sum_adamw(g8, w, m, v, name):
    rows = w.shape[0]
    tr = _tile(rows, (1024, 512, 256, 128, 64, 32, 16, 8))

    def body(g_ref, w_ref, m_ref, v_ref, go_ref, d_ref, mo_ref, vo_ref):
        g = g_ref[0].astype(F32)
        for s in range(1, N_DEV):
            g = g + g_ref[s].astype(F32)
        m_new = ADAM_B1 * m_ref[...] + (1.0 - ADAM_B1) * g
        v_new = ADAM_B2 * v_ref[...] + (1.0 - ADAM_B2) * (g * g)
        m_hat = m_new / (1.0 - ADAM_B1 ** ADAM_STEP)
        v_hat = v_new / (1.0 - ADAM_B2 ** ADAM_STEP)
        go_ref[...] = g
        d_ref[...] = -ADAM_LR * (m_hat / (jnp.sqrt(v_hat) + ADAM_EPS) + ADAM_WD * w_ref[...])
        mo_ref[...] = m_new
        vo_ref[...] = v_new

    spec = pl.BlockSpec((tr, LANES), lambda i: (i, 0))
    shape = jax.ShapeDtypeStruct((rows, LANES), F32)
    return pl.pallas_call(
        body, name=name,
        grid=(rows // tr,),
        in_specs=[pl.BlockSpec((N_DEV, tr, LANES), lambda i: (0, i, 0)), spec, spec, spec],
        out_specs=[spec, spec, spec, spec],
        out_shape=[shape, shape, shape, shape],
        compiler_params=_params("parallel"),
    )(g8, w, m, v)


BIG = (("w_mod", 2), ("w_in", 2), ("w_branch", 3), ("w_out", 1), ("w_mlp1", 2), ("w_mlp2", 1))
SMALL = ("c_ctx", "b_mod", "norm1_w", "norm2_w", "attn_q_norm", "attn_k_norm", "ssd_conv_b", "ssd_dt_bias",
         "ssd_a_log", "ssd_d", "ssd_norm_w", "ret_log_decay", "ret_gn_w", "final_norm_w")
CONV_AXIS = 2
ORDER = ("c_ctx", "w_mod", "b_mod", "norm1_w", "norm2_w", "w_in", "attn_q_norm", "attn_k_norm", "ssd_conv_w",
         "ssd_conv_b", "ssd_dt_bias", "ssd_a_log", "ssd_d", "ssd_norm_w", "ret_log_decay", "ret_gn_w", "w_branch",
         "w_out", "w_mlp1", "w_mlp2", "final_norm_w")


def _pack(arrays, row_multiple, dtype=F32):
    flat = jnp.concatenate([a.astype(dtype) for a in arrays], axis=-1)
    n = flat.shape[-1]
    per = LANES * row_multiple
    padded = -(-n // per) * per
    flat = jnp.pad(flat, [(0, 0)] * (flat.ndim - 1) + [(0, padded - n)])
    return flat.reshape(flat.shape[:-1] + (padded // LANES, LANES))


def _unpack(slab, shapes):
    flat = slab.reshape(slab.shape[:-2] + (-1,))
    out, off = [], 0
    for shp in shapes:
        size = math.prod(shp)
        out.append(flat[..., off:off + size].reshape(flat.shape[:-1] + tuple(shp)))
        off += size
    return out


def _to_shards(full, axis):
    size = full.shape[axis] // N_DEV
    return jnp.stack([lax.slice_in_dim(full, d * size, (d + 1) * size, axis=axis).reshape(-1) for d in range(N_DEV)])


def _from_shards(shards, axis):
    return jnp.concatenate([shards[d] for d in range(N_DEV)], axis=axis)


def kernel(x, c, ctx, c_ctx, w_mod, b_mod, norm1_w, norm2_w, w_in, attn_q_norm, attn_k_norm, ssd_conv_w, ssd_conv_b, ssd_dt_bias, ssd_a_log, ssd_d, ssd_norm_w, ret_log_decay, ret_gn_w, w_branch, w_out, w_mlp1, w_mlp2, final_norm_w, loss_target, m_c_ctx, m_w_mod, m_b_mod, m_norm1_w, m_norm2_w, m_w_in, m_attn_q_norm, m_attn_k_norm, m_ssd_conv_w, m_ssd_conv_b, m_ssd_dt_bias, m_ssd_a_log, m_ssd_d, m_ssd_norm_w, m_ret_log_decay, m_ret_gn_w, m_w_branch, m_w_out, m_w_mlp1, m_w_mlp2, m_final_norm_w, v_c_ctx, v_w_mod, v_b_mod, v_norm1_w, v_norm2_w, v_w_in, v_attn_q_norm, v_attn_k_norm, v_ssd_conv_w, v_ssd_conv_b, v_ssd_dt_bias, v_ssd_a_log, v_ssd_d, v_ssd_norm_w, v_ret_log_decay, v_ret_gn_w, v_w_branch, v_w_out, v_w_mlp1, v_w_mlp2, v_final_norm_w):
    args = dict(locals())
    weights = {n: args[n] for n in ORDER}
    mom1 = {n: args["m_" + n] for n in ORDER}
    mom2 = {n: args["v_" + n] for n in ORDER}
    me = 4 * lax.axis_index("x") + 2 * lax.axis_index("y") + lax.axis_index("c")

    big_names = [n for n, _ in BIG]
    shard_shapes = [weights[n].shape for n in big_names]
    gathered = _all_gather(_pack([weights[n].reshape(-1) for n in big_names], 16, jnp.bfloat16), "gather_weights")
    full = {}
    for (name, axis), shards in zip(BIG, _unpack(gathered, shard_shapes)):
        full[name] = _from_shards(shards, axis).astype(F32)
    conv_shape = ssd_conv_w.shape
    conv_all = _all_gather(_pack([ssd_conv_w.reshape(-1)], 8), "gather_conv")
    full["ssd_conv_w"] = _from_shards(_unpack(conv_all, [conv_shape])[0], CONV_AXIS)
    for n in SMALL:
        full[n] = weights[n]

    loss, (gw, gx) = jax.value_and_grad(_local_loss, argnums=(0, 1))(full, x[0], c[0], ctx[0], loss_target[0])
    loss = lax.psum(loss, MESH_AXES)

    g_send = _pack([_to_shards(gw[n], axis) for n, axis in BIG], 1024, jnp.bfloat16)
    g_recv = _all_to_all(g_send, "scatter_grads")
    slabs = [_pack([d[n].reshape(-1) for n in big_names], 1024) for d in (weights, mom1, mom2)]
    big_out = [_unpack(s, shard_shapes) for s in _sum_adamw(g_recv, *slabs, "adamw_big")]

    conv_full_shape = full["ssd_conv_w"].shape
    small_shapes = [weights[n].shape for n in SMALL]
    partial = _pack([gw[n].reshape(-1) for n in SMALL] + [gw["ssd_conv_w"].reshape(-1)], 8)
    parts = _unpack(_all_gather(partial, "gather_small_grads"), small_shapes + [conv_full_shape])
    conv_part = lax.dynamic_slice_in_dim(parts[-1], me * conv_shape[CONV_AXIS], conv_shape[CONV_AXIS], CONV_AXIS + 1)
    small_names = list(SMALL) + ["ssd_conv_w"]
    g8_small = _pack([p.reshape(N_DEV, -1) for p in parts[:-1]] + [conv_part.reshape(N_DEV, -1)], 8)
    slabs = [_pack([d[n].reshape(-1) for n in small_names], 8) for d in (weights, mom1, mom2)]
    small_out = [_unpack(s, small_shapes + [conv_shape]) for s in _sum_adamw(g8_small, *slabs, "adamw_small")]

    result = {}
    for kind, big_k, small_k in zip(("grad", "delta", "new_m", "new_v"), big_out, small_out):
        for n, arr in zip(big_names, big_k):
            result[kind, n] = arr
        for n, arr in zip(small_names, small_k):
            result[kind, n] = arr
    outs = [loss, gx[None]]
    for kind in ("grad", "delta", "new_m", "new_v"):
        outs += [result[kind, n] for n in ORDER]
    return tuple(outs)
```

```python
import functools
import math

import jax
import jax.numpy as jnp
from jax import lax
from jax.experimental import pallas as pl
from jax.experimental.pallas import tpu as pltpu

F32 = jnp.float32
MXU_DTYPE = jnp.bfloat16
VMEM_LIMIT_BYTES = 48 * 1024 * 1024
LANES = 128
N_DEV = 8
MESH_AXES = ("x", "y", "c")

D_MODEL = 1024
GRID_W = 64
NORM_EPS = 1e-6
ROPE_THETA = 10000.0
ATTN_HEADS, ATTN_KV_HEADS, ATTN_HEAD_DIM = 8, 2, 64
ATTN_GROUP = ATTN_HEADS // ATTN_KV_HEADS
SSD_HEADS, SSD_HEAD_DIM, SSD_GROUPS, SSD_STATE = 8, 64, 2, 128
SSD_D_INNER = SSD_HEADS * SSD_HEAD_DIM
RET_HEADS, RET_DK, RET_DV = 4, 128, 128
SCAN_CHUNK = 128
N_BRANCH = 3
DEPTH = 2

IN_SPLITS = (512, 128, 128, 512, 1024, 16, 512, 512, 512, 512, 3072)
IN_DIM = sum(IN_SPLITS)
DT_COLS = 16
DT_PAD = LANES - DT_COLS
TAIL_PAD = 128
IN_DIM_PADDED = IN_DIM + DT_PAD + TAIL_PAD
DT_END = sum(IN_SPLITS[:6])

ADAM_LR, ADAM_B1, ADAM_B2, ADAM_EPS, ADAM_WD, ADAM_STEP = 0.001, 0.9, 0.999, 1e-08, 0.01, 10


def _tile(dim, prefs):
    for p in prefs:
        if dim % p == 0:
            return p
    return dim


def _params(*sem):
    return pltpu.CompilerParams(dimension_semantics=sem, vmem_limit_bytes=VMEM_LIMIT_BYTES)


def _mm(a, b, *, ta=False, tb=False, out_shards=False, name):
    if ta:
        kdim, m = a.shape
    else:
        m, kdim = a.shape
    b_shards = b.ndim == 3
    if b_shards:
        rows_b, cols_b = b.shape[1], N_DEV * b.shape[2]
    else:
        rows_b, cols_b = b.shape
    n, kdim_b = (rows_b, cols_b) if tb else (cols_b, rows_b)
    assert kdim == kdim_b, (a.shape, b.shape, ta, tb)
    tm = _tile(m, (1024, 768, 512, 256, 128))
    n_tile_of = n // N_DEV if (out_shards or (b_shards and not tb)) else n
    k_tile_of = kdim // N_DEV if (b_shards and tb) else kdim
    tn = _tile(n_tile_of, (1280, 1024, 768, 512, 384, 256, 128))
    tk = _tile(k_tile_of, (1024, 768, 512, 256, 128))
    dims = (((0 if ta else 1,), (1 if tb else 0,)), ((), ()))

    def body(a_ref, b_ref, o_ref):
        part = lax.dot_general(a_ref[...].astype(MXU_DTYPE), b_ref[...].astype(MXU_DTYPE), dims,
                               preferred_element_type=F32)

        @pl.when(pl.program_id(2) == 0)
        def _():
            o_ref[...] = part

        @pl.when(pl.program_id(2) > 0)
        def _():
            o_ref[...] += part

    a_spec = pl.BlockSpec((tk, tm), lambda i, j, k: (k, i)) if ta else pl.BlockSpec((tm, tk), lambda i, j, k: (i, k))
    if not b_shards:
        b_spec = pl.BlockSpec((tn, tk), lambda i, j, k: (j, k)) if tb else pl.BlockSpec((tk, tn), lambda i, j, k: (k, j))
    elif tb:
        per = b.shape[2] // tk
        b_spec = pl.BlockSpec((None, tn, tk), lambda i, j, k: (k // per, j, k % per))
    else:
        per = b.shape[2] // tn
        b_spec = pl.BlockSpec((None, tk, tn), lambda i, j, k: (j // per, k, j % per))
    if out_shards:
        per_out = n // N_DEV // tn
        out_spec = pl.BlockSpec((None, tm, tn), lambda i, j, k: (j // per_out, i, j % per_out))
        out_shape = jax.ShapeDtypeStruct((N_DEV, m, n // N_DEV), F32)
    else:
        out_spec = pl.BlockSpec((tm, tn), lambda i, j, k: (i, j))
        out_shape = jax.ShapeDtypeStruct((m, n), F32)
    return pl.pallas_call(
        body, name=name,
        grid=(m // tm, n // tn, kdim // tk),
        in_specs=[a_spec, b_spec],
        out_specs=out_spec,
        out_shape=out_shape,
        compiler_params=_params("parallel", "parallel", "arbitrary"),
    )(a, b)


@jax.custom_vjp
def matmul(a, b, b_grad):
    return _mm(a, b, name="mm_fwd")


def _matmul_fwd(a, b, b_grad):
    return _mm(a, b, name="mm_fwd"), (a, b)


def _matmul_bwd(res, g):
    a, b = res
    dw = _mm(a, g, ta=True, out_shards=b.ndim == 3, name="mm_dw")
    return _mm(g, b, tb=True, name="mm_dx"), jnp.zeros_like(b), dw


matmul.defvjp(_matmul_fwd, _matmul_bwd)


def _bf(x):
    return x.astype(MXU_DTYPE)


def _dot(a, b):
    return jnp.dot(_bf(a), _bf(b), preferred_element_type=F32)


def _dot_nt(a, b):
    return lax.dot_general(_bf(a), _bf(b), (((1,), (1,)), ((), ())), preferred_element_type=F32)


def _dot_tn(a, b):
    return lax.dot_general(_bf(a), _bf(b), (((0,), (0,)), ((), ())), preferred_element_type=F32)


NEG_BIG = -1e30


def _attn_tiles(t, ctx_len):
    tq = _tile(ctx_len, (256, 128))
    assert t % tq == 0 and ctx_len % tq == 0
    tk = _tile(t, (768, 512, 256, 128))
    return tq, tk


def _head_scores(q_ref, k_bf, g, ki, tk, ctx_len, masked):
    q = (q_ref[0, g] * (ATTN_HEAD_DIM ** -0.5)).astype(MXU_DTYPE)
    s = _dot_nt(q, k_bf)
    if masked:
        col = ki * tk + lax.broadcasted_iota(jnp.int32, s.shape, 1)
        s = jnp.where(col < ctx_len, s, NEG_BIG)
    return q, s


def _attn_cases(qi, ki, tq, tk, ctx_len, compute):
    ctx_q = (qi + 1) * tq <= ctx_len

    @pl.when(jnp.logical_not(ctx_q))
    def _():
        compute(False)

    @pl.when(jnp.logical_and(ctx_q, ki * tk < ctx_len))
    def _():
        compute(True)


def _attn_fwd_call(q, k, v, ctx_len):
    kvh, grp, t, hd = q.shape
    tq, tk = _attn_tiles(t, ctx_len)
    nkb = t // tk

    def body(q_ref, k_ref, v_ref, o_ref, lse_ref, m_sc, l_sc, acc_sc):
        qi, ki = pl.program_id(1), pl.program_id(2)

        @pl.when(ki == 0)
        def _():
            m_sc[...] = jnp.full(m_sc.shape, NEG_BIG, F32)
            l_sc[...] = jnp.zeros(l_sc.shape, F32)
            acc_sc[...] = jnp.zeros(acc_sc.shape, F32)

        def compute(masked):
            k_bf, v_bf = _bf(k_ref[0]), _bf(v_ref[0])
            for g in range(grp):
                _, s = _head_scores(q_ref, k_bf, g, ki, tk, ctx_len, masked)
                m_prev = m_sc[g]
                m_new = jnp.maximum(m_prev, jnp.max(s, axis=1, keepdims=True))
                alpha = jnp.exp(m_prev - m_new)
                p = jnp.exp(s - m_new)
                l_sc[g] = alpha * l_sc[g] + jnp.sum(p, axis=1, keepdims=True)
                acc_sc[g] = alpha * acc_sc[g] + _dot(p, v_bf)
                m_sc[g] = m_new

        _attn_cases(qi, ki, tq, tk, ctx_len, compute)

        @pl.when(ki == nkb - 1)
        def _():
            o_ref[0] = acc_sc[...] / l_sc[...]
            lse_ref[0] = m_sc[...] + jnp.log(l_sc[...])

    return pl.pallas_call(
        body, name="attn_fwd",
        grid=(kvh, t // tq, nkb),
        in_specs=[pl.BlockSpec((1, grp, tq, hd), lambda h, i, j: (h, 0, i, 0)),
                  pl.BlockSpec((1, tk, hd), lambda h, i, j: (h, j, 0)),
                  pl.BlockSpec((1, tk, hd), lambda h, i, j: (h, j, 0))],
        out_specs=[pl.BlockSpec((1, grp, tq, hd), lambda h, i, j: (h, 0, i, 0)),
                   pl.BlockSpec((1, grp, tq, 1), lambda h, i, j: (h, 0, i, 0))],
        out_shape=[jax.ShapeDtypeStruct(q.shape, F32), jax.ShapeDtypeStruct((kvh, grp, t, 1), F32)],
        scratch_shapes=[pltpu.VMEM((grp, tq, 1), F32), pltpu.VMEM((grp, tq, 1), F32), pltpu.VMEM((grp, tq, hd), F32)],
        compiler_params=_params("parallel", "parallel", "arbitrary"),
    )(q, k, v)


def _head_probs(q_ref, k_bf, v_bf, o_ref, do_ref, lse_ref, g, ki, tk, ctx_len, masked):
    q, s = _head_scores(q_ref, k_bf, g, ki, tk, ctx_len, masked)
    do = do_ref[0, g]
    delta = jnp.sum(do * o_ref[0, g], axis=1, keepdims=True)
    p = jnp.exp(s - lse_ref[0, g])
    do = _bf(do)
    ds = p * (_dot_nt(do, v_bf) - delta)
    return q, do, p, ds


def _attn_bwd_call(q, k, v, o, lse, do, ctx_len):
    kvh, grp, t, hd = q.shape
    tq, tk = _attn_tiles(t, ctx_len)
    nqb, nkb = t // tq, t // tk

    def body(q_ref, k_ref, v_ref, o_ref, lse_ref, do_ref, dq_hbm, dk_ref, dv_ref, dq_sc, dk_sc, dv_sc, dq_out,
             dq_sem):
        hi, ki, qi = pl.program_id(0), pl.program_id(1), pl.program_id(2)
        rows = pl.ds(pl.multiple_of(qi * tq, tq), tq)

        @pl.when(ki == 0)
        def _():
            dq_sc[:, rows, :] = jnp.zeros((grp, tq, hd), F32)

        @pl.when(qi == 0)
        def _():
            dk_sc[...] = jnp.zeros(dk_sc.shape, F32)
            dv_sc[...] = jnp.zeros(dv_sc.shape, F32)

        def compute(masked):
            k_bf, v_bf = _bf(k_ref[0]), _bf(v_ref[0])
            dk_part = jnp.zeros(dk_sc.shape, F32)
            dv_part = jnp.zeros(dv_sc.shape, F32)
            for g in range(grp):
                qs, dob, p, ds = _head_probs(q_ref, k_bf, v_bf, o_ref, do_ref, lse_ref, g, ki, tk, ctx_len, masked)
                dv_part = dv_part + _dot_tn(p, dob)
                dk_part = dk_part + _dot_tn(ds, qs)
                dq_sc[g, rows, :] += _dot(ds, k_bf)
            dk_sc[...] += dk_part
            dv_sc[...] += dv_part

        _attn_cases(qi, ki, tq, tk, ctx_len, compute)

        @pl.when(ki == nkb - 1)
        def _():
            dq_out[...] = dq_sc[:, rows, :] * (hd ** -0.5)
            done = pltpu.make_async_copy(dq_out, dq_hbm.at[hi, :, rows, :], dq_sem)
            done.start()
            done.wait()

        @pl.when(qi == nqb - 1)
        def _():
            dk_ref[0] = dk_sc[...]
            dv_ref[0] = dv_sc[...]

    qspec = pl.BlockSpec((1, grp, tq, hd), lambda h, j, i: (h, 0, i, 0))
    kspec = pl.BlockSpec((1, tk, hd), lambda h, j, i: (h, j, 0))
    return pl.pallas_call(
        body, name="attn_bwd",
        grid=(kvh, t // tk, nqb),
        in_specs=[qspec, kspec, kspec, qspec, pl.BlockSpec((1, grp, tq, 1), lambda h, j, i: (h, 0, i, 0)), qspec],
        out_specs=[pl.BlockSpec(memory_space=pl.ANY), kspec, kspec],
        out_shape=[jax.ShapeDtypeStruct(q.shape, F32), jax.ShapeDtypeStruct(k.shape, F32),
                   jax.ShapeDtypeStruct(v.shape, F32)],
        scratch_shapes=[pltpu.VMEM((grp, t, hd), F32), pltpu.VMEM((tk, hd), F32), pltpu.VMEM((tk, hd), F32),
                        pltpu.VMEM((grp, tq, hd), F32), pltpu.SemaphoreType.DMA],
        compiler_params=_params("arbitrary", "arbitrary", "arbitrary"),
    )(q, k, v, o, lse, do)


@functools.partial(jax.custom_vjp, nondiff_argnums=(3,))
def attention(q, k, v, ctx_len):
    return _attn_fwd_call(q, k, v, ctx_len)[0]


def _attention_fwd(q, k, v, ctx_len):
    o, lse = _attn_fwd_call(q, k, v, ctx_len)
    return o, (q, k, v, o, lse)


def _attention_bwd(ctx_len, res, do):
    q, k, v, o, lse = res
    return tuple(_attn_bwd_call(q, k, v, o, lse, do, ctx_len))


attention.defvjp(_attention_fwd, _attention_bwd)


def _chunk_order(step, n_chunks, n_ctx_chunks, reverse):
    if not reverse:
        return step
    return jnp.where(step < n_ctx_chunks, n_ctx_chunks - 1 - step, n_chunks + n_ctx_chunks - 1 - step)


def _scan_masks(chunk, reverse):
    row = lax.broadcasted_iota(jnp.int32, (chunk, chunk), 0)
    col = lax.broadcasted_iota(jnp.int32, (chunk, chunk), 1)
    vis = (col >= row) if reverse else (col <= row)
    vis_t = (row >= col) if reverse else (row <= col)
    return vis, vis.astype(F32), vis_t.astype(F32)


def _cum_decay(a_col, a_row, vis_f):
    hi = lax.Precision.HIGHEST
    cum_col = jnp.dot(vis_f, a_col, precision=hi, preferred_element_type=F32)
    cum_row = lax.dot_general(a_row, vis_f, (((1,), (1,)), ((), ())), precision=hi, preferred_element_type=F32)
    total = jnp.sum(a_col, axis=0, keepdims=True)
    return cum_col, cum_row, total


def _scan_specs(chunk, n_chunks, n_ctx_chunks, reverse, backward, widths):
    def order(i):
        step = (n_chunks - 1 - i) if backward else i
        return _chunk_order(step, n_chunks, n_ctx_chunks, reverse)

    return [pl.BlockSpec((chunk, w), lambda i: (order(i), 0)) for w in widths], order


def _scan_fwd_call(q, k, v, a_col, a_row, *, groups, per_group, dk, dv, ctx_len, reverse):
    t = q.shape[0]
    chunk = SCAN_CHUNK
    n_chunks, n_ctx = t // chunk, ctx_len // chunk
    heads = groups * per_group
    (q_spec, k_spec, v_spec, acol_spec), order = _scan_specs(
        chunk, n_chunks, n_ctx, reverse, False, (groups * dk, groups * dk, heads * dv, LANES))

    def body(q_ref, k_ref, v_ref, acol_ref, arow_ref, y_ref, st_ref, s_sc):
        @pl.when(pl.program_id(0) == 0)
        def _():
            s_sc[...] = jnp.zeros(s_sc.shape, F32)

        st_ref[0] = s_sc[...]
        vis, vis_f, _ = _scan_masks(chunk, reverse)
        cum_col, cum_row, total = _cum_decay(acol_ref[...], arow_ref[...], vis_f)
        for g in range(groups):
            qg = q_ref[:, g * dk:(g + 1) * dk]
            kg = k_ref[:, g * dk:(g + 1) * dk]
            qk = _dot_nt(qg, kg)
            for r in range(per_group):
                h = g * per_group + r
                ccol = cum_col[:, h:h + 1]
                decay = jnp.exp(jnp.where(vis, ccol - cum_row[h:h + 1, :], NEG_BIG))
                vh = v_ref[:, h * dv:(h + 1) * dv]
                s_in = s_sc[h]
                y = _dot(qk * decay, vh) + jnp.exp(ccol) * _dot(qg, s_in)
                y_ref[:, h * dv:(h + 1) * dv] = y
                tot = total[:, h:h + 1]
                s_sc[h] = jnp.exp(tot) * s_in + _dot_tn(kg * jnp.exp(tot - ccol), vh)

    return pl.pallas_call(
        body, name="scan_fwd",
        grid=(n_chunks,),
        in_specs=[q_spec, k_spec, v_spec, acol_spec, pl.BlockSpec((8, chunk), lambda i: (0, order(i)))],
        out_specs=[v_spec, pl.BlockSpec((1, heads, dk, dv), lambda i: (order(i), 0, 0, 0))],
        out_shape=[jax.ShapeDtypeStruct(v.shape, F32), jax.ShapeDtypeStruct((n_chunks, heads, dk, dv), F32)],
        scratch_shapes=[pltpu.VMEM((heads, dk, dv), F32)],
        compiler_params=_params("arbitrary"),
    )(q, k, v, a_col, a_row)


def _scan_bwd_call(q, k, v, a_col, a_row, states, dy, *, groups, per_group, dk, dv, ctx_len, reverse):
    t = q.shape[0]
    chunk = SCAN_CHUNK
    n_chunks, n_ctx = t // chunk, ctx_len // chunk
    heads = groups * per_group
    (q_spec, k_spec, v_spec, acol_spec), order = _scan_specs(
        chunk, n_chunks, n_ctx, reverse, True, (groups * dk, groups * dk, heads * dv, LANES))
    arow_spec = pl.BlockSpec((8, chunk), lambda i: (0, order(i)))
    last = 0 if reverse else chunk - 1

    def body(q_ref, k_ref, v_ref, acol_ref, arow_ref, st_ref, dy_ref, dq_ref, dk_ref, dv_ref, da_ref, dat_ref,
             ds_sc):
        @pl.when(pl.program_id(0) == 0)
        def _():
            ds_sc[...] = jnp.zeros(ds_sc.shape, F32)

        vis, vis_f, vis_tf = _scan_masks(chunk, reverse)
        cum_col, cum_row, total = _cum_decay(acol_ref[...], arow_ref[...], vis_f)
        lane = lax.broadcasted_iota(jnp.int32, (chunk, LANES), 1)
        row = lax.broadcasted_iota(jnp.int32, (chunk, LANES), 0)
        sub = lax.broadcasted_iota(jnp.int32, (8, chunk), 0)
        dcum = jnp.zeros((chunk, LANES), F32)
        dcum_t = jnp.zeros((8, chunk), F32)
        for g in range(groups):
            qg = q_ref[:, g * dk:(g + 1) * dk]
            kg = k_ref[:, g * dk:(g + 1) * dk]
            qk = _dot_nt(qg, kg)
            dq_g = jnp.zeros((chunk, dk), F32)
            dk_g = jnp.zeros((chunk, dk), F32)
            for r in range(per_group):
                h = g * per_group + r
                ccol = cum_col[:, h:h + 1]
                decay = jnp.exp(jnp.where(vis, ccol - cum_row[h:h + 1, :], NEG_BIG))
                vh = v_ref[:, h * dv:(h + 1) * dv]
                dyh = dy_ref[:, h * dv:(h + 1) * dv]
                s_in = st_ref[0, h]
                ds_out = ds_sc[h]
                tot = total[:, h:h + 1]
                e_in = jnp.exp(ccol)
                e_out = jnp.exp(tot - ccol)
                e_tot = jnp.exp(tot)
                k_out = kg * e_out
                dv_ref[:, h * dv:(h + 1) * dv] = _dot_tn(qk * decay, dyh) + _dot(k_out, ds_out)
                dqk = _dot_nt(dyh, vh) * decay
                dq_in = e_in * _dot_nt(dyh, s_in)
                dk_out = e_out * _dot_nt(vh, ds_out)
                dq_h = _dot(dqk, kg) + dq_in
                dk_h = _dot_tn(dqk, qg) + dk_out
                s_out = e_tot * s_in + _dot_tn(k_out, vh)
                edge = jnp.sum(jnp.sum(s_out * ds_out, axis=1, keepdims=True), axis=0, keepdims=True)
                w_seg = dqk * qk
                dcum_h = (jnp.sum(w_seg, axis=1, keepdims=True) + jnp.sum(dq_in * qg, axis=1, keepdims=True)
                          - jnp.sum(dk_out * kg, axis=1, keepdims=True))
                dcum = jnp.where(lane == h, dcum_h + jnp.where(row == last, edge, 0.0), dcum)
                dcum_t = jnp.where(sub == h, -jnp.sum(w_seg, axis=0, keepdims=True), dcum_t)
                ds_sc[h] = e_tot * ds_out + _dot_tn(qg, e_in * dyh)
                dq_g = dq_g + dq_h
                dk_g = dk_g + dk_h
            dq_ref[:, g * dk:(g + 1) * dk] = dq_g
            dk_ref[:, g * dk:(g + 1) * dk] = dk_g
        hi = lax.Precision.HIGHEST
        da_ref[...] = jnp.dot(vis_tf, dcum, precision=hi, preferred_element_type=F32)
        dat_ref[...] = jnp.dot(dcum_t, vis_f, precision=hi, preferred_element_type=F32)

    return pl.pallas_call(
        body, name="scan_bwd",
        grid=(n_chunks,),
        in_specs=[q_spec, k_spec, v_spec, acol_spec, arow_spec,
                  pl.BlockSpec((1, heads, dk, dv), lambda i: (order(i), 0, 0, 0)), v_spec],
        out_specs=[q_spec, k_spec, v_spec, acol_spec, arow_spec],
        out_shape=[jax.ShapeDtypeStruct(q.shape, F32), jax.ShapeDtypeStruct(k.shape, F32),
                   jax.ShapeDtypeStruct(v.shape, F32), jax.ShapeDtypeStruct((t, LANES), F32),
                   jax.ShapeDtypeStruct((8, t), F32)],
        scratch_shapes=[pltpu.VMEM((heads, dk, dv), F32)],
        compiler_params=_params("arbitrary"),
    )(q, k, v, a_col, a_row, states, dy)


def _decay_layouts(a):
    t, heads = a.shape
    a_col = jnp.pad(a, ((0, 0), (0, LANES - heads)))
    a_row = jnp.pad(a.T, ((0, 8 - heads), (0, 0)))
    return a_col, a_row


@functools.partial(jax.custom_vjp, nondiff_argnums=(4,))
def linear_scan(q, k, v, a, cfg):
    a_col, a_row = _decay_layouts(a)
    return _scan_fwd_call(q, k, v, a_col, a_row, **dict(cfg))[0]


def _linear_scan_fwd(q, k, v, a, cfg):
    a_col, a_row = _decay_layouts(a)
    y, states = _scan_fwd_call(q, k, v, a_col, a_row, **dict(cfg))
    return y, (q, k, v, a, states)


def _linear_scan_bwd(cfg, res, dy):
    q, k, v, a, states = res
    a_col, a_row = _decay_layouts(a)
    dq, dk, dv, da, da_t = _scan_bwd_call(q, k, v, a_col, a_row, states, dy, **dict(cfg))
    heads = a.shape[1]
    return dq, dk, dv, da[:, :heads] + da_t[:heads].T


linear_scan.defvjp(_linear_scan_fwd, _linear_scan_bwd)


def _scan_cfg(groups, per_group, dk, dv, ctx_len, reverse):
    return (("groups", groups), ("per_group", per_group), ("dk", dk), ("dv", dv), ("ctx_len", ctx_len),
            ("reverse", reverse))


def _rms_norm(x, w):
    return x * lax.rsqrt(jnp.mean(x * x, axis=-1, keepdims=True) + NORM_EPS) * w


def _rope(x, cos, sin):
    half = x.shape[-1] // 2
    x1, x2 = x[..., :half], x[..., half:]
    cs, sn = cos[:, None, :], sin[:, None, :]
    return jnp.concatenate([x1 * cs - x2 * sn, x1 * sn + x2 * cs], axis=-1)


def _axial_tables(n_lat, n_ctx):
    freqs = ATTN_HEAD_DIM // 4
    rows = n_lat // GRID_W
    row = jnp.repeat(jnp.arange(rows, dtype=F32), GRID_W)
    col = jnp.tile(jnp.arange(GRID_W, dtype=F32), rows)
    inv = ROPE_THETA ** (-jnp.arange(freqs, dtype=F32) / freqs)
    ang = jnp.concatenate([row[:, None] * inv, col[:, None] * inv], axis=-1)
    cos = jnp.concatenate([jnp.ones((n_ctx, 2 * freqs), F32), jnp.cos(ang)], axis=0)
    sin = jnp.concatenate([jnp.zeros((n_ctx, 2 * freqs), F32), jnp.sin(ang)], axis=0)
    return cos, sin


def _seq_tables(t):
    pos = jnp.arange(t, dtype=F32)
    inv = ROPE_THETA ** (-jnp.linspace(0.0, 1.0, RET_DK // 2, dtype=F32))
    ang = pos[:, None] * inv
    return jnp.cos(ang), jnp.sin(ang)


def _pad_w_in(w_in):
    d = w_in.shape[0]
    return jnp.concatenate([w_in[:, :DT_END], jnp.zeros((d, DT_PAD), w_in.dtype), w_in[:, DT_END:],
                            jnp.zeros((d, TAIL_PAD), w_in.dtype)], axis=1)


def _split_proj(p):
    widths = list(IN_SPLITS)
    widths[5] = LANES
    out, off = [], 0
    for w in widths:
        out.append(p[:, off:off + w])
        off += w
    out[5] = out[5][:, :DT_COLS]
    return out


def _mixer(u, w, wq, layer, n_ctx, tables):
    t = u.shape[0]
    attn_rope, ret_rope, seg_first, seg_last = tables
    proj = matmul(u, wq["w_in"][layer], _pad_w_in(w["w_in"][layer]))
    aq, ak, av, z, xbc_raw, dt_raw, rq, rk, rv, rg, gate_logits = _split_proj(proj)

    q = _rope(_rms_norm(aq.reshape(t, ATTN_HEADS, ATTN_HEAD_DIM), w["attn_q_norm"][layer]), *attn_rope)
    k = _rope(_rms_norm(ak.reshape(t, ATTN_KV_HEADS, ATTN_HEAD_DIM), w["attn_k_norm"][layer]), *attn_rope)
    q4 = q.reshape(t, ATTN_KV_HEADS, ATTN_GROUP, ATTN_HEAD_DIM).transpose(1, 2, 0, 3)
    k3 = k.transpose(1, 0, 2)
    v3 = av.reshape(t, ATTN_KV_HEADS, ATTN_HEAD_DIM).transpose(1, 0, 2)
    o4 = attention(q4, k3, v3, n_ctx)
    br_attn = o4.transpose(2, 0, 1, 3).reshape(t, ATTN_HEADS * ATTN_HEAD_DIM)

    cw, cb = w["ssd_conv_w"][layer], w["ssd_conv_b"][layer]
    zero_row = jnp.zeros((1, xbc_raw.shape[1]), F32)
    prev = jnp.concatenate([zero_row, xbc_raw[:-1]], axis=0) * (1.0 - seg_first)
    nxt = jnp.concatenate([xbc_raw[1:], zero_row], axis=0) * (1.0 - seg_last)
    xbc = jax.nn.silu(prev * cw[0] + xbc_raw * cw[1] + nxt * cw[2] + cb)
    gn = SSD_GROUPS * SSD_STATE
    xs = xbc[:, :SSD_D_INNER]
    bm = xbc[:, SSD_D_INNER:SSD_D_INNER + gn]
    cm = xbc[:, SSD_D_INNER + gn:]
    dt = jax.nn.softplus(dt_raw.reshape(t, 2, SSD_HEADS) + w["ssd_dt_bias"][layer])
    a_neg = -jnp.exp(w["ssd_a_log"][layer])
    xs_h = xs.reshape(t, SSD_HEADS, SSD_HEAD_DIM)
    y_ssd = jnp.zeros((t, SSD_D_INNER), F32)
    for d, reverse in ((0, False), (1, True)):
        dtd = dt[:, d]
        cfg = _scan_cfg(SSD_GROUPS, SSD_HEADS // SSD_GROUPS, SSD_STATE, SSD_HEAD_DIM, n_ctx, reverse)
        y_ssd = y_ssd + linear_scan(cm, bm, (xs_h * dtd[:, :, None]).reshape(t, SSD_D_INNER), dtd * a_neg[d], cfg)
    y_ssd = y_ssd.reshape(t, SSD_HEADS, SSD_HEAD_DIM) + w["ssd_d"][layer][:, None] * xs_h
    br_ssd = _rms_norm(y_ssd.reshape(t, SSD_D_INNER) * jax.nn.silu(z), w["ssd_norm_w"][layer])

    lg = -jnp.exp(w["ret_log_decay"][layer])
    rq_r = _rope(rq.reshape(t, RET_HEADS, RET_DK), *ret_rope).reshape(t, RET_HEADS * RET_DK)
    rk_r = (_rope(rk.reshape(t, RET_HEADS, RET_DK), *ret_rope) * (RET_DK ** -0.5)).reshape(t, RET_HEADS * RET_DK)
    y_ret = jnp.zeros((t, RET_HEADS * RET_DV), F32)
    for d, reverse in ((0, False), (1, True)):
        cfg = _scan_cfg(RET_HEADS, 1, RET_DK, RET_DV, n_ctx, reverse)
        y_ret = y_ret + linear_scan(rq_r, rk_r, rv, jnp.broadcast_to(lg[d][None, :], (t, RET_HEADS)), cfg)
    y4 = y_ret.reshape(t, RET_HEADS, RET_DV)
    yc = y4 - jnp.mean(y4, axis=-1, keepdims=True)
    yn = yc * lax.rsqrt(jnp.mean(yc * yc, axis=-1, keepdims=True) + NORM_EPS)
    br_ret = yn.reshape(t, RET_HEADS * RET_DV) * w["ret_gn_w"][layer] * jax.nn.silu(rg)

    gates = jax.nn.sigmoid(gate_logits.reshape(t, N_BRANCH, D_MODEL))
    branches = (br_attn, br_ssd, br_ret)
    merged = sum(gates[:, j] * matmul(branches[j], wq["w_branch"][layer][j], w["w_branch"][layer][j])
                 for j in range(N_BRANCH))
    return matmul(merged, wq["w_out"][layer], w["w_out"][layer])


def _local_loss(w, x, c, ctx, target, wq):
    n, m = x.shape[0], ctx.shape[0]
    t = n + m
    is_lat = (jnp.arange(t) >= m)[:, None]
    pos = jnp.arange(t)[:, None]
    seg_first = ((pos == 0) | (pos == m)).astype(F32)
    seg_last = ((pos == m - 1) | (pos == t - 1)).astype(F32)
    tables = (_axial_tables(n, m), _seq_tables(t), seg_first, seg_last)
    h = jnp.concatenate([ctx, x], axis=0)
    cond = jax.nn.silu(jnp.stack([c, w["c_ctx"]], axis=0))
    cond8 = jnp.concatenate([cond, jnp.zeros((6, D_MODEL), F32)], axis=0)
    for layer in range(DEPTH):
        mod = (matmul(cond8, wq["w_mod"][layer], w["w_mod"][layer])[:2] + w["b_mod"][layer]).reshape(2, 6, D_MODEL)

        def sel(j, mod=mod):
            return jnp.where(is_lat, mod[0, j][None, :], mod[1, j][None, :])

        u = _rms_norm(h, w["norm1_w"][layer]) * (1 + sel(1)) + sel(0)
        h = h + sel(2) * _mixer(u, w, wq, layer, m, tables)
        v = _rms_norm(h, w["norm2_w"][layer]) * (1 + sel(4)) + sel(3)
        hid = jax.nn.relu(matmul(v, wq["w_mlp1"][layer], w["w_mlp1"][layer]))
        h = h + sel(5) * matmul(hid * hid, wq["w_mlp2"][layer], w["w_mlp2"][layer])
    y = _rms_norm(h[m:], w["final_norm_w"])
    return 0.5 * jnp.sum(jnp.mean(jnp.square(y - target), axis=-1))


def _coords():
    return lax.axis_index("x"), lax.axis_index("y"), lax.axis_index("c")


def _all_gather(blocks, name):
    n = len(blocks)

    def body(*refs):
        x_refs, out_refs = refs[:n], refs[n:2 * n]
        send_sems, recv_sems, local_sems = refs[2 * n:]
        x, y, c = _coords()
        me, sibling = (x, y, c), (x, y, 1 - c)
        chips = [(1 - x, y), (x, 1 - y), (1 - x, 1 - y)]

        def copy(k, i, blk, to, from_input=False):
            slot = out_refs[i].at[4 * blk[0] + 2 * blk[1] + blk[2]]
            return pltpu.make_async_remote_copy(
                src_ref=x_refs[i] if from_input else slot, dst_ref=slot,
                send_sem=send_sems.at[k * n + i], recv_sem=recv_sems.at[k * n + i],
                device_id=to, device_id_type=pl.DeviceIdType.MESH)

        mine = [pltpu.make_async_copy(x_refs[i], out_refs[i].at[4 * x + 2 * y + c], local_sems.at[i])
                for i in range(n)]
        for cp in mine:
            cp.start()
        first = [copy(0, i, me, sibling, True) for i in range(n)]
        first += [copy(1 + j, i, me, (*chip, c), True) for j, chip in enumerate(chips) for i in range(n)]
        for cp in first:
            cp.start()
        passed = []
        for j, chip in enumerate(chips):
            for i in range(n):
                copy(1 + j, i, (*chip, c), me).wait_recv()
                passed.append(copy(4 + j, i, (*chip, c), sibling))
                passed[-1].start()
        for i in range(n):
            copy(0, i, sibling, me).wait_recv()
        for j, chip in enumerate(chips):
            for i in range(n):
                copy(4 + j, i, (*chip, 1 - c), me).wait_recv()
        for cp in first + passed:
            cp.wait_send()
        for cp in mine:
            cp.wait()

    return pl.pallas_call(
        body, name=name,
        out_shape=[jax.ShapeDtypeStruct((N_DEV,) + b.shape, b.dtype) for b in blocks],
        in_specs=[pl.BlockSpec(memory_space=pl.ANY)] * n,
        out_specs=[pl.BlockSpec(memory_space=pl.ANY)] * n,
        scratch_shapes=[pltpu.SemaphoreType.DMA((7 * n,)), pltpu.SemaphoreType.DMA((7 * n,)),
                        pltpu.SemaphoreType.DMA((n,))],
    )(*blocks)


def _all_to_all(arrays, name):
    n = len(arrays)

    def body(*refs):
        g_refs, out_refs = refs[:n], refs[n:2 * n]
        send_sems, recv_sems, local_sems = refs[2 * n:]
        x, y, c = _coords()
        me = 4 * x + 2 * y + c
        mine = [pltpu.make_async_copy(g_refs[i].at[me], out_refs[i].at[me], local_sems.at[i]) for i in range(n)]
        for cp in mine:
            cp.start()
        copies = []
        for k in range(1, N_DEV):
            bx, by, bc = (k >> 2) & 1, (k >> 1) & 1, k & 1
            px, py, pc = (1 - x if bx else x), (1 - y if by else y), (1 - c if bc else c)
            peer = 4 * px + 2 * py + pc
            for i in range(n):
                copies.append(pltpu.make_async_remote_copy(
                    src_ref=g_refs[i].at[peer], dst_ref=out_refs[i].at[me],
                    send_sem=send_sems.at[(k - 1) * n + i], recv_sem=recv_sems.at[(k - 1) * n + i],
                    device_id=(px, py, pc), device_id_type=pl.DeviceIdType.MESH))
        for cp in copies:
            cp.start()
        for cp in copies:
            cp.wait_recv()
        for cp in copies:
            cp.wait_send()
        for cp in mine:
            cp.wait()

    return pl.pallas_call(
        body, name=name,
        out_shape=[jax.ShapeDtypeStruct(a.shape, a.dtype) for a in arrays],
        in_specs=[pl.BlockSpec(memory_space=pl.ANY)] * n,
        out_specs=[pl.BlockSpec(memory_space=pl.ANY)] * n,
        scratch_shapes=[pltpu.SemaphoreType.DMA((7 * n,)), pltpu.SemaphoreType.DMA((7 * n,)),
                        pltpu.SemaphoreType.DMA((n,))],
    )(*arrays)


ADAMW_BLOCK_ELEMS = 256 * 1024


def _sum_adamw(g8, w, m, v, name):
    rows, cols = w.shape
    tr = _tile(rows, [r for r in (2048, 1024, 512, 256, 128, 64, 32, 16) if r * cols <= ADAMW_BLOCK_ELEMS])

    def body(g_ref, w_ref, m_ref, v_ref, go_ref, d_ref, mo_ref, vo_ref):
        g = g_ref[0].astype(F32)
        for s in range(1, N_DEV):
            g = g + g_ref[s].astype(F32)
        m_new = ADAM_B1 * m_ref[...] + (1.0 - ADAM_B1) * g
        v_new = ADAM_B2 * v_ref[...] + (1.0 - ADAM_B2) * (g * g)
        m_hat = m_new / (1.0 - ADAM_B1 ** ADAM_STEP)
        v_hat = v_new / (1.0 - ADAM_B2 ** ADAM_STEP)
        go_ref[...] = g
        d_ref[...] = -ADAM_LR * (m_hat / (jnp.sqrt(v_hat) + ADAM_EPS) + ADAM_WD * w_ref[...])
        mo_ref[...] = m_new
        vo_ref[...] = v_new

    spec = pl.BlockSpec((tr, cols), lambda i: (i, 0))
    shape = jax.ShapeDtypeStruct((rows, cols), F32)
    return pl.pallas_call(
        body, name=name,
        grid=(rows // tr,),
        in_specs=[pl.BlockSpec((N_DEV, tr, cols), lambda i: (0, i, 0)), spec, spec, spec],
        out_specs=[spec, spec, spec, spec],
        out_shape=[shape, shape, shape, shape],
        compiler_params=_params("parallel"),
    )(g8, w, m, v)


BIG = ("w_mod", "w_in", "w_branch", "w_out", "w_mlp1", "w_mlp2")
COL_SHARDED = ("w_mod", "w_mlp1")
ROW_SHARDED = ("w_out", "w_mlp2")
SMALL = ("c_ctx", "b_mod", "norm1_w", "norm2_w", "attn_q_norm", "attn_k_norm", "ssd_conv_b", "ssd_dt_bias",
         "ssd_a_log", "ssd_d", "ssd_norm_w", "ret_log_decay", "ret_gn_w", "final_norm_w")
CONV_AXIS = 2
ORDER = ("c_ctx", "w_mod", "b_mod", "norm1_w", "norm2_w", "w_in", "attn_q_norm", "attn_k_norm", "ssd_conv_w",
         "ssd_conv_b", "ssd_dt_bias", "ssd_a_log", "ssd_d", "ssd_norm_w", "ret_log_decay", "ret_gn_w", "w_branch",
         "w_out", "w_mlp1", "w_mlp2", "final_norm_w")


def _compute_weights(gathered):
    wq, carrier = {}, {}
    for name in BIG:
        g = gathered[name]
        per_layer = []
        for layer in range(DEPTH):
            gl = g[:, layer]
            if name in COL_SHARDED:
                per_layer.append(gl)
            elif name in ROW_SHARDED:
                per_layer.append(gl.reshape(-1, gl.shape[-1]))
            elif name == "w_in":
                per_layer.append(_pad_w_in(jnp.concatenate([gl[d] for d in range(N_DEV)], axis=-1)))
            else:
                per_layer.append([jnp.concatenate([gl[d, j] for d in range(N_DEV)], axis=-1) for j in range(N_BRANCH)])
        wq[name] = per_layer
    for name in BIG:
        if name == "w_in":
            carrier[name] = [jnp.zeros((D_MODEL, IN_DIM), F32) for _ in range(DEPTH)]
        else:
            carrier[name] = jax.tree.map(lambda a: jnp.zeros(a.shape, F32), wq[name])
    return wq, carrier


def _grad_shards(gw):
    out = {}
    for name in BIG:
        per_layer = []
        for layer in range(DEPTH):
            g = gw[name][layer]
            if name in COL_SHARDED:
                per_layer.append(g)
            elif name in ROW_SHARDED:
                per_layer.append(g.reshape(N_DEV, -1, g.shape[-1]))
            elif name == "w_in":
                size = IN_DIM // N_DEV
                per_layer.append(jnp.stack([g[:, d * size:(d + 1) * size] for d in range(N_DEV)]))
            else:
                per_layer.append(jnp.stack([gj.reshape(gj.shape[0], N_DEV, -1).transpose(1, 0, 2) for gj in g], axis=1))
        out[name] = jnp.stack(per_layer, axis=1)
    return out


def _pack(arrays, row_multiple):
    flat = jnp.concatenate(arrays, axis=-1)
    n = flat.shape[-1]
    per = LANES * row_multiple
    padded = -(-n // per) * per
    flat = jnp.pad(flat, [(0, 0)] * (flat.ndim - 1) + [(0, padded - n)])
    return flat.reshape(flat.shape[:-1] + (padded // LANES, LANES))


def _unpack(slab, shapes):
    flat = slab.reshape(slab.shape[:-2] + (-1,))
    out, off = [], 0
    for shp in shapes:
        size = math.prod(shp)
        out.append(flat[..., off:off + size].reshape(flat.shape[:-1] + tuple(shp)))
        off += size
    return out


def kernel(x, c, ctx, c_ctx, w_mod, b_mod, norm1_w, norm2_w, w_in, attn_q_norm, attn_k_norm, ssd_conv_w, ssd_conv_b, ssd_dt_bias, ssd_a_log, ssd_d, ssd_norm_w, ret_log_decay, ret_gn_w, w_branch, w_out, w_mlp1, w_mlp2, final_norm_w, loss_target, m_c_ctx, m_w_mod, m_b_mod, m_norm1_w, m_norm2_w, m_w_in, m_attn_q_norm, m_attn_k_norm, m_ssd_conv_w, m_ssd_conv_b, m_ssd_dt_bias, m_ssd_a_log, m_ssd_d, m_ssd_norm_w, m_ret_log_decay, m_ret_gn_w, m_w_branch, m_w_out, m_w_mlp1, m_w_mlp2, m_final_norm_w, v_c_ctx, v_w_mod, v_b_mod, v_norm1_w, v_norm2_w, v_w_in, v_attn_q_norm, v_attn_k_norm, v_ssd_conv_w, v_ssd_conv_b, v_ssd_dt_bias, v_ssd_a_log, v_ssd_d, v_ssd_norm_w, v_ret_log_decay, v_ret_gn_w, v_w_branch, v_w_out, v_w_mlp1, v_w_mlp2, v_final_norm_w):
    args = dict(locals())
    weights = {n: args[n] for n in ORDER}
    mom1 = {n: args["m_" + n] for n in ORDER}
    mom2 = {n: args["v_" + n] for n in ORDER}
    me = 4 * lax.axis_index("x") + 2 * lax.axis_index("y") + lax.axis_index("c")

    gathered = _all_gather([weights[n].astype(MXU_DTYPE) for n in BIG], "gather_weights")
    wq, params = _compute_weights(dict(zip(BIG, gathered)))
    conv_shape = ssd_conv_w.shape
    conv_all = _all_gather([_pack([ssd_conv_w.reshape(-1)], 8)], "gather_conv")[0]
    params["ssd_conv_w"] = jnp.concatenate(list(_unpack(conv_all, [conv_shape])[0]), axis=CONV_AXIS)
    for n in SMALL:
        params[n] = weights[n]

    loss, (gw, gx) = jax.value_and_grad(_local_loss, argnums=(0, 1))(params, x[0], c[0], ctx[0], loss_target[0], wq)
    loss = lax.psum(loss, MESH_AXES)

    g_send = _grad_shards(gw)
    g_recv = _all_to_all([g_send[n].astype(jnp.bfloat16) for n in BIG], "scatter_grads")
    result = {}
    for n, g8 in zip(BIG, g_recv):
        shape = weights[n].shape
        as2d = lambda a: a.reshape(-1, shape[-1])
        outs = _sum_adamw(g8.reshape(N_DEV, -1, shape[-1]), as2d(weights[n]), as2d(mom1[n]), as2d(mom2[n]), "adamw_" + n)
        for kind, arr in zip(("grad", "delta", "new_m", "new_v"), outs):
            result[kind, n] = arr.reshape(shape)

    conv_full_shape = params["ssd_conv_w"].shape
    small_shapes = [weights[n].shape for n in SMALL]
    partial = _pack([gw[n].reshape(-1) for n in SMALL] + [gw["ssd_conv_w"].reshape(-1)], 8)
    parts = _unpack(_all_gather([partial], "gather_small_grads")[0], small_shapes + [conv_full_shape])
    conv_part = lax.dynamic_slice_in_dim(parts[-1], me * conv_shape[CONV_AXIS], conv_shape[CONV_AXIS], CONV_AXIS + 1)
    small_names = list(SMALL) + ["ssd_conv_w"]
    g8_small = _pack([p.reshape(N_DEV, -1) for p in parts[:-1]] + [conv_part.reshape(N_DEV, -1)], 8)
    slabs = [_pack([d[n].reshape(-1) for n in small_names], 8) for d in (weights, mom1, mom2)]
    small_out = [_unpack(s, small_shapes + [conv_shape]) for s in _sum_adamw(g8_small, *slabs, "adamw_small")]
    for kind, small_k in zip(("grad", "delta", "new_m", "new_v"), small_out):
        for n, arr in zip(small_names, small_k):
            result[kind, n] = arr

    outs = [loss, gx[None]]
    for kind in ("grad", "delta", "new_m", "new_v"):
        outs += [result[kind, n] for n in ORDER]
    return tuple(outs)
```

```python
import functools
import math

import jax
import jax.numpy as jnp
from jax import lax
from jax.experimental import pallas as pl
from jax.experimental.pallas import tpu as pltpu

F32 = jnp.float32
MXU_DTYPE = jnp.bfloat16
VMEM_LIMIT_BYTES = 48 * 1024 * 1024
LANES = 128
N_DEV = 8
MESH_AXES = ("x", "y", "c")

D_MODEL = 1024
GRID_W = 64
NORM_EPS = 1e-6
ROPE_THETA = 10000.0
ATTN_HEADS, ATTN_KV_HEADS, ATTN_HEAD_DIM = 8, 2, 64
ATTN_GROUP = ATTN_HEADS // ATTN_KV_HEADS
SSD_HEADS, SSD_HEAD_DIM, SSD_GROUPS, SSD_STATE = 8, 64, 2, 128
SSD_D_INNER = SSD_HEADS * SSD_HEAD_DIM
RET_HEADS, RET_DK, RET_DV = 4, 128, 128
SCAN_CHUNK = 128
N_BRANCH = 3
DEPTH = 2

IN_SPLITS = (512, 128, 128, 512, 1024, 16, 512, 512, 512, 512, 3072)
IN_DIM = sum(IN_SPLITS)
DT_COLS = 16
DT_PAD = LANES - DT_COLS
TAIL_PAD = 128
IN_DIM_PADDED = IN_DIM + DT_PAD + TAIL_PAD
DT_END = sum(IN_SPLITS[:6])

ADAM_LR, ADAM_B1, ADAM_B2, ADAM_EPS, ADAM_WD, ADAM_STEP = 0.001, 0.9, 0.999, 1e-08, 0.01, 10


def _tile(dim, prefs):
    for p in prefs:
        if dim % p == 0:
            return p
    return dim


def _params(*sem):
    return pltpu.CompilerParams(dimension_semantics=sem, vmem_limit_bytes=VMEM_LIMIT_BYTES)


def _mm(a, b, *, ta=False, tb=False, out_shards=False, epilogue=None, extra=None, name):
    if ta:
        kdim, m = a.shape
    else:
        m, kdim = a.shape
    b_shards = b.ndim == 3
    if b_shards:
        rows_b, cols_b = b.shape[1], N_DEV * b.shape[2]
    else:
        rows_b, cols_b = b.shape
    n, kdim_b = (rows_b, cols_b) if tb else (cols_b, rows_b)
    assert kdim == kdim_b, (a.shape, b.shape, ta, tb)
    tm = _tile(m, (1024, 768, 512, 256, 128))
    n_tile_of = n // N_DEV if (out_shards or (b_shards and not tb)) else n
    k_tile_of = kdim // N_DEV if (b_shards and tb) else kdim
    tn = _tile(n_tile_of, (1280, 1024, 768, 512, 384, 256, 128))
    tk = _tile(k_tile_of, (1024, 768, 512, 256, 128))
    dims = (((0 if ta else 1,), (1 if tb else 0,)), ((), ()))

    n_k = kdim // tk

    def body(a_ref, b_ref, *rest):
        o_ref = rest[-1]
        part = lax.dot_general(a_ref[...].astype(MXU_DTYPE), b_ref[...].astype(MXU_DTYPE), dims,
                               preferred_element_type=F32)

        @pl.when(pl.program_id(2) == 0)
        def _():
            o_ref[...] = part

        @pl.when(pl.program_id(2) > 0)
        def _():
            o_ref[...] += part

        if epilogue is not None:
            @pl.when(pl.program_id(2) == n_k - 1)
            def _():
                acc = o_ref[...]
                if epilogue == "sq_relu":
                    r = jnp.maximum(acc, 0.0)
                    o_ref[...] = r * r
                else:
                    o_ref[...] = acc * (2.0 * jnp.sqrt(rest[0][...]))

    a_spec = pl.BlockSpec((tk, tm), lambda i, j, k: (k, i)) if ta else pl.BlockSpec((tm, tk), lambda i, j, k: (i, k))
    if not b_shards:
        b_spec = pl.BlockSpec((tn, tk), lambda i, j, k: (j, k)) if tb else pl.BlockSpec((tk, tn), lambda i, j, k: (k, j))
    elif tb:
        per = b.shape[2] // tk
        b_spec = pl.BlockSpec((None, tn, tk), lambda i, j, k: (k // per, j, k % per))
    else:
        per = b.shape[2] // tn
        b_spec = pl.BlockSpec((None, tk, tn), lambda i, j, k: (j // per, k, j % per))
    if out_shards:
        per_out = n // N_DEV // tn
        out_spec = pl.BlockSpec((None, tm, tn), lambda i, j, k: (j // per_out, i, j % per_out))
        out_shape = jax.ShapeDtypeStruct((N_DEV, m, n // N_DEV), F32)
    else:
        out_spec = pl.BlockSpec((tm, tn), lambda i, j, k: (i, j))
        out_shape = jax.ShapeDtypeStruct((m, n), F32)
    assert epilogue in (None, "sq_relu", "d_sq_relu") and (extra is not None) == (epilogue == "d_sq_relu")
    operands, in_specs = [a, b], [a_spec, b_spec]
    if extra is not None:
        assert not out_shards and extra.shape == (m, n)
        operands.append(extra)
        in_specs.append(out_spec)
    return pl.pallas_call(
        body, name=name,
        grid=(m // tm, n // tn, n_k),
        in_specs=in_specs,
        out_specs=out_spec,
        out_shape=out_shape,
        compiler_params=_params("parallel", "parallel", "arbitrary"),
    )(*operands)


@jax.custom_vjp
def matmul(a, b, b_grad):
    return _mm(a, b, name="mm_fwd")


def _matmul_fwd(a, b, b_grad):
    return _mm(a, b, name="mm_fwd"), (a, b)


def _matmul_bwd(res, g):
    a, b = res
    dw = _mm(a, g, ta=True, out_shards=b.ndim == 3, name="mm_dw")
    return _mm(g, b, tb=True, name="mm_dx"), jnp.zeros_like(b), dw


matmul.defvjp(_matmul_fwd, _matmul_bwd)


@jax.custom_vjp
def sq_relu_mlp(x, w1, w1_grad, w2, w2_grad):
    return _mm(_mm(x, w1, epilogue="sq_relu", name="mlp_up"), w2, name="mlp_down")


def _sq_relu_mlp_fwd(x, w1, w1_grad, w2, w2_grad):
    hid = _mm(x, w1, epilogue="sq_relu", name="mlp_up")
    return _mm(hid, w2, name="mlp_down"), (x, w1, w2, hid)


def _sq_relu_mlp_bwd(res, g):
    x, w1, w2, hid = res
    d_pre = _mm(g, w2, tb=True, epilogue="d_sq_relu", extra=hid, name="mlp_down_dx")
    dw2 = _mm(hid, g, ta=True, out_shards=w2.ndim == 3, name="mlp_down_dw")
    dw1 = _mm(x, d_pre, ta=True, out_shards=w1.ndim == 3, name="mlp_up_dw")
    return _mm(d_pre, w1, tb=True, name="mlp_up_dx"), jnp.zeros_like(w1), dw1, jnp.zeros_like(w2), dw2


sq_relu_mlp.defvjp(_sq_relu_mlp_fwd, _sq_relu_mlp_bwd)


NORM_TILE_PREFS = (768, 512, 256, 128)


def _norm_mod_pieces(h_ref, w_ref, shift_ref, scale_ref, tile, tm, ctx_len):
    x = h_ref[...]
    rstd = lax.rsqrt(jnp.mean(x * x, axis=1, keepdims=True) + NORM_EPS)
    xn = x * rstd
    is_ctx = tile * tm + lax.broadcasted_iota(jnp.int32, (tm, 1), 0) < ctx_len
    scale = jnp.where(is_ctx, scale_ref[1:2, :], scale_ref[0:1, :])
    shift = jnp.where(is_ctx, shift_ref[1:2, :], shift_ref[0:1, :])
    return xn, rstd, is_ctx, scale, shift


def _norm_mod_fwd_call(h, w, shift, scale, ctx_len):
    t, d = h.shape
    tm = _tile(t, NORM_TILE_PREFS)

    def body(h_ref, w_ref, shift_ref, scale_ref, u_ref):
        xn, _, _, sc, sh = _norm_mod_pieces(h_ref, w_ref, shift_ref, scale_ref, pl.program_id(0), tm, ctx_len)
        u_ref[...] = xn * w_ref[...] * (1.0 + sc) + sh

    row = pl.BlockSpec((tm, d), lambda i: (i, 0))
    return pl.pallas_call(
        body, name="norm_mod_fwd",
        grid=(t // tm,),
        in_specs=[row, pl.BlockSpec((1, d), lambda i: (0, 0)), pl.BlockSpec((2, d), lambda i: (0, 0)),
                  pl.BlockSpec((2, d), lambda i: (0, 0))],
        out_specs=row,
        out_shape=jax.ShapeDtypeStruct((t, d), F32),
        compiler_params=_params("parallel"),
    )(h, w, shift, scale)


def _norm_mod_bwd_call(h, w, shift, scale, du, ctx_len):
    t, d = h.shape
    tm = _tile(t, NORM_TILE_PREFS)

    def body(h_ref, w_ref, shift_ref, scale_ref, du_ref, dh_ref, sums_ref):
        xn, rstd, is_ctx, sc, _ = _norm_mod_pieces(h_ref, w_ref, shift_ref, scale_ref, pl.program_id(0), tm, ctx_len)
        du = du_ref[...]
        wv = w_ref[...]
        dy = du * (1.0 + sc)
        dxn = dy * wv
        dh_ref[...] = rstd * (dxn - xn * jnp.mean(dxn * xn, axis=1, keepdims=True))
        dsc = du * (xn * wv)

        def colsum(v):
            return jnp.sum(v, axis=0, keepdims=True)

        dshift_all, dshift_ctx = colsum(du), colsum(jnp.where(is_ctx, du, 0.0))
        dscale_all, dscale_ctx = colsum(dsc), colsum(jnp.where(is_ctx, dsc, 0.0))
        part = jnp.concatenate([colsum(dy * xn), dshift_all - dshift_ctx, dshift_ctx, dscale_all - dscale_ctx,
                                dscale_ctx, jnp.zeros((3, d), F32)], axis=0)

        @pl.when(pl.program_id(0) == 0)
        def _():
            sums_ref[...] = part

        @pl.when(pl.program_id(0) > 0)
        def _():
            sums_ref[...] += part

    row = pl.BlockSpec((tm, d), lambda i: (i, 0))
    return pl.pallas_call(
        body, name="norm_mod_bwd",
        grid=(t // tm,),
        in_specs=[row, pl.BlockSpec((1, d), lambda i: (0, 0)), pl.BlockSpec((2, d), lambda i: (0, 0)),
                  pl.BlockSpec((2, d), lambda i: (0, 0)), row],
        out_specs=[row, pl.BlockSpec((8, d), lambda i: (0, 0))],
        out_shape=[jax.ShapeDtypeStruct((t, d), F32), jax.ShapeDtypeStruct((8, d), F32)],
        compiler_params=_params("arbitrary"),
    )(h, w, shift, scale, du)


@functools.partial(jax.custom_vjp, nondiff_argnums=(4,))
def norm_mod(h, w, shift, scale, ctx_len):
    return _norm_mod_fwd_call(h, w[None, :], shift, scale, ctx_len)


def _norm_mod_fwd(h, w, shift, scale, ctx_len):
    return _norm_mod_fwd_call(h, w[None, :], shift, scale, ctx_len), (h, w, shift, scale)


def _norm_mod_bwd(ctx_len, res, du):
    h, w, shift, scale = res
    dh, sums = _norm_mod_bwd_call(h, w[None, :], shift, scale, du, ctx_len)
    return dh, sums[0], sums[1:3], sums[3:5]


norm_mod.defvjp(_norm_mod_fwd, _norm_mod_bwd)


def _bf(x):
    return x.astype(MXU_DTYPE)


def _dot(a, b):
    return jnp.dot(_bf(a), _bf(b), preferred_element_type=F32)


def _dot_nt(a, b):
    return lax.dot_general(_bf(a), _bf(b), (((1,), (1,)), ((), ())), preferred_element_type=F32)


def _dot_tn(a, b):
    return lax.dot_general(_bf(a), _bf(b), (((0,), (0,)), ((), ())), preferred_element_type=F32)


ROWWISE_VMEM_BYTES = 20 * 1024 * 1024


@functools.partial(jax.custom_vjp, nondiff_argnums=(1,))
def _split_lanes(x, n):
    w = x.shape[1] // n
    return tuple(x[:, i * w:(i + 1) * w] for i in range(n))


def _split_lanes_fwd(x, n):
    return _split_lanes(x, n), None


def _split_lanes_bwd(n, _, gs):
    return (jnp.concatenate(gs, axis=1),)


_split_lanes.defvjp(_split_lanes_fwd, _split_lanes_bwd)


def _rowwise_tile(t, widths):
    for tm in (768, 512, 256, 128):
        if t % tm == 0 and tm * 8 * sum(widths) <= ROWWISE_VMEM_BYTES:
            return tm
    raise ValueError((t, widths))


def _rowwise(name, fn, out_w, ctx_len):
    def is_ctx(tm):
        return pl.program_id(0) * tm + lax.broadcasted_iota(jnp.int32, (tm, 1), 0) < ctx_len

    def specs(params, rows, tm):
        return ([pl.BlockSpec(p.shape, lambda i: (0, 0)) for p in params]
                + [pl.BlockSpec((tm, r.shape[1]), lambda i: (i, 0)) for r in rows])

    def fwd_call(params, rows):
        t = rows[0].shape[0]
        tm = _rowwise_tile(t, [r.shape[1] for r in rows] + [out_w])
        n_p = len(params)

        def body(*refs):
            vals = [r[...] for r in refs[:-1]]
            refs[-1][...] = fn(is_ctx(tm), tuple(vals[:n_p]), tuple(vals[n_p:]))

        return pl.pallas_call(
            body, name=name + "_fwd", grid=(t // tm,),
            in_specs=specs(params, rows, tm),
            out_specs=pl.BlockSpec((tm, out_w), lambda i: (i, 0)),
            out_shape=jax.ShapeDtypeStruct((t, out_w), F32),
            compiler_params=_params("parallel"),
        )(*params, *rows)

    def bwd_call(params, rows, dout):
        t = rows[0].shape[0]
        tm = _rowwise_tile(t, [2 * r.shape[1] for r in rows] + [out_w])
        n_p, n_r = len(params), len(rows)

        def body(*refs):
            vals = [r[...] for r in refs[:n_p + n_r + 1]]
            dx_refs = refs[n_p + n_r + 1:n_p + 2 * n_r + 1]
            dp_refs = refs[n_p + 2 * n_r + 1:]
            ctx_rows = is_ctx(tm)
            _, vjp = jax.vjp(lambda p, x: fn(ctx_rows, p, x), tuple(vals[:n_p]), tuple(vals[n_p:n_p + n_r]))
            dp, dx = vjp(vals[-1])
            for ref, v in zip(dx_refs, dx):
                ref[...] = v

            @pl.when(pl.program_id(0) == 0)
            def _():
                for ref, v in zip(dp_refs, dp):
                    ref[...] = v

            @pl.when(pl.program_id(0) > 0)
            def _():
                for ref, v in zip(dp_refs, dp):
                    ref[...] += v

        row_specs = [pl.BlockSpec((tm, r.shape[1]), lambda i: (i, 0)) for r in rows]
        outs = pl.pallas_call(
            body, name=name + "_bwd", grid=(t // tm,),
            in_specs=specs(params, rows, tm) + [pl.BlockSpec((tm, out_w), lambda i: (i, 0))],
            out_specs=row_specs + [pl.BlockSpec(p.shape, lambda i: (0, 0)) for p in params],
            out_shape=[jax.ShapeDtypeStruct(r.shape, F32) for r in rows]
            + [jax.ShapeDtypeStruct(p.shape, F32) for p in params],
            compiler_params=_params("arbitrary"),
        )(*params, *rows, dout)
        return tuple(outs[n_r:]), tuple(outs[:n_r])

    @jax.custom_vjp
    def op(params, rows):
        return fwd_call(params, rows)

    op.defvjp(lambda params, rows: (fwd_call(params, rows), (params, rows)),
              lambda res, g: bwd_call(res[0], res[1], g))
    return op


def _silu(x):
    return x * jax.nn.sigmoid(x)


def _ret_finish_tile(is_ctx, params, rows):
    (gn_w,), (y_f, y_b, gate) = params, rows
    heads = []
    for yh in _split_lanes(y_f + y_b, RET_HEADS):
        yc = yh - jnp.mean(yh, axis=1, keepdims=True)
        heads.append(yc * lax.rsqrt(jnp.mean(yc * yc, axis=1, keepdims=True) + NORM_EPS))
    return jnp.concatenate(heads, axis=1) * gn_w * _silu(gate)


def _ssd_finish_tile(is_ctx, params, rows):
    (d_skip, norm_w), (y_f, y_b, xs, z) = params, rows
    g = (y_f + y_b + d_skip * xs) * _silu(z)
    return g * lax.rsqrt(jnp.mean(g * g, axis=1, keepdims=True) + NORM_EPS) * norm_w


def _gate_merge_tile(is_ctx, params, rows):
    y0, y1, y2, logits = rows
    return sum(jax.nn.sigmoid(g) * y for g, y in zip(_split_lanes(logits, N_BRANCH), (y0, y1, y2)))


def _loss_rows_tile(is_ctx, params, rows):
    (norm_w,), (h, target) = params, rows
    y = h * lax.rsqrt(jnp.mean(h * h, axis=1, keepdims=True) + NORM_EPS) * norm_w
    err = jnp.mean(jnp.square(y - target), axis=1, keepdims=True)
    return jnp.broadcast_to(err, (h.shape[0], LANES))


def _gated_residual_tile(is_ctx, params, rows):
    (gate,), (h, update) = params, rows
    return h + jnp.where(is_ctx, gate[1:2, :], gate[0:1, :]) * update


NEG_BIG = -1e30


def _attn_tiles(t, ctx_len):
    tq = _tile(ctx_len, (256, 128))
    assert t % tq == 0 and ctx_len % tq == 0
    tk = _tile(t, (768, 512, 256, 128))
    return tq, tk


def _head_scores(q_ref, k_bf, g, ki, tk, ctx_len, masked):
    q = (q_ref[0, g] * (ATTN_HEAD_DIM ** -0.5)).astype(MXU_DTYPE)
    s = _dot_nt(q, k_bf)
    if masked:
        col = ki * tk + lax.broadcasted_iota(jnp.int32, s.shape, 1)
        s = jnp.where(col < ctx_len, s, NEG_BIG)
    return q, s


def _attn_cases(qi, ki, tq, tk, ctx_len, compute):
    ctx_q = (qi + 1) * tq <= ctx_len

    @pl.when(jnp.logical_not(ctx_q))
    def _():
        compute(False)

    @pl.when(jnp.logical_and(ctx_q, ki * tk < ctx_len))
    def _():
        compute(True)


def _attn_fwd_call(q, k, v, ctx_len):
    kvh, grp, t, hd = q.shape
    tq, tk = _attn_tiles(t, ctx_len)
    nkb = t // tk

    def body(q_ref, k_ref, v_ref, o_ref, lse_ref, m_sc, l_sc, acc_sc):
        qi, ki = pl.program_id(1), pl.program_id(2)

        @pl.when(ki == 0)
        def _():
            m_sc[...] = jnp.full(m_sc.shape, NEG_BIG, F32)
            l_sc[...] = jnp.zeros(l_sc.shape, F32)
            acc_sc[...] = jnp.zeros(acc_sc.shape, F32)

        def compute(masked):
            k_bf, v_bf = _bf(k_ref[0]), _bf(v_ref[0])
            for g in range(grp):
                _, s = _head_scores(q_ref, k_bf, g, ki, tk, ctx_len, masked)
                m_prev = m_sc[g]
                m_new = jnp.maximum(m_prev, jnp.max(s, axis=1, keepdims=True))
                alpha = jnp.exp(m_prev - m_new)
                p = jnp.exp(s - m_new)
                l_sc[g] = alpha * l_sc[g] + jnp.sum(p, axis=1, keepdims=True)
                acc_sc[g] = alpha * acc_sc[g] + _dot(p, v_bf)
                m_sc[g] = m_new

        _attn_cases(qi, ki, tq, tk, ctx_len, compute)

        @pl.when(ki == nkb - 1)
        def _():
            o_ref[0] = acc_sc[...] / l_sc[...]
            lse_ref[0] = m_sc[...] + jnp.log(l_sc[...])

    return pl.pallas_call(
        body, name="attn_fwd",
        grid=(kvh, t // tq, nkb),
        in_specs=[pl.BlockSpec((1, grp, tq, hd), lambda h, i, j: (h, 0, i, 0)),
                  pl.BlockSpec((1, tk, hd), lambda h, i, j: (h, j, 0)),
                  pl.BlockSpec((1, tk, hd), lambda h, i, j: (h, j, 0))],
        out_specs=[pl.BlockSpec((1, grp, tq, hd), lambda h, i, j: (h, 0, i, 0)),
                   pl.BlockSpec((1, grp, tq, 1), lambda h, i, j: (h, 0, i, 0))],
        out_shape=[jax.ShapeDtypeStruct(q.shape, F32), jax.ShapeDtypeStruct((kvh, grp, t, 1), F32)],
        scratch_shapes=[pltpu.VMEM((grp, tq, 1), F32), pltpu.VMEM((grp, tq, 1), F32), pltpu.VMEM((grp, tq, hd), F32)],
        compiler_params=_params("parallel", "parallel", "arbitrary"),
    )(q, k, v)


def _head_probs(q_ref, k_bf, v_bf, o_ref, do_ref, lse_ref, g, ki, tk, ctx_len, masked):
    q, s = _head_scores(q_ref, k_bf, g, ki, tk, ctx_len, masked)
    do = do_ref[0, g]
    delta = jnp.sum(do * o_ref[0, g], axis=1, keepdims=True)
    p = jnp.exp(s - lse_ref[0, g])
    do = _bf(do)
    ds = p * (_dot_nt(do, v_bf) - delta)
    return q, do, p, ds


def _attn_bwd_call(q, k, v, o, lse, do, ctx_len):
    kvh, grp, t, hd = q.shape
    tq, tk = _attn_tiles(t, ctx_len)
    nqb, nkb = t // tq, t // tk

    def body(q_ref, k_ref, v_ref, o_ref, lse_ref, do_ref, dq_hbm, dk_ref, dv_ref, dq_sc, dk_sc, dv_sc, dq_out,
             dq_sem):
        hi, ki, qi = pl.program_id(0), pl.program_id(1), pl.program_id(2)
        rows = pl.ds(pl.multiple_of(qi * tq, tq), tq)

        @pl.when(ki == 0)
        def _():
            dq_sc[:, rows, :] = jnp.zeros((grp, tq, hd), F32)

        @pl.when(qi == 0)
        def _():
            dk_sc[...] = jnp.zeros(dk_sc.shape, F32)
            dv_sc[...] = jnp.zeros(dv_sc.shape, F32)

        def compute(masked):
            k_bf, v_bf = _bf(k_ref[0]), _bf(v_ref[0])
            dk_part = jnp.zeros(dk_sc.shape, F32)
            dv_part = jnp.zeros(dv_sc.shape, F32)
            for g in range(grp):
                qs, dob, p, ds = _head_probs(q_ref, k_bf, v_bf, o_ref, do_ref, lse_ref, g, ki, tk, ctx_len, masked)
                dv_part = dv_part + _dot_tn(p, dob)
                dk_part = dk_part + _dot_tn(ds, qs)
                dq_sc[g, rows, :] += _dot(ds, k_bf)
            dk_sc[...] += dk_part
            dv_sc[...] += dv_part

        _attn_cases(qi, ki, tq, tk, ctx_len, compute)

        @pl.when(ki == nkb - 1)
        def _():
            dq_out[...] = dq_sc[:, rows, :] * (hd ** -0.5)
            done = pltpu.make_async_copy(dq_out, dq_hbm.at[hi, :, rows, :], dq_sem)
            done.start()
            done.wait()

        @pl.when(qi == nqb - 1)
        def _():
            dk_ref[0] = dk_sc[...]
            dv_ref[0] = dv_sc[...]

    qspec = pl.BlockSpec((1, grp, tq, hd), lambda h, j, i: (h, 0, i, 0))
    kspec = pl.BlockSpec((1, tk, hd), lambda h, j, i: (h, j, 0))
    return pl.pallas_call(
        body, name="attn_bwd",
        grid=(kvh, t // tk, nqb),
        in_specs=[qspec, kspec, kspec, qspec, pl.BlockSpec((1, grp, tq, 1), lambda h, j, i: (h, 0, i, 0)), qspec],
        out_specs=[pl.BlockSpec(memory_space=pl.ANY), kspec, kspec],
        out_shape=[jax.ShapeDtypeStruct(q.shape, F32), jax.ShapeDtypeStruct(k.shape, F32),
                   jax.ShapeDtypeStruct(v.shape, F32)],
        scratch_shapes=[pltpu.VMEM((grp, t, hd), F32), pltpu.VMEM((tk, hd), F32), pltpu.VMEM((tk, hd), F32),
                        pltpu.VMEM((grp, tq, hd), F32), pltpu.SemaphoreType.DMA],
        compiler_params=_params("arbitrary", "arbitrary", "arbitrary"),
    )(q, k, v, o, lse, do)


@functools.partial(jax.custom_vjp, nondiff_argnums=(3,))
def attention(q, k, v, ctx_len):
    return _attn_fwd_call(q, k, v, ctx_len)[0]


def _attention_fwd(q, k, v, ctx_len):
    o, lse = _attn_fwd_call(q, k, v, ctx_len)
    return o, (q, k, v, o, lse)


def _attention_bwd(ctx_len, res, do):
    q, k, v, o, lse = res
    return tuple(_attn_bwd_call(q, k, v, o, lse, do, ctx_len))


attention.defvjp(_attention_fwd, _attention_bwd)


def _chunk_order(step, n_chunks, n_ctx_chunks, reverse):
    if not reverse:
        return step
    return jnp.where(step < n_ctx_chunks, n_ctx_chunks - 1 - step, n_chunks + n_ctx_chunks - 1 - step)


def _scan_masks(chunk, reverse):
    row = lax.broadcasted_iota(jnp.int32, (chunk, chunk), 0)
    col = lax.broadcasted_iota(jnp.int32, (chunk, chunk), 1)
    vis = (col >= row) if reverse else (col <= row)
    vis_t = (row >= col) if reverse else (row <= col)
    return vis, vis.astype(F32), vis_t.astype(F32)


def _cum_decay(a_col, a_row, vis_f):
    hi = lax.Precision.HIGHEST
    cum_col = jnp.dot(vis_f, a_col, precision=hi, preferred_element_type=F32)
    cum_row = lax.dot_general(a_row, vis_f, (((1,), (1,)), ((), ())), precision=hi, preferred_element_type=F32)
    total = jnp.sum(a_col, axis=0, keepdims=True)
    return cum_col, cum_row, total


def _scan_specs(chunk, n_chunks, n_ctx_chunks, reverse, backward, widths):
    def order(i):
        step = (n_chunks - 1 - i) if backward else i
        return _chunk_order(step, n_chunks, n_ctx_chunks, reverse)

    return [pl.BlockSpec((chunk, w), lambda i: (order(i), 0)) for w in widths], order


def _scan_fwd_call(q, k, v, a_col, a_row, *, groups, per_group, dk, dv, ctx_len, reverse):
    t = q.shape[0]
    chunk = SCAN_CHUNK
    n_chunks, n_ctx = t // chunk, ctx_len // chunk
    heads = groups * per_group
    (q_spec, k_spec, v_spec, acol_spec), order = _scan_specs(
        chunk, n_chunks, n_ctx, reverse, False, (groups * dk, groups * dk, heads * dv, LANES))

    def body(q_ref, k_ref, v_ref, acol_ref, arow_ref, y_ref, st_ref, s_sc):
        @pl.when(pl.program_id(0) == 0)
        def _():
            s_sc[...] = jnp.zeros(s_sc.shape, F32)

        st_ref[0] = s_sc[...]
        vis, vis_f, _ = _scan_masks(chunk, reverse)
        cum_col, cum_row, total = _cum_decay(acol_ref[...], arow_ref[...], vis_f)
        for g in range(groups):
            qg = q_ref[:, g * dk:(g + 1) * dk]
            kg = k_ref[:, g * dk:(g + 1) * dk]
            qk = _dot_nt(qg, kg)
            for r in range(per_group):
                h = g * per_group + r
                ccol = cum_col[:, h:h + 1]
                decay = jnp.exp(jnp.where(vis, ccol - cum_row[h:h + 1, :], NEG_BIG))
                vh = v_ref[:, h * dv:(h + 1) * dv]
                s_in = s_sc[h]
                y = _dot(qk * decay, vh) + jnp.exp(ccol) * _dot(qg, s_in)
                y_ref[:, h * dv:(h + 1) * dv] = y
                tot = total[:, h:h + 1]
                s_sc[h] = jnp.exp(tot) * s_in + _dot_tn(kg * jnp.exp(tot - ccol), vh)

    return pl.pallas_call(
        body, name="scan_fwd",
        grid=(n_chunks,),
        in_specs=[q_spec, k_spec, v_spec, acol_spec, pl.BlockSpec((8, chunk), lambda i: (0, order(i)))],
        out_specs=[v_spec, pl.BlockSpec((1, heads, dk, dv), lambda i: (order(i), 0, 0, 0))],
        out_shape=[jax.ShapeDtypeStruct(v.shape, F32), jax.ShapeDtypeStruct((n_chunks, heads, dk, dv), F32)],
        scratch_shapes=[pltpu.VMEM((heads, dk, dv), F32)],
        compiler_params=_params("arbitrary"),
    )(q, k, v, a_col, a_row)


def _scan_bwd_call(q, k, v, a_col, a_row, states, dy, *, groups, per_group, dk, dv, ctx_len, reverse):
    t = q.shape[0]
    chunk = SCAN_CHUNK
    n_chunks, n_ctx = t // chunk, ctx_len // chunk
    heads = groups * per_group
    (q_spec, k_spec, v_spec, acol_spec), order = _scan_specs(
        chunk, n_chunks, n_ctx, reverse, True, (groups * dk, groups * dk, heads * dv, LANES))
    arow_spec = pl.BlockSpec((8, chunk), lambda i: (0, order(i)))
    last = 0 if reverse else chunk - 1

    def body(q_ref, k_ref, v_ref, acol_ref, arow_ref, st_ref, dy_ref, dq_ref, dk_ref, dv_ref, da_ref, dat_ref,
             ds_sc):
        @pl.when(pl.program_id(0) == 0)
        def _():
            ds_sc[...] = jnp.zeros(ds_sc.shape, F32)

        vis, vis_f, vis_tf = _scan_masks(chunk, reverse)
        cum_col, cum_row, total = _cum_decay(acol_ref[...], arow_ref[...], vis_f)
        lane = lax.broadcasted_iota(jnp.int32, (chunk, LANES), 1)
        row = lax.broadcasted_iota(jnp.int32, (chunk, LANES), 0)
        sub = lax.broadcasted_iota(jnp.int32, (8, chunk), 0)
        dcum = jnp.zeros((chunk, LANES), F32)
        dcum_t = jnp.zeros((8, chunk), F32)
        for g in range(groups):
            qg = q_ref[:, g * dk:(g + 1) * dk]
            kg = k_ref[:, g * dk:(g + 1) * dk]
            qk = _dot_nt(qg, kg)
            dq_g = jnp.zeros((chunk, dk), F32)
            dk_g = jnp.zeros((chunk, dk), F32)
            for r in range(per_group):
                h = g * per_group + r
                ccol = cum_col[:, h:h + 1]
                decay = jnp.exp(jnp.where(vis, ccol - cum_row[h:h + 1, :], NEG_BIG))
                vh = v_ref[:, h * dv:(h + 1) * dv]
                dyh = dy_ref[:, h * dv:(h + 1) * dv]
                s_in = st_ref[0, h]
                ds_out = ds_sc[h]
                tot = total[:, h:h + 1]
                e_in = jnp.exp(ccol)
                e_out = jnp.exp(tot - ccol)
                e_tot = jnp.exp(tot)
                k_out = kg * e_out
                dv_ref[:, h * dv:(h + 1) * dv] = _dot_tn(qk * decay, dyh) + _dot(k_out, ds_out)
                dqk = _dot_nt(dyh, vh) * decay
                dq_in = e_in * _dot_nt(dyh, s_in)
                dk_out = e_out * _dot_nt(vh, ds_out)
                dq_h = _dot(dqk, kg) + dq_in
                dk_h = _dot_tn(dqk, qg) + dk_out
                s_out = e_tot * s_in + _dot_tn(k_out, vh)
                edge = jnp.sum(jnp.sum(s_out * ds_out, axis=1, keepdims=True), axis=0, keepdims=True)
                w_seg = dqk * qk
                dcum_h = (jnp.sum(w_seg, axis=1, keepdims=True) + jnp.sum(dq_in * qg, axis=1, keepdims=True)
                          - jnp.sum(dk_out * kg, axis=1, keepdims=True))
                dcum = jnp.where(lane == h, dcum_h + jnp.where(row == last, edge, 0.0), dcum)
                dcum_t = jnp.where(sub == h, -jnp.sum(w_seg, axis=0, keepdims=True), dcum_t)
                ds_sc[h] = e_tot * ds_out + _dot_tn(qg, e_in * dyh)
                dq_g = dq_g + dq_h
                dk_g = dk_g + dk_h
            dq_ref[:, g * dk:(g + 1) * dk] = dq_g
            dk_ref[:, g * dk:(g + 1) * dk] = dk_g
        hi = lax.Precision.HIGHEST
        da_ref[...] = jnp.dot(vis_tf, dcum, precision=hi, preferred_element_type=F32)
        dat_ref[...] = jnp.dot(dcum_t, vis_f, precision=hi, preferred_element_type=F32)

    return pl.pallas_call(
        body, name="scan_bwd",
        grid=(n_chunks,),
        in_specs=[q_spec, k_spec, v_spec, acol_spec, arow_spec,
                  pl.BlockSpec((1, heads, dk, dv), lambda i: (order(i), 0, 0, 0)), v_spec],
        out_specs=[q_spec, k_spec, v_spec, acol_spec, arow_spec],
        out_shape=[jax.ShapeDtypeStruct(q.shape, F32), jax.ShapeDtypeStruct(k.shape, F32),
                   jax.ShapeDtypeStruct(v.shape, F32), jax.ShapeDtypeStruct((t, LANES), F32),
                   jax.ShapeDtypeStruct((8, t), F32)],
        scratch_shapes=[pltpu.VMEM((heads, dk, dv), F32)],
        compiler_params=_params("arbitrary"),
    )(q, k, v, a_col, a_row, states, dy)


def _decay_layouts(a):
    t, heads = a.shape
    a_col = jnp.pad(a, ((0, 0), (0, LANES - heads)))
    a_row = jnp.pad(a.T, ((0, 8 - heads), (0, 0)))
    return a_col, a_row


@functools.partial(jax.custom_vjp, nondiff_argnums=(4,))
def linear_scan(q, k, v, a, cfg):
    a_col, a_row = _decay_layouts(a)
    return _scan_fwd_call(q, k, v, a_col, a_row, **dict(cfg))[0]


def _linear_scan_fwd(q, k, v, a, cfg):
    a_col, a_row = _decay_layouts(a)
    y, states = _scan_fwd_call(q, k, v, a_col, a_row, **dict(cfg))
    return y, (q, k, v, a, states)


def _linear_scan_bwd(cfg, res, dy):
    q, k, v, a, states = res
    a_col, a_row = _decay_layouts(a)
    dq, dk, dv, da, da_t = _scan_bwd_call(q, k, v, a_col, a_row, states, dy, **dict(cfg))
    heads = a.shape[1]
    return dq, dk, dv, da[:, :heads] + da_t[:heads].T


linear_scan.defvjp(_linear_scan_fwd, _linear_scan_bwd)


def _scan_cfg(groups, per_group, dk, dv, ctx_len, reverse):
    return (("groups", groups), ("per_group", per_group), ("dk", dk), ("dv", dv), ("ctx_len", ctx_len),
            ("reverse", reverse))


def _rms_norm(x, w):
    return x * lax.rsqrt(jnp.mean(x * x, axis=-1, keepdims=True) + NORM_EPS) * w


def _rope(x, cos, sin):
    half = x.shape[-1] // 2
    x1, x2 = x[..., :half], x[..., half:]
    cs, sn = cos[:, None, :], sin[:, None, :]
    return jnp.concatenate([x1 * cs - x2 * sn, x1 * sn + x2 * cs], axis=-1)


def _axial_tables(n_lat, n_ctx):
    freqs = ATTN_HEAD_DIM // 4
    rows = n_lat // GRID_W
    row = jnp.repeat(jnp.arange(rows, dtype=F32), GRID_W)
    col = jnp.tile(jnp.arange(GRID_W, dtype=F32), rows)
    inv = ROPE_THETA ** (-jnp.arange(freqs, dtype=F32) / freqs)
    ang = jnp.concatenate([row[:, None] * inv, col[:, None] * inv], axis=-1)
    cos = jnp.concatenate([jnp.ones((n_ctx, 2 * freqs), F32), jnp.cos(ang)], axis=0)
    sin = jnp.concatenate([jnp.zeros((n_ctx, 2 * freqs), F32), jnp.sin(ang)], axis=0)
    return cos, sin


def _seq_tables(t):
    pos = jnp.arange(t, dtype=F32)
    inv = ROPE_THETA ** (-jnp.linspace(0.0, 1.0, RET_DK // 2, dtype=F32))
    ang = pos[:, None] * inv
    return jnp.cos(ang), jnp.sin(ang)


def _pad_w_in(w_in):
    d = w_in.shape[0]
    return jnp.concatenate([w_in[:, :DT_END], jnp.zeros((d, DT_PAD), w_in.dtype), w_in[:, DT_END:],
                            jnp.zeros((d, TAIL_PAD), w_in.dtype)], axis=1)


def _split_proj(p):
    widths = list(IN_SPLITS)
    widths[5] = LANES
    out, off = [], 0
    for w in widths:
        out.append(p[:, off:off + w])
        off += w
    out[5] = out[5][:, :DT_COLS]
    return out


def _mixer(u, w, wq, layer, n_ctx, tables):
    t = u.shape[0]
    attn_rope, ret_rope, seg_first, seg_last = tables
    proj = matmul(u, wq["w_in"][layer], _pad_w_in(w["w_in"][layer]))
    aq, ak, av, z, xbc_raw, dt_raw, rq, rk, rv, rg, gate_logits = _split_proj(proj)

    q = _rope(_rms_norm(aq.reshape(t, ATTN_HEADS, ATTN_HEAD_DIM), w["attn_q_norm"][layer]), *attn_rope)
    k = _rope(_rms_norm(ak.reshape(t, ATTN_KV_HEADS, ATTN_HEAD_DIM), w["attn_k_norm"][layer]), *attn_rope)
    q4 = q.reshape(t, ATTN_KV_HEADS, ATTN_GROUP, ATTN_HEAD_DIM).transpose(1, 2, 0, 3)
    k3 = k.transpose(1, 0, 2)
    v3 = av.reshape(t, ATTN_KV_HEADS, ATTN_HEAD_DIM).transpose(1, 0, 2)
    o4 = attention(q4, k3, v3, n_ctx)
    br_attn = o4.transpose(2, 0, 1, 3).reshape(t, ATTN_HEADS * ATTN_HEAD_DIM)

    cw, cb = w["ssd_conv_w"][layer], w["ssd_conv_b"][layer]
    zero_row = jnp.zeros((1, xbc_raw.shape[1]), F32)
    prev = jnp.concatenate([zero_row, xbc_raw[:-1]], axis=0) * (1.0 - seg_first)
    nxt = jnp.concatenate([xbc_raw[1:], zero_row], axis=0) * (1.0 - seg_last)
    xbc = jax.nn.silu(prev * cw[0] + xbc_raw * cw[1] + nxt * cw[2] + cb)
    gn = SSD_GROUPS * SSD_STATE
    xs = xbc[:, :SSD_D_INNER]
    bm = xbc[:, SSD_D_INNER:SSD_D_INNER + gn]
    cm = xbc[:, SSD_D_INNER + gn:]
    dt = jax.nn.softplus(dt_raw.reshape(t, 2, SSD_HEADS) + w["ssd_dt_bias"][layer])
    a_neg = -jnp.exp(w["ssd_a_log"][layer])
    xs_h = xs.reshape(t, SSD_HEADS, SSD_HEAD_DIM)
    y_ssd = []
    for d, reverse in ((0, False), (1, True)):
        dtd = dt[:, d]
        cfg = _scan_cfg(SSD_GROUPS, SSD_HEADS // SSD_GROUPS, SSD_STATE, SSD_HEAD_DIM, n_ctx, reverse)
        y_ssd.append(linear_scan(cm, bm, (xs_h * dtd[:, :, None]).reshape(t, SSD_D_INNER), dtd * a_neg[d], cfg))
    d_skip = jnp.repeat(w["ssd_d"][layer], SSD_HEAD_DIM)[None, :]
    br_ssd = _rowwise("ssd_finish", _ssd_finish_tile, SSD_D_INNER, n_ctx)(
        (d_skip, w["ssd_norm_w"][layer][None, :]), (y_ssd[0], y_ssd[1], xs, z))

    lg = -jnp.exp(w["ret_log_decay"][layer])
    rq_r = _rope(rq.reshape(t, RET_HEADS, RET_DK), *ret_rope).reshape(t, RET_HEADS * RET_DK)
    rk_r = (_rope(rk.reshape(t, RET_HEADS, RET_DK), *ret_rope) * (RET_DK ** -0.5)).reshape(t, RET_HEADS * RET_DK)
    y_ret = []
    for d, reverse in ((0, False), (1, True)):
        cfg = _scan_cfg(RET_HEADS, 1, RET_DK, RET_DV, n_ctx, reverse)
        y_ret.append(linear_scan(rq_r, rk_r, rv, jnp.broadcast_to(lg[d][None, :], (t, RET_HEADS)), cfg))
    br_ret = _rowwise("ret_finish", _ret_finish_tile, RET_HEADS * RET_DV, n_ctx)(
        (w["ret_gn_w"][layer][None, :],), (y_ret[0], y_ret[1], rg))

    projected = tuple(matmul(br, wq["w_branch"][layer][j], w["w_branch"][layer][j])
                      for j, br in enumerate((br_attn, br_ssd, br_ret)))
    merged = _rowwise("gate_merge", _gate_merge_tile, D_MODEL, n_ctx)((), projected + (gate_logits,))
    return matmul(merged, wq["w_out"][layer], w["w_out"][layer])


def _local_loss(w, x, c, ctx, target, wq):
    n, m = x.shape[0], ctx.shape[0]
    t = n + m
    pos = jnp.arange(t)[:, None]
    seg_first = ((pos == 0) | (pos == m)).astype(F32)
    seg_last = ((pos == m - 1) | (pos == t - 1)).astype(F32)
    tables = (_axial_tables(n, m), _seq_tables(t), seg_first, seg_last)
    h = jnp.concatenate([ctx, x], axis=0)
    cond = jax.nn.silu(jnp.stack([c, w["c_ctx"]], axis=0))
    cond8 = jnp.concatenate([cond, jnp.zeros((6, D_MODEL), F32)], axis=0)
    for layer in range(DEPTH):
        mod = (matmul(cond8, wq["w_mod"][layer], w["w_mod"][layer])[:2] + w["b_mod"][layer]).reshape(2, 6, D_MODEL)

        u = norm_mod(h, w["norm1_w"][layer], mod[:, 0], mod[:, 1], m)
        residual = _rowwise("gated_residual", _gated_residual_tile, D_MODEL, m)
        h = residual((mod[:, 2],), (h, _mixer(u, w, wq, layer, m, tables)))
        v = norm_mod(h, w["norm2_w"][layer], mod[:, 3], mod[:, 4], m)
        mlp = sq_relu_mlp(v, wq["w_mlp1"][layer], w["w_mlp1"][layer], wq["w_mlp2"][layer], w["w_mlp2"][layer])
        residual = _rowwise("gated_residual", _gated_residual_tile, D_MODEL, m)
        h = residual((mod[:, 5],), (h, mlp))
    per_token = _rowwise("loss_rows", _loss_rows_tile, LANES, 0)((w["final_norm_w"][None, :],), (h[m:], target))
    return 0.5 * jnp.sum(per_token[:, 0])


def _coords():
    return lax.axis_index("x"), lax.axis_index("y"), lax.axis_index("c")


def _all_gather(blocks, name):
    n = len(blocks)

    def body(*refs):
        x_refs, out_refs = refs[:n], refs[n:2 * n]
        send_sems, recv_sems, local_sems = refs[2 * n:]
        x, y, c = _coords()
        me, sibling = (x, y, c), (x, y, 1 - c)
        chips = [(1 - x, y), (x, 1 - y), (1 - x, 1 - y)]

        def copy(k, i, blk, to, from_input=False):
            slot = out_refs[i].at[4 * blk[0] + 2 * blk[1] + blk[2]]
            return pltpu.make_async_remote_copy(
                src_ref=x_refs[i] if from_input else slot, dst_ref=slot,
                send_sem=send_sems.at[k * n + i], recv_sem=recv_sems.at[k * n + i],
                device_id=to, device_id_type=pl.DeviceIdType.MESH)

        mine = [pltpu.make_async_copy(x_refs[i], out_refs[i].at[4 * x + 2 * y + c], local_sems.at[i])
                for i in range(n)]
        for cp in mine:
            cp.start()
        first = [copy(0, i, me, sibling, True) for i in range(n)]
        first += [copy(1 + j, i, me, (*chip, c), True) for j, chip in enumerate(chips) for i in range(n)]
        for cp in first:
            cp.start()
        passed = []
        for j, chip in enumerate(chips):
            for i in range(n):
                copy(1 + j, i, (*chip, c), me).wait_recv()
                passed.append(copy(4 + j, i, (*chip, c), sibling))
                passed[-1].start()
        for i in range(n):
            copy(0, i, sibling, me).wait_recv()
        for j, chip in enumerate(chips):
            for i in range(n):
                copy(4 + j, i, (*chip, 1 - c), me).wait_recv()
        for cp in first + passed:
            cp.wait_send()
        for cp in mine:
            cp.wait()

    return pl.pallas_call(
        body, name=name,
        out_shape=[jax.ShapeDtypeStruct((N_DEV,) + b.shape, b.dtype) for b in blocks],
        in_specs=[pl.BlockSpec(memory_space=pl.ANY)] * n,
        out_specs=[pl.BlockSpec(memory_space=pl.ANY)] * n,
        scratch_shapes=[pltpu.SemaphoreType.DMA((7 * n,)), pltpu.SemaphoreType.DMA((7 * n,)),
                        pltpu.SemaphoreType.DMA((n,))],
    )(*blocks)


def _all_to_all(arrays, name):
    n = len(arrays)

    def body(*refs):
        g_refs, out_refs = refs[:n], refs[n:2 * n]
        send_sems, recv_sems, local_sems = refs[2 * n:]
        x, y, c = _coords()
        me = 4 * x + 2 * y + c
        mine = [pltpu.make_async_copy(g_refs[i].at[me], out_refs[i].at[me], local_sems.at[i]) for i in range(n)]
        for cp in mine:
            cp.start()
        copies = []
        for k in range(1, N_DEV):
            bx, by, bc = (k >> 2) & 1, (k >> 1) & 1, k & 1
            px, py, pc = (1 - x if bx else x), (1 - y if by else y), (1 - c if bc else c)
            peer = 4 * px + 2 * py + pc
            for i in range(n):
                copies.append(pltpu.make_async_remote_copy(
                    src_ref=g_refs[i].at[peer], dst_ref=out_refs[i].at[me],
                    send_sem=send_sems.at[(k - 1) * n + i], recv_sem=recv_sems.at[(k - 1) * n + i],
                    device_id=(px, py, pc), device_id_type=pl.DeviceIdType.MESH))
        for cp in copies:
            cp.start()
        for cp in copies:
            cp.wait_recv()
        for cp in copies:
            cp.wait_send()
        for cp in mine:
            cp.wait()

    return pl.pallas_call(
        body, name=name,
        out_shape=[jax.ShapeDtypeStruct(a.shape, a.dtype) for a in arrays],
        in_specs=[pl.BlockSpec(memory_space=pl.ANY)] * n,
        out_specs=[pl.BlockSpec(memory_space=pl.ANY)] * n,
        scratch_shapes=[pltpu.SemaphoreType.DMA((7 * n,)), pltpu.SemaphoreType.DMA((7 * n,)),
                        pltpu.SemaphoreType.DMA((n,))],
    )(*arrays)


ADAMW_BLOCK_ELEMS = 256 * 1024


def _sum_adamw(g8, w, m, v, name):
    rows, cols = w.shape
    tr = _tile(rows, [r for r in (2048, 1024, 512, 256, 128, 64, 32, 16) if r * cols <= ADAMW_BLOCK_ELEMS])

    def body(g_ref, w_ref, m_ref, v_ref, go_ref, d_ref, mo_ref, vo_ref):
        g = g_ref[0].astype(F32)
        for s in range(1, N_DEV):
            g = g + g_ref[s].astype(F32)
        m_new = ADAM_B1 * m_ref[...] + (1.0 - ADAM_B1) * g
        v_new = ADAM_B2 * v_ref[...] + (1.0 - ADAM_B2) * (g * g)
        m_hat = m_new / (1.0 - ADAM_B1 ** ADAM_STEP)
        v_hat = v_new / (1.0 - ADAM_B2 ** ADAM_STEP)
        go_ref[...] = g
        d_ref[...] = -ADAM_LR * (m_hat / (jnp.sqrt(v_hat) + ADAM_EPS) + ADAM_WD * w_ref[...])
        mo_ref[...] = m_new
        vo_ref[...] = v_new

    spec = pl.BlockSpec((tr, cols), lambda i: (i, 0))
    shape = jax.ShapeDtypeStruct((rows, cols), F32)
    return pl.pallas_call(
        body, name=name,
        grid=(rows // tr,),
        in_specs=[pl.BlockSpec((N_DEV, tr, cols), lambda i: (0, i, 0)), spec, spec, spec],
        out_specs=[spec, spec, spec, spec],
        out_shape=[shape, shape, shape, shape],
        compiler_params=_params("parallel"),
    )(g8, w, m, v)


BIG = ("w_mod", "w_in", "w_branch", "w_out", "w_mlp1", "w_mlp2")
COL_SHARDED = ("w_mod", "w_mlp1")
ROW_SHARDED = ("w_out", "w_mlp2")
SMALL = ("c_ctx", "b_mod", "norm1_w", "norm2_w", "attn_q_norm", "attn_k_norm", "ssd_conv_b", "ssd_dt_bias",
         "ssd_a_log", "ssd_d", "ssd_norm_w", "ret_log_decay", "ret_gn_w", "final_norm_w")
CONV_AXIS = 2
ORDER = ("c_ctx", "w_mod", "b_mod", "norm1_w", "norm2_w", "w_in", "attn_q_norm", "attn_k_norm", "ssd_conv_w",
         "ssd_conv_b", "ssd_dt_bias", "ssd_a_log", "ssd_d", "ssd_norm_w", "ret_log_decay", "ret_gn_w", "w_branch",
         "w_out", "w_mlp1", "w_mlp2", "final_norm_w")


def _compute_weights(gathered):
    wq, carrier = {}, {}
    for name in BIG:
        g = gathered[name]
        per_layer = []
        for layer in range(DEPTH):
            gl = g[:, layer]
            if name in COL_SHARDED:
                per_layer.append(gl)
            elif name in ROW_SHARDED:
                per_layer.append(gl.reshape(-1, gl.shape[-1]))
            elif name == "w_in":
                per_layer.append(_pad_w_in(jnp.concatenate([gl[d] for d in range(N_DEV)], axis=-1)))
            else:
                per_layer.append([jnp.concatenate([gl[d, j] for d in range(N_DEV)], axis=-1) for j in range(N_BRANCH)])
        wq[name] = per_layer
    for name in BIG:
        if name == "w_in":
            carrier[name] = [jnp.zeros((D_MODEL, IN_DIM), F32) for _ in range(DEPTH)]
        else:
            carrier[name] = jax.tree.map(lambda a: jnp.zeros(a.shape, F32), wq[name])
    return wq, carrier


def _grad_shards(gw):
    out = {}
    for name in BIG:
        per_layer = []
        for layer in range(DEPTH):
            g = gw[name][layer]
            if name in COL_SHARDED:
                per_layer.append(g)
            elif name in ROW_SHARDED:
                per_layer.append(g.reshape(N_DEV, -1, g.shape[-1]))
            elif name == "w_in":
                size = IN_DIM // N_DEV
                per_layer.append(jnp.stack([g[:, d * size:(d + 1) * size] for d in range(N_DEV)]))
            else:
                per_layer.append(jnp.stack([gj.reshape(gj.shape[0], N_DEV, -1).transpose(1, 0, 2) for gj in g], axis=1))
        out[name] = jnp.stack(per_layer, axis=1)
    return out


def _pack(arrays, row_multiple):
    flat = jnp.concatenate(arrays, axis=-1)
    n = flat.shape[-1]
    per = LANES * row_multiple
    padded = -(-n // per) * per
    flat = jnp.pad(flat, [(0, 0)] * (flat.ndim - 1) + [(0, padded - n)])
    return flat.reshape(flat.shape[:-1] + (padded // LANES, LANES))


def _unpack(slab, shapes):
    flat = slab.reshape(slab.shape[:-2] + (-1,))
    out, off = [], 0
    for shp in shapes:
        size = math.prod(shp)
        out.append(flat[..., off:off + size].reshape(flat.shape[:-1] + tuple(shp)))
        off += size
    return out


def kernel(x, c, ctx, c_ctx, w_mod, b_mod, norm1_w, norm2_w, w_in, attn_q_norm, attn_k_norm, ssd_conv_w, ssd_conv_b, ssd_dt_bias, ssd_a_log, ssd_d, ssd_norm_w, ret_log_decay, ret_gn_w, w_branch, w_out, w_mlp1, w_mlp2, final_norm_w, loss_target, m_c_ctx, m_w_mod, m_b_mod, m_norm1_w, m_norm2_w, m_w_in, m_attn_q_norm, m_attn_k_norm, m_ssd_conv_w, m_ssd_conv_b, m_ssd_dt_bias, m_ssd_a_log, m_ssd_d, m_ssd_norm_w, m_ret_log_decay, m_ret_gn_w, m_w_branch, m_w_out, m_w_mlp1, m_w_mlp2, m_final_norm_w, v_c_ctx, v_w_mod, v_b_mod, v_norm1_w, v_norm2_w, v_w_in, v_attn_q_norm, v_attn_k_norm, v_ssd_conv_w, v_ssd_conv_b, v_ssd_dt_bias, v_ssd_a_log, v_ssd_d, v_ssd_norm_w, v_ret_log_decay, v_ret_gn_w, v_w_branch, v_w_out, v_w_mlp1, v_w_mlp2, v_final_norm_w):
    args = dict(locals())
    weights = {n: args[n] for n in ORDER}
    mom1 = {n: args["m_" + n] for n in ORDER}
    mom2 = {n: args["v_" + n] for n in ORDER}
    me = 4 * lax.axis_index("x") + 2 * lax.axis_index("y") + lax.axis_index("c")

    gathered = _all_gather([weights[n].astype(MXU_DTYPE) for n in BIG], "gather_weights")
    wq, params = _compute_weights(dict(zip(BIG, gathered)))
    conv_shape = ssd_conv_w.shape
    conv_all = _all_gather([_pack([ssd_conv_w.reshape(-1)], 8)], "gather_conv")[0]
    params["ssd_conv_w"] = jnp.concatenate(list(_unpack(conv_all, [conv_shape])[0]), axis=CONV_AXIS)
    for n in SMALL:
        params[n] = weights[n]

    loss, (gw, gx) = jax.value_and_grad(_local_loss, argnums=(0, 1))(params, x[0], c[0], ctx[0], loss_target[0], wq)
    loss = lax.psum(loss, MESH_AXES)

    g_send = _grad_shards(gw)
    g_recv = _all_to_all([g_send[n].astype(jnp.bfloat16) for n in BIG], "scatter_grads")
    result = {}
    for n, g8 in zip(BIG, g_recv):
        shape = weights[n].shape
        as2d = lambda a: a.reshape(-1, shape[-1])
        outs = _sum_adamw(g8.reshape(N_DEV, -1, shape[-1]), as2d(weights[n]), as2d(mom1[n]), as2d(mom2[n]), "adamw_" + n)
        for kind, arr in zip(("grad", "delta", "new_m", "new_v"), outs):
            result[kind, n] = arr.reshape(shape)

    conv_full_shape = params["ssd_conv_w"].shape
    small_shapes = [weights[n].shape for n in SMALL]
    partial = _pack([gw[n].reshape(-1) for n in SMALL] + [gw["ssd_conv_w"].reshape(-1)], 8)
    parts = _unpack(_all_gather([partial], "gather_small_grads")[0], small_shapes + [conv_full_shape])
    conv_part = lax.dynamic_slice_in_dim(parts[-1], me * conv_shape[CONV_AXIS], conv_shape[CONV_AXIS], CONV_AXIS + 1)
    small_names = list(SMALL) + ["ssd_conv_w"]
    g8_small = _pack([p.reshape(N_DEV, -1) for p in parts[:-1]] + [conv_part.reshape(N_DEV, -1)], 8)
    slabs = [_pack([d[n].reshape(-1) for n in small_names], 8) for d in (weights, mom1, mom2)]
    small_out = [_unpack(s, small_shapes + [conv_shape]) for s in _sum_adamw(g8_small, *slabs, "adamw_small")]
    for kind, small_k in zip(("grad", "delta", "new_m", "new_v"), small_out):
        for n, arr in zip(small_names, small_k):
            result[kind, n] = arr

    outs = [loss, gx[None]]
    for kind in ("grad", "delta", "new_m", "new_v"):
        outs += [result[kind, n] for n in ORDER]
    return tuple(outs)
```

```python
import functools
import math

import jax
import jax.numpy as jnp
from jax import lax
from jax.experimental import pallas as pl
from jax.experimental.pallas import tpu as pltpu

F32 = jnp.float32
MXU_DTYPE = jnp.bfloat16
VMEM_LIMIT_BYTES = 48 * 1024 * 1024
LANES = 128
N_DEV = 8
MESH_AXES = ("x", "y", "c")

D_MODEL = 1024
GRID_W = 64
NORM_EPS = 1e-6
ROPE_THETA = 10000.0
ATTN_HEADS, ATTN_KV_HEADS, ATTN_HEAD_DIM = 8, 2, 64
ATTN_GROUP = ATTN_HEADS // ATTN_KV_HEADS
SSD_HEADS, SSD_HEAD_DIM, SSD_GROUPS, SSD_STATE = 8, 64, 2, 128
SSD_D_INNER = SSD_HEADS * SSD_HEAD_DIM
RET_HEADS, RET_DK, RET_DV = 4, 128, 128
SCAN_CHUNK = 128
N_BRANCH = 3
DEPTH = 2

IN_SPLITS = (512, 128, 128, 512, 1024, 16, 512, 512, 512, 512, 3072)
IN_DIM = sum(IN_SPLITS)
DT_COLS = 16
DT_PAD = LANES - DT_COLS
TAIL_PAD = 128
IN_DIM_PADDED = IN_DIM + DT_PAD + TAIL_PAD
DT_END = sum(IN_SPLITS[:6])

ADAM_LR, ADAM_B1, ADAM_B2, ADAM_EPS, ADAM_WD, ADAM_STEP = 0.001, 0.9, 0.999, 1e-08, 0.01, 10


def _tile(dim, prefs):
    for p in prefs:
        if dim % p == 0:
            return p
    return dim


def _params(*sem):
    return pltpu.CompilerParams(dimension_semantics=sem, vmem_limit_bytes=VMEM_LIMIT_BYTES)


def _mm(a, b, *, ta=False, tb=False, out_shards=False, epilogue=None, extra=None, name):
    if ta:
        kdim, m = a.shape
    else:
        m, kdim = a.shape
    b_shards = b.ndim == 3
    if b_shards:
        rows_b, cols_b = b.shape[1], N_DEV * b.shape[2]
    else:
        rows_b, cols_b = b.shape
    n, kdim_b = (rows_b, cols_b) if tb else (cols_b, rows_b)
    assert kdim == kdim_b, (a.shape, b.shape, ta, tb)
    tm = _tile(m, (1024, 768, 512, 256, 128))
    n_tile_of = n // N_DEV if (out_shards or (b_shards and not tb)) else n
    k_tile_of = kdim // N_DEV if (b_shards and tb) else kdim
    tn = _tile(n_tile_of, (1280, 1024, 768, 512, 384, 256, 128))
    tk = _tile(k_tile_of, (1024, 768, 512, 256, 128))
    dims = (((0 if ta else 1,), (1 if tb else 0,)), ((), ()))

    n_k = kdim // tk

    def body(a_ref, b_ref, *rest):
        o_ref = rest[-1]
        part = lax.dot_general(a_ref[...].astype(MXU_DTYPE), b_ref[...].astype(MXU_DTYPE), dims,
                               preferred_element_type=F32)

        @pl.when(pl.program_id(2) == 0)
        def _():
            o_ref[...] = part

        @pl.when(pl.program_id(2) > 0)
        def _():
            o_ref[...] += part

        if epilogue is not None:
            @pl.when(pl.program_id(2) == n_k - 1)
            def _():
                acc = o_ref[...]
                if epilogue == "sq_relu":
                    r = jnp.maximum(acc, 0.0)
                    o_ref[...] = r * r
                else:
                    o_ref[...] = acc * (2.0 * jnp.sqrt(rest[0][...]))

    a_spec = pl.BlockSpec((tk, tm), lambda i, j, k: (k, i)) if ta else pl.BlockSpec((tm, tk), lambda i, j, k: (i, k))
    if not b_shards:
        b_spec = pl.BlockSpec((tn, tk), lambda i, j, k: (j, k)) if tb else pl.BlockSpec((tk, tn), lambda i, j, k: (k, j))
    elif tb:
        per = b.shape[2] // tk
        b_spec = pl.BlockSpec((None, tn, tk), lambda i, j, k: (k // per, j, k % per))
    else:
        per = b.shape[2] // tn
        b_spec = pl.BlockSpec((None, tk, tn), lambda i, j, k: (j // per, k, j % per))
    if out_shards:
        per_out = n // N_DEV // tn
        out_spec = pl.BlockSpec((None, tm, tn), lambda i, j, k: (j // per_out, i, j % per_out))
        out_shape = jax.ShapeDtypeStruct((N_DEV, m, n // N_DEV), F32)
    else:
        out_spec = pl.BlockSpec((tm, tn), lambda i, j, k: (i, j))
        out_shape = jax.ShapeDtypeStruct((m, n), F32)
    assert epilogue in (None, "sq_relu", "d_sq_relu") and (extra is not None) == (epilogue == "d_sq_relu")
    operands, in_specs = [a, b], [a_spec, b_spec]
    if extra is not None:
        assert not out_shards and extra.shape == (m, n)
        operands.append(extra)
        in_specs.append(out_spec)
    return pl.pallas_call(
        body, name=name,
        grid=(m // tm, n // tn, n_k),
        in_specs=in_specs,
        out_specs=out_spec,
        out_shape=out_shape,
        compiler_params=_params("parallel", "parallel", "arbitrary"),
    )(*operands)


@jax.custom_vjp
def matmul(a, b, b_grad):
    return _mm(a, b, name="mm_fwd")


def _matmul_fwd(a, b, b_grad):
    return _mm(a, b, name="mm_fwd"), (a, b)


def _matmul_bwd(res, g):
    a, b = res
    dw = _mm(a, g, ta=True, out_shards=b.ndim == 3, name="mm_dw")
    return _mm(g, b, tb=True, name="mm_dx"), jnp.zeros_like(b), dw


matmul.defvjp(_matmul_fwd, _matmul_bwd)


@jax.custom_vjp
def sq_relu_mlp(x, w1, w1_grad, w2, w2_grad):
    return _mm(_mm(x, w1, epilogue="sq_relu", name="mlp_up"), w2, name="mlp_down")


def _sq_relu_mlp_fwd(x, w1, w1_grad, w2, w2_grad):
    hid = _mm(x, w1, epilogue="sq_relu", name="mlp_up")
    return _mm(hid, w2, name="mlp_down"), (x, w1, w2, hid)


def _sq_relu_mlp_bwd(res, g):
    x, w1, w2, hid = res
    d_pre = _mm(g, w2, tb=True, epilogue="d_sq_relu", extra=hid, name="mlp_down_dx")
    dw2 = _mm(hid, g, ta=True, out_shards=w2.ndim == 3, name="mlp_down_dw")
    dw1 = _mm(x, d_pre, ta=True, out_shards=w1.ndim == 3, name="mlp_up_dw")
    return _mm(d_pre, w1, tb=True, name="mlp_up_dx"), jnp.zeros_like(w1), dw1, jnp.zeros_like(w2), dw2


sq_relu_mlp.defvjp(_sq_relu_mlp_fwd, _sq_relu_mlp_bwd)


NORM_TILE_PREFS = (768, 512, 256, 128)


def _norm_mod_pieces(h_ref, w_ref, shift_ref, scale_ref, tile, tm, ctx_len):
    x = h_ref[...]
    rstd = lax.rsqrt(jnp.mean(x * x, axis=1, keepdims=True) + NORM_EPS)
    xn = x * rstd
    is_ctx = tile * tm + lax.broadcasted_iota(jnp.int32, (tm, 1), 0) < ctx_len
    scale = jnp.where(is_ctx, scale_ref[1:2, :], scale_ref[0:1, :])
    shift = jnp.where(is_ctx, shift_ref[1:2, :], shift_ref[0:1, :])
    return xn, rstd, is_ctx, scale, shift


def _norm_mod_fwd_call(h, w, shift, scale, ctx_len):
    t, d = h.shape
    tm = _tile(t, NORM_TILE_PREFS)

    def body(h_ref, w_ref, shift_ref, scale_ref, u_ref):
        xn, _, _, sc, sh = _norm_mod_pieces(h_ref, w_ref, shift_ref, scale_ref, pl.program_id(0), tm, ctx_len)
        u_ref[...] = xn * w_ref[...] * (1.0 + sc) + sh

    row = pl.BlockSpec((tm, d), lambda i: (i, 0))
    return pl.pallas_call(
        body, name="norm_mod_fwd",
        grid=(t // tm,),
        in_specs=[row, pl.BlockSpec((1, d), lambda i: (0, 0)), pl.BlockSpec((2, d), lambda i: (0, 0)),
                  pl.BlockSpec((2, d), lambda i: (0, 0))],
        out_specs=row,
        out_shape=jax.ShapeDtypeStruct((t, d), F32),
        compiler_params=_params("parallel"),
    )(h, w, shift, scale)


def _norm_mod_bwd_call(h, w, shift, scale, du, ctx_len):
    t, d = h.shape
    tm = _tile(t, NORM_TILE_PREFS)

    def body(h_ref, w_ref, shift_ref, scale_ref, du_ref, dh_ref, sums_ref):
        xn, rstd, is_ctx, sc, _ = _norm_mod_pieces(h_ref, w_ref, shift_ref, scale_ref, pl.program_id(0), tm, ctx_len)
        du = du_ref[...]
        wv = w_ref[...]
        dy = du * (1.0 + sc)
        dxn = dy * wv
        dh_ref[...] = rstd * (dxn - xn * jnp.mean(dxn * xn, axis=1, keepdims=True))
        dsc = du * (xn * wv)

        def colsum(v):
            return jnp.sum(v, axis=0, keepdims=True)

        dshift_all, dshift_ctx = colsum(du), colsum(jnp.where(is_ctx, du, 0.0))
        dscale_all, dscale_ctx = colsum(dsc), colsum(jnp.where(is_ctx, dsc, 0.0))
        part = jnp.concatenate([colsum(dy * xn), dshift_all - dshift_ctx, dshift_ctx, dscale_all - dscale_ctx,
                                dscale_ctx, jnp.zeros((3, d), F32)], axis=0)

        @pl.when(pl.program_id(0) == 0)
        def _():
            sums_ref[...] = part

        @pl.when(pl.program_id(0) > 0)
        def _():
            sums_ref[...] += part

    row = pl.BlockSpec((tm, d), lambda i: (i, 0))
    return pl.pallas_call(
        body, name="norm_mod_bwd",
        grid=(t // tm,),
        in_specs=[row, pl.BlockSpec((1, d), lambda i: (0, 0)), pl.BlockSpec((2, d), lambda i: (0, 0)),
                  pl.BlockSpec((2, d), lambda i: (0, 0)), row],
        out_specs=[row, pl.BlockSpec((8, d), lambda i: (0, 0))],
        out_shape=[jax.ShapeDtypeStruct((t, d), F32), jax.ShapeDtypeStruct((8, d), F32)],
        compiler_params=_params("arbitrary"),
    )(h, w, shift, scale, du)


@functools.partial(jax.custom_vjp, nondiff_argnums=(4,))
def norm_mod(h, w, shift, scale, ctx_len):
    return _norm_mod_fwd_call(h, w[None, :], shift, scale, ctx_len)


def _norm_mod_fwd(h, w, shift, scale, ctx_len):
    return _norm_mod_fwd_call(h, w[None, :], shift, scale, ctx_len), (h, w, shift, scale)


def _norm_mod_bwd(ctx_len, res, du):
    h, w, shift, scale = res
    dh, sums = _norm_mod_bwd_call(h, w[None, :], shift, scale, du, ctx_len)
    return dh, sums[0], sums[1:3], sums[3:5]


norm_mod.defvjp(_norm_mod_fwd, _norm_mod_bwd)


def _bf(x):
    return x.astype(MXU_DTYPE)


def _dot(a, b):
    return jnp.dot(_bf(a), _bf(b), preferred_element_type=F32)


def _dot_nt(a, b):
    return lax.dot_general(_bf(a), _bf(b), (((1,), (1,)), ((), ())), preferred_element_type=F32)


def _dot_tn(a, b):
    return lax.dot_general(_bf(a), _bf(b), (((0,), (0,)), ((), ())), preferred_element_type=F32)


ROWWISE_VMEM_BYTES = 20 * 1024 * 1024


@functools.partial(jax.custom_vjp, nondiff_argnums=(1,))
def _split_lanes(x, n):
    w = x.shape[1] // n
    return tuple(x[:, i * w:(i + 1) * w] for i in range(n))


def _split_lanes_fwd(x, n):
    return _split_lanes(x, n), None


def _split_lanes_bwd(n, _, gs):
    return (jnp.concatenate(gs, axis=1),)


_split_lanes.defvjp(_split_lanes_fwd, _split_lanes_bwd)


def _rowwise_tile(t, widths):
    for tm in (768, 512, 256, 128):
        if t % tm == 0 and tm * 8 * sum(widths) <= ROWWISE_VMEM_BYTES:
            return tm
    raise ValueError((t, widths))


def _rowwise(name, fn, out_w, ctx_len):
    def is_ctx(tm):
        return pl.program_id(0) * tm + lax.broadcasted_iota(jnp.int32, (tm, 1), 0) < ctx_len

    def specs(params, rows, tm):
        return ([pl.BlockSpec(p.shape, lambda i: (0, 0)) for p in params]
                + [pl.BlockSpec((tm, r.shape[1]), lambda i: (i, 0)) for r in rows])

    def fwd_call(params, rows):
        t = rows[0].shape[0]
        tm = _rowwise_tile(t, [r.shape[1] for r in rows] + [out_w])
        n_p = len(params)

        def body(*refs):
            vals = [r[...] for r in refs[:-1]]
            refs[-1][...] = fn(is_ctx(tm), tuple(vals[:n_p]), tuple(vals[n_p:]))

        return pl.pallas_call(
            body, name=name + "_fwd", grid=(t // tm,),
            in_specs=specs(params, rows, tm),
            out_specs=pl.BlockSpec((tm, out_w), lambda i: (i, 0)),
            out_shape=jax.ShapeDtypeStruct((t, out_w), F32),
            compiler_params=_params("parallel"),
        )(*params, *rows)

    def bwd_call(params, rows, dout):
        t = rows[0].shape[0]
        tm = _rowwise_tile(t, [2 * r.shape[1] for r in rows] + [out_w])
        n_p, n_r = len(params), len(rows)

        def body(*refs):
            vals = [r[...] for r in refs[:n_p + n_r + 1]]
            dx_refs = refs[n_p + n_r + 1:n_p + 2 * n_r + 1]
            dp_refs = refs[n_p + 2 * n_r + 1:]
            ctx_rows = is_ctx(tm)
            _, vjp = jax.vjp(lambda p, x: fn(ctx_rows, p, x), tuple(vals[:n_p]), tuple(vals[n_p:n_p + n_r]))
            dp, dx = vjp(vals[-1])
            for ref, v in zip(dx_refs, dx):
                ref[...] = v

            @pl.when(pl.program_id(0) == 0)
            def _():
                for ref, v in zip(dp_refs, dp):
                    ref[...] = v

            @pl.when(pl.program_id(0) > 0)
            def _():
                for ref, v in zip(dp_refs, dp):
                    ref[...] += v

        row_specs = [pl.BlockSpec((tm, r.shape[1]), lambda i: (i, 0)) for r in rows]
        outs = pl.pallas_call(
            body, name=name + "_bwd", grid=(t // tm,),
            in_specs=specs(params, rows, tm) + [pl.BlockSpec((tm, out_w), lambda i: (i, 0))],
            out_specs=row_specs + [pl.BlockSpec(p.shape, lambda i: (0, 0)) for p in params],
            out_shape=[jax.ShapeDtypeStruct(r.shape, F32) for r in rows]
            + [jax.ShapeDtypeStruct(p.shape, F32) for p in params],
            compiler_params=_params("arbitrary"),
        )(*params, *rows, dout)
        return tuple(outs[n_r:]), tuple(outs[:n_r])

    @jax.custom_vjp
    def op(params, rows):
        return fwd_call(params, rows)

    op.defvjp(lambda params, rows: (fwd_call(params, rows), (params, rows)),
              lambda res, g: bwd_call(res[0], res[1], g))
    return op


def _silu(x):
    return x * jax.nn.sigmoid(x)


def _ret_finish_tile(is_ctx, params, rows):
    (gn_w,), (y_f, y_b, gate) = params, rows
    heads = []
    for yh in _split_lanes(y_f + y_b, RET_HEADS):
        yc = yh - jnp.mean(yh, axis=1, keepdims=True)
        heads.append(yc * lax.rsqrt(jnp.mean(yc * yc, axis=1, keepdims=True) + NORM_EPS))
    return jnp.concatenate(heads, axis=1) * gn_w * _silu(gate)


def _ssd_finish_tile(is_ctx, params, rows):
    (d_skip, norm_w), (y_f, y_b, xs, z) = params, rows
    g = (y_f + y_b + d_skip * xs) * _silu(z)
    return g * lax.rsqrt(jnp.mean(g * g, axis=1, keepdims=True) + NORM_EPS) * norm_w


def _gate_merge_tile(is_ctx, params, rows):
    y0, y1, y2, logits = rows
    return sum(jax.nn.sigmoid(g) * y for g, y in zip(_split_lanes(logits, N_BRANCH), (y0, y1, y2)))


def _qk_prep_tile(is_ctx, params, rows):
    (q_w, k_w), (aq, ak, cos, sin) = params, rows
    out = []
    for x, norm_w, heads in ((aq, q_w, ATTN_HEADS), (ak, k_w, ATTN_KV_HEADS)):
        for xh in _split_lanes(x, heads):
            xn = xh * lax.rsqrt(jnp.mean(xh * xh, axis=1, keepdims=True) + NORM_EPS) * norm_w
            x1, x2 = _split_lanes(xn, 2)
            out += [x1 * cos - x2 * sin, x1 * sin + x2 * cos]
    return jnp.concatenate(out, axis=1)


def _ret_prep_tile(is_ctx, params, rows):
    rq, rk, cos, sin = rows
    out = []
    for x, scale in ((rq, 1.0), (rk, RET_DK ** -0.5)):
        for xh in _split_lanes(x, RET_HEADS):
            x1, x2 = _split_lanes(xh, 2)
            out += [(x1 * cos - x2 * sin) * scale, (x1 * sin + x2 * cos) * scale]
    return jnp.concatenate(out, axis=1)


def _loss_rows_tile(is_ctx, params, rows):
    (norm_w,), (h, target) = params, rows
    y = h * lax.rsqrt(jnp.mean(h * h, axis=1, keepdims=True) + NORM_EPS) * norm_w
    err = jnp.mean(jnp.square(y - target), axis=1, keepdims=True)
    return jnp.broadcast_to(err, (h.shape[0], LANES))


def _gated_residual_tile(is_ctx, params, rows):
    (gate,), (h, update) = params, rows
    return h + jnp.where(is_ctx, gate[1:2, :], gate[0:1, :]) * update


NEG_BIG = -1e30


def _attn_tiles(t, ctx_len, backward=False):
    tq = _tile(ctx_len, (256, 128))
    assert t % tq == 0 and ctx_len % tq == 0
    tk = _tile(t, (1408, 768, 512, 256, 128) if backward else (2816, 1408, 768, 512, 256, 128))
    return tq, tk


def _head_scores(q_ref, k_bf, g, ki, tk, ctx_len, masked):
    q = (q_ref[0, g] * (ATTN_HEAD_DIM ** -0.5)).astype(MXU_DTYPE)
    s = _dot_nt(q, k_bf)
    if masked:
        col = ki * tk + lax.broadcasted_iota(jnp.int32, s.shape, 1)
        s = jnp.where(col < ctx_len, s, NEG_BIG)
    return q, s


def _attn_cases(qi, ki, tq, tk, ctx_len, compute):
    ctx_q = (qi + 1) * tq <= ctx_len

    @pl.when(jnp.logical_not(ctx_q))
    def _():
        compute(False)

    @pl.when(jnp.logical_and(ctx_q, ki * tk < ctx_len))
    def _():
        compute(True)


def _attn_fwd_call(q, k, v, ctx_len):
    kvh, grp, t, hd = q.shape
    tq, tk = _attn_tiles(t, ctx_len)
    nkb = t // tk

    def body(q_ref, k_ref, v_ref, o_ref, lse_ref, m_sc, l_sc, acc_sc):
        qi, ki = pl.program_id(1), pl.program_id(2)

        @pl.when(ki == 0)
        def _():
            m_sc[...] = jnp.full(m_sc.shape, NEG_BIG, F32)
            l_sc[...] = jnp.zeros(l_sc.shape, F32)
            acc_sc[...] = jnp.zeros(acc_sc.shape, F32)

        def compute(masked):
            k_bf, v_bf = _bf(k_ref[0]), _bf(v_ref[0])
            for g in range(grp):
                _, s = _head_scores(q_ref, k_bf, g, ki, tk, ctx_len, masked)
                m_prev = m_sc[g]
                m_new = jnp.maximum(m_prev, jnp.max(s, axis=1, keepdims=True))
                alpha = jnp.exp(m_prev - m_new)
                p = jnp.exp(s - m_new)
                l_sc[g] = alpha * l_sc[g] + jnp.sum(p, axis=1, keepdims=True)
                acc_sc[g] = alpha * acc_sc[g] + _dot(p, v_bf)
                m_sc[g] = m_new

        _attn_cases(qi, ki, tq, tk, ctx_len, compute)

        @pl.when(ki == nkb - 1)
        def _():
            o_ref[0] = acc_sc[...] / l_sc[...]
            lse_ref[0] = m_sc[...] + jnp.log(l_sc[...])

    return pl.pallas_call(
        body, name="attn_fwd",
        grid=(kvh, t // tq, nkb),
        in_specs=[pl.BlockSpec((1, grp, tq, hd), lambda h, i, j: (h, 0, i, 0)),
                  pl.BlockSpec((1, tk, hd), lambda h, i, j: (h, j, 0)),
                  pl.BlockSpec((1, tk, hd), lambda h, i, j: (h, j, 0))],
        out_specs=[pl.BlockSpec((1, grp, tq, hd), lambda h, i, j: (h, 0, i, 0)),
                   pl.BlockSpec((1, grp, tq, 1), lambda h, i, j: (h, 0, i, 0))],
        out_shape=[jax.ShapeDtypeStruct(q.shape, F32), jax.ShapeDtypeStruct((kvh, grp, t, 1), F32)],
        scratch_shapes=[pltpu.VMEM((grp, tq, 1), F32), pltpu.VMEM((grp, tq, 1), F32), pltpu.VMEM((grp, tq, hd), F32)],
        compiler_params=_params("parallel", "parallel", "arbitrary"),
    )(q, k, v)


def _head_probs(q_ref, k_bf, v_bf, o_ref, do_ref, lse_ref, g, ki, tk, ctx_len, masked):
    q, s = _head_scores(q_ref, k_bf, g, ki, tk, ctx_len, masked)
    do = do_ref[0, g]
    delta = jnp.sum(do * o_ref[0, g], axis=1, keepdims=True)
    p = jnp.exp(s - lse_ref[0, g])
    do = _bf(do)
    ds = p * (_dot_nt(do, v_bf) - delta)
    return q, do, p, ds


def _attn_bwd_call(q, k, v, o, lse, do, ctx_len):
    kvh, grp, t, hd = q.shape
    tq, tk = _attn_tiles(t, ctx_len, backward=True)
    nqb, nkb = t // tq, t // tk

    def body(q_ref, k_ref, v_ref, o_ref, lse_ref, do_ref, dq_hbm, dk_ref, dv_ref, dq_sc, dk_sc, dv_sc, dq_out,
             dq_sem):
        hi, ki, qi = pl.program_id(0), pl.program_id(1), pl.program_id(2)
        rows = pl.ds(pl.multiple_of(qi * tq, tq), tq)

        @pl.when(ki == 0)
        def _():
            dq_sc[:, rows, :] = jnp.zeros((grp, tq, hd), F32)

        @pl.when(qi == 0)
        def _():
            dk_sc[...] = jnp.zeros(dk_sc.shape, F32)
            dv_sc[...] = jnp.zeros(dv_sc.shape, F32)

        def compute(masked):
            k_bf, v_bf = _bf(k_ref[0]), _bf(v_ref[0])
            dk_part = jnp.zeros(dk_sc.shape, F32)
            dv_part = jnp.zeros(dv_sc.shape, F32)
            for g in range(grp):
                qs, dob, p, ds = _head_probs(q_ref, k_bf, v_bf, o_ref, do_ref, lse_ref, g, ki, tk, ctx_len, masked)
                dv_part = dv_part + _dot_tn(p, dob)
                dk_part = dk_part + _dot_tn(ds, qs)
                dq_sc[g, rows, :] += _dot(ds, k_bf)
            dk_sc[...] += dk_part
            dv_sc[...] += dv_part

        _attn_cases(qi, ki, tq, tk, ctx_len, compute)

        @pl.when(ki == nkb - 1)
        def _():
            dq_out[...] = dq_sc[:, rows, :] * (hd ** -0.5)
            done = pltpu.make_async_copy(dq_out, dq_hbm.at[hi, :, rows, :], dq_sem)
            done.start()
            done.wait()

        @pl.when(qi == nqb - 1)
        def _():
            dk_ref[0] = dk_sc[...]
            dv_ref[0] = dv_sc[...]

    qspec = pl.BlockSpec((1, grp, tq, hd), lambda h, j, i: (h, 0, i, 0))
    kspec = pl.BlockSpec((1, tk, hd), lambda h, j, i: (h, j, 0))
    return pl.pallas_call(
        body, name="attn_bwd",
        grid=(kvh, t // tk, nqb),
        in_specs=[qspec, kspec, kspec, qspec, pl.BlockSpec((1, grp, tq, 1), lambda h, j, i: (h, 0, i, 0)), qspec],
        out_specs=[pl.BlockSpec(memory_space=pl.ANY), kspec, kspec],
        out_shape=[jax.ShapeDtypeStruct(q.shape, F32), jax.ShapeDtypeStruct(k.shape, F32),
                   jax.ShapeDtypeStruct(v.shape, F32)],
        scratch_shapes=[pltpu.VMEM((grp, t, hd), F32), pltpu.VMEM((tk, hd), F32), pltpu.VMEM((tk, hd), F32),
                        pltpu.VMEM((grp, tq, hd), F32), pltpu.SemaphoreType.DMA],
        compiler_params=_params("arbitrary", "arbitrary", "arbitrary"),
    )(q, k, v, o, lse, do)


@functools.partial(jax.custom_vjp, nondiff_argnums=(3,))
def attention(q, k, v, ctx_len):
    return _attn_fwd_call(q, k, v, ctx_len)[0]


def _attention_fwd(q, k, v, ctx_len):
    o, lse = _attn_fwd_call(q, k, v, ctx_len)
    return o, (q, k, v, o, lse)


def _attention_bwd(ctx_len, res, do):
    q, k, v, o, lse = res
    return tuple(_attn_bwd_call(q, k, v, o, lse, do, ctx_len))


attention.defvjp(_attention_fwd, _attention_bwd)


def _chunk_order(step, n_chunks, n_ctx_chunks, reverse):
    if not reverse:
        return step
    return jnp.where(step < n_ctx_chunks, n_ctx_chunks - 1 - step, n_chunks + n_ctx_chunks - 1 - step)


def _scan_masks(chunk, reverse):
    row = lax.broadcasted_iota(jnp.int32, (chunk, chunk), 0)
    col = lax.broadcasted_iota(jnp.int32, (chunk, chunk), 1)
    vis = (col >= row) if reverse else (col <= row)
    vis_t = (row >= col) if reverse else (row <= col)
    return vis, vis.astype(F32), vis_t.astype(F32)


def _cum_decay(a_col, a_row, vis_f):
    hi = lax.Precision.HIGHEST
    cum_col = jnp.dot(vis_f, a_col, precision=hi, preferred_element_type=F32)
    cum_row = lax.dot_general(a_row, vis_f, (((1,), (1,)), ((), ())), precision=hi, preferred_element_type=F32)
    total = jnp.sum(a_col, axis=0, keepdims=True)
    return cum_col, cum_row, total


def _scan_specs(chunk, n_chunks, n_ctx_chunks, reverse, backward, widths):
    def order(i):
        step = (n_chunks - 1 - i) if backward else i
        return _chunk_order(step, n_chunks, n_ctx_chunks, reverse)

    return [pl.BlockSpec((chunk, w), lambda i: (order(i), 0)) for w in widths], order


def _scan_fwd_call(q, k, v, a_col, a_row, *, groups, per_group, dk, dv, ctx_len, reverse):
    t = q.shape[0]
    chunk = SCAN_CHUNK
    n_chunks, n_ctx = t // chunk, ctx_len // chunk
    heads = groups * per_group
    (q_spec, k_spec, v_spec, acol_spec), order = _scan_specs(
        chunk, n_chunks, n_ctx, reverse, False, (groups * dk, groups * dk, heads * dv, LANES))

    def body(q_ref, k_ref, v_ref, acol_ref, arow_ref, y_ref, st_ref, s_sc):
        @pl.when(pl.program_id(0) == 0)
        def _():
            s_sc[...] = jnp.zeros(s_sc.shape, F32)

        st_ref[0] = s_sc[...]
        vis, vis_f, _ = _scan_masks(chunk, reverse)
        cum_col, cum_row, total = _cum_decay(acol_ref[...], arow_ref[...], vis_f)
        for g in range(groups):
            qg = q_ref[:, g * dk:(g + 1) * dk]
            kg = k_ref[:, g * dk:(g + 1) * dk]
            qk = _dot_nt(qg, kg)
            for r in range(per_group):
                h = g * per_group + r
                ccol = cum_col[:, h:h + 1]
                decay = jnp.exp(jnp.where(vis, ccol - cum_row[h:h + 1, :], NEG_BIG))
                vh = v_ref[:, h * dv:(h + 1) * dv]
                s_in = s_sc[h]
                y = _dot(qk * decay, vh) + jnp.exp(ccol) * _dot(qg, s_in)
                y_ref[:, h * dv:(h + 1) * dv] = y
                tot = total[:, h:h + 1]
                s_sc[h] = jnp.exp(tot) * s_in + _dot_tn(kg * jnp.exp(tot - ccol), vh)

    return pl.pallas_call(
        body, name="scan_fwd",
        grid=(n_chunks,),
        in_specs=[q_spec, k_spec, v_spec, acol_spec, pl.BlockSpec((8, chunk), lambda i: (0, order(i)))],
        out_specs=[v_spec, pl.BlockSpec((1, heads, dk, dv), lambda i: (order(i), 0, 0, 0))],
        out_shape=[jax.ShapeDtypeStruct(v.shape, F32), jax.ShapeDtypeStruct((n_chunks, heads, dk, dv), F32)],
        scratch_shapes=[pltpu.VMEM((heads, dk, dv), F32)],
        compiler_params=_params("arbitrary"),
    )(q, k, v, a_col, a_row)


def _scan_bwd_call(q, k, v, a_col, a_row, states, dy, *, groups, per_group, dk, dv, ctx_len, reverse):
    t = q.shape[0]
    chunk = SCAN_CHUNK
    n_chunks, n_ctx = t // chunk, ctx_len // chunk
    heads = groups * per_group
    (q_spec, k_spec, v_spec, acol_spec), order = _scan_specs(
        chunk, n_chunks, n_ctx, reverse, True, (groups * dk, groups * dk, heads * dv, LANES))
    arow_spec = pl.BlockSpec((8, chunk), lambda i: (0, order(i)))
    last = 0 if reverse else chunk - 1

    def body(q_ref, k_ref, v_ref, acol_ref, arow_ref, st_ref, dy_ref, dq_ref, dk_ref, dv_ref, da_ref, dat_ref,
             ds_sc):
        @pl.when(pl.program_id(0) == 0)
        def _():
            ds_sc[...] = jnp.zeros(ds_sc.shape, F32)

        vis, vis_f, vis_tf = _scan_masks(chunk, reverse)
        cum_col, cum_row, total = _cum_decay(acol_ref[...], arow_ref[...], vis_f)
        lane = lax.broadcasted_iota(jnp.int32, (chunk, LANES), 1)
        row = lax.broadcasted_iota(jnp.int32, (chunk, LANES), 0)
        sub = lax.broadcasted_iota(jnp.int32, (8, chunk), 0)
        dcum = jnp.zeros((chunk, LANES), F32)
        dcum_t = jnp.zeros((8, chunk), F32)
        for g in range(groups):
            qg = q_ref[:, g * dk:(g + 1) * dk]
            kg = k_ref[:, g * dk:(g + 1) * dk]
            qk = _dot_nt(qg, kg)
            dq_g = jnp.zeros((chunk, dk), F32)
            dk_g = jnp.zeros((chunk, dk), F32)
            for r in range(per_group):
                h = g * per_group + r
                ccol = cum_col[:, h:h + 1]
                decay = jnp.exp(jnp.where(vis, ccol - cum_row[h:h + 1, :], NEG_BIG))
                vh = v_ref[:, h * dv:(h + 1) * dv]
                dyh = dy_ref[:, h * dv:(h + 1) * dv]
                s_in = st_ref[0, h]
                ds_out = ds_sc[h]
                tot = total[:, h:h + 1]
                e_in = jnp.exp(ccol)
                e_out = jnp.exp(tot - ccol)
                e_tot = jnp.exp(tot)
                k_out = kg * e_out
                dv_ref[:, h * dv:(h + 1) * dv] = _dot_tn(qk * decay, dyh) + _dot(k_out, ds_out)
                dqk = _dot_nt(dyh, vh) * decay
                dq_in = e_in * _dot_nt(dyh, s_in)
                dk_out = e_out * _dot_nt(vh, ds_out)
                dq_h = _dot(dqk, kg) + dq_in
                dk_h = _dot_tn(dqk, qg) + dk_out
                s_out = e_tot * s_in + _dot_tn(k_out, vh)
                edge = jnp.sum(jnp.sum(s_out * ds_out, axis=1, keepdims=True), axis=0, keepdims=True)
                w_seg = dqk * qk
                dcum_h = (jnp.sum(w_seg, axis=1, keepdims=True) + jnp.sum(dq_in * qg, axis=1, keepdims=True)
                          - jnp.sum(dk_out * kg, axis=1, keepdims=True))
                dcum = jnp.where(lane == h, dcum_h + jnp.where(row == last, edge, 0.0), dcum)
                dcum_t = jnp.where(sub == h, -jnp.sum(w_seg, axis=0, keepdims=True), dcum_t)
                ds_sc[h] = e_tot * ds_out + _dot_tn(qg, e_in * dyh)
                dq_g = dq_g + dq_h
                dk_g = dk_g + dk_h
            dq_ref[:, g * dk:(g + 1) * dk] = dq_g
            dk_ref[:, g * dk:(g + 1) * dk] = dk_g
        hi = lax.Precision.HIGHEST
        da_ref[...] = jnp.dot(vis_tf, dcum, precision=hi, preferred_element_type=F32)
        dat_ref[...] = jnp.dot(dcum_t, vis_f, precision=hi, preferred_element_type=F32)

    return pl.pallas_call(
        body, name="scan_bwd",
        grid=(n_chunks,),
        in_specs=[q_spec, k_spec, v_spec, acol_spec, arow_spec,
                  pl.BlockSpec((1, heads, dk, dv), lambda i: (order(i), 0, 0, 0)), v_spec],
        out_specs=[q_spec, k_spec, v_spec, acol_spec, arow_spec],
        out_shape=[jax.ShapeDtypeStruct(q.shape, F32), jax.ShapeDtypeStruct(k.shape, F32),
                   jax.ShapeDtypeStruct(v.shape, F32), jax.ShapeDtypeStruct((t, LANES), F32),
                   jax.ShapeDtypeStruct((8, t), F32)],
        scratch_shapes=[pltpu.VMEM((heads, dk, dv), F32)],
        compiler_params=_params("arbitrary"),
    )(q, k, v, a_col, a_row, states, dy)


def _decay_layouts(a):
    t, heads = a.shape
    a_col = jnp.pad(a, ((0, 0), (0, LANES - heads)))
    a_row = jnp.pad(a.T, ((0, 8 - heads), (0, 0)))
    return a_col, a_row


@functools.partial(jax.custom_vjp, nondiff_argnums=(4,))
def linear_scan(q, k, v, a, cfg):
    a_col, a_row = _decay_layouts(a)
    return _scan_fwd_call(q, k, v, a_col, a_row, **dict(cfg))[0]


def _linear_scan_fwd(q, k, v, a, cfg):
    a_col, a_row = _decay_layouts(a)
    y, states = _scan_fwd_call(q, k, v, a_col, a_row, **dict(cfg))
    return y, (q, k, v, a, states)


def _linear_scan_bwd(cfg, res, dy):
    q, k, v, a, states = res
    a_col, a_row = _decay_layouts(a)
    dq, dk, dv, da, da_t = _scan_bwd_call(q, k, v, a_col, a_row, states, dy, **dict(cfg))
    heads = a.shape[1]
    return dq, dk, dv, da[:, :heads] + da_t[:heads].T


linear_scan.defvjp(_linear_scan_fwd, _linear_scan_bwd)


def _scan_cfg(groups, per_group, dk, dv, ctx_len, reverse):
    return (("groups", groups), ("per_group", per_group), ("dk", dk), ("dv", dv), ("ctx_len", ctx_len),
            ("reverse", reverse))


def _axial_tables(n_lat, n_ctx):
    freqs = ATTN_HEAD_DIM // 4
    rows = n_lat // GRID_W
    row = jnp.repeat(jnp.arange(rows, dtype=F32), GRID_W)
    col = jnp.tile(jnp.arange(GRID_W, dtype=F32), rows)
    inv = ROPE_THETA ** (-jnp.arange(freqs, dtype=F32) / freqs)
    ang = jnp.concatenate([row[:, None] * inv, col[:, None] * inv], axis=-1)
    cos = jnp.concatenate([jnp.ones((n_ctx, 2 * freqs), F32), jnp.cos(ang)], axis=0)
    sin = jnp.concatenate([jnp.zeros((n_ctx, 2 * freqs), F32), jnp.sin(ang)], axis=0)
    return cos, sin


def _seq_tables(t):
    pos = jnp.arange(t, dtype=F32)
    inv = ROPE_THETA ** (-jnp.linspace(0.0, 1.0, RET_DK // 2, dtype=F32))
    ang = pos[:, None] * inv
    return jnp.cos(ang), jnp.sin(ang)


def _pad_w_in(w_in):
    d = w_in.shape[0]
    return jnp.concatenate([w_in[:, :DT_END], jnp.zeros((d, DT_PAD), w_in.dtype), w_in[:, DT_END:],
                            jnp.zeros((d, TAIL_PAD), w_in.dtype)], axis=1)


def _split_proj(p):
    widths = list(IN_SPLITS)
    widths[5] = LANES
    out, off = [], 0
    for w in widths:
        out.append(p[:, off:off + w])
        off += w
    out[5] = out[5][:, :DT_COLS]
    return out


def _mixer(u, w, wq, layer, n_ctx, tables):
    t = u.shape[0]
    attn_rope, ret_rope, seg_first, seg_last = tables
    proj = matmul(u, wq["w_in"][layer], _pad_w_in(w["w_in"][layer]))
    aq, ak, av, z, xbc_raw, dt_raw, rq, rk, rv, rg, gate_logits = _split_proj(proj)

    q_width = ATTN_HEADS * ATTN_HEAD_DIM
    qk = _rowwise("qk_prep", _qk_prep_tile, q_width + ATTN_KV_HEADS * ATTN_HEAD_DIM, n_ctx)(
        (w["attn_q_norm"][layer][None, :], w["attn_k_norm"][layer][None, :]), (aq, ak) + attn_rope)
    q4 = qk[:, :q_width].reshape(t, ATTN_KV_HEADS, ATTN_GROUP, ATTN_HEAD_DIM).transpose(1, 2, 0, 3)
    k3 = qk[:, q_width:].reshape(t, ATTN_KV_HEADS, ATTN_HEAD_DIM).transpose(1, 0, 2)
    v3 = av.reshape(t, ATTN_KV_HEADS, ATTN_HEAD_DIM).transpose(1, 0, 2)
    o4 = attention(q4, k3, v3, n_ctx)
    br_attn = o4.transpose(2, 0, 1, 3).reshape(t, ATTN_HEADS * ATTN_HEAD_DIM)

    cw, cb = w["ssd_conv_w"][layer], w["ssd_conv_b"][layer]
    zero_row = jnp.zeros((1, xbc_raw.shape[1]), F32)
    prev = jnp.concatenate([zero_row, xbc_raw[:-1]], axis=0) * (1.0 - seg_first)
    nxt = jnp.concatenate([xbc_raw[1:], zero_row], axis=0) * (1.0 - seg_last)
    xbc = jax.nn.silu(prev * cw[0] + xbc_raw * cw[1] + nxt * cw[2] + cb)
    gn = SSD_GROUPS * SSD_STATE
    xs = xbc[:, :SSD_D_INNER]
    bm = xbc[:, SSD_D_INNER:SSD_D_INNER + gn]
    cm = xbc[:, SSD_D_INNER + gn:]
    dt = jax.nn.softplus(dt_raw.reshape(t, 2, SSD_HEADS) + w["ssd_dt_bias"][layer])
    a_neg = -jnp.exp(w["ssd_a_log"][layer])
    xs_h = xs.reshape(t, SSD_HEADS, SSD_HEAD_DIM)
    y_ssd = []
    for d, reverse in ((0, False), (1, True)):
        dtd = dt[:, d]
        cfg = _scan_cfg(SSD_GROUPS, SSD_HEADS // SSD_GROUPS, SSD_STATE, SSD_HEAD_DIM, n_ctx, reverse)
        y_ssd.append(linear_scan(cm, bm, (xs_h * dtd[:, :, None]).reshape(t, SSD_D_INNER), dtd * a_neg[d], cfg))
    d_skip = jnp.repeat(w["ssd_d"][layer], SSD_HEAD_DIM)[None, :]
    br_ssd = _rowwise("ssd_finish", _ssd_finish_tile, SSD_D_INNER, n_ctx)(
        (d_skip, w["ssd_norm_w"][layer][None, :]), (y_ssd[0], y_ssd[1], xs, z))

    lg = -jnp.exp(w["ret_log_decay"][layer])
    ret_width = RET_HEADS * RET_DK
    ret_qk = _rowwise("ret_prep", _ret_prep_tile, 2 * ret_width, n_ctx)((), (rq, rk) + ret_rope)
    rq_r, rk_r = ret_qk[:, :ret_width], ret_qk[:, ret_width:]
    y_ret = []
    for d, reverse in ((0, False), (1, True)):
        cfg = _scan_cfg(RET_HEADS, 1, RET_DK, RET_DV, n_ctx, reverse)
        y_ret.append(linear_scan(rq_r, rk_r, rv, jnp.broadcast_to(lg[d][None, :], (t, RET_HEADS)), cfg))
    br_ret = _rowwise("ret_finish", _ret_finish_tile, RET_HEADS * RET_DV, n_ctx)(
        (w["ret_gn_w"][layer][None, :],), (y_ret[0], y_ret[1], rg))

    projected = tuple(matmul(br, wq["w_branch"][layer][j], w["w_branch"][layer][j])
                      for j, br in enumerate((br_attn, br_ssd, br_ret)))
    merged = _rowwise("gate_merge", _gate_merge_tile, D_MODEL, n_ctx)((), projected + (gate_logits,))
    return matmul(merged, wq["w_out"][layer], w["w_out"][layer])


def _local_loss(w, x, c, ctx, target, wq):
    n, m = x.shape[0], ctx.shape[0]
    t = n + m
    pos = jnp.arange(t)[:, None]
    seg_first = ((pos == 0) | (pos == m)).astype(F32)
    seg_last = ((pos == m - 1) | (pos == t - 1)).astype(F32)
    tables = (_axial_tables(n, m), _seq_tables(t), seg_first, seg_last)
    h = jnp.concatenate([ctx, x], axis=0)
    cond = jax.nn.silu(jnp.stack([c, w["c_ctx"]], axis=0))
    cond8 = jnp.concatenate([cond, jnp.zeros((6, D_MODEL), F32)], axis=0)
    for layer in range(DEPTH):
        mod = (matmul(cond8, wq["w_mod"][layer], w["w_mod"][layer])[:2] + w["b_mod"][layer]).reshape(2, 6, D_MODEL)

        u = norm_mod(h, w["norm1_w"][layer], mod[:, 0], mod[:, 1], m)
        residual = _rowwise("gated_residual", _gated_residual_tile, D_MODEL, m)
        h = residual((mod[:, 2],), (h, _mixer(u, w, wq, layer, m, tables)))
        v = norm_mod(h, w["norm2_w"][layer], mod[:, 3], mod[:, 4], m)
        mlp = sq_relu_mlp(v, wq["w_mlp1"][layer], w["w_mlp1"][layer], wq["w_mlp2"][layer], w["w_mlp2"][layer])
        residual = _rowwise("gated_residual", _gated_residual_tile, D_MODEL, m)
        h = residual((mod[:, 5],), (h, mlp))
    per_token = _rowwise("loss_rows", _loss_rows_tile, LANES, 0)((w["final_norm_w"][None, :],), (h[m:], target))
    return 0.5 * jnp.sum(per_token[:, 0])


def _coords():
    return lax.axis_index("x"), lax.axis_index("y"), lax.axis_index("c")


def _all_gather(blocks, name):
    n = len(blocks)

    def body(*refs):
        x_refs, out_refs = refs[:n], refs[n:2 * n]
        send_sems, recv_sems, local_sems = refs[2 * n:]
        x, y, c = _coords()
        me, sibling = (x, y, c), (x, y, 1 - c)
        chips = [(1 - x, y), (x, 1 - y), (1 - x, 1 - y)]

        def copy(k, i, blk, to, from_input=False):
            slot = out_refs[i].at[4 * blk[0] + 2 * blk[1] + blk[2]]
            return pltpu.make_async_remote_copy(
                src_ref=x_refs[i] if from_input else slot, dst_ref=slot,
                send_sem=send_sems.at[k * n + i], recv_sem=recv_sems.at[k * n + i],
                device_id=to, device_id_type=pl.DeviceIdType.MESH)

        mine = [pltpu.make_async_copy(x_refs[i], out_refs[i].at[4 * x + 2 * y + c], local_sems.at[i])
                for i in range(n)]
        for cp in mine:
            cp.start()
        first = [copy(0, i, me, sibling, True) for i in range(n)]
        first += [copy(1 + j, i, me, (*chip, c), True) for j, chip in enumerate(chips) for i in range(n)]
        for cp in first:
            cp.start()
        passed = []
        for j, chip in enumerate(chips):
            for i in range(n):
                copy(1 + j, i, (*chip, c), me).wait_recv()
                passed.append(copy(4 + j, i, (*chip, c), sibling))
                passed[-1].start()
        for i in range(n):
            copy(0, i, sibling, me).wait_recv()
        for j, chip in enumerate(chips):
            for i in range(n):
                copy(4 + j, i, (*chip, 1 - c), me).wait_recv()
        for cp in first + passed:
            cp.wait_send()
        for cp in mine:
            cp.wait()

    return pl.pallas_call(
        body, name=name,
        out_shape=[jax.ShapeDtypeStruct((N_DEV,) + b.shape, b.dtype) for b in blocks],
        in_specs=[pl.BlockSpec(memory_space=pl.ANY)] * n,
        out_specs=[pl.BlockSpec(memory_space=pl.ANY)] * n,
        scratch_shapes=[pltpu.SemaphoreType.DMA((7 * n,)), pltpu.SemaphoreType.DMA((7 * n,)),
                        pltpu.SemaphoreType.DMA((n,))],
    )(*blocks)


def _all_to_all(arrays, name):
    n = len(arrays)

    def body(*refs):
        g_refs, out_refs = refs[:n], refs[n:2 * n]
        send_sems, recv_sems, local_sems = refs[2 * n:]
        x, y, c = _coords()
        me = 4 * x + 2 * y + c
        mine = [pltpu.make_async_copy(g_refs[i].at[me], out_refs[i].at[me], local_sems.at[i]) for i in range(n)]
        for cp in mine:
            cp.start()
        copies = []
        for k in range(1, N_DEV):
            bx, by, bc = (k >> 2) & 1, (k >> 1) & 1, k & 1
            px, py, pc = (1 - x if bx else x), (1 - y if by else y), (1 - c if bc else c)
            peer = 4 * px + 2 * py + pc
            for i in range(n):
                copies.append(pltpu.make_async_remote_copy(
                    src_ref=g_refs[i].at[peer], dst_ref=out_refs[i].at[me],
                    send_sem=send_sems.at[(k - 1) * n + i], recv_sem=recv_sems.at[(k - 1) * n + i],
                    device_id=(px, py, pc), device_id_type=pl.DeviceIdType.MESH))
        for cp in copies:
            cp.start()
        for cp in copies:
            cp.wait_recv()
        for cp in copies:
            cp.wait_send()
        for cp in mine:
            cp.wait()

    return pl.pallas_call(
        body, name=name,
        out_shape=[jax.ShapeDtypeStruct(a.shape, a.dtype) for a in arrays],
        in_specs=[pl.BlockSpec(memory_space=pl.ANY)] * n,
        out_specs=[pl.BlockSpec(memory_space=pl.ANY)] * n,
        scratch_shapes=[pltpu.SemaphoreType.DMA((7 * n,)), pltpu.SemaphoreType.DMA((7 * n,)),
                        pltpu.SemaphoreType.DMA((n,))],
    )(*arrays)


ADAMW_BLOCK_ELEMS = 256 * 1024


def _sum_adamw(g8, w, m, v, name):
    rows, cols = w.shape
    tr = _tile(rows, [r for r in (2048, 1024, 512, 256, 128, 64, 32, 16) if r * cols <= ADAMW_BLOCK_ELEMS])

    def body(g_ref, w_ref, m_ref, v_ref, go_ref, d_ref, mo_ref, vo_ref):
        g = g_ref[0].astype(F32)
        for s in range(1, N_DEV):
            g = g + g_ref[s].astype(F32)
        m_new = ADAM_B1 * m_ref[...] + (1.0 - ADAM_B1) * g
        v_new = ADAM_B2 * v_ref[...] + (1.0 - ADAM_B2) * (g * g)
        m_hat = m_new / (1.0 - ADAM_B1 ** ADAM_STEP)
        v_hat = v_new / (1.0 - ADAM_B2 ** ADAM_STEP)
        go_ref[...] = g
        d_ref[...] = -ADAM_LR * (m_hat / (jnp.sqrt(v_hat) + ADAM_EPS) + ADAM_WD * w_ref[...])
        mo_ref[...] = m_new
        vo_ref[...] = v_new

    spec = pl.BlockSpec((tr, cols), lambda i: (i, 0))
    shape = jax.ShapeDtypeStruct((rows, cols), F32)
    return pl.pallas_call(
        body, name=name,
        grid=(rows // tr,),
        in_specs=[pl.BlockSpec((N_DEV, tr, cols), lambda i: (0, i, 0)), spec, spec, spec],
        out_specs=[spec, spec, spec, spec],
        out_shape=[shape, shape, shape, shape],
        compiler_params=_params("parallel"),
    )(g8, w, m, v)


BIG = ("w_mod", "w_in", "w_branch", "w_out", "w_mlp1", "w_mlp2")
COL_SHARDED = ("w_mod", "w_mlp1")
ROW_SHARDED = ("w_out", "w_mlp2")
SMALL = ("c_ctx", "b_mod", "norm1_w", "norm2_w", "attn_q_norm", "attn_k_norm", "ssd_conv_b", "ssd_dt_bias",
         "ssd_a_log", "ssd_d", "ssd_norm_w", "ret_log_decay", "ret_gn_w", "final_norm_w")
CONV_AXIS = 2
ORDER = ("c_ctx", "w_mod", "b_mod", "norm1_w", "norm2_w", "w_in", "attn_q_norm", "attn_k_norm", "ssd_conv_w",
         "ssd_conv_b", "ssd_dt_bias", "ssd_a_log", "ssd_d", "ssd_norm_w", "ret_log_decay", "ret_gn_w", "w_branch",
         "w_out", "w_mlp1", "w_mlp2", "final_norm_w")


def _compute_weights(gathered):
    wq, carrier = {}, {}
    for name in BIG:
        g = gathered[name]
        per_layer = []
        for layer in range(DEPTH):
            gl = g[:, layer]
            if name in COL_SHARDED:
                per_layer.append(gl)
            elif name in ROW_SHARDED:
                per_layer.append(gl.reshape(-1, gl.shape[-1]))
            elif name == "w_in":
                per_layer.append(_pad_w_in(jnp.concatenate([gl[d] for d in range(N_DEV)], axis=-1)))
            else:
                per_layer.append([jnp.concatenate([gl[d, j] for d in range(N_DEV)], axis=-1) for j in range(N_BRANCH)])
        wq[name] = per_layer
    for name in BIG:
        if name == "w_in":
            carrier[name] = [jnp.zeros((D_MODEL, IN_DIM), F32) for _ in range(DEPTH)]
        else:
            carrier[name] = jax.tree.map(lambda a: jnp.zeros(a.shape, F32), wq[name])
    return wq, carrier


def _grad_shards(gw):
    out = {}
    for name in BIG:
        per_layer = []
        for layer in range(DEPTH):
            g = gw[name][layer]
            if name in COL_SHARDED:
                per_layer.append(g)
            elif name in ROW_SHARDED:
                per_layer.append(g.reshape(N_DEV, -1, g.shape[-1]))
            elif name == "w_in":
                size = IN_DIM // N_DEV
                per_layer.append(jnp.stack([g[:, d * size:(d + 1) * size] for d in range(N_DEV)]))
            else:
                per_layer.append(jnp.stack([gj.reshape(gj.shape[0], N_DEV, -1).transpose(1, 0, 2) for gj in g], axis=1))
        out[name] = jnp.stack(per_layer, axis=1)
    return out


def _pack(arrays, row_multiple):
    flat = jnp.concatenate(arrays, axis=-1)
    n = flat.shape[-1]
    per = LANES * row_multiple
    padded = -(-n // per) * per
    flat = jnp.pad(flat, [(0, 0)] * (flat.ndim - 1) + [(0, padded - n)])
    return flat.reshape(flat.shape[:-1] + (padded // LANES, LANES))


def _unpack(slab, shapes):
    flat = slab.reshape(slab.shape[:-2] + (-1,))
    out, off = [], 0
    for shp in shapes:
        size = math.prod(shp)
        out.append(flat[..., off:off + size].reshape(flat.shape[:-1] + tuple(shp)))
        off += size
    return out


def kernel(x, c, ctx, c_ctx, w_mod, b_mod, norm1_w, norm2_w, w_in, attn_q_norm, attn_k_norm, ssd_conv_w, ssd_conv_b, ssd_dt_bias, ssd_a_log, ssd_d, ssd_norm_w, ret_log_decay, ret_gn_w, w_branch, w_out, w_mlp1, w_mlp2, final_norm_w, loss_target, m_c_ctx, m_w_mod, m_b_mod, m_norm1_w, m_norm2_w, m_w_in, m_attn_q_norm, m_attn_k_norm, m_ssd_conv_w, m_ssd_conv_b, m_ssd_dt_bias, m_ssd_a_log, m_ssd_d, m_ssd_norm_w, m_ret_log_decay, m_ret_gn_w, m_w_branch, m_w_out, m_w_mlp1, m_w_mlp2, m_final_norm_w, v_c_ctx, v_w_mod, v_b_mod, v_norm1_w, v_norm2_w, v_w_in, v_attn_q_norm, v_attn_k_norm, v_ssd_conv_w, v_ssd_conv_b, v_ssd_dt_bias, v_ssd_a_log, v_ssd_d, v_ssd_norm_w, v_ret_log_decay, v_ret_gn_w, v_w_branch, v_w_out, v_w_mlp1, v_w_mlp2, v_final_norm_w):
    args = dict(locals())
    weights = {n: args[n] for n in ORDER}
    mom1 = {n: args["m_" + n] for n in ORDER}
    mom2 = {n: args["v_" + n] for n in ORDER}
    me = 4 * lax.axis_index("x") + 2 * lax.axis_index("y") + lax.axis_index("c")

    gathered = _all_gather([weights[n].astype(MXU_DTYPE) for n in BIG], "gather_weights")
    wq, params = _compute_weights(dict(zip(BIG, gathered)))
    conv_shape = ssd_conv_w.shape
    conv_all = _all_gather([_pack([ssd_conv_w.reshape(-1)], 8)], "gather_conv")[0]
    params["ssd_conv_w"] = jnp.concatenate(list(_unpack(conv_all, [conv_shape])[0]), axis=CONV_AXIS)
    for n in SMALL:
        params[n] = weights[n]

    loss, (gw, gx) = jax.value_and_grad(_local_loss, argnums=(0, 1))(params, x[0], c[0], ctx[0], loss_target[0], wq)
    loss = lax.psum(loss, MESH_AXES)

    g_send = _grad_shards(gw)
    g_recv = _all_to_all([g_send[n].astype(jnp.bfloat16) for n in BIG], "scatter_grads")
    result = {}
    for n, g8 in zip(BIG, g_recv):
        shape = weights[n].shape
        as2d = lambda a: a.reshape(-1, shape[-1])
        outs = _sum_adamw(g8.reshape(N_DEV, -1, shape[-1]), as2d(weights[n]), as2d(mom1[n]), as2d(mom2[n]), "adamw_" + n)
        for kind, arr in zip(("grad", "delta", "new_m", "new_v"), outs):
            result[kind, n] = arr.reshape(shape)

    conv_full_shape = params["ssd_conv_w"].shape
    small_shapes = [weights[n].shape for n in SMALL]
    partial = _pack([gw[n].reshape(-1) for n in SMALL] + [gw["ssd_conv_w"].reshape(-1)], 8)
    parts = _unpack(_all_gather([partial], "gather_small_grads")[0], small_shapes + [conv_full_shape])
    conv_part = lax.dynamic_slice_in_dim(parts[-1], me * conv_shape[CONV_AXIS], conv_shape[CONV_AXIS], CONV_AXIS + 1)
    small_names = list(SMALL) + ["ssd_conv_w"]
    g8_small = _pack([p.reshape(N_DEV, -1) for p in parts[:-1]] + [conv_part.reshape(N_DEV, -1)], 8)
    slabs = [_pack([d[n].reshape(-1) for n in small_names], 8) for d in (weights, mom1, mom2)]
    small_out = [_unpack(s, small_shapes + [conv_shape]) for s in _sum_adamw(g8_small, *slabs, "adamw_small")]
    for kind, small_k in zip(("grad", "delta", "new_m", "new_v"), small_out):
        for n, arr in zip(small_names, small_k):
            result[kind, n] = arr

    outs = [loss, gx[None]]
    for kind in ("grad", "delta", "new_m", "new_v"):
        outs += [result[kind, n] for n in ORDER]
    return tuple(outs)
```

```python
import functools
import math

import jax
import jax.numpy as jnp
from jax import lax
from jax.experimental import pallas as pl
from jax.experimental.pallas import tpu as pltpu

F32 = jnp.float32
MXU_DTYPE = jnp.bfloat16
VMEM_LIMIT_BYTES = 48 * 1024 * 1024
LANES = 128
N_DEV = 8
MESH_AXES = ("x", "y", "c")

D_MODEL = 1024
GRID_W = 64
NORM_EPS = 1e-6
ROPE_THETA = 10000.0
ATTN_HEADS, ATTN_KV_HEADS, ATTN_HEAD_DIM = 8, 2, 64
ATTN_GROUP = ATTN_HEADS // ATTN_KV_HEADS
SSD_HEADS, SSD_HEAD_DIM, SSD_GROUPS, SSD_STATE = 8, 64, 2, 128
SSD_D_INNER = SSD_HEADS * SSD_HEAD_DIM
RET_HEADS, RET_DK, RET_DV = 4, 128, 128
SCAN_CHUNK = 128
N_BRANCH = 3
DEPTH = 2

IN_SPLITS = (512, 128, 128, 512, 1024, 16, 512, 512, 512, 512, 3072)
IN_DIM = sum(IN_SPLITS)
DT_COLS = 16
DT_PAD = LANES - DT_COLS
TAIL_PAD = 128
IN_DIM_PADDED = IN_DIM + DT_PAD + TAIL_PAD
DT_END = sum(IN_SPLITS[:6])

ADAM_LR, ADAM_B1, ADAM_B2, ADAM_EPS, ADAM_WD, ADAM_STEP = 0.001, 0.9, 0.999, 1e-08, 0.01, 10


def _tile(dim, prefs):
    for p in prefs:
        if dim % p == 0:
            return p
    return dim


def _params(*sem):
    return pltpu.CompilerParams(dimension_semantics=sem, vmem_limit_bytes=VMEM_LIMIT_BYTES)


def _mm(a, b, *, ta=False, tb=False, out_shards=False, epilogue=None, extra=None, name):
    if ta:
        kdim, m = a.shape
    else:
        m, kdim = a.shape
    b_shards = b.ndim == 3
    if b_shards:
        rows_b, cols_b = b.shape[1], N_DEV * b.shape[2]
    else:
        rows_b, cols_b = b.shape
    n, kdim_b = (rows_b, cols_b) if tb else (cols_b, rows_b)
    assert kdim == kdim_b, (a.shape, b.shape, ta, tb)
    tm = _tile(m, (1024, 768, 512, 256, 128))
    n_tile_of = n // N_DEV if (out_shards or (b_shards and not tb)) else n
    k_tile_of = kdim // N_DEV if (b_shards and tb) else kdim
    tn = _tile(n_tile_of, (1280, 1024, 768, 512, 384, 256, 128))
    tk = _tile(k_tile_of, (1024, 768, 512, 256, 128))
    dims = (((0 if ta else 1,), (1 if tb else 0,)), ((), ()))

    n_k = kdim // tk

    def body(a_ref, b_ref, *rest):
        o_ref = rest[-1]
        part = lax.dot_general(a_ref[...].astype(MXU_DTYPE), b_ref[...].astype(MXU_DTYPE), dims,
                               preferred_element_type=F32)

        @pl.when(pl.program_id(2) == 0)
        def _():
            o_ref[...] = part

        @pl.when(pl.program_id(2) > 0)
        def _():
            o_ref[...] += part

        if epilogue is not None:
            @pl.when(pl.program_id(2) == n_k - 1)
            def _():
                acc = o_ref[...]
                if epilogue == "sq_relu":
                    r = jnp.maximum(acc, 0.0)
                    o_ref[...] = r * r
                else:
                    o_ref[...] = acc * (2.0 * jnp.sqrt(rest[0][...]))

    a_spec = pl.BlockSpec((tk, tm), lambda i, j, k: (k, i)) if ta else pl.BlockSpec((tm, tk), lambda i, j, k: (i, k))
    if not b_shards:
        b_spec = pl.BlockSpec((tn, tk), lambda i, j, k: (j, k)) if tb else pl.BlockSpec((tk, tn), lambda i, j, k: (k, j))
    elif tb:
        per = b.shape[2] // tk
        b_spec = pl.BlockSpec((None, tn, tk), lambda i, j, k: (k // per, j, k % per))
    else:
        per = b.shape[2] // tn
        b_spec = pl.BlockSpec((None, tk, tn), lambda i, j, k: (j // per, k, j % per))
    if out_shards:
        per_out = n // N_DEV // tn
        out_spec = pl.BlockSpec((None, tm, tn), lambda i, j, k: (j // per_out, i, j % per_out))
        out_shape = jax.ShapeDtypeStruct((N_DEV, m, n // N_DEV), F32)
    else:
        out_spec = pl.BlockSpec((tm, tn), lambda i, j, k: (i, j))
        out_shape = jax.ShapeDtypeStruct((m, n), F32)
    assert epilogue in (None, "sq_relu", "d_sq_relu") and (extra is not None) == (epilogue == "d_sq_relu")
    operands, in_specs = [a, b], [a_spec, b_spec]
    if extra is not None:
        assert not out_shards and extra.shape == (m, n)
        operands.append(extra)
        in_specs.append(out_spec)
    return pl.pallas_call(
        body, name=name,
        grid=(m // tm, n // tn, n_k),
        in_specs=in_specs,
        out_specs=out_spec,
        out_shape=out_shape,
        compiler_params=_params("parallel", "parallel", "arbitrary"),
    )(*operands)


@jax.custom_vjp
def matmul(a, b, b_grad):
    return _mm(a, b, name="mm_fwd")


def _matmul_fwd(a, b, b_grad):
    return _mm(a, b, name="mm_fwd"), (a, b)


def _matmul_bwd(res, g):
    a, b = res
    dw = _mm(a, g, ta=True, out_shards=b.ndim == 3, name="mm_dw")
    return _mm(g, b, tb=True, name="mm_dx"), jnp.zeros_like(b), dw


matmul.defvjp(_matmul_fwd, _matmul_bwd)


@jax.custom_vjp
def sq_relu_mlp(x, w1, w1_grad, w2, w2_grad):
    return _mm(_mm(x, w1, epilogue="sq_relu", name="mlp_up"), w2, name="mlp_down")


def _sq_relu_mlp_fwd(x, w1, w1_grad, w2, w2_grad):
    hid = _mm(x, w1, epilogue="sq_relu", name="mlp_up")
    return _mm(hid, w2, name="mlp_down"), (x, w1, w2, hid)


def _sq_relu_mlp_bwd(res, g):
    x, w1, w2, hid = res
    d_pre = _mm(g, w2, tb=True, epilogue="d_sq_relu", extra=hid, name="mlp_down_dx")
    dw2 = _mm(hid, g, ta=True, out_shards=w2.ndim == 3, name="mlp_down_dw")
    dw1 = _mm(x, d_pre, ta=True, out_shards=w1.ndim == 3, name="mlp_up_dw")
    return _mm(d_pre, w1, tb=True, name="mlp_up_dx"), jnp.zeros_like(w1), dw1, jnp.zeros_like(w2), dw2


sq_relu_mlp.defvjp(_sq_relu_mlp_fwd, _sq_relu_mlp_bwd)


NORM_TILE_PREFS = (768, 512, 256, 128)


def _norm_mod_pieces(h_ref, w_ref, shift_ref, scale_ref, tile, tm, ctx_len):
    x = h_ref[...]
    rstd = lax.rsqrt(jnp.mean(x * x, axis=1, keepdims=True) + NORM_EPS)
    xn = x * rstd
    is_ctx = tile * tm + lax.broadcasted_iota(jnp.int32, (tm, 1), 0) < ctx_len
    scale = jnp.where(is_ctx, scale_ref[1:2, :], scale_ref[0:1, :])
    shift = jnp.where(is_ctx, shift_ref[1:2, :], shift_ref[0:1, :])
    return xn, rstd, is_ctx, scale, shift


def _norm_mod_fwd_call(h, w, shift, scale, ctx_len):
    t, d = h.shape
    tm = _tile(t, NORM_TILE_PREFS)

    def body(h_ref, w_ref, shift_ref, scale_ref, u_ref):
        xn, _, _, sc, sh = _norm_mod_pieces(h_ref, w_ref, shift_ref, scale_ref, pl.program_id(0), tm, ctx_len)
        u_ref[...] = xn * w_ref[...] * (1.0 + sc) + sh

    row = pl.BlockSpec((tm, d), lambda i: (i, 0))
    return pl.pallas_call(
        body, name="norm_mod_fwd",
        grid=(t // tm,),
        in_specs=[row, pl.BlockSpec((1, d), lambda i: (0, 0)), pl.BlockSpec((2, d), lambda i: (0, 0)),
                  pl.BlockSpec((2, d), lambda i: (0, 0))],
        out_specs=row,
        out_shape=jax.ShapeDtypeStruct((t, d), F32),
        compiler_params=_params("parallel"),
    )(h, w, shift, scale)


def _norm_mod_bwd_call(h, w, shift, scale, du, ctx_len):
    t, d = h.shape
    tm = _tile(t, NORM_TILE_PREFS)

    def body(h_ref, w_ref, shift_ref, scale_ref, du_ref, dh_ref, sums_ref):
        xn, rstd, is_ctx, sc, _ = _norm_mod_pieces(h_ref, w_ref, shift_ref, scale_ref, pl.program_id(0), tm, ctx_len)
        du = du_ref[...]
        wv = w_ref[...]
        dy = du * (1.0 + sc)
        dxn = dy * wv
        dh_ref[...] = rstd * (dxn - xn * jnp.mean(dxn * xn, axis=1, keepdims=True))
        dsc = du * (xn * wv)

        def colsum(v):
            return jnp.sum(v, axis=0, keepdims=True)

        dshift_all, dshift_ctx = colsum(du), colsum(jnp.where(is_ctx, du, 0.0))
        dscale_all, dscale_ctx = colsum(dsc), colsum(jnp.where(is_ctx, dsc, 0.0))
        part = jnp.concatenate([colsum(dy * xn), dshift_all - dshift_ctx, dshift_ctx, dscale_all - dscale_ctx,
                                dscale_ctx, jnp.zeros((3, d), F32)], axis=0)

        @pl.when(pl.program_id(0) == 0)
        def _():
            sums_ref[...] = part

        @pl.when(pl.program_id(0) > 0)
        def _():
            sums_ref[...] += part

    row = pl.BlockSpec((tm, d), lambda i: (i, 0))
    return pl.pallas_call(
        body, name="norm_mod_bwd",
        grid=(t // tm,),
        in_specs=[row, pl.BlockSpec((1, d), lambda i: (0, 0)), pl.BlockSpec((2, d), lambda i: (0, 0)),
                  pl.BlockSpec((2, d), lambda i: (0, 0)), row],
        out_specs=[row, pl.BlockSpec((8, d), lambda i: (0, 0))],
        out_shape=[jax.ShapeDtypeStruct((t, d), F32), jax.ShapeDtypeStruct((8, d), F32)],
        compiler_params=_params("arbitrary"),
    )(h, w, shift, scale, du)


@functools.partial(jax.custom_vjp, nondiff_argnums=(4,))
def norm_mod(h, w, shift, scale, ctx_len):
    return _norm_mod_fwd_call(h, w[None, :], shift, scale, ctx_len)


def _norm_mod_fwd(h, w, shift, scale, ctx_len):
    return _norm_mod_fwd_call(h, w[None, :], shift, scale, ctx_len), (h, w, shift, scale)


def _norm_mod_bwd(ctx_len, res, du):
    h, w, shift, scale = res
    dh, sums = _norm_mod_bwd_call(h, w[None, :], shift, scale, du, ctx_len)
    return dh, sums[0], sums[1:3], sums[3:5]


norm_mod.defvjp(_norm_mod_fwd, _norm_mod_bwd)


def _bf(x):
    return x.astype(MXU_DTYPE)


def _dot(a, b):
    return jnp.dot(_bf(a), _bf(b), preferred_element_type=F32)


def _dot_nt(a, b):
    return lax.dot_general(_bf(a), _bf(b), (((1,), (1,)), ((), ())), preferred_element_type=F32)


def _dot_tn(a, b):
    return lax.dot_general(_bf(a), _bf(b), (((0,), (0,)), ((), ())), preferred_element_type=F32)


ROWWISE_VMEM_BYTES = 20 * 1024 * 1024


@functools.partial(jax.custom_vjp, nondiff_argnums=(1,))
def _split_lanes(x, n):
    w = x.shape[1] // n
    return tuple(x[:, i * w:(i + 1) * w] for i in range(n))


def _split_lanes_fwd(x, n):
    return _split_lanes(x, n), None


def _split_lanes_bwd(n, _, gs):
    return (jnp.concatenate(gs, axis=1),)


_split_lanes.defvjp(_split_lanes_fwd, _split_lanes_bwd)


def _rowwise_tile(t, widths):
    for tm in (768, 512, 256, 128):
        if t % tm == 0 and tm * 8 * sum(widths) <= ROWWISE_VMEM_BYTES:
            return tm
    raise ValueError((t, widths))


def _rowwise(name, fn, out_w, ctx_len):
    def is_ctx(tm):
        return pl.program_id(0) * tm + lax.broadcasted_iota(jnp.int32, (tm, 1), 0) < ctx_len

    def specs(params, rows, tm):
        return ([pl.BlockSpec(p.shape, lambda i: (0, 0)) for p in params]
                + [pl.BlockSpec((tm, r.shape[1]), lambda i: (i, 0)) for r in rows])

    def fwd_call(params, rows):
        t = rows[0].shape[0]
        tm = _rowwise_tile(t, [r.shape[1] for r in rows] + [out_w])
        n_p = len(params)

        def body(*refs):
            vals = [r[...] for r in refs[:-1]]
            refs[-1][...] = fn(is_ctx(tm), tuple(vals[:n_p]), tuple(vals[n_p:]))

        return pl.pallas_call(
            body, name=name + "_fwd", grid=(t // tm,),
            in_specs=specs(params, rows, tm),
            out_specs=pl.BlockSpec((tm, out_w), lambda i: (i, 0)),
            out_shape=jax.ShapeDtypeStruct((t, out_w), F32),
            compiler_params=_params("parallel"),
        )(*params, *rows)

    def bwd_call(params, rows, dout):
        t = rows[0].shape[0]
        tm = _rowwise_tile(t, [2 * r.shape[1] for r in rows] + [out_w])
        n_p, n_r = len(params), len(rows)

        def body(*refs):
            vals = [r[...] for r in refs[:n_p + n_r + 1]]
            dx_refs = refs[n_p + n_r + 1:n_p + 2 * n_r + 1]
            dp_refs = refs[n_p + 2 * n_r + 1:]
            ctx_rows = is_ctx(tm)
            _, vjp = jax.vjp(lambda p, x: fn(ctx_rows, p, x), tuple(vals[:n_p]), tuple(vals[n_p:n_p + n_r]))
            dp, dx = vjp(vals[-1])
            for ref, v in zip(dx_refs, dx):
                ref[...] = v

            @pl.when(pl.program_id(0) == 0)
            def _():
                for ref, v in zip(dp_refs, dp):
                    ref[...] = v

            @pl.when(pl.program_id(0) > 0)
            def _():
                for ref, v in zip(dp_refs, dp):
                    ref[...] += v

        row_specs = [pl.BlockSpec((tm, r.shape[1]), lambda i: (i, 0)) for r in rows]
        outs = pl.pallas_call(
            body, name=name + "_bwd", grid=(t // tm,),
            in_specs=specs(params, rows, tm) + [pl.BlockSpec((tm, out_w), lambda i: (i, 0))],
            out_specs=row_specs + [pl.BlockSpec(p.shape, lambda i: (0, 0)) for p in params],
            out_shape=[jax.ShapeDtypeStruct(r.shape, F32) for r in rows]
            + [jax.ShapeDtypeStruct(p.shape, F32) for p in params],
            compiler_params=_params("arbitrary"),
        )(*params, *rows, dout)
        return tuple(outs[n_r:]), tuple(outs[:n_r])

    @jax.custom_vjp
    def op(params, rows):
        return fwd_call(params, rows)

    op.defvjp(lambda params, rows: (fwd_call(params, rows), (params, rows)),
              lambda res, g: bwd_call(res[0], res[1], g))
    return op


def _silu(x):
    return x * jax.nn.sigmoid(x)


def _ret_finish_tile(is_ctx, params, rows):
    (gn_w,), (y_f, y_b, gate) = params, rows
    heads = []
    for yh in _split_lanes(y_f + y_b, RET_HEADS):
        yc = yh - jnp.mean(yh, axis=1, keepdims=True)
        heads.append(yc * lax.rsqrt(jnp.mean(yc * yc, axis=1, keepdims=True) + NORM_EPS))
    return jnp.concatenate(heads, axis=1) * gn_w * _silu(gate)


def _ssd_finish_tile(is_ctx, params, rows):
    (d_skip, norm_w), (y_f, y_b, xs, z) = params, rows
    g = (y_f + y_b + d_skip * xs) * _silu(z)
    return g * lax.rsqrt(jnp.mean(g * g, axis=1, keepdims=True) + NORM_EPS) * norm_w


def _gate_merge_tile(is_ctx, params, rows):
    y0, y1, y2, logits = rows
    return sum(jax.nn.sigmoid(g) * y for g, y in zip(_split_lanes(logits, N_BRANCH), (y0, y1, y2)))


def _qk_prep_tile(is_ctx, params, rows):
    (q_w, k_w), (aq, ak, cos, sin) = params, rows
    out = []
    for x, norm_w, heads in ((aq, q_w, ATTN_HEADS), (ak, k_w, ATTN_KV_HEADS)):
        for xh in _split_lanes(x, heads):
            xn = xh * lax.rsqrt(jnp.mean(xh * xh, axis=1, keepdims=True) + NORM_EPS) * norm_w
            x1, x2 = _split_lanes(xn, 2)
            out += [x1 * cos - x2 * sin, x1 * sin + x2 * cos]
    return jnp.concatenate(out, axis=1)


def _ret_prep_tile(is_ctx, params, rows):
    rq, rk, cos, sin = rows
    out = []
    for x, scale in ((rq, 1.0), (rk, RET_DK ** -0.5)):
        for xh in _split_lanes(x, RET_HEADS):
            x1, x2 = _split_lanes(xh, 2)
            out += [(x1 * cos - x2 * sin) * scale, (x1 * sin + x2 * cos) * scale]
    return jnp.concatenate(out, axis=1)


def _loss_rows_tile(is_ctx, params, rows):
    (norm_w,), (h, target) = params, rows
    y = h * lax.rsqrt(jnp.mean(h * h, axis=1, keepdims=True) + NORM_EPS) * norm_w
    err = jnp.mean(jnp.square(y - target), axis=1, keepdims=True)
    return jnp.broadcast_to(err, (h.shape[0], LANES))


def _gated_residual_tile(is_ctx, params, rows):
    (gate,), (h, update) = params, rows
    return h + jnp.where(is_ctx, gate[1:2, :], gate[0:1, :]) * update


NEG_BIG = -1e30


def _attn_tiles(t, ctx_len, backward=False):
    tq = _tile(ctx_len, (256, 128))
    assert t % tq == 0 and ctx_len % tq == 0
    tk = _tile(t, (1408, 768, 512, 256, 128) if backward else (2816, 1408, 768, 512, 256, 128))
    return tq, tk


def _head_scores(q_ref, k_bf, g, ki, tk, ctx_len, masked):
    q = (q_ref[0, g] * (ATTN_HEAD_DIM ** -0.5)).astype(MXU_DTYPE)
    s = _dot_nt(q, k_bf)
    if masked:
        col = ki * tk + lax.broadcasted_iota(jnp.int32, s.shape, 1)
        s = jnp.where(col < ctx_len, s, NEG_BIG)
    return q, s


def _attn_cases(qi, ki, tq, tk, ctx_len, compute):
    ctx_q = (qi + 1) * tq <= ctx_len

    @pl.when(jnp.logical_not(ctx_q))
    def _():
        compute(False)

    @pl.when(jnp.logical_and(ctx_q, ki * tk < ctx_len))
    def _():
        compute(True)


def _attn_fwd_call(q, k, v, ctx_len):
    kvh, grp, t, hd = q.shape
    tq, tk = _attn_tiles(t, ctx_len)
    nkb = t // tk

    def body(q_ref, k_ref, v_ref, o_ref, lse_ref, m_sc, l_sc, acc_sc):
        qi, ki = pl.program_id(1), pl.program_id(2)

        @pl.when(ki == 0)
        def _():
            m_sc[...] = jnp.full(m_sc.shape, NEG_BIG, F32)
            l_sc[...] = jnp.zeros(l_sc.shape, F32)
            acc_sc[...] = jnp.zeros(acc_sc.shape, F32)

        def compute(masked):
            k_bf, v_bf = _bf(k_ref[0]), _bf(v_ref[0])
            for g in range(grp):
                _, s = _head_scores(q_ref, k_bf, g, ki, tk, ctx_len, masked)
                m_prev = m_sc[g]
                m_new = jnp.maximum(m_prev, jnp.max(s, axis=1, keepdims=True))
                alpha = jnp.exp(m_prev - m_new)
                p = jnp.exp(s - m_new)
                l_sc[g] = alpha * l_sc[g] + jnp.sum(p, axis=1, keepdims=True)
                acc_sc[g] = alpha * acc_sc[g] + _dot(p, v_bf)
                m_sc[g] = m_new

        _attn_cases(qi, ki, tq, tk, ctx_len, compute)

        @pl.when(ki == nkb - 1)
        def _():
            o_ref[0] = acc_sc[...] / l_sc[...]
            lse_ref[0] = m_sc[...] + jnp.log(l_sc[...])

    return pl.pallas_call(
        body, name="attn_fwd",
        grid=(kvh, t // tq, nkb),
        in_specs=[pl.BlockSpec((1, grp, tq, hd), lambda h, i, j: (h, 0, i, 0)),
                  pl.BlockSpec((1, tk, hd), lambda h, i, j: (h, j, 0)),
                  pl.BlockSpec((1, tk, hd), lambda h, i, j: (h, j, 0))],
        out_specs=[pl.BlockSpec((1, grp, tq, hd), lambda h, i, j: (h, 0, i, 0)),
                   pl.BlockSpec((1, grp, tq, 1), lambda h, i, j: (h, 0, i, 0))],
        out_shape=[jax.ShapeDtypeStruct(q.shape, F32), jax.ShapeDtypeStruct((kvh, grp, t, 1), F32)],
        scratch_shapes=[pltpu.VMEM((grp, tq, 1), F32), pltpu.VMEM((grp, tq, 1), F32), pltpu.VMEM((grp, tq, hd), F32)],
        compiler_params=_params("parallel", "parallel", "arbitrary"),
    )(q, k, v)


def _head_probs(q_ref, k_bf, v_bf, o_ref, do_ref, lse_ref, g, ki, tk, ctx_len, masked):
    q, s = _head_scores(q_ref, k_bf, g, ki, tk, ctx_len, masked)
    do = do_ref[0, g]
    delta = jnp.sum(do * o_ref[0, g], axis=1, keepdims=True)
    p = jnp.exp(s - lse_ref[0, g])
    do = _bf(do)
    ds = p * (_dot_nt(do, v_bf) - delta)
    return q, do, p, ds


def _attn_bwd_call(q, k, v, o, lse, do, ctx_len):
    kvh, grp, t, hd = q.shape
    tq, tk = _attn_tiles(t, ctx_len, backward=True)
    nqb, nkb = t // tq, t // tk

    def body(q_ref, k_ref, v_ref, o_ref, lse_ref, do_ref, dq_hbm, dk_ref, dv_ref, dq_sc, dk_sc, dv_sc, dq_out,
             dq_sem):
        hi, ki, qi = pl.program_id(0), pl.program_id(1), pl.program_id(2)
        rows = pl.ds(pl.multiple_of(qi * tq, tq), tq)

        @pl.when(ki == 0)
        def _():
            dq_sc[:, rows, :] = jnp.zeros((grp, tq, hd), F32)

        @pl.when(qi == 0)
        def _():
            dk_sc[...] = jnp.zeros(dk_sc.shape, F32)
            dv_sc[...] = jnp.zeros(dv_sc.shape, F32)

        def compute(masked):
            k_bf, v_bf = _bf(k_ref[0]), _bf(v_ref[0])
            dk_part = jnp.zeros(dk_sc.shape, F32)
            dv_part = jnp.zeros(dv_sc.shape, F32)
            for g in range(grp):
                qs, dob, p, ds = _head_probs(q_ref, k_bf, v_bf, o_ref, do_ref, lse_ref, g, ki, tk, ctx_len, masked)
                dv_part = dv_part + _dot_tn(p, dob)
                dk_part = dk_part + _dot_tn(ds, qs)
                dq_sc[g, rows, :] += _dot(ds, k_bf)
            dk_sc[...] += dk_part
            dv_sc[...] += dv_part

        _attn_cases(qi, ki, tq, tk, ctx_len, compute)

        @pl.when(ki == nkb - 1)
        def _():
            dq_out[...] = dq_sc[:, rows, :] * (hd ** -0.5)
            done = pltpu.make_async_copy(dq_out, dq_hbm.at[hi, :, rows, :], dq_sem)
            done.start()
            done.wait()

        @pl.when(qi == nqb - 1)
        def _():
            dk_ref[0] = dk_sc[...]
            dv_ref[0] = dv_sc[...]

    qspec = pl.BlockSpec((1, grp, tq, hd), lambda h, j, i: (h, 0, i, 0))
    kspec = pl.BlockSpec((1, tk, hd), lambda h, j, i: (h, j, 0))
    return pl.pallas_call(
        body, name="attn_bwd",
        grid=(kvh, t // tk, nqb),
        in_specs=[qspec, kspec, kspec, qspec, pl.BlockSpec((1, grp, tq, 1), lambda h, j, i: (h, 0, i, 0)), qspec],
        out_specs=[pl.BlockSpec(memory_space=pl.ANY), kspec, kspec],
        out_shape=[jax.ShapeDtypeStruct(q.shape, F32), jax.ShapeDtypeStruct(k.shape, F32),
                   jax.ShapeDtypeStruct(v.shape, F32)],
        scratch_shapes=[pltpu.VMEM((grp, t, hd), F32), pltpu.VMEM((tk, hd), F32), pltpu.VMEM((tk, hd), F32),
                        pltpu.VMEM((grp, tq, hd), F32), pltpu.SemaphoreType.DMA],
        compiler_params=_params("arbitrary", "arbitrary", "arbitrary"),
    )(q, k, v, o, lse, do)


@functools.partial(jax.custom_vjp, nondiff_argnums=(3,))
def attention(q, k, v, ctx_len):
    return _attn_fwd_call(q, k, v, ctx_len)[0]


def _attention_fwd(q, k, v, ctx_len):
    o, lse = _attn_fwd_call(q, k, v, ctx_len)
    return o, (q, k, v, o, lse)


def _attention_bwd(ctx_len, res, do):
    q, k, v, o, lse = res
    return tuple(_attn_bwd_call(q, k, v, o, lse, do, ctx_len))


attention.defvjp(_attention_fwd, _attention_bwd)


def _chunk_order(step, n_chunks, n_ctx_chunks, reverse):
    if not reverse:
        return step
    return jnp.where(step < n_ctx_chunks, n_ctx_chunks - 1 - step, n_chunks + n_ctx_chunks - 1 - step)


def _scan_masks(chunk, reverse):
    row = lax.broadcasted_iota(jnp.int32, (chunk, chunk), 0)
    col = lax.broadcasted_iota(jnp.int32, (chunk, chunk), 1)
    vis = (col >= row) if reverse else (col <= row)
    vis_t = (row >= col) if reverse else (row <= col)
    return vis, vis.astype(F32), vis_t.astype(F32)


def _cum_decay(a_col, a_row, vis_f):
    hi = lax.Precision.HIGHEST
    cum_col = jnp.dot(vis_f, a_col, precision=hi, preferred_element_type=F32)
    cum_row = lax.dot_general(a_row, vis_f, (((1,), (1,)), ((), ())), precision=hi, preferred_element_type=F32)
    total = jnp.sum(a_col, axis=0, keepdims=True)
    return cum_col, cum_row, total


SCAN_CHUNKS_PER_STEP = 2


def _scan_specs(t, ctx_len, reverse, backward, widths):
    rows = SCAN_CHUNK * SCAN_CHUNKS_PER_STEP
    assert t % rows == 0 and ctx_len % rows == 0
    n_blocks, n_ctx_blocks = t // rows, ctx_len // rows

    def order(i):
        step = (n_blocks - 1 - i) if backward else i
        return _chunk_order(step, n_blocks, n_ctx_blocks, reverse)

    halves = list(range(SCAN_CHUNKS_PER_STEP))
    if reverse != backward:
        halves.reverse()
    return [pl.BlockSpec((rows, w), lambda i: (order(i), 0)) for w in widths], order, n_blocks, halves


def _scan_fwd_call(q, k, v, a_col, a_row, *, groups, per_group, dk, dv, ctx_len, reverse):
    t = q.shape[0]
    chunk, per_step = SCAN_CHUNK, SCAN_CHUNKS_PER_STEP
    heads = groups * per_group
    (q_spec, k_spec, v_spec, acol_spec), order, n_blocks, halves = _scan_specs(
        t, ctx_len, reverse, False, (groups * dk, groups * dk, heads * dv, LANES))

    def body(q_ref, k_ref, v_ref, acol_ref, arow_ref, y_ref, st_ref, s_sc):
        @pl.when(pl.program_id(0) == 0)
        def _():
            s_sc[...] = jnp.zeros(s_sc.shape, F32)

        vis, vis_f, _ = _scan_masks(chunk, reverse)
        for half in halves:
            rows = slice(half * chunk, (half + 1) * chunk)
            st_ref[half] = s_sc[...]
            cum_col, cum_row, total = _cum_decay(acol_ref[rows, :], arow_ref[:, rows], vis_f)
            for g in range(groups):
                qg = q_ref[rows, g * dk:(g + 1) * dk]
                kg = k_ref[rows, g * dk:(g + 1) * dk]
                qk = _dot_nt(qg, kg)
                for r in range(per_group):
                    h = g * per_group + r
                    ccol = cum_col[:, h:h + 1]
                    decay = jnp.exp(jnp.where(vis, ccol - cum_row[h:h + 1, :], NEG_BIG))
                    vh = v_ref[rows, h * dv:(h + 1) * dv]
                    s_in = s_sc[h]
                    y = _dot(qk * decay, vh) + jnp.exp(ccol) * _dot(qg, s_in)
                    y_ref[rows, h * dv:(h + 1) * dv] = y
                    tot = total[:, h:h + 1]
                    s_sc[h] = jnp.exp(tot) * s_in + _dot_tn(kg * jnp.exp(tot - ccol), vh)

    return pl.pallas_call(
        body, name="scan_fwd",
        grid=(n_blocks,),
        in_specs=[q_spec, k_spec, v_spec, acol_spec, pl.BlockSpec((8, chunk * per_step), lambda i: (0, order(i)))],
        out_specs=[v_spec, pl.BlockSpec((per_step, heads, dk, dv), lambda i: (order(i), 0, 0, 0))],
        out_shape=[jax.ShapeDtypeStruct(v.shape, F32),
                   jax.ShapeDtypeStruct((n_blocks * per_step, heads, dk, dv), F32)],
        scratch_shapes=[pltpu.VMEM((heads, dk, dv), F32)],
        compiler_params=_params("arbitrary"),
    )(q, k, v, a_col, a_row)


def _scan_bwd_call(q, k, v, a_col, a_row, states, dy, *, groups, per_group, dk, dv, ctx_len, reverse):
    t = q.shape[0]
    chunk, per_step = SCAN_CHUNK, SCAN_CHUNKS_PER_STEP
    heads = groups * per_group
    (q_spec, k_spec, v_spec, acol_spec), order, n_blocks, halves = _scan_specs(
        t, ctx_len, reverse, True, (groups * dk, groups * dk, heads * dv, LANES))
    arow_spec = pl.BlockSpec((8, chunk * per_step), lambda i: (0, order(i)))
    last = 0 if reverse else chunk - 1

    def body(q_ref, k_ref, v_ref, acol_ref, arow_ref, st_ref, dy_ref, dq_ref, dk_ref, dv_ref, da_ref, dat_ref,
             ds_sc):
        @pl.when(pl.program_id(0) == 0)
        def _():
            ds_sc[...] = jnp.zeros(ds_sc.shape, F32)

        vis, vis_f, vis_tf = _scan_masks(chunk, reverse)
        lane = lax.broadcasted_iota(jnp.int32, (chunk, LANES), 1)
        row = lax.broadcasted_iota(jnp.int32, (chunk, LANES), 0)
        sub = lax.broadcasted_iota(jnp.int32, (8, chunk), 0)
        for half in halves:
            rows = slice(half * chunk, (half + 1) * chunk)
            cum_col, cum_row, total = _cum_decay(acol_ref[rows, :], arow_ref[:, rows], vis_f)
            dcum = jnp.zeros((chunk, LANES), F32)
            dcum_t = jnp.zeros((8, chunk), F32)
            for g in range(groups):
                qg = q_ref[rows, g * dk:(g + 1) * dk]
                kg = k_ref[rows, g * dk:(g + 1) * dk]
                qk = _dot_nt(qg, kg)
                dq_g = jnp.zeros((chunk, dk), F32)
                dk_g = jnp.zeros((chunk, dk), F32)
                for r in range(per_group):
                    h = g * per_group + r
                    ccol = cum_col[:, h:h + 1]
                    decay = jnp.exp(jnp.where(vis, ccol - cum_row[h:h + 1, :], NEG_BIG))
                    vh = v_ref[rows, h * dv:(h + 1) * dv]
                    dyh = dy_ref[rows, h * dv:(h + 1) * dv]
                    s_in = st_ref[half, h]
                    ds_out = ds_sc[h]
                    tot = total[:, h:h + 1]
                    e_in = jnp.exp(ccol)
                    e_out = jnp.exp(tot - ccol)
                    e_tot = jnp.exp(tot)
                    k_out = kg * e_out
                    dv_ref[rows, h * dv:(h + 1) * dv] = _dot_tn(qk * decay, dyh) + _dot(k_out, ds_out)
                    dqk = _dot_nt(dyh, vh) * decay
                    dq_in = e_in * _dot_nt(dyh, s_in)
                    dk_out = e_out * _dot_nt(vh, ds_out)
                    dq_h = _dot(dqk, kg) + dq_in
                    dk_h = _dot_tn(dqk, qg) + dk_out
                    s_out = e_tot * s_in + _dot_tn(k_out, vh)
                    edge = jnp.sum(jnp.sum(s_out * ds_out, axis=1, keepdims=True), axis=0, keepdims=True)
                    w_seg = dqk * qk
                    dcum_h = (jnp.sum(w_seg, axis=1, keepdims=True) + jnp.sum(dq_in * qg, axis=1, keepdims=True)
                              - jnp.sum(dk_out * kg, axis=1, keepdims=True))
                    dcum = jnp.where(lane == h, dcum_h + jnp.where(row == last, edge, 0.0), dcum)
                    dcum_t = jnp.where(sub == h, -jnp.sum(w_seg, axis=0, keepdims=True), dcum_t)
                    ds_sc[h] = e_tot * ds_out + _dot_tn(qg, e_in * dyh)
                    dq_g = dq_g + dq_h
                    dk_g = dk_g + dk_h
                dq_ref[rows, g * dk:(g + 1) * dk] = dq_g
                dk_ref[rows, g * dk:(g + 1) * dk] = dk_g
            hi = lax.Precision.HIGHEST
            da_ref[rows, :] = jnp.dot(vis_tf, dcum, precision=hi, preferred_element_type=F32)
            dat_ref[:, rows] = jnp.dot(dcum_t, vis_f, precision=hi, preferred_element_type=F32)

    return pl.pallas_call(
        body, name="scan_bwd",
        grid=(n_blocks,),
        in_specs=[q_spec, k_spec, v_spec, acol_spec, arow_spec,
                  pl.BlockSpec((per_step, heads, dk, dv), lambda i: (order(i), 0, 0, 0)), v_spec],
        out_specs=[q_spec, k_spec, v_spec, acol_spec, arow_spec],
        out_shape=[jax.ShapeDtypeStruct(q.shape, F32), jax.ShapeDtypeStruct(k.shape, F32),
                   jax.ShapeDtypeStruct(v.shape, F32), jax.ShapeDtypeStruct((t, LANES), F32),
                   jax.ShapeDtypeStruct((8, t), F32)],
        scratch_shapes=[pltpu.VMEM((heads, dk, dv), F32)],
        compiler_params=_params("arbitrary"),
    )(q, k, v, a_col, a_row, states, dy)


def _decay_layouts(a):
    t, heads = a.shape
    a_col = jnp.pad(a, ((0, 0), (0, LANES - heads)))
    a_row = jnp.pad(a.T, ((0, 8 - heads), (0, 0)))
    return a_col, a_row


@functools.partial(jax.custom_vjp, nondiff_argnums=(4,))
def linear_scan(q, k, v, a, cfg):
    a_col, a_row = _decay_layouts(a)
    return _scan_fwd_call(q, k, v, a_col, a_row, **dict(cfg))[0]


def _linear_scan_fwd(q, k, v, a, cfg):
    a_col, a_row = _decay_layouts(a)
    y, states = _scan_fwd_call(q, k, v, a_col, a_row, **dict(cfg))
    return y, (q, k, v, a, states)


def _linear_scan_bwd(cfg, res, dy):
    q, k, v, a, states = res
    a_col, a_row = _decay_layouts(a)
    dq, dk, dv, da, da_t = _scan_bwd_call(q, k, v, a_col, a_row, states, dy, **dict(cfg))
    heads = a.shape[1]
    return dq, dk, dv, da[:, :heads] + da_t[:heads].T


linear_scan.defvjp(_linear_scan_fwd, _linear_scan_bwd)


def _scan_cfg(groups, per_group, dk, dv, ctx_len, reverse):
    return (("groups", groups), ("per_group", per_group), ("dk", dk), ("dv", dv), ("ctx_len", ctx_len),
            ("reverse", reverse))


def _axial_tables(n_lat, n_ctx):
    freqs = ATTN_HEAD_DIM // 4
    rows = n_lat // GRID_W
    row = jnp.repeat(jnp.arange(rows, dtype=F32), GRID_W)
    col = jnp.tile(jnp.arange(GRID_W, dtype=F32), rows)
    inv = ROPE_THETA ** (-jnp.arange(freqs, dtype=F32) / freqs)
    ang = jnp.concatenate([row[:, None] * inv, col[:, None] * inv], axis=-1)
    cos = jnp.concatenate([jnp.ones((n_ctx, 2 * freqs), F32), jnp.cos(ang)], axis=0)
    sin = jnp.concatenate([jnp.zeros((n_ctx, 2 * freqs), F32), jnp.sin(ang)], axis=0)
    return cos, sin


def _seq_tables(t):
    pos = jnp.arange(t, dtype=F32)
    inv = ROPE_THETA ** (-jnp.linspace(0.0, 1.0, RET_DK // 2, dtype=F32))
    ang = pos[:, None] * inv
    return jnp.cos(ang), jnp.sin(ang)


def _pad_w_in(w_in):
    d = w_in.shape[0]
    return jnp.concatenate([w_in[:, :DT_END], jnp.zeros((d, DT_PAD), w_in.dtype), w_in[:, DT_END:],
                            jnp.zeros((d, TAIL_PAD), w_in.dtype)], axis=1)


def _split_proj(p):
    widths = list(IN_SPLITS)
    widths[5] = LANES
    out, off = [], 0
    for w in widths:
        out.append(p[:, off:off + w])
        off += w
    out[5] = out[5][:, :DT_COLS]
    return out


def _mixer(u, w, wq, layer, n_ctx, tables):
    t = u.shape[0]
    attn_rope, ret_rope, seg_first, seg_last = tables
    proj = matmul(u, wq["w_in"][layer], _pad_w_in(w["w_in"][layer]))
    aq, ak, av, z, xbc_raw, dt_raw, rq, rk, rv, rg, gate_logits = _split_proj(proj)

    q_width = ATTN_HEADS * ATTN_HEAD_DIM
    qk = _rowwise("qk_prep", _qk_prep_tile, q_width + ATTN_KV_HEADS * ATTN_HEAD_DIM, n_ctx)(
        (w["attn_q_norm"][layer][None, :], w["attn_k_norm"][layer][None, :]), (aq, ak) + attn_rope)
    q4 = qk[:, :q_width].reshape(t, ATTN_KV_HEADS, ATTN_GROUP, ATTN_HEAD_DIM).transpose(1, 2, 0, 3)
    k3 = qk[:, q_width:].reshape(t, ATTN_KV_HEADS, ATTN_HEAD_DIM).transpose(1, 0, 2)
    v3 = av.reshape(t, ATTN_KV_HEADS, ATTN_HEAD_DIM).transpose(1, 0, 2)
    o4 = attention(q4, k3, v3, n_ctx)
    br_attn = o4.transpose(2, 0, 1, 3).reshape(t, ATTN_HEADS * ATTN_HEAD_DIM)

    cw, cb = w["ssd_conv_w"][layer], w["ssd_conv_b"][layer]
    zero_row = jnp.zeros((1, xbc_raw.shape[1]), F32)
    prev = jnp.concatenate([zero_row, xbc_raw[:-1]], axis=0) * (1.0 - seg_first)
    nxt = jnp.concatenate([xbc_raw[1:], zero_row], axis=0) * (1.0 - seg_last)
    xbc = jax.nn.silu(prev * cw[0] + xbc_raw * cw[1] + nxt * cw[2] + cb)
    gn = SSD_GROUPS * SSD_STATE
    xs = xbc[:, :SSD_D_INNER]
    bm = xbc[:, SSD_D_INNER:SSD_D_INNER + gn]
    cm = xbc[:, SSD_D_INNER + gn:]
    dt = jax.nn.softplus(dt_raw.reshape(t, 2, SSD_HEADS) + w["ssd_dt_bias"][layer])
    a_neg = -jnp.exp(w["ssd_a_log"][layer])
    xs_h = xs.reshape(t, SSD_HEADS, SSD_HEAD_DIM)
    y_ssd = []
    for d, reverse in ((0, False), (1, True)):
        dtd = dt[:, d]
        cfg = _scan_cfg(SSD_GROUPS, SSD_HEADS // SSD_GROUPS, SSD_STATE, SSD_HEAD_DIM, n_ctx, reverse)
        y_ssd.append(linear_scan(cm, bm, (xs_h * dtd[:, :, None]).reshape(t, SSD_D_INNER), dtd * a_neg[d], cfg))
    d_skip = jnp.repeat(w["ssd_d"][layer], SSD_HEAD_DIM)[None, :]
    br_ssd = _rowwise("ssd_finish", _ssd_finish_tile, SSD_D_INNER, n_ctx)(
        (d_skip, w["ssd_norm_w"][layer][None, :]), (y_ssd[0], y_ssd[1], xs, z))

    lg = -jnp.exp(w["ret_log_decay"][layer])
    ret_width = RET_HEADS * RET_DK
    ret_qk = _rowwise("ret_prep", _ret_prep_tile, 2 * ret_width, n_ctx)((), (rq, rk) + ret_rope)
    rq_r, rk_r = ret_qk[:, :ret_width], ret_qk[:, ret_width:]
    y_ret = []
    for d, reverse in ((0, False), (1, True)):
        cfg = _scan_cfg(RET_HEADS, 1, RET_DK, RET_DV, n_ctx, reverse)
        y_ret.append(linear_scan(rq_r, rk_r, rv, jnp.broadcast_to(lg[d][None, :], (t, RET_HEADS)), cfg))
    br_ret = _rowwise("ret_finish", _ret_finish_tile, RET_HEADS * RET_DV, n_ctx)(
        (w["ret_gn_w"][layer][None, :],), (y_ret[0], y_ret[1], rg))

    projected = tuple(matmul(br, wq["w_branch"][layer][j], w["w_branch"][layer][j])
                      for j, br in enumerate((br_attn, br_ssd, br_ret)))
    merged = _rowwise("gate_merge", _gate_merge_tile, D_MODEL, n_ctx)((), projected + (gate_logits,))
    return matmul(merged, wq["w_out"][layer], w["w_out"][layer])


def _local_loss(w, x, c, ctx, target, wq):
    n, m = x.shape[0], ctx.shape[0]
    t = n + m
    pos = jnp.arange(t)[:, None]
    seg_first = ((pos == 0) | (pos == m)).astype(F32)
    seg_last = ((pos == m - 1) | (pos == t - 1)).astype(F32)
    tables = (_axial_tables(n, m), _seq_tables(t), seg_first, seg_last)
    h = jnp.concatenate([ctx, x], axis=0)
    cond = jax.nn.silu(jnp.stack([c, w["c_ctx"]], axis=0))
    cond8 = jnp.concatenate([cond, jnp.zeros((6, D_MODEL), F32)], axis=0)
    for layer in range(DEPTH):
        mod = (matmul(cond8, wq["w_mod"][layer], w["w_mod"][layer])[:2] + w["b_mod"][layer]).reshape(2, 6, D_MODEL)

        u = norm_mod(h, w["norm1_w"][layer], mod[:, 0], mod[:, 1], m)
        residual = _rowwise("gated_residual", _gated_residual_tile, D_MODEL, m)
        h = residual((mod[:, 2],), (h, _mixer(u, w, wq, layer, m, tables)))
        v = norm_mod(h, w["norm2_w"][layer], mod[:, 3], mod[:, 4], m)
        mlp = sq_relu_mlp(v, wq["w_mlp1"][layer], w["w_mlp1"][layer], wq["w_mlp2"][layer], w["w_mlp2"][layer])
        residual = _rowwise("gated_residual", _gated_residual_tile, D_MODEL, m)
        h = residual((mod[:, 5],), (h, mlp))
    per_token = _rowwise("loss_rows", _loss_rows_tile, LANES, 0)((w["final_norm_w"][None, :],), (h[m:], target))
    return 0.5 * jnp.sum(per_token[:, 0])


def _coords():
    return lax.axis_index("x"), lax.axis_index("y"), lax.axis_index("c")


def _all_gather(blocks, name):
    n = len(blocks)

    def body(*refs):
        x_refs, out_refs = refs[:n], refs[n:2 * n]
        send_sems, recv_sems, local_sems = refs[2 * n:]
        x, y, c = _coords()
        me, sibling = (x, y, c), (x, y, 1 - c)
        chips = [(1 - x, y), (x, 1 - y), (1 - x, 1 - y)]

        def copy(k, i, blk, to, from_input=False):
            slot = out_refs[i].at[4 * blk[0] + 2 * blk[1] + blk[2]]
            return pltpu.make_async_remote_copy(
                src_ref=x_refs[i] if from_input else slot, dst_ref=slot,
                send_sem=send_sems.at[k * n + i], recv_sem=recv_sems.at[k * n + i],
                device_id=to, device_id_type=pl.DeviceIdType.MESH)

        mine = [pltpu.make_async_copy(x_refs[i], out_refs[i].at[4 * x + 2 * y + c], local_sems.at[i])
                for i in range(n)]
        for cp in mine:
            cp.start()
        first = [copy(0, i, me, sibling, True) for i in range(n)]
        first += [copy(1 + j, i, me, (*chip, c), True) for j, chip in enumerate(chips) for i in range(n)]
        for cp in first:
            cp.start()
        passed = []
        for j, chip in enumerate(chips):
            for i in range(n):
                copy(1 + j, i, (*chip, c), me).wait_recv()
                passed.append(copy(4 + j, i, (*chip, c), sibling))
                passed[-1].start()
        for i in range(n):
            copy(0, i, sibling, me).wait_recv()
        for j, chip in enumerate(chips):
            for i in range(n):
                copy(4 + j, i, (*chip, 1 - c), me).wait_recv()
        for cp in first + passed:
            cp.wait_send()
        for cp in mine:
            cp.wait()

    return pl.pallas_call(
        body, name=name,
        out_shape=[jax.ShapeDtypeStruct((N_DEV,) + b.shape, b.dtype) for b in blocks],
        in_specs=[pl.BlockSpec(memory_space=pl.ANY)] * n,
        out_specs=[pl.BlockSpec(memory_space=pl.ANY)] * n,
        scratch_shapes=[pltpu.SemaphoreType.DMA((7 * n,)), pltpu.SemaphoreType.DMA((7 * n,)),
                        pltpu.SemaphoreType.DMA((n,))],
    )(*blocks)


def _all_to_all(arrays, name):
    n = len(arrays)

    def body(*refs):
        g_refs, out_refs = refs[:n], refs[n:2 * n]
        send_sems, recv_sems, local_sems = refs[2 * n:]
        x, y, c = _coords()
        me = 4 * x + 2 * y + c
        mine = [pltpu.make_async_copy(g_refs[i].at[me], out_refs[i].at[me], local_sems.at[i]) for i in range(n)]
        for cp in mine:
            cp.start()
        copies = []
        for k in range(1, N_DEV):
            bx, by, bc = (k >> 2) & 1, (k >> 1) & 1, k & 1
            px, py, pc = (1 - x if bx else x), (1 - y if by else y), (1 - c if bc else c)
            peer = 4 * px + 2 * py + pc
            for i in range(n):
                copies.append(pltpu.make_async_remote_copy(
                    src_ref=g_refs[i].at[peer], dst_ref=out_refs[i].at[me],
                    send_sem=send_sems.at[(k - 1) * n + i], recv_sem=recv_sems.at[(k - 1) * n + i],
                    device_id=(px, py, pc), device_id_type=pl.DeviceIdType.MESH))
        for cp in copies:
            cp.start()
        for cp in copies:
            cp.wait_recv()
        for cp in copies:
            cp.wait_send()
        for cp in mine:
            cp.wait()

    return pl.pallas_call(
        body, name=name,
        out_shape=[jax.ShapeDtypeStruct(a.shape, a.dtype) for a in arrays],
        in_specs=[pl.BlockSpec(memory_space=pl.ANY)] * n,
        out_specs=[pl.BlockSpec(memory_space=pl.ANY)] * n,
        scratch_shapes=[pltpu.SemaphoreType.DMA((7 * n,)), pltpu.SemaphoreType.DMA((7 * n,)),
                        pltpu.SemaphoreType.DMA((n,))],
    )(*arrays)


ADAMW_BLOCK_ELEMS = 256 * 1024


def _sum_adamw(g8, w, m, v, name):
    rows, cols = w.shape
    tr = _tile(rows, [r for r in (2048, 1024, 512, 256, 128, 64, 32, 16) if r * cols <= ADAMW_BLOCK_ELEMS])

    def body(g_ref, w_ref, m_ref, v_ref, go_ref, d_ref, mo_ref, vo_ref):
        g = g_ref[0].astype(F32)
        for s in range(1, N_DEV):
            g = g + g_ref[s].astype(F32)
        m_new = ADAM_B1 * m_ref[...] + (1.0 - ADAM_B1) * g
        v_new = ADAM_B2 * v_ref[...] + (1.0 - ADAM_B2) * (g * g)
        m_hat = m_new / (1.0 - ADAM_B1 ** ADAM_STEP)
        v_hat = v_new / (1.0 - ADAM_B2 ** ADAM_STEP)
        go_ref[...] = g
        d_ref[...] = -ADAM_LR * (m_hat / (jnp.sqrt(v_hat) + ADAM_EPS) + ADAM_WD * w_ref[...])
        mo_ref[...] = m_new
        vo_ref[...] = v_new

    spec = pl.BlockSpec((tr, cols), lambda i: (i, 0))
    shape = jax.ShapeDtypeStruct((rows, cols), F32)
    return pl.pallas_call(
        body, name=name,
        grid=(rows // tr,),
        in_specs=[pl.BlockSpec((N_DEV, tr, cols), lambda i: (0, i, 0)), spec, spec, spec],
        out_specs=[spec, spec, spec, spec],
        out_shape=[shape, shape, shape, shape],
        compiler_params=_params("parallel"),
    )(g8, w, m, v)


BIG = ("w_mod", "w_in", "w_branch", "w_out", "w_mlp1", "w_mlp2")
COL_SHARDED = ("w_mod", "w_mlp1")
ROW_SHARDED = ("w_out", "w_mlp2")
SMALL = ("c_ctx", "b_mod", "norm1_w", "norm2_w", "attn_q_norm", "attn_k_norm", "ssd_conv_b", "ssd_dt_bias",
         "ssd_a_log", "ssd_d", "ssd_norm_w", "ret_log_decay", "ret_gn_w", "final_norm_w")
CONV_AXIS = 2
ORDER = ("c_ctx", "w_mod", "b_mod", "norm1_w", "norm2_w", "w_in", "attn_q_norm", "attn_k_norm", "ssd_conv_w",
         "ssd_conv_b", "ssd_dt_bias", "ssd_a_log", "ssd_d", "ssd_norm_w", "ret_log_decay", "ret_gn_w", "w_branch",
         "w_out", "w_mlp1", "w_mlp2", "final_norm_w")


def _compute_weights(gathered):
    wq, carrier = {}, {}
    for name in BIG:
        g = gathered[name]
        per_layer = []
        for layer in range(DEPTH):
            gl = g[:, layer]
            if name in COL_SHARDED:
                per_layer.append(gl)
            elif name in ROW_SHARDED:
                per_layer.append(gl.reshape(-1, gl.shape[-1]))
            elif name == "w_in":
                per_layer.append(_pad_w_in(jnp.concatenate([gl[d] for d in range(N_DEV)], axis=-1)))
            else:
                per_layer.append([jnp.concatenate([gl[d, j] for d in range(N_DEV)], axis=-1) for j in range(N_BRANCH)])
        wq[name] = per_layer
    for name in BIG:
        if name == "w_in":
            carrier[name] = [jnp.zeros((D_MODEL, IN_DIM), F32) for _ in range(DEPTH)]
        else:
            carrier[name] = jax.tree.map(lambda a: jnp.zeros(a.shape, F32), wq[name])
    return wq, carrier


def _grad_shards(gw):
    out = {}
    for name in BIG:
        per_layer = []
        for layer in range(DEPTH):
            g = gw[name][layer]
            if name in COL_SHARDED:
                per_layer.append(g)
            elif name in ROW_SHARDED:
                per_layer.append(g.reshape(N_DEV, -1, g.shape[-1]))
            elif name == "w_in":
                size = IN_DIM // N_DEV
                per_layer.append(jnp.stack([g[:, d * size:(d + 1) * size] for d in range(N_DEV)]))
            else:
                per_layer.append(jnp.stack([gj.reshape(gj.shape[0], N_DEV, -1).transpose(1, 0, 2) for gj in g], axis=1))
        out[name] = jnp.stack(per_layer, axis=1)
    return out


def _pack(arrays, row_multiple):
    flat = jnp.concatenate(arrays, axis=-1)
    n = flat.shape[-1]
    per = LANES * row_multiple
    padded = -(-n // per) * per
    flat = jnp.pad(flat, [(0, 0)] * (flat.ndim - 1) + [(0, padded - n)])
    return flat.reshape(flat.shape[:-1] + (padded // LANES, LANES))


def _unpack(slab, shapes):
    flat = slab.reshape(slab.shape[:-2] + (-1,))
    out, off = [], 0
    for shp in shapes:
        size = math.prod(shp)
        out.append(flat[..., off:off + size].reshape(flat.shape[:-1] + tuple(shp)))
        off += size
    return out


def kernel(x, c, ctx, c_ctx, w_mod, b_mod, norm1_w, norm2_w, w_in, attn_q_norm, attn_k_norm, ssd_conv_w, ssd_conv_b, ssd_dt_bias, ssd_a_log, ssd_d, ssd_norm_w, ret_log_decay, ret_gn_w, w_branch, w_out, w_mlp1, w_mlp2, final_norm_w, loss_target, m_c_ctx, m_w_mod, m_b_mod, m_norm1_w, m_norm2_w, m_w_in, m_attn_q_norm, m_attn_k_norm, m_ssd_conv_w, m_ssd_conv_b, m_ssd_dt_bias, m_ssd_a_log, m_ssd_d, m_ssd_norm_w, m_ret_log_decay, m_ret_gn_w, m_w_branch, m_w_out, m_w_mlp1, m_w_mlp2, m_final_norm_w, v_c_ctx, v_w_mod, v_b_mod, v_norm1_w, v_norm2_w, v_w_in, v_attn_q_norm, v_attn_k_norm, v_ssd_conv_w, v_ssd_conv_b, v_ssd_dt_bias, v_ssd_a_log, v_ssd_d, v_ssd_norm_w, v_ret_log_decay, v_ret_gn_w, v_w_branch, v_w_out, v_w_mlp1, v_w_mlp2, v_final_norm_w):
    args = dict(locals())
    weights = {n: args[n] for n in ORDER}
    mom1 = {n: args["m_" + n] for n in ORDER}
    mom2 = {n: args["v_" + n] for n in ORDER}
    me = 4 * lax.axis_index("x") + 2 * lax.axis_index("y") + lax.axis_index("c")

    gathered = _all_gather([weights[n].astype(MXU_DTYPE) for n in BIG], "gather_weights")
    wq, params = _compute_weights(dict(zip(BIG, gathered)))
    conv_shape = ssd_conv_w.shape
    conv_all = _all_gather([_pack([ssd_conv_w.reshape(-1)], 8)], "gather_conv")[0]
    params["ssd_conv_w"] = jnp.concatenate(list(_unpack(conv_all, [conv_shape])[0]), axis=CONV_AXIS)
    for n in SMALL:
        params[n] = weights[n]

    loss, (gw, gx) = jax.value_and_grad(_local_loss, argnums=(0, 1))(params, x[0], c[0], ctx[0], loss_target[0], wq)
    loss = lax.psum(loss, MESH_AXES)

    g_send = _grad_shards(gw)
    g_recv = _all_to_all([g_send[n].astype(jnp.bfloat16) for n in BIG], "scatter_grads")
    result = {}
    for n, g8 in zip(BIG, g_recv):
        shape = weights[n].shape
        as2d = lambda a: a.reshape(-1, shape[-1])
        outs = _sum_adamw(g8.reshape(N_DEV, -1, shape[-1]), as2d(weights[n]), as2d(mom1[n]), as2d(mom2[n]), "adamw_" + n)
        for kind, arr in zip(("grad", "delta", "new_m", "new_v"), outs):
            result[kind, n] = arr.reshape(shape)

    conv_full_shape = params["ssd_conv_w"].shape
    small_shapes = [weights[n].shape for n in SMALL]
    partial = _pack([gw[n].reshape(-1) for n in SMALL] + [gw["ssd_conv_w"].reshape(-1)], 8)
    parts = _unpack(_all_gather([partial], "gather_small_grads")[0], small_shapes + [conv_full_shape])
    conv_part = lax.dynamic_slice_in_dim(parts[-1], me * conv_shape[CONV_AXIS], conv_shape[CONV_AXIS], CONV_AXIS + 1)
    small_names = list(SMALL) + ["ssd_conv_w"]
    g8_small = _pack([p.reshape(N_DEV, -1) for p in parts[:-1]] + [conv_part.reshape(N_DEV, -1)], 8)
    slabs = [_pack([d[n].reshape(-1) for n in small_names], 8) for d in (weights, mom1, mom2)]
    small_out = [_unpack(s, small_shapes + [conv_shape]) for s in _sum_adamw(g8_small, *slabs, "adamw_small")]
    for kind, small_k in zip(("grad", "delta", "new_m", "new_v"), small_out):
        for n, arr in zip(small_names, small_k):
            result[kind, n] = arr

    outs = [loss, gx[None]]
    for kind in ("grad", "delta", "new_m", "new_v"):
        outs += [result[kind, n] for n in ORDER]
    return tuple(outs)
```

```python
import functools
import math

import jax
import jax.numpy as jnp
from jax import lax
from jax.experimental import pallas as pl
from jax.experimental.pallas import tpu as pltpu

F32 = jnp.float32
MXU_DTYPE = jnp.bfloat16
VMEM_LIMIT_BYTES = 48 * 1024 * 1024
LANES = 128
N_DEV = 8
MESH_AXES = ("x", "y", "c")

D_MODEL = 1024
GRID_W = 64
NORM_EPS = 1e-6
ROPE_THETA = 10000.0
ATTN_HEADS, ATTN_KV_HEADS, ATTN_HEAD_DIM = 8, 2, 64
ATTN_GROUP = ATTN_HEADS // ATTN_KV_HEADS
SSD_HEADS, SSD_HEAD_DIM, SSD_GROUPS, SSD_STATE = 8, 64, 2, 128
SSD_D_INNER = SSD_HEADS * SSD_HEAD_DIM
RET_HEADS, RET_DK, RET_DV = 4, 128, 128
SCAN_CHUNK = 128
N_BRANCH = 3
DEPTH = 2

IN_SPLITS = (512, 128, 128, 512, 1024, 16, 512, 512, 512, 512, 3072)
IN_DIM = sum(IN_SPLITS)
DT_COLS = 16
DT_PAD = LANES - DT_COLS
TAIL_PAD = 128
IN_DIM_PADDED = IN_DIM + DT_PAD + TAIL_PAD
DT_END = sum(IN_SPLITS[:6])

ADAM_LR, ADAM_B1, ADAM_B2, ADAM_EPS, ADAM_WD, ADAM_STEP = 0.001, 0.9, 0.999, 1e-08, 0.01, 10


def _tile(dim, prefs):
    for p in prefs:
        if dim % p == 0:
            return p
    return dim


def _params(*sem):
    return pltpu.CompilerParams(dimension_semantics=sem, vmem_limit_bytes=VMEM_LIMIT_BYTES)


def _mm(a, b, *, ta=False, tb=False, out_shards=False, epilogue=None, extra=None, name):
    if ta:
        kdim, m = a.shape
    else:
        m, kdim = a.shape
    b_shards = b.ndim == 3
    if b_shards:
        rows_b, cols_b = b.shape[1], N_DEV * b.shape[2]
    else:
        rows_b, cols_b = b.shape
    n, kdim_b = (rows_b, cols_b) if tb else (cols_b, rows_b)
    assert kdim == kdim_b, (a.shape, b.shape, ta, tb)
    tm = _tile(m, (1024, 768, 512, 256, 128))
    n_tile_of = n // N_DEV if (out_shards or (b_shards and not tb)) else n
    k_tile_of = kdim // N_DEV if (b_shards and tb) else kdim
    tn = _tile(n_tile_of, (1280, 1024, 768, 512, 384, 256, 128))
    tk = _tile(k_tile_of, (1024, 768, 512, 256, 128))
    dims = (((0 if ta else 1,), (1 if tb else 0,)), ((), ()))

    n_k = kdim // tk

    def body(a_ref, b_ref, *rest):
        o_ref = rest[-1]
        part = lax.dot_general(a_ref[...].astype(MXU_DTYPE), b_ref[...].astype(MXU_DTYPE), dims,
                               preferred_element_type=F32)

        @pl.when(pl.program_id(2) == 0)
        def _():
            o_ref[...] = part

        @pl.when(pl.program_id(2) > 0)
        def _():
            o_ref[...] += part

        if epilogue is not None:
            @pl.when(pl.program_id(2) == n_k - 1)
            def _():
                acc = o_ref[...]
                if epilogue == "sq_relu":
                    r = jnp.maximum(acc, 0.0)
                    o_ref[...] = r * r
                else:
                    o_ref[...] = acc * (2.0 * jnp.sqrt(rest[0][...]))

    a_spec = pl.BlockSpec((tk, tm), lambda i, j, k: (k, i)) if ta else pl.BlockSpec((tm, tk), lambda i, j, k: (i, k))
    if not b_shards:
        b_spec = pl.BlockSpec((tn, tk), lambda i, j, k: (j, k)) if tb else pl.BlockSpec((tk, tn), lambda i, j, k: (k, j))
    elif tb:
        per = b.shape[2] // tk
        b_spec = pl.BlockSpec((None, tn, tk), lambda i, j, k: (k // per, j, k % per))
    else:
        per = b.shape[2] // tn
        b_spec = pl.BlockSpec((None, tk, tn), lambda i, j, k: (j // per, k, j % per))
    if out_shards:
        per_out = n // N_DEV // tn
        out_spec = pl.BlockSpec((None, tm, tn), lambda i, j, k: (j // per_out, i, j % per_out))
        out_shape = jax.ShapeDtypeStruct((N_DEV, m, n // N_DEV), F32)
    else:
        out_spec = pl.BlockSpec((tm, tn), lambda i, j, k: (i, j))
        out_shape = jax.ShapeDtypeStruct((m, n), F32)
    assert epilogue in (None, "sq_relu", "d_sq_relu") and (extra is not None) == (epilogue == "d_sq_relu")
    operands, in_specs = [a, b], [a_spec, b_spec]
    if extra is not None:
        assert not out_shards and extra.shape == (m, n)
        operands.append(extra)
        in_specs.append(out_spec)
    return pl.pallas_call(
        body, name=name,
        grid=(m // tm, n // tn, n_k),
        in_specs=in_specs,
        out_specs=out_spec,
        out_shape=out_shape,
        compiler_params=_params("parallel", "parallel", "arbitrary"),
    )(*operands)


@jax.custom_vjp
def matmul(a, b, b_grad):
    return _mm(a, b, name="mm_fwd")


def _matmul_fwd(a, b, b_grad):
    return _mm(a, b, name="mm_fwd"), (a, b)


def _matmul_bwd(res, g):
    a, b = res
    dw = _mm(a, g, ta=True, out_shards=b.ndim == 3, name="mm_dw")
    return _mm(g, b, tb=True, name="mm_dx"), jnp.zeros_like(b), dw


matmul.defvjp(_matmul_fwd, _matmul_bwd)


@jax.custom_vjp
def sq_relu_mlp(x, w1, w1_grad, w2, w2_grad):
    return _mm(_mm(x, w1, epilogue="sq_relu", name="mlp_up"), w2, name="mlp_down")


def _sq_relu_mlp_fwd(x, w1, w1_grad, w2, w2_grad):
    hid = _mm(x, w1, epilogue="sq_relu", name="mlp_up")
    return _mm(hid, w2, name="mlp_down"), (x, w1, w2, hid)


def _sq_relu_mlp_bwd(res, g):
    x, w1, w2, hid = res
    d_pre = _mm(g, w2, tb=True, epilogue="d_sq_relu", extra=hid, name="mlp_down_dx")
    dw2 = _mm(hid, g, ta=True, out_shards=w2.ndim == 3, name="mlp_down_dw")
    dw1 = _mm(x, d_pre, ta=True, out_shards=w1.ndim == 3, name="mlp_up_dw")
    return _mm(d_pre, w1, tb=True, name="mlp_up_dx"), jnp.zeros_like(w1), dw1, jnp.zeros_like(w2), dw2


sq_relu_mlp.defvjp(_sq_relu_mlp_fwd, _sq_relu_mlp_bwd)


NORM_TILE_PREFS = (768, 512, 256, 128)


def _norm_mod_pieces(h_ref, w_ref, shift_ref, scale_ref, tile, tm, ctx_len):
    x = h_ref[...]
    rstd = lax.rsqrt(jnp.mean(x * x, axis=1, keepdims=True) + NORM_EPS)
    xn = x * rstd
    is_ctx = tile * tm + lax.broadcasted_iota(jnp.int32, (tm, 1), 0) < ctx_len
    scale = jnp.where(is_ctx, scale_ref[1:2, :], scale_ref[0:1, :])
    shift = jnp.where(is_ctx, shift_ref[1:2, :], shift_ref[0:1, :])
    return xn, rstd, is_ctx, scale, shift


def _norm_mod_fwd_call(h, w, shift, scale, ctx_len):
    t, d = h.shape
    tm = _tile(t, NORM_TILE_PREFS)

    def body(h_ref, w_ref, shift_ref, scale_ref, u_ref):
        xn, _, _, sc, sh = _norm_mod_pieces(h_ref, w_ref, shift_ref, scale_ref, pl.program_id(0), tm, ctx_len)
        u_ref[...] = xn * w_ref[...] * (1.0 + sc) + sh

    row = pl.BlockSpec((tm, d), lambda i: (i, 0))
    return pl.pallas_call(
        body, name="norm_mod_fwd",
        grid=(t // tm,),
        in_specs=[row, pl.BlockSpec((1, d), lambda i: (0, 0)), pl.BlockSpec((2, d), lambda i: (0, 0)),
                  pl.BlockSpec((2, d), lambda i: (0, 0))],
        out_specs=row,
        out_shape=jax.ShapeDtypeStruct((t, d), F32),
        compiler_params=_params("parallel"),
    )(h, w, shift, scale)


def _norm_mod_bwd_call(h, w, shift, scale, du, ctx_len):
    t, d = h.shape
    tm = _tile(t, NORM_TILE_PREFS)

    def body(h_ref, w_ref, shift_ref, scale_ref, du_ref, dh_ref, sums_ref):
        xn, rstd, is_ctx, sc, _ = _norm_mod_pieces(h_ref, w_ref, shift_ref, scale_ref, pl.program_id(0), tm, ctx_len)
        du = du_ref[...]
        wv = w_ref[...]
        dy = du * (1.0 + sc)
        dxn = dy * wv
        dh_ref[...] = rstd * (dxn - xn * jnp.mean(dxn * xn, axis=1, keepdims=True))
        dsc = du * (xn * wv)

        def colsum(v):
            return jnp.sum(v, axis=0, keepdims=True)

        dshift_all, dshift_ctx = colsum(du), colsum(jnp.where(is_ctx, du, 0.0))
        dscale_all, dscale_ctx = colsum(dsc), colsum(jnp.where(is_ctx, dsc, 0.0))
        part = jnp.concatenate([colsum(dy * xn), dshift_all - dshift_ctx, dshift_ctx, dscale_all - dscale_ctx,
                                dscale_ctx, jnp.zeros((3, d), F32)], axis=0)

        @pl.when(pl.program_id(0) == 0)
        def _():
            sums_ref[...] = part

        @pl.when(pl.program_id(0) > 0)
        def _():
            sums_ref[...] += part

    row = pl.BlockSpec((tm, d), lambda i: (i, 0))
    return pl.pallas_call(
        body, name="norm_mod_bwd",
        grid=(t // tm,),
        in_specs=[row, pl.BlockSpec((1, d), lambda i: (0, 0)), pl.BlockSpec((2, d), lambda i: (0, 0)),
                  pl.BlockSpec((2, d), lambda i: (0, 0)), row],
        out_specs=[row, pl.BlockSpec((8, d), lambda i: (0, 0))],
        out_shape=[jax.ShapeDtypeStruct((t, d), F32), jax.ShapeDtypeStruct((8, d), F32)],
        compiler_params=_params("arbitrary"),
    )(h, w, shift, scale, du)


@functools.partial(jax.custom_vjp, nondiff_argnums=(4,))
def norm_mod(h, w, shift, scale, ctx_len):
    return _norm_mod_fwd_call(h, w[None, :], shift, scale, ctx_len)


def _norm_mod_fwd(h, w, shift, scale, ctx_len):
    return _norm_mod_fwd_call(h, w[None, :], shift, scale, ctx_len), (h, w, shift, scale)


def _norm_mod_bwd(ctx_len, res, du):
    h, w, shift, scale = res
    dh, sums = _norm_mod_bwd_call(h, w[None, :], shift, scale, du, ctx_len)
    return dh, sums[0], sums[1:3], sums[3:5]


norm_mod.defvjp(_norm_mod_fwd, _norm_mod_bwd)


def _bf(x):
    return x.astype(MXU_DTYPE)


def _dot(a, b):
    return jnp.dot(_bf(a), _bf(b), preferred_element_type=F32)


def _dot_nt(a, b):
    return lax.dot_general(_bf(a), _bf(b), (((1,), (1,)), ((), ())), preferred_element_type=F32)


def _dot_tn(a, b):
    return lax.dot_general(_bf(a), _bf(b), (((0,), (0,)), ((), ())), preferred_element_type=F32)


ROWWISE_VMEM_BYTES = 20 * 1024 * 1024


@functools.partial(jax.custom_vjp, nondiff_argnums=(1,))
def _split_lanes(x, n):
    w = x.shape[1] // n
    return tuple(x[:, i * w:(i + 1) * w] for i in range(n))


def _split_lanes_fwd(x, n):
    return _split_lanes(x, n), None


def _split_lanes_bwd(n, _, gs):
    return (jnp.concatenate(gs, axis=1),)


_split_lanes.defvjp(_split_lanes_fwd, _split_lanes_bwd)


def _rowwise_tile(t, widths):
    for tm in (768, 512, 256, 128):
        if t % tm == 0 and tm * 8 * sum(widths) <= ROWWISE_VMEM_BYTES:
            return tm
    raise ValueError((t, widths))


def _rowwise(name, fn, out_w, ctx_len):
    def is_ctx(tm):
        return pl.program_id(0) * tm + lax.broadcasted_iota(jnp.int32, (tm, 1), 0) < ctx_len

    def specs(params, rows, tm):
        return ([pl.BlockSpec(p.shape, lambda i: (0, 0)) for p in params]
                + [pl.BlockSpec((tm, r.shape[1]), lambda i: (i, 0)) for r in rows])

    def fwd_call(params, rows):
        t = rows[0].shape[0]
        tm = _rowwise_tile(t, [r.shape[1] for r in rows] + [out_w])
        n_p = len(params)

        def body(*refs):
            vals = [r[...] for r in refs[:-1]]
            refs[-1][...] = fn(is_ctx(tm), tuple(vals[:n_p]), tuple(vals[n_p:]))

        return pl.pallas_call(
            body, name=name + "_fwd", grid=(t // tm,),
            in_specs=specs(params, rows, tm),
            out_specs=pl.BlockSpec((tm, out_w), lambda i: (i, 0)),
            out_shape=jax.ShapeDtypeStruct((t, out_w), F32),
            compiler_params=_params("parallel"),
        )(*params, *rows)

    def bwd_call(params, rows, dout):
        t = rows[0].shape[0]
        tm = _rowwise_tile(t, [2 * r.shape[1] for r in rows] + [out_w])
        n_p, n_r = len(params), len(rows)

        def body(*refs):
            vals = [r[...] for r in refs[:n_p + n_r + 1]]
            dx_refs = refs[n_p + n_r + 1:n_p + 2 * n_r + 1]
            dp_refs = refs[n_p + 2 * n_r + 1:]
            ctx_rows = is_ctx(tm)
            _, vjp = jax.vjp(lambda p, x: fn(ctx_rows, p, x), tuple(vals[:n_p]), tuple(vals[n_p:n_p + n_r]))
            dp, dx = vjp(vals[-1])
            for ref, v in zip(dx_refs, dx):
                ref[...] = v

            @pl.when(pl.program_id(0) == 0)
            def _():
                for ref, v in zip(dp_refs, dp):
                    ref[...] = v

            @pl.when(pl.program_id(0) > 0)
            def _():
                for ref, v in zip(dp_refs, dp):
                    ref[...] += v

        row_specs = [pl.BlockSpec((tm, r.shape[1]), lambda i: (i, 0)) for r in rows]
        outs = pl.pallas_call(
            body, name=name + "_bwd", grid=(t // tm,),
            in_specs=specs(params, rows, tm) + [pl.BlockSpec((tm, out_w), lambda i: (i, 0))],
            out_specs=row_specs + [pl.BlockSpec(p.shape, lambda i: (0, 0)) for p in params],
            out_shape=[jax.ShapeDtypeStruct(r.shape, F32) for r in rows]
            + [jax.ShapeDtypeStruct(p.shape, F32) for p in params],
            compiler_params=_params("arbitrary"),
        )(*params, *rows, dout)
        return tuple(outs[n_r:]), tuple(outs[:n_r])

    @jax.custom_vjp
    def op(params, rows):
        return fwd_call(params, rows)

    op.defvjp(lambda params, rows: (fwd_call(params, rows), (params, rows)),
              lambda res, g: bwd_call(res[0], res[1], g))
    return op


def _silu(x):
    return x * jax.nn.sigmoid(x)


def _ret_finish_tile(is_ctx, params, rows):
    (gn_w,), (y_f, y_b, gate) = params, rows
    heads = []
    for yh in _split_lanes(y_f + y_b, RET_HEADS):
        yc = yh - jnp.mean(yh, axis=1, keepdims=True)
        heads.append(yc * lax.rsqrt(jnp.mean(yc * yc, axis=1, keepdims=True) + NORM_EPS))
    return jnp.concatenate(heads, axis=1) * gn_w * _silu(gate)


def _ssd_finish_tile(is_ctx, params, rows):
    (d_skip, norm_w), (y_f, y_b, xs, z) = params, rows
    g = (y_f + y_b + d_skip * xs) * _silu(z)
    return g * lax.rsqrt(jnp.mean(g * g, axis=1, keepdims=True) + NORM_EPS) * norm_w


def _gate_merge_tile(is_ctx, params, rows):
    y0, y1, y2, logits = rows
    return sum(jax.nn.sigmoid(g) * y for g, y in zip(_split_lanes(logits, N_BRANCH), (y0, y1, y2)))


def _qk_prep_tile(is_ctx, params, rows):
    (q_w, k_w), (aq, ak, cos, sin) = params, rows
    out = []
    for x, norm_w, heads in ((aq, q_w, ATTN_HEADS), (ak, k_w, ATTN_KV_HEADS)):
        for xh in _split_lanes(x, heads):
            xn = xh * lax.rsqrt(jnp.mean(xh * xh, axis=1, keepdims=True) + NORM_EPS) * norm_w
            x1, x2 = _split_lanes(xn, 2)
            out += [x1 * cos - x2 * sin, x1 * sin + x2 * cos]
    return jnp.concatenate(out, axis=1)


def _ret_prep_tile(is_ctx, params, rows):
    rq, rk, cos, sin = rows
    out = []
    for x, scale in ((rq, 1.0), (rk, RET_DK ** -0.5)):
        for xh in _split_lanes(x, RET_HEADS):
            x1, x2 = _split_lanes(xh, 2)
            out += [(x1 * cos - x2 * sin) * scale, (x1 * sin + x2 * cos) * scale]
    return jnp.concatenate(out, axis=1)


def _loss_rows_tile(is_ctx, params, rows):
    (norm_w,), (h, target) = params, rows
    y = h * lax.rsqrt(jnp.mean(h * h, axis=1, keepdims=True) + NORM_EPS) * norm_w
    err = jnp.mean(jnp.square(y - target), axis=1, keepdims=True)
    return jnp.broadcast_to(err, (h.shape[0], LANES))


def _gated_residual_tile(is_ctx, params, rows):
    (gate,), (h, update) = params, rows
    return h + jnp.where(is_ctx, gate[1:2, :], gate[0:1, :]) * update


NEG_BIG = -1e30


def _attn_tiles(t, ctx_len, backward=False):
    tq = _tile(ctx_len, (256, 128))
    assert t % tq == 0 and ctx_len % tq == 0
    tk = _tile(t, (1408, 768, 512, 256, 128) if backward else (2816, 1408, 768, 512, 256, 128))
    return tq, tk


def _head_scores(q_ref, k_bf, g, ki, tk, ctx_len, masked):
    q = (q_ref[0, g] * (ATTN_HEAD_DIM ** -0.5)).astype(MXU_DTYPE)
    s = _dot_nt(q, k_bf)
    if masked:
        col = ki * tk + lax.broadcasted_iota(jnp.int32, s.shape, 1)
        s = jnp.where(col < ctx_len, s, NEG_BIG)
    return q, s


def _attn_cases(qi, ki, tq, tk, ctx_len, compute):
    ctx_q = (qi + 1) * tq <= ctx_len

    @pl.when(jnp.logical_not(ctx_q))
    def _():
        compute(False)

    @pl.when(jnp.logical_and(ctx_q, ki * tk < ctx_len))
    def _():
        compute(True)


def _attn_fwd_call(q, k, v, ctx_len):
    kvh, grp, t, hd = q.shape
    tq, tk = _attn_tiles(t, ctx_len)
    nkb = t // tk

    def body(q_ref, k_ref, v_ref, o_ref, lse_ref, m_sc, l_sc, acc_sc):
        qi, ki = pl.program_id(1), pl.program_id(2)

        @pl.when(ki == 0)
        def _():
            m_sc[...] = jnp.full(m_sc.shape, NEG_BIG, F32)
            l_sc[...] = jnp.zeros(l_sc.shape, F32)
            acc_sc[...] = jnp.zeros(acc_sc.shape, F32)

        def compute(masked):
            k_bf, v_bf = _bf(k_ref[0]), _bf(v_ref[0])
            for g in range(grp):
                _, s = _head_scores(q_ref, k_bf, g, ki, tk, ctx_len, masked)
                m_prev = m_sc[g]
                m_new = jnp.maximum(m_prev, jnp.max(s, axis=1, keepdims=True))
                alpha = jnp.exp(m_prev - m_new)
                p = jnp.exp(s - m_new)
                l_sc[g] = alpha * l_sc[g] + jnp.sum(p, axis=1, keepdims=True)
                acc_sc[g] = alpha * acc_sc[g] + _dot(p, v_bf)
                m_sc[g] = m_new

        _attn_cases(qi, ki, tq, tk, ctx_len, compute)

        @pl.when(ki == nkb - 1)
        def _():
            o_ref[0] = acc_sc[...] / l_sc[...]
            lse_ref[0] = m_sc[...] + jnp.log(l_sc[...])

    return pl.pallas_call(
        body, name="attn_fwd",
        grid=(kvh, t // tq, nkb),
        in_specs=[pl.BlockSpec((1, grp, tq, hd), lambda h, i, j: (h, 0, i, 0)),
                  pl.BlockSpec((1, tk, hd), lambda h, i, j: (h, j, 0)),
                  pl.BlockSpec((1, tk, hd), lambda h, i, j: (h, j, 0))],
        out_specs=[pl.BlockSpec((1, grp, tq, hd), lambda h, i, j: (h, 0, i, 0)),
                   pl.BlockSpec((1, grp, tq, 1), lambda h, i, j: (h, 0, i, 0))],
        out_shape=[jax.ShapeDtypeStruct(q.shape, F32), jax.ShapeDtypeStruct((kvh, grp, t, 1), F32)],
        scratch_shapes=[pltpu.VMEM((grp, tq, 1), F32), pltpu.VMEM((grp, tq, 1), F32), pltpu.VMEM((grp, tq, hd), F32)],
        compiler_params=_params("parallel", "parallel", "arbitrary"),
    )(q, k, v)


def _head_probs(q_ref, k_bf, v_bf, o_ref, do_ref, lse_ref, g, ki, tk, ctx_len, masked):
    q, s = _head_scores(q_ref, k_bf, g, ki, tk, ctx_len, masked)
    do = do_ref[0, g]
    delta = jnp.sum(do * o_ref[0, g], axis=1, keepdims=True)
    p = jnp.exp(s - lse_ref[0, g])
    do = _bf(do)
    ds = p * (_dot_nt(do, v_bf) - delta)
    return q, do, p, ds


def _attn_bwd_call(q, k, v, o, lse, do, ctx_len):
    kvh, grp, t, hd = q.shape
    tq, tk = _attn_tiles(t, ctx_len, backward=True)
    nqb, nkb = t // tq, t // tk

    def body(q_ref, k_ref, v_ref, o_ref, lse_ref, do_ref, dq_hbm, dk_ref, dv_ref, dq_sc, dk_sc, dv_sc, dq_out,
             dq_sem):
        hi, ki, qi = pl.program_id(0), pl.program_id(1), pl.program_id(2)
        rows = pl.ds(pl.multiple_of(qi * tq, tq), tq)

        @pl.when(ki == 0)
        def _():
            dq_sc[:, rows, :] = jnp.zeros((grp, tq, hd), F32)

        @pl.when(qi == 0)
        def _():
            dk_sc[...] = jnp.zeros(dk_sc.shape, F32)
            dv_sc[...] = jnp.zeros(dv_sc.shape, F32)

        def compute(masked):
            k_bf, v_bf = _bf(k_ref[0]), _bf(v_ref[0])
            dk_part = jnp.zeros(dk_sc.shape, F32)
            dv_part = jnp.zeros(dv_sc.shape, F32)
            for g in range(grp):
                qs, dob, p, ds = _head_probs(q_ref, k_bf, v_bf, o_ref, do_ref, lse_ref, g, ki, tk, ctx_len, masked)
                dv_part = dv_part + _dot_tn(p, dob)
                dk_part = dk_part + _dot_tn(ds, qs)
                dq_sc[g, rows, :] += _dot(ds, k_bf)
            dk_sc[...] += dk_part
            dv_sc[...] += dv_part

        _attn_cases(qi, ki, tq, tk, ctx_len, compute)

        @pl.when(ki == nkb - 1)
        def _():
            dq_out[...] = dq_sc[:, rows, :] * (hd ** -0.5)
            done = pltpu.make_async_copy(dq_out, dq_hbm.at[hi, :, rows, :], dq_sem)
            done.start()
            done.wait()

        @pl.when(qi == nqb - 1)
        def _():
            dk_ref[0] = dk_sc[...]
            dv_ref[0] = dv_sc[...]

    qspec = pl.BlockSpec((1, grp, tq, hd), lambda h, j, i: (h, 0, i, 0))
    kspec = pl.BlockSpec((1, tk, hd), lambda h, j, i: (h, j, 0))
    return pl.pallas_call(
        body, name="attn_bwd",
        grid=(kvh, t // tk, nqb),
        in_specs=[qspec, kspec, kspec, qspec, pl.BlockSpec((1, grp, tq, 1), lambda h, j, i: (h, 0, i, 0)), qspec],
        out_specs=[pl.BlockSpec(memory_space=pl.ANY), kspec, kspec],
        out_shape=[jax.ShapeDtypeStruct(q.shape, F32), jax.ShapeDtypeStruct(k.shape, F32),
                   jax.ShapeDtypeStruct(v.shape, F32)],
        scratch_shapes=[pltpu.VMEM((grp, t, hd), F32), pltpu.VMEM((tk, hd), F32), pltpu.VMEM((tk, hd), F32),
                        pltpu.VMEM((grp, tq, hd), F32), pltpu.SemaphoreType.DMA],
        compiler_params=_params("arbitrary", "arbitrary", "arbitrary"),
    )(q, k, v, o, lse, do)


@functools.partial(jax.custom_vjp, nondiff_argnums=(3,))
def attention(q, k, v, ctx_len):
    return _attn_fwd_call(q, k, v, ctx_len)[0]


def _attention_fwd(q, k, v, ctx_len):
    o, lse = _attn_fwd_call(q, k, v, ctx_len)
    return o, (q, k, v, o, lse)


def _attention_bwd(ctx_len, res, do):
    q, k, v, o, lse = res
    return tuple(_attn_bwd_call(q, k, v, o, lse, do, ctx_len))


attention.defvjp(_attention_fwd, _attention_bwd)


def _chunk_order(step, n_chunks, n_ctx_chunks, reverse):
    if not reverse:
        return step
    return jnp.where(step < n_ctx_chunks, n_ctx_chunks - 1 - step, n_chunks + n_ctx_chunks - 1 - step)


def _scan_masks(chunk, reverse):
    row = lax.broadcasted_iota(jnp.int32, (chunk, chunk), 0)
    col = lax.broadcasted_iota(jnp.int32, (chunk, chunk), 1)
    vis = (col >= row) if reverse else (col <= row)
    vis_t = (row >= col) if reverse else (row <= col)
    return vis, vis.astype(F32), vis_t.astype(F32)


def _cum_decay(a_col, a_row, vis_f):
    hi = lax.Precision.HIGHEST
    cum_col = jnp.dot(vis_f, a_col, precision=hi, preferred_element_type=F32)
    cum_row = lax.dot_general(a_row, vis_f, (((1,), (1,)), ((), ())), precision=hi, preferred_element_type=F32)
    total = jnp.sum(a_col, axis=0, keepdims=True)
    return cum_col, cum_row, total


SCAN_CHUNKS_PER_STEP = 2


def _scan_specs(t, ctx_len, reverse, backward, widths):
    rows = SCAN_CHUNK * SCAN_CHUNKS_PER_STEP
    assert t % rows == 0 and ctx_len % rows == 0
    n_blocks, n_ctx_blocks = t // rows, ctx_len // rows

    def order(i):
        step = (n_blocks - 1 - i) if backward else i
        return _chunk_order(step, n_blocks, n_ctx_blocks, reverse)

    halves = list(range(SCAN_CHUNKS_PER_STEP))
    if reverse != backward:
        halves.reverse()
    return [pl.BlockSpec((rows, w), lambda i: (order(i), 0)) for w in widths], order, n_blocks, halves


def _scan_fwd_call(q, k, v, a_col, a_row, *, groups, per_group, dk, dv, ctx_len, reverse):
    t = q.shape[0]
    chunk, per_step = SCAN_CHUNK, SCAN_CHUNKS_PER_STEP
    heads = groups * per_group
    (q_spec, k_spec, v_spec, acol_spec), order, n_blocks, halves = _scan_specs(
        t, ctx_len, reverse, False, (groups * dk, groups * dk, heads * dv, LANES))

    def body(q_ref, k_ref, v_ref, acol_ref, arow_ref, y_ref, st_ref, s_sc):
        @pl.when(pl.program_id(0) == 0)
        def _():
            s_sc[...] = jnp.zeros(s_sc.shape, F32)

        vis, vis_f, _ = _scan_masks(chunk, reverse)
        for half in halves:
            rows = slice(half * chunk, (half + 1) * chunk)
            st_ref[half] = s_sc[...]
            cum_col, cum_row, total = _cum_decay(acol_ref[rows, :], arow_ref[:, rows], vis_f)
            for g in range(groups):
                qg = q_ref[rows, g * dk:(g + 1) * dk]
                kg = k_ref[rows, g * dk:(g + 1) * dk]
                qk = _dot_nt(qg, kg)
                for r in range(per_group):
                    h = g * per_group + r
                    ccol = cum_col[:, h:h + 1]
                    decay = jnp.exp(jnp.where(vis, ccol - cum_row[h:h + 1, :], NEG_BIG))
                    vh = v_ref[rows, h * dv:(h + 1) * dv]
                    s_in = s_sc[h]
                    y = _dot(qk * decay, vh) + jnp.exp(ccol) * _dot(qg, s_in)
                    y_ref[rows, h * dv:(h + 1) * dv] = y
                    tot = total[:, h:h + 1]
                    s_sc[h] = jnp.exp(tot) * s_in + _dot_tn(kg * jnp.exp(tot - ccol), vh)

    return pl.pallas_call(
        body, name="scan_fwd",
        grid=(n_blocks,),
        in_specs=[q_spec, k_spec, v_spec, acol_spec, pl.BlockSpec((8, chunk * per_step), lambda i: (0, order(i)))],
        out_specs=[v_spec, pl.BlockSpec((per_step, heads, dk, dv), lambda i: (order(i), 0, 0, 0))],
        out_shape=[jax.ShapeDtypeStruct(v.shape, F32),
                   jax.ShapeDtypeStruct((n_blocks * per_step, heads, dk, dv), F32)],
        scratch_shapes=[pltpu.VMEM((heads, dk, dv), F32)],
        compiler_params=_params("arbitrary"),
    )(q, k, v, a_col, a_row)


def _scan_bwd_call(q, k, v, a_col, a_row, states, dy, *, groups, per_group, dk, dv, ctx_len, reverse):
    t = q.shape[0]
    chunk, per_step = SCAN_CHUNK, SCAN_CHUNKS_PER_STEP
    heads = groups * per_group
    (q_spec, k_spec, v_spec, acol_spec), order, n_blocks, halves = _scan_specs(
        t, ctx_len, reverse, True, (groups * dk, groups * dk, heads * dv, LANES))
    arow_spec = pl.BlockSpec((8, chunk * per_step), lambda i: (0, order(i)))
    last = 0 if reverse else chunk - 1

    def body(q_ref, k_ref, v_ref, acol_ref, arow_ref, st_ref, dy_ref, dq_ref, dk_ref, dv_ref, da_ref, dat_ref,
             ds_sc):
        @pl.when(pl.program_id(0) == 0)
        def _():
            ds_sc[...] = jnp.zeros(ds_sc.shape, F32)

        vis, vis_f, vis_tf = _scan_masks(chunk, reverse)
        lane = lax.broadcasted_iota(jnp.int32, (chunk, LANES), 1)
        row = lax.broadcasted_iota(jnp.int32, (chunk, LANES), 0)
        sub = lax.broadcasted_iota(jnp.int32, (8, chunk), 0)
        for half in halves:
            rows = slice(half * chunk, (half + 1) * chunk)
            cum_col, cum_row, total = _cum_decay(acol_ref[rows, :], arow_ref[:, rows], vis_f)
            dcum = jnp.zeros((chunk, LANES), F32)
            dcum_t = jnp.zeros((8, chunk), F32)
            for g in range(groups):
                qg = q_ref[rows, g * dk:(g + 1) * dk]
                kg = k_ref[rows, g * dk:(g + 1) * dk]
                qk = _dot_nt(qg, kg)
                dq_g = jnp.zeros((chunk, dk), F32)
                dk_g = jnp.zeros((chunk, dk), F32)
                for r in range(per_group):
                    h = g * per_group + r
                    ccol = cum_col[:, h:h + 1]
                    decay = jnp.exp(jnp.where(vis, ccol - cum_row[h:h + 1, :], NEG_BIG))
                    vh = v_ref[rows, h * dv:(h + 1) * dv]
                    dyh = dy_ref[rows, h * dv:(h + 1) * dv]
                    s_in = st_ref[half, h]
                    ds_out = ds_sc[h]
                    tot = total[:, h:h + 1]
                    e_in = jnp.exp(ccol)
                    e_out = jnp.exp(tot - ccol)
                    e_tot = jnp.exp(tot)
                    k_out = kg * e_out
                    dv_ref[rows, h * dv:(h + 1) * dv] = _dot_tn(qk * decay, dyh) + _dot(k_out, ds_out)
                    dqk = _dot_nt(dyh, vh) * decay
                    dq_in = e_in * _dot_nt(dyh, s_in)
                    dk_out = e_out * _dot_nt(vh, ds_out)
                    dq_h = _dot(dqk, kg) + dq_in
                    dk_h = _dot_tn(dqk, qg) + dk_out
                    s_out = e_tot * s_in + _dot_tn(k_out, vh)
                    edge = jnp.sum(jnp.sum(s_out * ds_out, axis=1, keepdims=True), axis=0, keepdims=True)
                    w_seg = dqk * qk
                    dcum_h = (jnp.sum(w_seg, axis=1, keepdims=True) + jnp.sum(dq_in * qg, axis=1, keepdims=True)
                              - jnp.sum(dk_out * kg, axis=1, keepdims=True))
                    dcum = jnp.where(lane == h, dcum_h + jnp.where(row == last, edge, 0.0), dcum)
                    dcum_t = jnp.where(sub == h, -jnp.sum(w_seg, axis=0, keepdims=True), dcum_t)
                    ds_sc[h] = e_tot * ds_out + _dot_tn(qg, e_in * dyh)
                    dq_g = dq_g + dq_h
                    dk_g = dk_g + dk_h
                dq_ref[rows, g * dk:(g + 1) * dk] = dq_g
                dk_ref[rows, g * dk:(g + 1) * dk] = dk_g
            hi = lax.Precision.HIGHEST
            da_ref[rows, :] = jnp.dot(vis_tf, dcum, precision=hi, preferred_element_type=F32)
            dat_ref[:, rows] = jnp.dot(dcum_t, vis_f, precision=hi, preferred_element_type=F32)

    return pl.pallas_call(
        body, name="scan_bwd",
        grid=(n_blocks,),
        in_specs=[q_spec, k_spec, v_spec, acol_spec, arow_spec,
                  pl.BlockSpec((per_step, heads, dk, dv), lambda i: (order(i), 0, 0, 0)), v_spec],
        out_specs=[q_spec, k_spec, v_spec, acol_spec, arow_spec],
        out_shape=[jax.ShapeDtypeStruct(q.shape, F32), jax.ShapeDtypeStruct(k.shape, F32),
                   jax.ShapeDtypeStruct(v.shape, F32), jax.ShapeDtypeStruct((t, LANES), F32),
                   jax.ShapeDtypeStruct((8, t), F32)],
        scratch_shapes=[pltpu.VMEM((heads, dk, dv), F32)],
        compiler_params=_params("arbitrary"),
    )(q, k, v, a_col, a_row, states, dy)


def _decay_layouts(a):
    t, heads = a.shape
    a_col = jnp.pad(a, ((0, 0), (0, LANES - heads)))
    a_row = jnp.pad(a.T, ((0, 8 - heads), (0, 0)))
    return a_col, a_row


@functools.partial(jax.custom_vjp, nondiff_argnums=(4,))
def linear_scan(q, k, v, a, cfg):
    a_col, a_row = _decay_layouts(a)
    return _scan_fwd_call(q, k, v, a_col, a_row, **dict(cfg))[0]


def _linear_scan_fwd(q, k, v, a, cfg):
    a_col, a_row = _decay_layouts(a)
    y, states = _scan_fwd_call(q, k, v, a_col, a_row, **dict(cfg))
    return y, (q, k, v, a, states)


def _linear_scan_bwd(cfg, res, dy):
    q, k, v, a, states = res
    a_col, a_row = _decay_layouts(a)
    dq, dk, dv, da, da_t = _scan_bwd_call(q, k, v, a_col, a_row, states, dy, **dict(cfg))
    heads = a.shape[1]
    return dq, dk, dv, da[:, :heads] + da_t[:heads].T


linear_scan.defvjp(_linear_scan_fwd, _linear_scan_bwd)


def _scan_cfg(groups, per_group, dk, dv, ctx_len, reverse):
    return (("groups", groups), ("per_group", per_group), ("dk", dk), ("dv", dv), ("ctx_len", ctx_len),
            ("reverse", reverse))


def _axial_tables(n_lat, n_ctx):
    freqs = ATTN_HEAD_DIM // 4
    rows = n_lat // GRID_W
    row = jnp.repeat(jnp.arange(rows, dtype=F32), GRID_W)
    col = jnp.tile(jnp.arange(GRID_W, dtype=F32), rows)
    inv = ROPE_THETA ** (-jnp.arange(freqs, dtype=F32) / freqs)
    ang = jnp.concatenate([row[:, None] * inv, col[:, None] * inv], axis=-1)
    cos = jnp.concatenate([jnp.ones((n_ctx, 2 * freqs), F32), jnp.cos(ang)], axis=0)
    sin = jnp.concatenate([jnp.zeros((n_ctx, 2 * freqs), F32), jnp.sin(ang)], axis=0)
    return cos, sin


def _seq_tables(t):
    pos = jnp.arange(t, dtype=F32)
    inv = ROPE_THETA ** (-jnp.linspace(0.0, 1.0, RET_DK // 2, dtype=F32))
    ang = pos[:, None] * inv
    return jnp.cos(ang), jnp.sin(ang)


def _pad_w_in(w_in):
    d = w_in.shape[0]
    return jnp.concatenate([w_in[:, :DT_END], jnp.zeros((d, DT_PAD), w_in.dtype), w_in[:, DT_END:],
                            jnp.zeros((d, TAIL_PAD), w_in.dtype)], axis=1)


def _split_proj(p):
    widths = list(IN_SPLITS)
    widths[5] = LANES
    out, off = [], 0
    for w in widths:
        out.append(p[:, off:off + w])
        off += w
    out[5] = out[5][:, :DT_COLS]
    return out


def _mixer(u, w, wq, layer, n_ctx, tables):
    t = u.shape[0]
    attn_rope, ret_rope, seg_first, seg_last = tables
    proj = matmul(u, wq["w_in"][layer], _pad_w_in(w["w_in"][layer]))
    aq, ak, av, z, xbc_raw, dt_raw, rq, rk, rv, rg, gate_logits = _split_proj(proj)

    q_width = ATTN_HEADS * ATTN_HEAD_DIM
    qk = _rowwise("qk_prep", _qk_prep_tile, q_width + ATTN_KV_HEADS * ATTN_HEAD_DIM, n_ctx)(
        (w["attn_q_norm"][layer][None, :], w["attn_k_norm"][layer][None, :]), (aq, ak) + attn_rope)
    q4 = qk[:, :q_width].reshape(t, ATTN_KV_HEADS, ATTN_GROUP, ATTN_HEAD_DIM).transpose(1, 2, 0, 3)
    k3 = qk[:, q_width:].reshape(t, ATTN_KV_HEADS, ATTN_HEAD_DIM).transpose(1, 0, 2)
    v3 = av.reshape(t, ATTN_KV_HEADS, ATTN_HEAD_DIM).transpose(1, 0, 2)
    o4 = attention(q4, k3, v3, n_ctx)
    br_attn = o4.transpose(2, 0, 1, 3).reshape(t, ATTN_HEADS * ATTN_HEAD_DIM)

    cw, cb = w["ssd_conv_w"][layer], w["ssd_conv_b"][layer]
    zero_row = jnp.zeros((1, xbc_raw.shape[1]), F32)
    prev = jnp.concatenate([zero_row, xbc_raw[:-1]], axis=0) * (1.0 - seg_first)
    nxt = jnp.concatenate([xbc_raw[1:], zero_row], axis=0) * (1.0 - seg_last)
    xbc = jax.nn.silu(prev * cw[0] + xbc_raw * cw[1] + nxt * cw[2] + cb)
    gn = SSD_GROUPS * SSD_STATE
    xs = xbc[:, :SSD_D_INNER]
    bm = xbc[:, SSD_D_INNER:SSD_D_INNER + gn]
    cm = xbc[:, SSD_D_INNER + gn:]
    dt = jax.nn.softplus(dt_raw.reshape(t, 2, SSD_HEADS) + w["ssd_dt_bias"][layer])
    a_neg = -jnp.exp(w["ssd_a_log"][layer])
    xs_h = xs.reshape(t, SSD_HEADS, SSD_HEAD_DIM)
    y_ssd = []
    for d, reverse in ((0, False), (1, True)):
        dtd = dt[:, d]
        cfg = _scan_cfg(SSD_GROUPS, SSD_HEADS // SSD_GROUPS, SSD_STATE, SSD_HEAD_DIM, n_ctx, reverse)
        y_ssd.append(linear_scan(cm, bm, (xs_h * dtd[:, :, None]).reshape(t, SSD_D_INNER), dtd * a_neg[d], cfg))
    d_skip = jnp.repeat(w["ssd_d"][layer], SSD_HEAD_DIM)[None, :]
    br_ssd = _rowwise("ssd_finish", _ssd_finish_tile, SSD_D_INNER, n_ctx)(
        (d_skip, w["ssd_norm_w"][layer][None, :]), (y_ssd[0], y_ssd[1], xs, z))

    lg = -jnp.exp(w["ret_log_decay"][layer])
    ret_width = RET_HEADS * RET_DK
    ret_qk = _rowwise("ret_prep", _ret_prep_tile, 2 * ret_width, n_ctx)((), (rq, rk) + ret_rope)
    rq_r, rk_r = ret_qk[:, :ret_width], ret_qk[:, ret_width:]
    y_ret = []
    for d, reverse in ((0, False), (1, True)):
        cfg = _scan_cfg(RET_HEADS, 1, RET_DK, RET_DV, n_ctx, reverse)
        y_ret.append(linear_scan(rq_r, rk_r, rv, jnp.broadcast_to(lg[d][None, :], (t, RET_HEADS)), cfg))
    br_ret = _rowwise("ret_finish", _ret_finish_tile, RET_HEADS * RET_DV, n_ctx)(
        (w["ret_gn_w"][layer][None, :],), (y_ret[0], y_ret[1], rg))

    projected = tuple(matmul(br, wq["w_branch"][layer][j], w["w_branch"][layer][j])
                      for j, br in enumerate((br_attn, br_ssd, br_ret)))
    merged = _rowwise("gate_merge", _gate_merge_tile, D_MODEL, n_ctx)((), projected + (gate_logits,))
    return matmul(merged, wq["w_out"][layer], w["w_out"][layer])


def _local_loss(w, x, c, ctx, target, wq):
    n, m = x.shape[0], ctx.shape[0]
    t = n + m
    pos = jnp.arange(t)[:, None]
    seg_first = ((pos == 0) | (pos == m)).astype(F32)
    seg_last = ((pos == m - 1) | (pos == t - 1)).astype(F32)
    tables = (_axial_tables(n, m), _seq_tables(t), seg_first, seg_last)
    h = jnp.concatenate([ctx, x], axis=0)
    cond = jax.nn.silu(jnp.stack([c, w["c_ctx"]], axis=0))
    cond8 = jnp.concatenate([cond, jnp.zeros((6, D_MODEL), F32)], axis=0)
    for layer in range(DEPTH):
        mod = (matmul(cond8, wq["w_mod"][layer], w["w_mod"][layer])[:2] + w["b_mod"][layer]).reshape(2, 6, D_MODEL)

        u = norm_mod(h, w["norm1_w"][layer], mod[:, 0], mod[:, 1], m)
        residual = _rowwise("gated_residual", _gated_residual_tile, D_MODEL, m)
        h = residual((mod[:, 2],), (h, _mixer(u, w, wq, layer, m, tables)))
        v = norm_mod(h, w["norm2_w"][layer], mod[:, 3], mod[:, 4], m)
        mlp = sq_relu_mlp(v, wq["w_mlp1"][layer], w["w_mlp1"][layer], wq["w_mlp2"][layer], w["w_mlp2"][layer])
        residual = _rowwise("gated_residual", _gated_residual_tile, D_MODEL, m)
        h = residual((mod[:, 5],), (h, mlp))
    per_token = _rowwise("loss_rows", _loss_rows_tile, LANES, 0)((w["final_norm_w"][None, :],), (h[m:], target))
    return 0.5 * jnp.sum(per_token[:, 0])


def _coords():
    return lax.axis_index("x"), lax.axis_index("y"), lax.axis_index("c")


def _all_gather(blocks, name):
    n = len(blocks)

    def body(*refs):
        x_refs, out_refs = refs[:n], refs[n:2 * n]
        send_sems, recv_sems, local_sems = refs[2 * n:]
        x, y, c = _coords()
        me, sibling = (x, y, c), (x, y, 1 - c)
        chips = [(1 - x, y), (x, 1 - y), (1 - x, 1 - y)]

        def copy(k, i, blk, to, from_input=False):
            slot = out_refs[i].at[4 * blk[0] + 2 * blk[1] + blk[2]]
            return pltpu.make_async_remote_copy(
                src_ref=x_refs[i] if from_input else slot, dst_ref=slot,
                send_sem=send_sems.at[k * n + i], recv_sem=recv_sems.at[k * n + i],
                device_id=to, device_id_type=pl.DeviceIdType.MESH)

        mine = [pltpu.make_async_copy(x_refs[i], out_refs[i].at[4 * x + 2 * y + c], local_sems.at[i])
                for i in range(n)]
        for cp in mine:
            cp.start()
        first = [copy(0, i, me, sibling, True) for i in range(n)]
        first += [copy(1 + j, i, me, (*chip, c), True) for j, chip in enumerate(chips) for i in range(n)]
        for cp in first:
            cp.start()
        passed = []
        for j, chip in enumerate(chips):
            for i in range(n):
                copy(1 + j, i, (*chip, c), me).wait_recv()
                passed.append(copy(4 + j, i, (*chip, c), sibling))
                passed[-1].start()
        for i in range(n):
            copy(0, i, sibling, me).wait_recv()
        for j, chip in enumerate(chips):
            for i in range(n):
                copy(4 + j, i, (*chip, 1 - c), me).wait_recv()
        for cp in first + passed:
            cp.wait_send()
        for cp in mine:
            cp.wait()

    return pl.pallas_call(
        body, name=name,
        out_shape=[jax.ShapeDtypeStruct((N_DEV,) + b.shape, b.dtype) for b in blocks],
        in_specs=[pl.BlockSpec(memory_space=pl.ANY)] * n,
        out_specs=[pl.BlockSpec(memory_space=pl.ANY)] * n,
        scratch_shapes=[pltpu.SemaphoreType.DMA((7 * n,)), pltpu.SemaphoreType.DMA((7 * n,)),
                        pltpu.SemaphoreType.DMA((n,))],
    )(*blocks)


def _pair_exchange(arrays, name):
    n = len(arrays)

    def body(*refs):
        g_refs, out_refs = refs[:n], refs[n:2 * n]
        send_sems, recv_sems = refs[2 * n:]
        x, y, c = _coords()
        copies = [pltpu.make_async_remote_copy(
            src_ref=g_refs[i].at[1 - c], dst_ref=out_refs[i], send_sem=send_sems.at[i], recv_sem=recv_sems.at[i],
            device_id=(x, y, 1 - c), device_id_type=pl.DeviceIdType.MESH) for i in range(n)]
        for cp in copies:
            cp.start()
        for cp in copies:
            cp.wait()

    return pl.pallas_call(
        body, name=name,
        out_shape=[jax.ShapeDtypeStruct(a.shape[1:], a.dtype) for a in arrays],
        in_specs=[pl.BlockSpec(memory_space=pl.ANY)] * n,
        out_specs=[pl.BlockSpec(memory_space=pl.ANY)] * n,
        scratch_shapes=[pltpu.SemaphoreType.DMA((n,)), pltpu.SemaphoreType.DMA((n,))],
    )(*arrays)


def _chip_exchange(arrays, name):
    n = len(arrays)

    def body(*refs):
        g_refs, out_refs = refs[:n], refs[n:2 * n]
        send_sems, recv_sems, local_sems = refs[2 * n:]
        x, y, c = _coords()
        me = 2 * x + y
        mine = [pltpu.make_async_copy(g_refs[i].at[me], out_refs[i].at[me], local_sems.at[i]) for i in range(n)]
        for cp in mine:
            cp.start()
        copies = []
        for k in range(1, 4):
            px, py = (1 - x if (k >> 1) & 1 else x), (1 - y if k & 1 else y)
            for i in range(n):
                copies.append(pltpu.make_async_remote_copy(
                    src_ref=g_refs[i].at[2 * px + py], dst_ref=out_refs[i].at[me],
                    send_sem=send_sems.at[(k - 1) * n + i], recv_sem=recv_sems.at[(k - 1) * n + i],
                    device_id=(px, py, c), device_id_type=pl.DeviceIdType.MESH))
        for cp in copies:
            cp.start()
        for cp in copies:
            cp.wait_recv()
        for cp in copies:
            cp.wait_send()
        for cp in mine:
            cp.wait()

    return pl.pallas_call(
        body, name=name,
        out_shape=[jax.ShapeDtypeStruct(a.shape, a.dtype) for a in arrays],
        in_specs=[pl.BlockSpec(memory_space=pl.ANY)] * n,
        out_specs=[pl.BlockSpec(memory_space=pl.ANY)] * n,
        scratch_shapes=[pltpu.SemaphoreType.DMA((3 * n,)), pltpu.SemaphoreType.DMA((3 * n,)),
                        pltpu.SemaphoreType.DMA((n,))],
    )(*arrays)


def _add_pair(a, b, name):
    rows, cols = a.shape
    tr = _tile(rows, [r for r in (2048, 1024, 512, 256, 128, 64, 32, 16) if r * cols <= 2 * ADAMW_BLOCK_ELEMS])

    def body(a_ref, b_ref, o_ref):
        o_ref[...] = (a_ref[...].astype(F32) + b_ref[...].astype(F32)).astype(o_ref.dtype)

    spec = pl.BlockSpec((tr, cols), lambda i: (i, 0))
    return pl.pallas_call(
        body, name=name, grid=(rows // tr,), in_specs=[spec, spec], out_specs=spec,
        out_shape=jax.ShapeDtypeStruct(a.shape, a.dtype), compiler_params=_params("parallel"),
    )(a, b)


ADAMW_BLOCK_ELEMS = 256 * 1024


def _sum_adamw(g8, w, m, v, name):
    rows, cols = w.shape
    tr = _tile(rows, [r for r in (2048, 1024, 512, 256, 128, 64, 32, 16) if r * cols <= ADAMW_BLOCK_ELEMS])

    def body(g_ref, w_ref, m_ref, v_ref, go_ref, d_ref, mo_ref, vo_ref):
        g = g_ref[0].astype(F32)
        for s in range(1, g8.shape[0]):
            g = g + g_ref[s].astype(F32)
        m_new = ADAM_B1 * m_ref[...] + (1.0 - ADAM_B1) * g
        v_new = ADAM_B2 * v_ref[...] + (1.0 - ADAM_B2) * (g * g)
        m_hat = m_new / (1.0 - ADAM_B1 ** ADAM_STEP)
        v_hat = v_new / (1.0 - ADAM_B2 ** ADAM_STEP)
        go_ref[...] = g
        d_ref[...] = -ADAM_LR * (m_hat / (jnp.sqrt(v_hat) + ADAM_EPS) + ADAM_WD * w_ref[...])
        mo_ref[...] = m_new
        vo_ref[...] = v_new

    spec = pl.BlockSpec((tr, cols), lambda i: (i, 0))
    shape = jax.ShapeDtypeStruct((rows, cols), F32)
    return pl.pallas_call(
        body, name=name,
        grid=(rows // tr,),
        in_specs=[pl.BlockSpec((g8.shape[0], tr, cols), lambda i: (0, i, 0)), spec, spec, spec],
        out_specs=[spec, spec, spec, spec],
        out_shape=[shape, shape, shape, shape],
        compiler_params=_params("parallel"),
    )(g8, w, m, v)


BIG = ("w_mod", "w_in", "w_branch", "w_out", "w_mlp1", "w_mlp2")
COL_SHARDED = ("w_mod", "w_mlp1")
ROW_SHARDED = ("w_out", "w_mlp2")
SMALL = ("c_ctx", "b_mod", "norm1_w", "norm2_w", "attn_q_norm", "attn_k_norm", "ssd_conv_b", "ssd_dt_bias",
         "ssd_a_log", "ssd_d", "ssd_norm_w", "ret_log_decay", "ret_gn_w", "final_norm_w")
CONV_AXIS = 2
ORDER = ("c_ctx", "w_mod", "b_mod", "norm1_w", "norm2_w", "w_in", "attn_q_norm", "attn_k_norm", "ssd_conv_w",
         "ssd_conv_b", "ssd_dt_bias", "ssd_a_log", "ssd_d", "ssd_norm_w", "ret_log_decay", "ret_gn_w", "w_branch",
         "w_out", "w_mlp1", "w_mlp2", "final_norm_w")


def _compute_weights(gathered):
    wq, carrier = {}, {}
    for name in BIG:
        g = gathered[name]
        per_layer = []
        for layer in range(DEPTH):
            gl = g[:, layer]
            if name in COL_SHARDED:
                per_layer.append(gl)
            elif name in ROW_SHARDED:
                per_layer.append(gl.reshape(-1, gl.shape[-1]))
            elif name == "w_in":
                per_layer.append(_pad_w_in(jnp.concatenate([gl[d] for d in range(N_DEV)], axis=-1)))
            else:
                per_layer.append([jnp.concatenate([gl[d, j] for d in range(N_DEV)], axis=-1) for j in range(N_BRANCH)])
        wq[name] = per_layer
    for name in BIG:
        if name == "w_in":
            carrier[name] = [jnp.zeros((D_MODEL, IN_DIM), F32) for _ in range(DEPTH)]
        else:
            carrier[name] = jax.tree.map(lambda a: jnp.zeros(a.shape, F32), wq[name])
    return wq, carrier


def _grad_shards(gw):
    out = {}
    for name in BIG:
        per_layer = []
        for layer in range(DEPTH):
            g = gw[name][layer]
            if name in COL_SHARDED:
                per_layer.append(g)
            elif name in ROW_SHARDED:
                per_layer.append(g.reshape(N_DEV, -1, g.shape[-1]))
            elif name == "w_in":
                size = IN_DIM // N_DEV
                per_layer.append(jnp.stack([g[:, d * size:(d + 1) * size] for d in range(N_DEV)]))
            else:
                per_layer.append(jnp.stack([gj.reshape(gj.shape[0], N_DEV, -1).transpose(1, 0, 2) for gj in g], axis=1))
        out[name] = jnp.stack(per_layer, axis=1)
    return out


def _pack(arrays, row_multiple):
    flat = jnp.concatenate(arrays, axis=-1)
    n = flat.shape[-1]
    per = LANES * row_multiple
    padded = -(-n // per) * per
    flat = jnp.pad(flat, [(0, 0)] * (flat.ndim - 1) + [(0, padded - n)])
    return flat.reshape(flat.shape[:-1] + (padded // LANES, LANES))


def _unpack(slab, shapes):
    flat = slab.reshape(slab.shape[:-2] + (-1,))
    out, off = [], 0
    for shp in shapes:
        size = math.prod(shp)
        out.append(flat[..., off:off + size].reshape(flat.shape[:-1] + tuple(shp)))
        off += size
    return out


def kernel(x, c, ctx, c_ctx, w_mod, b_mod, norm1_w, norm2_w, w_in, attn_q_norm, attn_k_norm, ssd_conv_w, ssd_conv_b, ssd_dt_bias, ssd_a_log, ssd_d, ssd_norm_w, ret_log_decay, ret_gn_w, w_branch, w_out, w_mlp1, w_mlp2, final_norm_w, loss_target, m_c_ctx, m_w_mod, m_b_mod, m_norm1_w, m_norm2_w, m_w_in, m_attn_q_norm, m_attn_k_norm, m_ssd_conv_w, m_ssd_conv_b, m_ssd_dt_bias, m_ssd_a_log, m_ssd_d, m_ssd_norm_w, m_ret_log_decay, m_ret_gn_w, m_w_branch, m_w_out, m_w_mlp1, m_w_mlp2, m_final_norm_w, v_c_ctx, v_w_mod, v_b_mod, v_norm1_w, v_norm2_w, v_w_in, v_attn_q_norm, v_attn_k_norm, v_ssd_conv_w, v_ssd_conv_b, v_ssd_dt_bias, v_ssd_a_log, v_ssd_d, v_ssd_norm_w, v_ret_log_decay, v_ret_gn_w, v_w_branch, v_w_out, v_w_mlp1, v_w_mlp2, v_final_norm_w):
    args = dict(locals())
    weights = {n: args[n] for n in ORDER}
    mom1 = {n: args["m_" + n] for n in ORDER}
    mom2 = {n: args["v_" + n] for n in ORDER}
    me = 4 * lax.axis_index("x") + 2 * lax.axis_index("y") + lax.axis_index("c")

    gathered = _all_gather([weights[n].astype(MXU_DTYPE) for n in BIG], "gather_weights")
    wq, params = _compute_weights(dict(zip(BIG, gathered)))
    conv_shape = ssd_conv_w.shape
    conv_all = _all_gather([_pack([ssd_conv_w.reshape(-1)], 8)], "gather_conv")[0]
    params["ssd_conv_w"] = jnp.concatenate(list(_unpack(conv_all, [conv_shape])[0]), axis=CONV_AXIS)
    for n in SMALL:
        params[n] = weights[n]

    loss, (gw, gx) = jax.value_and_grad(_local_loss, argnums=(0, 1))(params, x[0], c[0], ctx[0], loss_target[0], wq)
    loss = lax.psum(loss, MESH_AXES)

    g_send = _grad_shards(gw)
    by_core = [g_send[n].reshape((4, 2) + g_send[n].shape[1:]).swapaxes(0, 1).astype(jnp.bfloat16) for n in BIG]
    from_sibling = _pair_exchange(by_core, "scatter_grads_d2d")
    my_core = lax.axis_index("c")
    chip_sums = []
    for n, mine, theirs in zip(BIG, by_core, from_sibling):
        kept = lax.dynamic_index_in_dim(mine, my_core, 0, keepdims=False)
        chip_sums.append(_add_pair(kept.reshape(-1, kept.shape[-1]), theirs.reshape(-1, theirs.shape[-1]),
                                   "chip_sum_" + n).reshape(theirs.shape))
    g_recv = _chip_exchange(chip_sums, "scatter_grads_ici")
    result = {}
    for n, g8 in zip(BIG, g_recv):
        shape = weights[n].shape
        as2d = lambda a: a.reshape(-1, shape[-1])
        outs = _sum_adamw(g8.reshape(4, -1, shape[-1]), as2d(weights[n]), as2d(mom1[n]), as2d(mom2[n]), "adamw_" + n)
        for kind, arr in zip(("grad", "delta", "new_m", "new_v"), outs):
            result[kind, n] = arr.reshape(shape)

    conv_full_shape = params["ssd_conv_w"].shape
    small_shapes = [weights[n].shape for n in SMALL]
    partial = _pack([gw[n].reshape(-1) for n in SMALL] + [gw["ssd_conv_w"].reshape(-1)], 8)
    parts = _unpack(_all_gather([partial], "gather_small_grads")[0], small_shapes + [conv_full_shape])
    conv_part = lax.dynamic_slice_in_dim(parts[-1], me * conv_shape[CONV_AXIS], conv_shape[CONV_AXIS], CONV_AXIS + 1)
    small_names = list(SMALL) + ["ssd_conv_w"]
    g8_small = _pack([p.reshape(N_DEV, -1) for p in parts[:-1]] + [conv_part.reshape(N_DEV, -1)], 8)
    slabs = [_pack([d[n].reshape(-1) for n in small_names], 8) for d in (weights, mom1, mom2)]
    small_out = [_unpack(s, small_shapes + [conv_shape]) for s in _sum_adamw(g8_small, *slabs, "adamw_small")]
    for kind, small_k in zip(("grad", "delta", "new_m", "new_v"), small_out):
        for n, arr in zip(small_names, small_k):
            result[kind, n] = arr

    outs = [loss, gx[None]]
    for kind in ("grad", "delta", "new_m", "new_v"):
        outs += [result[kind, n] for n in ORDER]
    return tuple(outs)
```

```python
import functools
import math

import jax
import jax.numpy as jnp
from jax import lax
from jax.experimental import pallas as pl
from jax.experimental.pallas import tpu as pltpu

F32 = jnp.float32
MXU_DTYPE = jnp.bfloat16
VMEM_LIMIT_BYTES = 48 * 1024 * 1024
LANES = 128
N_DEV = 8
MESH_AXES = ("x", "y", "c")

D_MODEL = 1024
GRID_W = 64
NORM_EPS = 1e-6
ROPE_THETA = 10000.0
ATTN_HEADS, ATTN_KV_HEADS, ATTN_HEAD_DIM = 8, 2, 64
ATTN_GROUP = ATTN_HEADS // ATTN_KV_HEADS
SSD_HEADS, SSD_HEAD_DIM, SSD_GROUPS, SSD_STATE = 8, 64, 2, 128
SSD_D_INNER = SSD_HEADS * SSD_HEAD_DIM
RET_HEADS, RET_DK, RET_DV = 4, 128, 128
SCAN_CHUNK = 128
N_BRANCH = 3
DEPTH = 2

IN_SPLITS = (512, 128, 128, 512, 1024, 16, 512, 512, 512, 512, 3072)
IN_DIM = sum(IN_SPLITS)
DT_COLS = 16
DT_PAD = LANES - DT_COLS
TAIL_PAD = 128
IN_DIM_PADDED = IN_DIM + DT_PAD + TAIL_PAD
DT_END = sum(IN_SPLITS[:6])

ADAM_LR, ADAM_B1, ADAM_B2, ADAM_EPS, ADAM_WD, ADAM_STEP = 0.001, 0.9, 0.999, 1e-08, 0.01, 10


def _tile(dim, prefs):
    for p in prefs:
        if dim % p == 0:
            return p
    return dim


def _params(*sem):
    return pltpu.CompilerParams(dimension_semantics=sem, vmem_limit_bytes=VMEM_LIMIT_BYTES)


def _mm(a, b, *, ta=False, tb=False, out_shards=False, epilogue=None, extra=None, out_dtype=F32, name):
    if ta:
        kdim, m = a.shape
    else:
        m, kdim = a.shape
    b_shards = b.ndim == 3
    if b_shards:
        rows_b, cols_b = b.shape[1], N_DEV * b.shape[2]
    else:
        rows_b, cols_b = b.shape
    n, kdim_b = (rows_b, cols_b) if tb else (cols_b, rows_b)
    assert kdim == kdim_b, (a.shape, b.shape, ta, tb)
    tm = _tile(m, (1024, 768, 512, 256, 128))
    n_tile_of = n // N_DEV if (out_shards or (b_shards and not tb)) else n
    k_tile_of = kdim // N_DEV if (b_shards and tb) else kdim
    tn = _tile(n_tile_of, (1280, 1024, 768, 512, 384, 256, 128))
    tk = _tile(k_tile_of, (1024, 768, 512, 256, 128))
    dims = (((0 if ta else 1,), (1 if tb else 0,)), ((), ()))

    n_k = kdim // tk

    def body(a_ref, b_ref, *rest):
        o_ref = rest[-1]
        part = lax.dot_general(a_ref[...].astype(MXU_DTYPE), b_ref[...].astype(MXU_DTYPE), dims,
                               preferred_element_type=F32)

        def finish(acc):
            if epilogue == "sq_relu":
                acc = jnp.square(jnp.maximum(acc, 0.0))
            elif epilogue == "d_sq_relu":
                acc = acc * (2.0 * jnp.sqrt(rest[0][...].astype(F32)))
            return acc.astype(o_ref.dtype)

        if n_k == 1:
            o_ref[...] = finish(part)
            return

        @pl.when(pl.program_id(2) == 0)
        def _():
            o_ref[...] = part

        @pl.when(pl.program_id(2) > 0)
        def _():
            o_ref[...] += part

        if epilogue is not None:
            @pl.when(pl.program_id(2) == n_k - 1)
            def _():
                o_ref[...] = finish(o_ref[...])

    a_spec = pl.BlockSpec((tk, tm), lambda i, j, k: (k, i)) if ta else pl.BlockSpec((tm, tk), lambda i, j, k: (i, k))
    if not b_shards:
        b_spec = pl.BlockSpec((tn, tk), lambda i, j, k: (j, k)) if tb else pl.BlockSpec((tk, tn), lambda i, j, k: (k, j))
    elif tb:
        per = b.shape[2] // tk
        b_spec = pl.BlockSpec((None, tn, tk), lambda i, j, k: (k // per, j, k % per))
    else:
        per = b.shape[2] // tn
        b_spec = pl.BlockSpec((None, tk, tn), lambda i, j, k: (j // per, k, j % per))
    if out_shards:
        per_out = n // N_DEV // tn
        out_spec = pl.BlockSpec((None, tm, tn), lambda i, j, k: (j // per_out, i, j % per_out))
        out_shape = jax.ShapeDtypeStruct((N_DEV, m, n // N_DEV), out_dtype)
    else:
        out_spec = pl.BlockSpec((tm, tn), lambda i, j, k: (i, j))
        out_shape = jax.ShapeDtypeStruct((m, n), out_dtype)
    assert epilogue in (None, "sq_relu", "d_sq_relu") and (extra is not None) == (epilogue == "d_sq_relu")
    assert out_dtype == F32 or n_k == 1
    operands, in_specs = [a, b], [a_spec, b_spec]
    if extra is not None:
        assert not out_shards and extra.shape == (m, n)
        operands.append(extra)
        in_specs.append(out_spec)
    return pl.pallas_call(
        body, name=name,
        grid=(m // tm, n // tn, n_k),
        in_specs=in_specs,
        out_specs=out_spec,
        out_shape=out_shape,
        compiler_params=_params("parallel", "parallel", "arbitrary"),
    )(*operands)


@jax.custom_vjp
def matmul(a, b, b_grad):
    return _mm(a, b, name="mm_fwd")


def _matmul_fwd(a, b, b_grad):
    return _mm(a, b, name="mm_fwd"), (a, b)


def _matmul_bwd(res, g):
    a, b = res
    dw = _mm(a, g, ta=True, out_shards=b.ndim == 3, name="mm_dw")
    return _mm(g, b, tb=True, name="mm_dx"), jnp.zeros_like(b), dw


matmul.defvjp(_matmul_fwd, _matmul_bwd)


@jax.custom_vjp
def sq_relu_mlp(x, w1, w1_grad, w2, w2_grad):
    return _mm(_mm(x, w1, epilogue="sq_relu", out_dtype=MXU_DTYPE, name="mlp_up"), w2, name="mlp_down")


def _sq_relu_mlp_fwd(x, w1, w1_grad, w2, w2_grad):
    hid = _mm(x, w1, epilogue="sq_relu", out_dtype=MXU_DTYPE, name="mlp_up")
    return _mm(hid, w2, name="mlp_down"), (x, w1, w2, hid)


def _sq_relu_mlp_bwd(res, g):
    x, w1, w2, hid = res
    d_pre = _mm(g, w2, tb=True, epilogue="d_sq_relu", extra=hid, name="mlp_down_dx")
    dw2 = _mm(hid, g, ta=True, out_shards=w2.ndim == 3, name="mlp_down_dw")
    dw1 = _mm(x, d_pre, ta=True, out_shards=w1.ndim == 3, name="mlp_up_dw")
    return _mm(d_pre, w1, tb=True, name="mlp_up_dx"), jnp.zeros_like(w1), dw1, jnp.zeros_like(w2), dw2


sq_relu_mlp.defvjp(_sq_relu_mlp_fwd, _sq_relu_mlp_bwd)


NORM_TILE_PREFS = (768, 512, 256, 128)


def _norm_mod_pieces(h_ref, w_ref, shift_ref, scale_ref, tile, tm, ctx_len):
    x = h_ref[...]
    rstd = lax.rsqrt(jnp.mean(x * x, axis=1, keepdims=True) + NORM_EPS)
    xn = x * rstd
    is_ctx = tile * tm + lax.broadcasted_iota(jnp.int32, (tm, 1), 0) < ctx_len
    scale = jnp.where(is_ctx, scale_ref[1:2, :], scale_ref[0:1, :])
    shift = jnp.where(is_ctx, shift_ref[1:2, :], shift_ref[0:1, :])
    return xn, rstd, is_ctx, scale, shift


def _norm_mod_fwd_call(h, w, shift, scale, ctx_len):
    t, d = h.shape
    tm = _tile(t, NORM_TILE_PREFS)

    def body(h_ref, w_ref, shift_ref, scale_ref, u_ref):
        xn, _, _, sc, sh = _norm_mod_pieces(h_ref, w_ref, shift_ref, scale_ref, pl.program_id(0), tm, ctx_len)
        u_ref[...] = xn * w_ref[...] * (1.0 + sc) + sh

    row = pl.BlockSpec((tm, d), lambda i: (i, 0))
    return pl.pallas_call(
        body, name="norm_mod_fwd",
        grid=(t // tm,),
        in_specs=[row, pl.BlockSpec((1, d), lambda i: (0, 0)), pl.BlockSpec((2, d), lambda i: (0, 0)),
                  pl.BlockSpec((2, d), lambda i: (0, 0))],
        out_specs=row,
        out_shape=jax.ShapeDtypeStruct((t, d), F32),
        compiler_params=_params("parallel"),
    )(h, w, shift, scale)


def _norm_mod_bwd_call(h, w, shift, scale, du, ctx_len):
    t, d = h.shape
    tm = _tile(t, NORM_TILE_PREFS)

    def body(h_ref, w_ref, shift_ref, scale_ref, du_ref, dh_ref, sums_ref):
        xn, rstd, is_ctx, sc, _ = _norm_mod_pieces(h_ref, w_ref, shift_ref, scale_ref, pl.program_id(0), tm, ctx_len)
        du = du_ref[...]
        wv = w_ref[...]
        dy = du * (1.0 + sc)
        dxn = dy * wv
        dh_ref[...] = rstd * (dxn - xn * jnp.mean(dxn * xn, axis=1, keepdims=True))
        dsc = du * (xn * wv)

        def colsum(v):
            return jnp.sum(v, axis=0, keepdims=True)

        dshift_all, dshift_ctx = colsum(du), colsum(jnp.where(is_ctx, du, 0.0))
        dscale_all, dscale_ctx = colsum(dsc), colsum(jnp.where(is_ctx, dsc, 0.0))
        part = jnp.concatenate([colsum(dy * xn), dshift_all - dshift_ctx, dshift_ctx, dscale_all - dscale_ctx,
                                dscale_ctx, jnp.zeros((3, d), F32)], axis=0)

        @pl.when(pl.program_id(0) == 0)
        def _():
            sums_ref[...] = part

        @pl.when(pl.program_id(0) > 0)
        def _():
            sums_ref[...] += part

    row = pl.BlockSpec((tm, d), lambda i: (i, 0))
    return pl.pallas_call(
        body, name="norm_mod_bwd",
        grid=(t // tm,),
        in_specs=[row, pl.BlockSpec((1, d), lambda i: (0, 0)), pl.BlockSpec((2, d), lambda i: (0, 0)),
                  pl.BlockSpec((2, d), lambda i: (0, 0)), row],
        out_specs=[row, pl.BlockSpec((8, d), lambda i: (0, 0))],
        out_shape=[jax.ShapeDtypeStruct((t, d), F32), jax.ShapeDtypeStruct((8, d), F32)],
        compiler_params=_params("arbitrary"),
    )(h, w, shift, scale, du)


@functools.partial(jax.custom_vjp, nondiff_argnums=(4,))
def norm_mod(h, w, shift, scale, ctx_len):
    return _norm_mod_fwd_call(h, w[None, :], shift, scale, ctx_len)


def _norm_mod_fwd(h, w, shift, scale, ctx_len):
    return _norm_mod_fwd_call(h, w[None, :], shift, scale, ctx_len), (h, w, shift, scale)


def _norm_mod_bwd(ctx_len, res, du):
    h, w, shift, scale = res
    dh, sums = _norm_mod_bwd_call(h, w[None, :], shift, scale, du, ctx_len)
    return dh, sums[0], sums[1:3], sums[3:5]


norm_mod.defvjp(_norm_mod_fwd, _norm_mod_bwd)


def _bf(x):
    return x.astype(MXU_DTYPE)


def _dot(a, b):
    return jnp.dot(_bf(a), _bf(b), preferred_element_type=F32)


def _dot_nt(a, b):
    return lax.dot_general(_bf(a), _bf(b), (((1,), (1,)), ((), ())), preferred_element_type=F32)


def _dot_tn(a, b):
    return lax.dot_general(_bf(a), _bf(b), (((0,), (0,)), ((), ())), preferred_element_type=F32)


ROWWISE_VMEM_BYTES = 20 * 1024 * 1024


@functools.partial(jax.custom_vjp, nondiff_argnums=(1,))
def _split_lanes(x, n):
    w = x.shape[1] // n
    return tuple(x[:, i * w:(i + 1) * w] for i in range(n))


def _split_lanes_fwd(x, n):
    return _split_lanes(x, n), None


def _split_lanes_bwd(n, _, gs):
    return (jnp.concatenate(gs, axis=1),)


_split_lanes.defvjp(_split_lanes_fwd, _split_lanes_bwd)


def _rowwise_tile(t, widths):
    for tm in (768, 512, 256, 128):
        if t % tm == 0 and tm * 8 * sum(widths) <= ROWWISE_VMEM_BYTES:
            return tm
    raise ValueError((t, widths))


def _rowwise(name, fn, out_w, ctx_len):
    def is_ctx(tm):
        return pl.program_id(0) * tm + lax.broadcasted_iota(jnp.int32, (tm, 1), 0) < ctx_len

    def specs(params, rows, tm):
        return ([pl.BlockSpec(p.shape, lambda i: (0, 0)) for p in params]
                + [pl.BlockSpec((tm, r.shape[1]), lambda i: (i, 0)) for r in rows])

    def fwd_call(params, rows):
        t = rows[0].shape[0]
        tm = _rowwise_tile(t, [r.shape[1] for r in rows] + [out_w])
        n_p = len(params)

        def body(*refs):
            vals = [r[...] for r in refs[:-1]]
            refs[-1][...] = fn(is_ctx(tm), tuple(vals[:n_p]), tuple(vals[n_p:]))

        return pl.pallas_call(
            body, name=name + "_fwd", grid=(t // tm,),
            in_specs=specs(params, rows, tm),
            out_specs=pl.BlockSpec((tm, out_w), lambda i: (i, 0)),
            out_shape=jax.ShapeDtypeStruct((t, out_w), F32),
            compiler_params=_params("parallel"),
        )(*params, *rows)

    def bwd_call(params, rows, dout):
        t = rows[0].shape[0]
        tm = _rowwise_tile(t, [2 * r.shape[1] for r in rows] + [out_w])
        n_p, n_r = len(params), len(rows)

        def body(*refs):
            vals = [r[...] for r in refs[:n_p + n_r + 1]]
            dx_refs = refs[n_p + n_r + 1:n_p + 2 * n_r + 1]
            dp_refs = refs[n_p + 2 * n_r + 1:]
            ctx_rows = is_ctx(tm)
            _, vjp = jax.vjp(lambda p, x: fn(ctx_rows, p, x), tuple(vals[:n_p]), tuple(vals[n_p:n_p + n_r]))
            dp, dx = vjp(vals[-1])
            for ref, v in zip(dx_refs, dx):
                ref[...] = v

            @pl.when(pl.program_id(0) == 0)
            def _():
                for ref, v in zip(dp_refs, dp):
                    ref[...] = v

            @pl.when(pl.program_id(0) > 0)
            def _():
                for ref, v in zip(dp_refs, dp):
                    ref[...] += v

        row_specs = [pl.BlockSpec((tm, r.shape[1]), lambda i: (i, 0)) for r in rows]
        outs = pl.pallas_call(
            body, name=name + "_bwd", grid=(t // tm,),
            in_specs=specs(params, rows, tm) + [pl.BlockSpec((tm, out_w), lambda i: (i, 0))],
            out_specs=row_specs + [pl.BlockSpec(p.shape, lambda i: (0, 0)) for p in params],
            out_shape=[jax.ShapeDtypeStruct(r.shape, F32) for r in rows]
            + [jax.ShapeDtypeStruct(p.shape, F32) for p in params],
            compiler_params=_params("arbitrary"),
        )(*params, *rows, dout)
        return tuple(outs[n_r:]), tuple(outs[:n_r])

    @jax.custom_vjp
    def op(params, rows):
        return fwd_call(params, rows)

    op.defvjp(lambda params, rows: (fwd_call(params, rows), (params, rows)),
              lambda res, g: bwd_call(res[0], res[1], g))
    return op


def _silu(x):
    return x * jax.nn.sigmoid(x)


def _ret_finish_tile(is_ctx, params, rows):
    (gn_w,), (y_f, y_b, gate) = params, rows
    heads = []
    for yh in _split_lanes(y_f + y_b, RET_HEADS):
        yc = yh - jnp.mean(yh, axis=1, keepdims=True)
        heads.append(yc * lax.rsqrt(jnp.mean(yc * yc, axis=1, keepdims=True) + NORM_EPS))
    return jnp.concatenate(heads, axis=1) * gn_w * _silu(gate)


def _ssd_finish_tile(is_ctx, params, rows):
    (d_skip, norm_w), (y_f, y_b, xs, z) = params, rows
    g = (y_f + y_b + d_skip * xs) * _silu(z)
    return g * lax.rsqrt(jnp.mean(g * g, axis=1, keepdims=True) + NORM_EPS) * norm_w


def _gate_merge_tile(is_ctx, params, rows):
    y0, y1, y2, logits = rows
    return sum(jax.nn.sigmoid(g) * y for g, y in zip(_split_lanes(logits, N_BRANCH), (y0, y1, y2)))


def _qk_prep_tile(is_ctx, params, rows):
    (q_w, k_w), (aq, ak, cos, sin) = params, rows
    out = []
    for x, norm_w, heads in ((aq, q_w, ATTN_HEADS), (ak, k_w, ATTN_KV_HEADS)):
        for xh in _split_lanes(x, heads):
            xn = xh * lax.rsqrt(jnp.mean(xh * xh, axis=1, keepdims=True) + NORM_EPS) * norm_w
            x1, x2 = _split_lanes(xn, 2)
            out += [x1 * cos - x2 * sin, x1 * sin + x2 * cos]
    return jnp.concatenate(out, axis=1)


def _ret_prep_tile(is_ctx, params, rows):
    rq, rk, cos, sin = rows
    out = []
    for x, scale in ((rq, 1.0), (rk, RET_DK ** -0.5)):
        for xh in _split_lanes(x, RET_HEADS):
            x1, x2 = _split_lanes(xh, 2)
            out += [(x1 * cos - x2 * sin) * scale, (x1 * sin + x2 * cos) * scale]
    return jnp.concatenate(out, axis=1)


def _loss_rows_tile(is_ctx, params, rows):
    (norm_w,), (h, target) = params, rows
    y = h * lax.rsqrt(jnp.mean(h * h, axis=1, keepdims=True) + NORM_EPS) * norm_w
    err = jnp.mean(jnp.square(y - target), axis=1, keepdims=True)
    return jnp.broadcast_to(err, (h.shape[0], LANES))


def _gated_residual_tile(is_ctx, params, rows):
    (gate,), (h, update) = params, rows
    return h + jnp.where(is_ctx, gate[1:2, :], gate[0:1, :]) * update


NEG_BIG = -1e30


def _attn_tiles(t, ctx_len, backward=False):
    tq = _tile(ctx_len, (256, 128))
    assert t % tq == 0 and ctx_len % tq == 0
    tk = _tile(t, (2816, 1408, 768, 512, 256, 128) if backward else (4224, 2816, 1408, 768, 512, 256, 128))
    return tq, tk


def _head_scores(q_ref, k_bf, g, ki, tk, ctx_len, masked):
    q = (q_ref[0, g] * (ATTN_HEAD_DIM ** -0.5)).astype(MXU_DTYPE)
    s = _dot_nt(q, k_bf)
    if masked:
        col = ki * tk + lax.broadcasted_iota(jnp.int32, s.shape, 1)
        s = jnp.where(col < ctx_len, s, NEG_BIG)
    return q, s


def _attn_cases(qi, ki, tq, tk, ctx_len, compute):
    ctx_q = (qi + 1) * tq <= ctx_len

    @pl.when(jnp.logical_not(ctx_q))
    def _():
        compute(False)

    @pl.when(jnp.logical_and(ctx_q, ki * tk < ctx_len))
    def _():
        compute(True)


def _attn_fwd_call(q, k, v, ctx_len):
    kvh, grp, t, hd = q.shape
    tq, tk = _attn_tiles(t, ctx_len)
    nkb = t // tk

    def body(q_ref, k_ref, v_ref, o_ref, lse_ref, m_sc, l_sc, acc_sc):
        qi, ki = pl.program_id(1), pl.program_id(2)

        @pl.when(ki == 0)
        def _():
            m_sc[...] = jnp.full(m_sc.shape, NEG_BIG, F32)
            l_sc[...] = jnp.zeros(l_sc.shape, F32)
            acc_sc[...] = jnp.zeros(acc_sc.shape, F32)

        def compute(masked):
            k_bf, v_bf = _bf(k_ref[0]), _bf(v_ref[0])
            for g in range(grp):
                _, s = _head_scores(q_ref, k_bf, g, ki, tk, ctx_len, masked)
                m_prev = m_sc[g]
                m_new = jnp.maximum(m_prev, jnp.max(s, axis=1, keepdims=True))
                alpha = jnp.exp(m_prev - m_new)
                p = jnp.exp(s - m_new)
                l_sc[g] = alpha * l_sc[g] + jnp.sum(p, axis=1, keepdims=True)
                acc_sc[g] = alpha * acc_sc[g] + _dot(p, v_bf)
                m_sc[g] = m_new

        _attn_cases(qi, ki, tq, tk, ctx_len, compute)

        @pl.when(ki == nkb - 1)
        def _():
            o_ref[0] = acc_sc[...] / l_sc[...]
            lse_ref[0] = m_sc[...] + jnp.log(l_sc[...])

    return pl.pallas_call(
        body, name="attn_fwd",
        grid=(kvh, t // tq, nkb),
        in_specs=[pl.BlockSpec((1, grp, tq, hd), lambda h, i, j: (h, 0, i, 0)),
                  pl.BlockSpec((1, tk, hd), lambda h, i, j: (h, j, 0)),
                  pl.BlockSpec((1, tk, hd), lambda h, i, j: (h, j, 0))],
        out_specs=[pl.BlockSpec((1, grp, tq, hd), lambda h, i, j: (h, 0, i, 0)),
                   pl.BlockSpec((1, grp, tq, 1), lambda h, i, j: (h, 0, i, 0))],
        out_shape=[jax.ShapeDtypeStruct(q.shape, F32), jax.ShapeDtypeStruct((kvh, grp, t, 1), F32)],
        scratch_shapes=[pltpu.VMEM((grp, tq, 1), F32), pltpu.VMEM((grp, tq, 1), F32), pltpu.VMEM((grp, tq, hd), F32)],
        compiler_params=_params("parallel", "parallel", "arbitrary"),
    )(q, k, v)


def _head_probs(q_ref, k_bf, v_bf, o_ref, do_ref, lse_ref, g, ki, tk, ctx_len, masked):
    q, s = _head_scores(q_ref, k_bf, g, ki, tk, ctx_len, masked)
    do = do_ref[0, g]
    delta = jnp.sum(do * o_ref[0, g], axis=1, keepdims=True)
    p = jnp.exp(s - lse_ref[0, g])
    do = _bf(do)
    ds = p * (_dot_nt(do, v_bf) - delta)
    return q, do, p, ds


def _attn_bwd_call(q, k, v, o, lse, do, ctx_len):
    kvh, grp, t, hd = q.shape
    tq, tk = _attn_tiles(t, ctx_len, backward=True)
    nqb, nkb = t // tq, t // tk

    def body(q_ref, k_ref, v_ref, o_ref, lse_ref, do_ref, dq_hbm, dk_ref, dv_ref, dq_sc, dk_sc, dv_sc, dq_out,
             dq_sem):
        hi, ki, qi = pl.program_id(0), pl.program_id(1), pl.program_id(2)
        rows = pl.ds(pl.multiple_of(qi * tq, tq), tq)

        @pl.when(ki == 0)
        def _():
            dq_sc[:, rows, :] = jnp.zeros((grp, tq, hd), F32)

        @pl.when(qi == 0)
        def _():
            dk_sc[...] = jnp.zeros(dk_sc.shape, F32)
            dv_sc[...] = jnp.zeros(dv_sc.shape, F32)

        def compute(masked):
            k_bf, v_bf = _bf(k_ref[0]), _bf(v_ref[0])
            dk_part = jnp.zeros(dk_sc.shape, F32)
            dv_part = jnp.zeros(dv_sc.shape, F32)
            for g in range(grp):
                qs, dob, p, ds = _head_probs(q_ref, k_bf, v_bf, o_ref, do_ref, lse_ref, g, ki, tk, ctx_len, masked)
                dv_part = dv_part + _dot_tn(p, dob)
                dk_part = dk_part + _dot_tn(ds, qs)
                dq_sc[g, rows, :] += _dot(ds, k_bf)
            dk_sc[...] += dk_part
            dv_sc[...] += dv_part

        _attn_cases(qi, ki, tq, tk, ctx_len, compute)

        @pl.when(ki == nkb - 1)
        def _():
            dq_out[...] = dq_sc[:, rows, :] * (hd ** -0.5)
            done = pltpu.make_async_copy(dq_out, dq_hbm.at[hi, :, rows, :], dq_sem)
            done.start()
            done.wait()

        @pl.when(qi == nqb - 1)
        def _():
            dk_ref[0] = dk_sc[...]
            dv_ref[0] = dv_sc[...]

    qspec = pl.BlockSpec((1, grp, tq, hd), lambda h, j, i: (h, 0, i, 0))
    kspec = pl.BlockSpec((1, tk, hd), lambda h, j, i: (h, j, 0))
    return pl.pallas_call(
        body, name="attn_bwd",
        grid=(kvh, t // tk, nqb),
        in_specs=[qspec, kspec, kspec, qspec, pl.BlockSpec((1, grp, tq, 1), lambda h, j, i: (h, 0, i, 0)), qspec],
        out_specs=[pl.BlockSpec(memory_space=pl.ANY), kspec, kspec],
        out_shape=[jax.ShapeDtypeStruct(q.shape, F32), jax.ShapeDtypeStruct(k.shape, F32),
                   jax.ShapeDtypeStruct(v.shape, F32)],
        scratch_shapes=[pltpu.VMEM((grp, t, hd), F32), pltpu.VMEM((tk, hd), F32), pltpu.VMEM((tk, hd), F32),
                        pltpu.VMEM((grp, tq, hd), F32), pltpu.SemaphoreType.DMA],
        compiler_params=_params("arbitrary", "arbitrary", "arbitrary"),
    )(q, k, v, o, lse, do)


@functools.partial(jax.custom_vjp, nondiff_argnums=(3,))
def attention(q, k, v, ctx_len):
    return _attn_fwd_call(q, k, v, ctx_len)[0]


def _attention_fwd(q, k, v, ctx_len):
    o, lse = _attn_fwd_call(q, k, v, ctx_len)
    return o, (q, k, v, o, lse)


def _attention_bwd(ctx_len, res, do):
    q, k, v, o, lse = res
    return tuple(_attn_bwd_call(q, k, v, o, lse, do, ctx_len))


attention.defvjp(_attention_fwd, _attention_bwd)


def _chunk_order(step, n_chunks, n_ctx_chunks, reverse):
    if not reverse:
        return step
    return jnp.where(step < n_ctx_chunks, n_ctx_chunks - 1 - step, n_chunks + n_ctx_chunks - 1 - step)


def _scan_masks(chunk, reverse):
    row = lax.broadcasted_iota(jnp.int32, (chunk, chunk), 0)
    col = lax.broadcasted_iota(jnp.int32, (chunk, chunk), 1)
    vis = (col >= row) if reverse else (col <= row)
    vis_t = (row >= col) if reverse else (row <= col)
    return vis, vis.astype(F32), vis_t.astype(F32)


def _cum_decay(a_col, a_row, vis_f):
    hi = lax.Precision.HIGHEST
    cum_col = jnp.dot(vis_f, a_col, precision=hi, preferred_element_type=F32)
    cum_row = lax.dot_general(a_row, vis_f, (((1,), (1,)), ((), ())), precision=hi, preferred_element_type=F32)
    total = jnp.sum(a_col, axis=0, keepdims=True)
    return cum_col, cum_row, total


SCAN_CHUNKS_PER_STEP = 2


def _scan_specs(t, ctx_len, reverse, backward, widths):
    rows = SCAN_CHUNK * SCAN_CHUNKS_PER_STEP
    assert t % rows == 0 and ctx_len % rows == 0
    n_blocks, n_ctx_blocks = t // rows, ctx_len // rows

    def order(i):
        step = (n_blocks - 1 - i) if backward else i
        return _chunk_order(step, n_blocks, n_ctx_blocks, reverse)

    halves = list(range(SCAN_CHUNKS_PER_STEP))
    if reverse != backward:
        halves.reverse()
    return [pl.BlockSpec((rows, w), lambda i: (order(i), 0)) for w in widths], order, n_blocks, halves


def _scan_fwd_call(q, k, v, a_col, a_row, *, groups, per_group, dk, dv, ctx_len, reverse):
    t = q.shape[0]
    chunk, per_step = SCAN_CHUNK, SCAN_CHUNKS_PER_STEP
    heads = groups * per_group
    (q_spec, k_spec, v_spec, acol_spec), order, n_blocks, halves = _scan_specs(
        t, ctx_len, reverse, False, (groups * dk, groups * dk, heads * dv, LANES))

    def body(q_ref, k_ref, v_ref, acol_ref, arow_ref, y_ref, st_ref, s_sc):
        @pl.when(pl.program_id(0) == 0)
        def _():
            s_sc[...] = jnp.zeros(s_sc.shape, F32)

        vis, vis_f, _ = _scan_masks(chunk, reverse)
        for half in halves:
            rows = slice(half * chunk, (half + 1) * chunk)
            st_ref[half] = s_sc[...]
            cum_col, cum_row, total = _cum_decay(acol_ref[rows, :], arow_ref[:, rows], vis_f)
            for g in range(groups):
                qg = q_ref[rows, g * dk:(g + 1) * dk]
                kg = k_ref[rows, g * dk:(g + 1) * dk]
                qk = _dot_nt(qg, kg)
                for r in range(per_group):
                    h = g * per_group + r
                    ccol = cum_col[:, h:h + 1]
                    decay = jnp.exp(jnp.where(vis, ccol - cum_row[h:h + 1, :], NEG_BIG))
                    vh = v_ref[rows, h * dv:(h + 1) * dv]
                    s_in = s_sc[h]
                    y = _dot(qk * decay, vh) + jnp.exp(ccol) * _dot(qg, s_in)
                    y_ref[rows, h * dv:(h + 1) * dv] = y
                    tot = total[:, h:h + 1]
                    s_sc[h] = jnp.exp(tot) * s_in + _dot_tn(kg * jnp.exp(tot - ccol), vh)

    return pl.pallas_call(
        body, name="scan_fwd",
        grid=(n_blocks,),
        in_specs=[q_spec, k_spec, v_spec, acol_spec, pl.BlockSpec((8, chunk * per_step), lambda i: (0, order(i)))],
        out_specs=[v_spec, pl.BlockSpec((per_step, heads, dk, dv), lambda i: (order(i), 0, 0, 0))],
        out_shape=[jax.ShapeDtypeStruct(v.shape, F32),
                   jax.ShapeDtypeStruct((n_blocks * per_step, heads, dk, dv), F32)],
        scratch_shapes=[pltpu.VMEM((heads, dk, dv), F32)],
        compiler_params=_params("arbitrary"),
    )(q, k, v, a_col, a_row)


def _scan_bwd_call(q, k, v, a_col, a_row, states, dy, *, groups, per_group, dk, dv, ctx_len, reverse):
    t = q.shape[0]
    chunk, per_step = SCAN_CHUNK, SCAN_CHUNKS_PER_STEP
    heads = groups * per_group
    (q_spec, k_spec, v_spec, acol_spec), order, n_blocks, halves = _scan_specs(
        t, ctx_len, reverse, True, (groups * dk, groups * dk, heads * dv, LANES))
    arow_spec = pl.BlockSpec((8, chunk * per_step), lambda i: (0, order(i)))
    last = 0 if reverse else chunk - 1

    def body(q_ref, k_ref, v_ref, acol_ref, arow_ref, st_ref, dy_ref, dq_ref, dk_ref, dv_ref, da_ref, dat_ref,
             ds_sc):
        @pl.when(pl.program_id(0) == 0)
        def _():
            ds_sc[...] = jnp.zeros(ds_sc.shape, F32)

        vis, vis_f, vis_tf = _scan_masks(chunk, reverse)
        lane = lax.broadcasted_iota(jnp.int32, (chunk, LANES), 1)
        row = lax.broadcasted_iota(jnp.int32, (chunk, LANES), 0)
        sub = lax.broadcasted_iota(jnp.int32, (8, chunk), 0)
        for half in halves:
            rows = slice(half * chunk, (half + 1) * chunk)
            cum_col, cum_row, total = _cum_decay(acol_ref[rows, :], arow_ref[:, rows], vis_f)
            dcum = jnp.zeros((chunk, LANES), F32)
            dcum_t = jnp.zeros((8, chunk), F32)
            for g in range(groups):
                qg = q_ref[rows, g * dk:(g + 1) * dk]
                kg = k_ref[rows, g * dk:(g + 1) * dk]
                qk = _dot_nt(qg, kg)
                dq_g = jnp.zeros((chunk, dk), F32)
                dk_g = jnp.zeros((chunk, dk), F32)
                for r in range(per_group):
                    h = g * per_group + r
                    ccol = cum_col[:, h:h + 1]
                    decay = jnp.exp(jnp.where(vis, ccol - cum_row[h:h + 1, :], NEG_BIG))
                    vh = v_ref[rows, h * dv:(h + 1) * dv]
                    dyh = dy_ref[rows, h * dv:(h + 1) * dv]
                    s_in = st_ref[half, h]
                    ds_out = ds_sc[h]
                    tot = total[:, h:h + 1]
                    e_in = jnp.exp(ccol)
                    e_out = jnp.exp(tot - ccol)
                    e_tot = jnp.exp(tot)
                    k_out = kg * e_out
                    dv_ref[rows, h * dv:(h + 1) * dv] = _dot_tn(qk * decay, dyh) + _dot(k_out, ds_out)
                    dqk = _dot_nt(dyh, vh) * decay
                    dq_in = e_in * _dot_nt(dyh, s_in)
                    dk_out = e_out * _dot_nt(vh, ds_out)
                    dq_h = _dot(dqk, kg) + dq_in
                    dk_h = _dot_tn(dqk, qg) + dk_out
                    s_out = e_tot * s_in + _dot_tn(k_out, vh)
                    edge = jnp.sum(jnp.sum(s_out * ds_out, axis=1, keepdims=True), axis=0, keepdims=True)
                    w_seg = dqk * qk
                    dcum_h = (jnp.sum(w_seg, axis=1, keepdims=True) + jnp.sum(dq_in * qg, axis=1, keepdims=True)
                              - jnp.sum(dk_out * kg, axis=1, keepdims=True))
                    dcum = jnp.where(lane == h, dcum_h + jnp.where(row == last, edge, 0.0), dcum)
                    dcum_t = jnp.where(sub == h, -jnp.sum(w_seg, axis=0, keepdims=True), dcum_t)
                    ds_sc[h] = e_tot * ds_out + _dot_tn(qg, e_in * dyh)
                    dq_g = dq_g + dq_h
                    dk_g = dk_g + dk_h
                dq_ref[rows, g * dk:(g + 1) * dk] = dq_g
                dk_ref[rows, g * dk:(g + 1) * dk] = dk_g
            hi = lax.Precision.HIGHEST
            da_ref[rows, :] = jnp.dot(vis_tf, dcum, precision=hi, preferred_element_type=F32)
            dat_ref[:, rows] = jnp.dot(dcum_t, vis_f, precision=hi, preferred_element_type=F32)

    return pl.pallas_call(
        body, name="scan_bwd",
        grid=(n_blocks,),
        in_specs=[q_spec, k_spec, v_spec, acol_spec, arow_spec,
                  pl.BlockSpec((per_step, heads, dk, dv), lambda i: (order(i), 0, 0, 0)), v_spec],
        out_specs=[q_spec, k_spec, v_spec, acol_spec, arow_spec],
        out_shape=[jax.ShapeDtypeStruct(q.shape, F32), jax.ShapeDtypeStruct(k.shape, F32),
                   jax.ShapeDtypeStruct(v.shape, F32), jax.ShapeDtypeStruct((t, LANES), F32),
                   jax.ShapeDtypeStruct((8, t), F32)],
        scratch_shapes=[pltpu.VMEM((heads, dk, dv), F32)],
        compiler_params=_params("arbitrary"),
    )(q, k, v, a_col, a_row, states, dy)


def _decay_layouts(a):
    t, heads = a.shape
    a_col = jnp.pad(a, ((0, 0), (0, LANES - heads)))
    a_row = jnp.pad(a.T, ((0, 8 - heads), (0, 0)))
    return a_col, a_row


@functools.partial(jax.custom_vjp, nondiff_argnums=(4,))
def linear_scan(q, k, v, a, cfg):
    a_col, a_row = _decay_layouts(a)
    return _scan_fwd_call(q, k, v, a_col, a_row, **dict(cfg))[0]


def _linear_scan_fwd(q, k, v, a, cfg):
    a_col, a_row = _decay_layouts(a)
    y, states = _scan_fwd_call(q, k, v, a_col, a_row, **dict(cfg))
    return y, (q, k, v, a, states)


def _linear_scan_bwd(cfg, res, dy):
    q, k, v, a, states = res
    a_col, a_row = _decay_layouts(a)
    dq, dk, dv, da, da_t = _scan_bwd_call(q, k, v, a_col, a_row, states, dy, **dict(cfg))
    heads = a.shape[1]
    return dq, dk, dv, da[:, :heads] + da_t[:heads].T


linear_scan.defvjp(_linear_scan_fwd, _linear_scan_bwd)


def _scan_cfg(groups, per_group, dk, dv, ctx_len, reverse):
    return (("groups", groups), ("per_group", per_group), ("dk", dk), ("dv", dv), ("ctx_len", ctx_len),
            ("reverse", reverse))


def _axial_tables(n_lat, n_ctx):
    freqs = ATTN_HEAD_DIM // 4
    rows = n_lat // GRID_W
    row = jnp.repeat(jnp.arange(rows, dtype=F32), GRID_W)
    col = jnp.tile(jnp.arange(GRID_W, dtype=F32), rows)
    inv = ROPE_THETA ** (-jnp.arange(freqs, dtype=F32) / freqs)
    ang = jnp.concatenate([row[:, None] * inv, col[:, None] * inv], axis=-1)
    cos = jnp.concatenate([jnp.ones((n_ctx, 2 * freqs), F32), jnp.cos(ang)], axis=0)
    sin = jnp.concatenate([jnp.zeros((n_ctx, 2 * freqs), F32), jnp.sin(ang)], axis=0)
    return cos, sin


def _seq_tables(t):
    pos = jnp.arange(t, dtype=F32)
    inv = ROPE_THETA ** (-jnp.linspace(0.0, 1.0, RET_DK // 2, dtype=F32))
    ang = pos[:, None] * inv
    return jnp.cos(ang), jnp.sin(ang)


def _pad_w_in(w_in):
    d = w_in.shape[0]
    return jnp.concatenate([w_in[:, :DT_END], jnp.zeros((d, DT_PAD), w_in.dtype), w_in[:, DT_END:],
                            jnp.zeros((d, TAIL_PAD), w_in.dtype)], axis=1)


def _split_proj(p):
    widths = list(IN_SPLITS)
    widths[5] = LANES
    out, off = [], 0
    for w in widths:
        out.append(p[:, off:off + w])
        off += w
    out[5] = out[5][:, :DT_COLS]
    return out


def _mixer(u, w, wq, layer, n_ctx, tables):
    t = u.shape[0]
    attn_rope, ret_rope, seg_first, seg_last = tables
    proj = matmul(u, wq["w_in"][layer], _pad_w_in(w["w_in"][layer]))
    aq, ak, av, z, xbc_raw, dt_raw, rq, rk, rv, rg, gate_logits = _split_proj(proj)

    q_width = ATTN_HEADS * ATTN_HEAD_DIM
    qk = _rowwise("qk_prep", _qk_prep_tile, q_width + ATTN_KV_HEADS * ATTN_HEAD_DIM, n_ctx)(
        (w["attn_q_norm"][layer][None, :], w["attn_k_norm"][layer][None, :]), (aq, ak) + attn_rope)
    q4 = qk[:, :q_width].reshape(t, ATTN_KV_HEADS, ATTN_GROUP, ATTN_HEAD_DIM).transpose(1, 2, 0, 3)
    k3 = qk[:, q_width:].reshape(t, ATTN_KV_HEADS, ATTN_HEAD_DIM).transpose(1, 0, 2)
    v3 = av.reshape(t, ATTN_KV_HEADS, ATTN_HEAD_DIM).transpose(1, 0, 2)
    o4 = attention(q4, k3, v3, n_ctx)
    br_attn = o4.transpose(2, 0, 1, 3).reshape(t, ATTN_HEADS * ATTN_HEAD_DIM)

    cw, cb = w["ssd_conv_w"][layer], w["ssd_conv_b"][layer]
    zero_row = jnp.zeros((1, xbc_raw.shape[1]), F32)
    prev = jnp.concatenate([zero_row, xbc_raw[:-1]], axis=0) * (1.0 - seg_first)
    nxt = jnp.concatenate([xbc_raw[1:], zero_row], axis=0) * (1.0 - seg_last)
    xbc = jax.nn.silu(prev * cw[0] + xbc_raw * cw[1] + nxt * cw[2] + cb)
    gn = SSD_GROUPS * SSD_STATE
    xs = xbc[:, :SSD_D_INNER]
    bm = xbc[:, SSD_D_INNER:SSD_D_INNER + gn]
    cm = xbc[:, SSD_D_INNER + gn:]
    dt = jax.nn.softplus(dt_raw.reshape(t, 2, SSD_HEADS) + w["ssd_dt_bias"][layer])
    a_neg = -jnp.exp(w["ssd_a_log"][layer])
    xs_h = xs.reshape(t, SSD_HEADS, SSD_HEAD_DIM)
    y_ssd = []
    for d, reverse in ((0, False), (1, True)):
        dtd = dt[:, d]
        cfg = _scan_cfg(SSD_GROUPS, SSD_HEADS // SSD_GROUPS, SSD_STATE, SSD_HEAD_DIM, n_ctx, reverse)
        y_ssd.append(linear_scan(cm, bm, (xs_h * dtd[:, :, None]).reshape(t, SSD_D_INNER), dtd * a_neg[d], cfg))
    d_skip = jnp.repeat(w["ssd_d"][layer], SSD_HEAD_DIM)[None, :]
    br_ssd = _rowwise("ssd_finish", _ssd_finish_tile, SSD_D_INNER, n_ctx)(
        (d_skip, w["ssd_norm_w"][layer][None, :]), (y_ssd[0], y_ssd[1], xs, z))

    lg = -jnp.exp(w["ret_log_decay"][layer])
    ret_width = RET_HEADS * RET_DK
    ret_qk = _rowwise("ret_prep", _ret_prep_tile, 2 * ret_width, n_ctx)((), (rq, rk) + ret_rope)
    rq_r, rk_r = ret_qk[:, :ret_width], ret_qk[:, ret_width:]
    y_ret = []
    for d, reverse in ((0, False), (1, True)):
        cfg = _scan_cfg(RET_HEADS, 1, RET_DK, RET_DV, n_ctx, reverse)
        y_ret.append(linear_scan(rq_r, rk_r, rv, jnp.broadcast_to(lg[d][None, :], (t, RET_HEADS)), cfg))
    br_ret = _rowwise("ret_finish", _ret_finish_tile, RET_HEADS * RET_DV, n_ctx)(
        (w["ret_gn_w"][layer][None, :],), (y_ret[0], y_ret[1], rg))

    projected = tuple(matmul(br, wq["w_branch"][layer][j], w["w_branch"][layer][j])
                      for j, br in enumerate((br_attn, br_ssd, br_ret)))
    merged = _rowwise("gate_merge", _gate_merge_tile, D_MODEL, n_ctx)((), projected + (gate_logits,))
    return matmul(merged, wq["w_out"][layer], w["w_out"][layer])


def _local_loss(w, x, c, ctx, target, wq):
    n, m = x.shape[0], ctx.shape[0]
    t = n + m
    pos = jnp.arange(t)[:, None]
    seg_first = ((pos == 0) | (pos == m)).astype(F32)
    seg_last = ((pos == m - 1) | (pos == t - 1)).astype(F32)
    tables = (_axial_tables(n, m), _seq_tables(t), seg_first, seg_last)
    h = jnp.concatenate([ctx, x], axis=0)
    cond = jax.nn.silu(jnp.stack([c, w["c_ctx"]], axis=0))
    cond8 = jnp.concatenate([cond, jnp.zeros((6, D_MODEL), F32)], axis=0)
    for layer in range(DEPTH):
        mod = (matmul(cond8, wq["w_mod"][layer], w["w_mod"][layer])[:2] + w["b_mod"][layer]).reshape(2, 6, D_MODEL)

        u = norm_mod(h, w["norm1_w"][layer], mod[:, 0], mod[:, 1], m)
        residual = _rowwise("gated_residual", _gated_residual_tile, D_MODEL, m)
        h = residual((mod[:, 2],), (h, _mixer(u, w, wq, layer, m, tables)))
        v = norm_mod(h, w["norm2_w"][layer], mod[:, 3], mod[:, 4], m)
        mlp = sq_relu_mlp(v, wq["w_mlp1"][layer], w["w_mlp1"][layer], wq["w_mlp2"][layer], w["w_mlp2"][layer])
        residual = _rowwise("gated_residual", _gated_residual_tile, D_MODEL, m)
        h = residual((mod[:, 5],), (h, mlp))
    per_token = _rowwise("loss_rows", _loss_rows_tile, LANES, 0)((w["final_norm_w"][None, :],), (h[m:], target))
    return 0.5 * jnp.sum(per_token[:, 0])


def _coords():
    return lax.axis_index("x"), lax.axis_index("y"), lax.axis_index("c")


def _all_gather(blocks, name):
    n = len(blocks)

    def body(*refs):
        x_refs, out_refs = refs[:n], refs[n:2 * n]
        send_sems, recv_sems, local_sems = refs[2 * n:]
        x, y, c = _coords()
        me, sibling = (x, y, c), (x, y, 1 - c)
        chips = [(1 - x, y), (x, 1 - y), (1 - x, 1 - y)]

        def copy(k, i, blk, to, from_input=False):
            slot = out_refs[i].at[4 * blk[0] + 2 * blk[1] + blk[2]]
            return pltpu.make_async_remote_copy(
                src_ref=x_refs[i] if from_input else slot, dst_ref=slot,
                send_sem=send_sems.at[k * n + i], recv_sem=recv_sems.at[k * n + i],
                device_id=to, device_id_type=pl.DeviceIdType.MESH)

        mine = [pltpu.make_async_copy(x_refs[i], out_refs[i].at[4 * x + 2 * y + c], local_sems.at[i])
                for i in range(n)]
        for cp in mine:
            cp.start()
        first = [copy(0, i, me, sibling, True) for i in range(n)]
        first += [copy(1 + j, i, me, (*chip, c), True) for j, chip in enumerate(chips) for i in range(n)]
        for cp in first:
            cp.start()
        passed = []
        for j, chip in enumerate(chips):
            for i in range(n):
                copy(1 + j, i, (*chip, c), me).wait_recv()
                passed.append(copy(4 + j, i, (*chip, c), sibling))
                passed[-1].start()
        for i in range(n):
            copy(0, i, sibling, me).wait_recv()
        for j, chip in enumerate(chips):
            for i in range(n):
                copy(4 + j, i, (*chip, 1 - c), me).wait_recv()
        for cp in first + passed:
            cp.wait_send()
        for cp in mine:
            cp.wait()

    return pl.pallas_call(
        body, name=name,
        out_shape=[jax.ShapeDtypeStruct((N_DEV,) + b.shape, b.dtype) for b in blocks],
        in_specs=[pl.BlockSpec(memory_space=pl.ANY)] * n,
        out_specs=[pl.BlockSpec(memory_space=pl.ANY)] * n,
        scratch_shapes=[pltpu.SemaphoreType.DMA((7 * n,)), pltpu.SemaphoreType.DMA((7 * n,)),
                        pltpu.SemaphoreType.DMA((n,))],
    )(*blocks)


def _pair_exchange(arrays, name):
    n = len(arrays)

    def body(*refs):
        g_refs, out_refs = refs[:n], refs[n:2 * n]
        send_sems, recv_sems = refs[2 * n:]
        x, y, c = _coords()
        copies = [pltpu.make_async_remote_copy(
            src_ref=g_refs[i].at[1 - c], dst_ref=out_refs[i], send_sem=send_sems.at[i], recv_sem=recv_sems.at[i],
            device_id=(x, y, 1 - c), device_id_type=pl.DeviceIdType.MESH) for i in range(n)]
        for cp in copies:
            cp.start()
        for cp in copies:
            cp.wait()

    return pl.pallas_call(
        body, name=name,
        out_shape=[jax.ShapeDtypeStruct(a.shape[1:], a.dtype) for a in arrays],
        in_specs=[pl.BlockSpec(memory_space=pl.ANY)] * n,
        out_specs=[pl.BlockSpec(memory_space=pl.ANY)] * n,
        scratch_shapes=[pltpu.SemaphoreType.DMA((n,)), pltpu.SemaphoreType.DMA((n,))],
    )(*arrays)


def _chip_exchange(arrays, name):
    n = len(arrays)

    def body(*refs):
        g_refs, out_refs = refs[:n], refs[n:2 * n]
        send_sems, recv_sems, local_sems = refs[2 * n:]
        x, y, c = _coords()
        me = 2 * x + y
        mine = [pltpu.make_async_copy(g_refs[i].at[me], out_refs[i].at[me], local_sems.at[i]) for i in range(n)]
        for cp in mine:
            cp.start()
        copies = []
        for k in range(1, 4):
            px, py = (1 - x if (k >> 1) & 1 else x), (1 - y if k & 1 else y)
            for i in range(n):
                copies.append(pltpu.make_async_remote_copy(
                    src_ref=g_refs[i].at[2 * px + py], dst_ref=out_refs[i].at[me],
                    send_sem=send_sems.at[(k - 1) * n + i], recv_sem=recv_sems.at[(k - 1) * n + i],
                    device_id=(px, py, c), device_id_type=pl.DeviceIdType.MESH))
        for cp in copies:
            cp.start()
        for cp in copies:
            cp.wait_recv()
        for cp in copies:
            cp.wait_send()
        for cp in mine:
            cp.wait()

    return pl.pallas_call(
        body, name=name,
        out_shape=[jax.ShapeDtypeStruct(a.shape, a.dtype) for a in arrays],
        in_specs=[pl.BlockSpec(memory_space=pl.ANY)] * n,
        out_specs=[pl.BlockSpec(memory_space=pl.ANY)] * n,
        scratch_shapes=[pltpu.SemaphoreType.DMA((3 * n,)), pltpu.SemaphoreType.DMA((3 * n,)),
                        pltpu.SemaphoreType.DMA((n,))],
    )(*arrays)


def _add_pair(a, b, name):
    rows, cols = a.shape
    tr = _tile(rows, [r for r in (2048, 1024, 512, 256, 128, 64, 32, 16) if r * cols <= 2 * ADAMW_BLOCK_ELEMS])

    def body(a_ref, b_ref, o_ref):
        o_ref[...] = (a_ref[...].astype(F32) + b_ref[...].astype(F32)).astype(o_ref.dtype)

    spec = pl.BlockSpec((tr, cols), lambda i: (i, 0))
    return pl.pallas_call(
        body, name=name, grid=(rows // tr,), in_specs=[spec, spec], out_specs=spec,
        out_shape=jax.ShapeDtypeStruct(a.shape, a.dtype), compiler_params=_params("parallel"),
    )(a, b)


ADAMW_BLOCK_ELEMS = 256 * 1024


def _sum_adamw(g8, w, m, v, name):
    rows, cols = w.shape
    tr = _tile(rows, [r for r in (2048, 1024, 512, 256, 128, 64, 32, 16) if r * cols <= ADAMW_BLOCK_ELEMS])

    def body(g_ref, w_ref, m_ref, v_ref, go_ref, d_ref, mo_ref, vo_ref):
        g = g_ref[0].astype(F32)
        for s in range(1, g8.shape[0]):
            g = g + g_ref[s].astype(F32)
        m_new = ADAM_B1 * m_ref[...] + (1.0 - ADAM_B1) * g
        v_new = ADAM_B2 * v_ref[...] + (1.0 - ADAM_B2) * (g * g)
        m_hat = m_new / (1.0 - ADAM_B1 ** ADAM_STEP)
        v_hat = v_new / (1.0 - ADAM_B2 ** ADAM_STEP)
        go_ref[...] = g
        d_ref[...] = -ADAM_LR * (m_hat / (jnp.sqrt(v_hat) + ADAM_EPS) + ADAM_WD * w_ref[...])
        mo_ref[...] = m_new
        vo_ref[...] = v_new

    spec = pl.BlockSpec((tr, cols), lambda i: (i, 0))
    shape = jax.ShapeDtypeStruct((rows, cols), F32)
    return pl.pallas_call(
        body, name=name,
        grid=(rows // tr,),
        in_specs=[pl.BlockSpec((g8.shape[0], tr, cols), lambda i: (0, i, 0)), spec, spec, spec],
        out_specs=[spec, spec, spec, spec],
        out_shape=[shape, shape, shape, shape],
        compiler_params=_params("parallel"),
    )(g8, w, m, v)


BIG = ("w_mod", "w_in", "w_branch", "w_out", "w_mlp1", "w_mlp2")
COL_SHARDED = ("w_mod", "w_mlp1")
ROW_SHARDED = ("w_out", "w_mlp2")
SMALL = ("c_ctx", "b_mod", "norm1_w", "norm2_w", "attn_q_norm", "attn_k_norm", "ssd_conv_b", "ssd_dt_bias",
         "ssd_a_log", "ssd_d", "ssd_norm_w", "ret_log_decay", "ret_gn_w", "final_norm_w")
CONV_AXIS = 2
ORDER = ("c_ctx", "w_mod", "b_mod", "norm1_w", "norm2_w", "w_in", "attn_q_norm", "attn_k_norm", "ssd_conv_w",
         "ssd_conv_b", "ssd_dt_bias", "ssd_a_log", "ssd_d", "ssd_norm_w", "ret_log_decay", "ret_gn_w", "w_branch",
         "w_out", "w_mlp1", "w_mlp2", "final_norm_w")


def _compute_weights(gathered):
    wq, carrier = {}, {}
    for name in BIG:
        g = gathered[name]
        per_layer = []
        for layer in range(DEPTH):
            gl = g[:, layer]
            if name in COL_SHARDED:
                per_layer.append(gl)
            elif name in ROW_SHARDED:
                per_layer.append(gl.reshape(-1, gl.shape[-1]))
            elif name == "w_in":
                per_layer.append(_pad_w_in(jnp.concatenate([gl[d] for d in range(N_DEV)], axis=-1)))
            else:
                per_layer.append([jnp.concatenate([gl[d, j] for d in range(N_DEV)], axis=-1) for j in range(N_BRANCH)])
        wq[name] = per_layer
    for name in BIG:
        if name == "w_in":
            carrier[name] = [jnp.zeros((D_MODEL, IN_DIM), F32) for _ in range(DEPTH)]
        else:
            carrier[name] = jax.tree.map(lambda a: jnp.zeros(a.shape, F32), wq[name])
    return wq, carrier


def _grad_shards(gw):
    out = {}
    for name in BIG:
        per_layer = []
        for layer in range(DEPTH):
            g = gw[name][layer]
            if name in COL_SHARDED:
                per_layer.append(g)
            elif name in ROW_SHARDED:
                per_layer.append(g.reshape(N_DEV, -1, g.shape[-1]))
            elif name == "w_in":
                size = IN_DIM // N_DEV
                per_layer.append(jnp.stack([g[:, d * size:(d + 1) * size] for d in range(N_DEV)]))
            else:
                per_layer.append(jnp.stack([gj.reshape(gj.shape[0], N_DEV, -1).transpose(1, 0, 2) for gj in g], axis=1))
        out[name] = jnp.stack(per_layer, axis=1)
    return out


def _pack(arrays, row_multiple):
    flat = jnp.concatenate(arrays, axis=-1)
    n = flat.shape[-1]
    per = LANES * row_multiple
    padded = -(-n // per) * per
    flat = jnp.pad(flat, [(0, 0)] * (flat.ndim - 1) + [(0, padded - n)])
    return flat.reshape(flat.shape[:-1] + (padded // LANES, LANES))


def _unpack(slab, shapes):
    flat = slab.reshape(slab.shape[:-2] + (-1,))
    out, off = [], 0
    for shp in shapes:
        size = math.prod(shp)
        out.append(flat[..., off:off + size].reshape(flat.shape[:-1] + tuple(shp)))
        off += size
    return out


def kernel(x, c, ctx, c_ctx, w_mod, b_mod, norm1_w, norm2_w, w_in, attn_q_norm, attn_k_norm, ssd_conv_w, ssd_conv_b, ssd_dt_bias, ssd_a_log, ssd_d, ssd_norm_w, ret_log_decay, ret_gn_w, w_branch, w_out, w_mlp1, w_mlp2, final_norm_w, loss_target, m_c_ctx, m_w_mod, m_b_mod, m_norm1_w, m_norm2_w, m_w_in, m_attn_q_norm, m_attn_k_norm, m_ssd_conv_w, m_ssd_conv_b, m_ssd_dt_bias, m_ssd_a_log, m_ssd_d, m_ssd_norm_w, m_ret_log_decay, m_ret_gn_w, m_w_branch, m_w_out, m_w_mlp1, m_w_mlp2, m_final_norm_w, v_c_ctx, v_w_mod, v_b_mod, v_norm1_w, v_norm2_w, v_w_in, v_attn_q_norm, v_attn_k_norm, v_ssd_conv_w, v_ssd_conv_b, v_ssd_dt_bias, v_ssd_a_log, v_ssd_d, v_ssd_norm_w, v_ret_log_decay, v_ret_gn_w, v_w_branch, v_w_out, v_w_mlp1, v_w_mlp2, v_final_norm_w):
    args = dict(locals())
    weights = {n: args[n] for n in ORDER}
    mom1 = {n: args["m_" + n] for n in ORDER}
    mom2 = {n: args["v_" + n] for n in ORDER}
    me = 4 * lax.axis_index("x") + 2 * lax.axis_index("y") + lax.axis_index("c")

    gathered = _all_gather([weights[n].astype(MXU_DTYPE) for n in BIG], "gather_weights")
    wq, params = _compute_weights(dict(zip(BIG, gathered)))
    conv_shape = ssd_conv_w.shape
    conv_all = _all_gather([_pack([ssd_conv_w.reshape(-1)], 8)], "gather_conv")[0]
    params["ssd_conv_w"] = jnp.concatenate(list(_unpack(conv_all, [conv_shape])[0]), axis=CONV_AXIS)
    for n in SMALL:
        params[n] = weights[n]

    loss, (gw, gx) = jax.value_and_grad(_local_loss, argnums=(0, 1))(params, x[0], c[0], ctx[0], loss_target[0], wq)
    loss = lax.psum(loss, MESH_AXES)

    g_send = _grad_shards(gw)
    by_core = [g_send[n].reshape((4, 2) + g_send[n].shape[1:]).swapaxes(0, 1).astype(jnp.bfloat16) for n in BIG]
    from_sibling = _pair_exchange(by_core, "scatter_grads_d2d")
    my_core = lax.axis_index("c")
    chip_sums = []
    for n, mine, theirs in zip(BIG, by_core, from_sibling):
        kept = lax.dynamic_index_in_dim(mine, my_core, 0, keepdims=False)
        chip_sums.append(_add_pair(kept.reshape(-1, kept.shape[-1]), theirs.reshape(-1, theirs.shape[-1]),
                                   "chip_sum_" + n).reshape(theirs.shape))
    g_recv = _chip_exchange(chip_sums, "scatter_grads_ici")
    result = {}
    for n, g8 in zip(BIG, g_recv):
        shape = weights[n].shape
        as2d = lambda a: a.reshape(-1, shape[-1])
        outs = _sum_adamw(g8.reshape(4, -1, shape[-1]), as2d(weights[n]), as2d(mom1[n]), as2d(mom2[n]), "adamw_" + n)
        for kind, arr in zip(("grad", "delta", "new_m", "new_v"), outs):
            result[kind, n] = arr.reshape(shape)

    conv_full_shape = params["ssd_conv_w"].shape
    small_shapes = [weights[n].shape for n in SMALL]
    partial = _pack([gw[n].reshape(-1) for n in SMALL] + [gw["ssd_conv_w"].reshape(-1)], 8)
    parts = _unpack(_all_gather([partial], "gather_small_grads")[0], small_shapes + [conv_full_shape])
    conv_part = lax.dynamic_slice_in_dim(parts[-1], me * conv_shape[CONV_AXIS], conv_shape[CONV_AXIS], CONV_AXIS + 1)
    small_names = list(SMALL) + ["ssd_conv_w"]
    g8_small = _pack([p.reshape(N_DEV, -1) for p in parts[:-1]] + [conv_part.reshape(N_DEV, -1)], 8)
    slabs = [_pack([d[n].reshape(-1) for n in small_names], 8) for d in (weights, mom1, mom2)]
    small_out = [_unpack(s, small_shapes + [conv_shape]) for s in _sum_adamw(g8_small, *slabs, "adamw_small")]
    for kind, small_k in zip(("grad", "delta", "new_m", "new_v"), small_out):
        for n, arr in zip(small_names, small_k):
            result[kind, n] = arr

    outs = [loss, gx[None]]
    for kind in ("grad", "delta", "new_m", "new_v"):
        outs += [result[kind, n] for n in ORDER]
    return tuple(outs)
```

```python
import functools
import math

import jax
import jax.numpy as jnp
from jax import lax
from jax.experimental import pallas as pl
from jax.experimental.pallas import tpu as pltpu

F32 = jnp.float32
MXU_DTYPE = jnp.bfloat16
VMEM_LIMIT_BYTES = 48 * 1024 * 1024
LANES = 128
N_DEV = 8
MESH_AXES = ("x", "y", "c")

D_MODEL = 1024
GRID_W = 64
NORM_EPS = 1e-6
ROPE_THETA = 10000.0
ATTN_HEADS, ATTN_KV_HEADS, ATTN_HEAD_DIM = 8, 2, 64
ATTN_GROUP = ATTN_HEADS // ATTN_KV_HEADS
SSD_HEADS, SSD_HEAD_DIM, SSD_GROUPS, SSD_STATE = 8, 64, 2, 128
SSD_D_INNER = SSD_HEADS * SSD_HEAD_DIM
RET_HEADS, RET_DK, RET_DV = 4, 128, 128
SCAN_CHUNK = 128
N_BRANCH = 3
DEPTH = 2

IN_SPLITS = (512, 128, 128, 512, 1024, 16, 512, 512, 512, 512, 3072)
IN_DIM = sum(IN_SPLITS)
DT_COLS = 16
DT_PAD = LANES - DT_COLS
TAIL_PAD = 128
IN_DIM_PADDED = IN_DIM + DT_PAD + TAIL_PAD
DT_END = sum(IN_SPLITS[:6])

ADAM_LR, ADAM_B1, ADAM_B2, ADAM_EPS, ADAM_WD, ADAM_STEP = 0.001, 0.9, 0.999, 1e-08, 0.01, 10


def _tile(dim, prefs):
    for p in prefs:
        if dim % p == 0:
            return p
    return dim


def _params(*sem):
    return pltpu.CompilerParams(dimension_semantics=sem, vmem_limit_bytes=VMEM_LIMIT_BYTES)


def _mm(a, b, *, ta=False, tb=False, out_shards=False, epilogue=None, extra=None, out_dtype=F32, name):
    if ta:
        kdim, m = a.shape
    else:
        m, kdim = a.shape
    b_shards = b.ndim == 3
    if b_shards:
        rows_b, cols_b = b.shape[1], N_DEV * b.shape[2]
    else:
        rows_b, cols_b = b.shape
    n, kdim_b = (rows_b, cols_b) if tb else (cols_b, rows_b)
    assert kdim == kdim_b, (a.shape, b.shape, ta, tb)
    tm = _tile(m, (1024, 768, 512, 256, 128))
    n_tile_of = n // N_DEV if (out_shards or (b_shards and not tb)) else n
    k_tile_of = kdim // N_DEV if (b_shards and tb) else kdim
    tn = _tile(n_tile_of, (1280, 1024, 768, 512, 384, 256, 128))
    tk = _tile(k_tile_of, (1024, 768, 512, 256, 128))
    dims = (((0 if ta else 1,), (1 if tb else 0,)), ((), ()))

    n_k = kdim // tk

    def body(a_ref, b_ref, *rest):
        o_ref = rest[-1]
        part = lax.dot_general(a_ref[...].astype(MXU_DTYPE), b_ref[...].astype(MXU_DTYPE), dims,
                               preferred_element_type=F32)

        def finish(acc):
            if epilogue == "sq_relu":
                acc = jnp.square(jnp.maximum(acc, 0.0))
            elif epilogue == "d_sq_relu":
                acc = acc * (2.0 * jnp.sqrt(rest[0][...].astype(F32)))
            return acc.astype(o_ref.dtype)

        if n_k == 1:
            o_ref[...] = finish(part)
            return

        @pl.when(pl.program_id(2) == 0)
        def _():
            o_ref[...] = part

        @pl.when(pl.program_id(2) > 0)
        def _():
            o_ref[...] += part

        if epilogue is not None:
            @pl.when(pl.program_id(2) == n_k - 1)
            def _():
                o_ref[...] = finish(o_ref[...])

    a_spec = pl.BlockSpec((tk, tm), lambda i, j, k: (k, i)) if ta else pl.BlockSpec((tm, tk), lambda i, j, k: (i, k))
    if not b_shards:
        b_spec = pl.BlockSpec((tn, tk), lambda i, j, k: (j, k)) if tb else pl.BlockSpec((tk, tn), lambda i, j, k: (k, j))
    elif tb:
        per = b.shape[2] // tk
        b_spec = pl.BlockSpec((None, tn, tk), lambda i, j, k: (k // per, j, k % per))
    else:
        per = b.shape[2] // tn
        b_spec = pl.BlockSpec((None, tk, tn), lambda i, j, k: (j // per, k, j % per))
    if out_shards:
        per_out = n // N_DEV // tn
        out_spec = pl.BlockSpec((None, tm, tn), lambda i, j, k: (j // per_out, i, j % per_out))
        out_shape = jax.ShapeDtypeStruct((N_DEV, m, n // N_DEV), out_dtype)
    else:
        out_spec = pl.BlockSpec((tm, tn), lambda i, j, k: (i, j))
        out_shape = jax.ShapeDtypeStruct((m, n), out_dtype)
    assert epilogue in (None, "sq_relu", "d_sq_relu") and (extra is not None) == (epilogue == "d_sq_relu")
    assert out_dtype == F32 or n_k == 1
    operands, in_specs = [a, b], [a_spec, b_spec]
    if extra is not None:
        assert not out_shards and extra.shape == (m, n)
        operands.append(extra)
        in_specs.append(out_spec)
    return pl.pallas_call(
        body, name=name,
        grid=(m // tm, n // tn, n_k),
        in_specs=in_specs,
        out_specs=out_spec,
        out_shape=out_shape,
        compiler_params=_params("parallel", "parallel", "arbitrary"),
    )(*operands)


@jax.custom_vjp
def matmul(a, b, b_grad):
    return _mm(a, b, name="mm_fwd")


def _matmul_fwd(a, b, b_grad):
    return _mm(a, b, name="mm_fwd"), (a, b)


def _matmul_bwd(res, g):
    a, b = res
    dw = _mm(a, g, ta=True, out_shards=b.ndim == 3, name="mm_dw")
    return _mm(g, b, tb=True, name="mm_dx"), jnp.zeros_like(b), dw


matmul.defvjp(_matmul_fwd, _matmul_bwd)


@jax.custom_vjp
def sq_relu_mlp(x, w1, w1_grad, w2, w2_grad):
    return _mm(_mm(x, w1, epilogue="sq_relu", out_dtype=MXU_DTYPE, name="mlp_up"), w2, name="mlp_down")


def _sq_relu_mlp_fwd(x, w1, w1_grad, w2, w2_grad):
    hid = _mm(x, w1, epilogue="sq_relu", out_dtype=MXU_DTYPE, name="mlp_up")
    return _mm(hid, w2, name="mlp_down"), (x, w1, w2, hid)


def _sq_relu_mlp_bwd(res, g):
    x, w1, w2, hid = res
    d_pre = _mm(g, w2, tb=True, epilogue="d_sq_relu", extra=hid, name="mlp_down_dx")
    dw2 = _mm(hid, g, ta=True, out_shards=w2.ndim == 3, name="mlp_down_dw")
    dw1 = _mm(x, d_pre, ta=True, out_shards=w1.ndim == 3, name="mlp_up_dw")
    return _mm(d_pre, w1, tb=True, name="mlp_up_dx"), jnp.zeros_like(w1), dw1, jnp.zeros_like(w2), dw2


sq_relu_mlp.defvjp(_sq_relu_mlp_fwd, _sq_relu_mlp_bwd)


NORM_TILE_PREFS = (768, 512, 256, 128)


def _norm_mod_pieces(h_ref, w_ref, shift_ref, scale_ref, tile, tm, ctx_len):
    x = h_ref[...]
    rstd = lax.rsqrt(jnp.mean(x * x, axis=1, keepdims=True) + NORM_EPS)
    xn = x * rstd
    is_ctx = tile * tm + lax.broadcasted_iota(jnp.int32, (tm, 1), 0) < ctx_len
    scale = jnp.where(is_ctx, scale_ref[1:2, :], scale_ref[0:1, :])
    shift = jnp.where(is_ctx, shift_ref[1:2, :], shift_ref[0:1, :])
    return xn, rstd, is_ctx, scale, shift


def _norm_mod_fwd_call(h, w, shift, scale, ctx_len):
    t, d = h.shape
    tm = _tile(t, NORM_TILE_PREFS)

    def body(h_ref, w_ref, shift_ref, scale_ref, u_ref):
        xn, _, _, sc, sh = _norm_mod_pieces(h_ref, w_ref, shift_ref, scale_ref, pl.program_id(0), tm, ctx_len)
        u_ref[...] = xn * w_ref[...] * (1.0 + sc) + sh

    row = pl.BlockSpec((tm, d), lambda i: (i, 0))
    return pl.pallas_call(
        body, name="norm_mod_fwd",
        grid=(t // tm,),
        in_specs=[row, pl.BlockSpec((1, d), lambda i: (0, 0)), pl.BlockSpec((2, d), lambda i: (0, 0)),
                  pl.BlockSpec((2, d), lambda i: (0, 0))],
        out_specs=row,
        out_shape=jax.ShapeDtypeStruct((t, d), F32),
        compiler_params=_params("parallel"),
    )(h, w, shift, scale)


def _norm_mod_bwd_call(h, w, shift, scale, du, ctx_len):
    t, d = h.shape
    tm = _tile(t, NORM_TILE_PREFS)

    def body(h_ref, w_ref, shift_ref, scale_ref, du_ref, dh_ref, sums_ref):
        xn, rstd, is_ctx, sc, _ = _norm_mod_pieces(h_ref, w_ref, shift_ref, scale_ref, pl.program_id(0), tm, ctx_len)
        du = du_ref[...]
        wv = w_ref[...]
        dy = du * (1.0 + sc)
        dxn = dy * wv
        dh_ref[...] = rstd * (dxn - xn * jnp.mean(dxn * xn, axis=1, keepdims=True))
        dsc = du * (xn * wv)

        def colsum(v):
            return jnp.sum(v, axis=0, keepdims=True)

        dshift_all, dshift_ctx = colsum(du), colsum(jnp.where(is_ctx, du, 0.0))
        dscale_all, dscale_ctx = colsum(dsc), colsum(jnp.where(is_ctx, dsc, 0.0))
        part = jnp.concatenate([colsum(dy * xn), dshift_all - dshift_ctx, dshift_ctx, dscale_all - dscale_ctx,
                                dscale_ctx, jnp.zeros((3, d), F32)], axis=0)

        @pl.when(pl.program_id(0) == 0)
        def _():
            sums_ref[...] = part

        @pl.when(pl.program_id(0) > 0)
        def _():
            sums_ref[...] += part

    row = pl.BlockSpec((tm, d), lambda i: (i, 0))
    return pl.pallas_call(
        body, name="norm_mod_bwd",
        grid=(t // tm,),
        in_specs=[row, pl.BlockSpec((1, d), lambda i: (0, 0)), pl.BlockSpec((2, d), lambda i: (0, 0)),
                  pl.BlockSpec((2, d), lambda i: (0, 0)), row],
        out_specs=[row, pl.BlockSpec((8, d), lambda i: (0, 0))],
        out_shape=[jax.ShapeDtypeStruct((t, d), F32), jax.ShapeDtypeStruct((8, d), F32)],
        compiler_params=_params("arbitrary"),
    )(h, w, shift, scale, du)


@functools.partial(jax.custom_vjp, nondiff_argnums=(4,))
def norm_mod(h, w, shift, scale, ctx_len):
    return _norm_mod_fwd_call(h, w[None, :], shift, scale, ctx_len)


def _norm_mod_fwd(h, w, shift, scale, ctx_len):
    return _norm_mod_fwd_call(h, w[None, :], shift, scale, ctx_len), (h, w, shift, scale)


def _norm_mod_bwd(ctx_len, res, du):
    h, w, shift, scale = res
    dh, sums = _norm_mod_bwd_call(h, w[None, :], shift, scale, du, ctx_len)
    return dh, sums[0], sums[1:3], sums[3:5]


norm_mod.defvjp(_norm_mod_fwd, _norm_mod_bwd)


def _bf(x):
    return x.astype(MXU_DTYPE)


def _dot(a, b):
    return jnp.dot(_bf(a), _bf(b), preferred_element_type=F32)


def _dot_nt(a, b):
    return lax.dot_general(_bf(a), _bf(b), (((1,), (1,)), ((), ())), preferred_element_type=F32)


def _dot_tn(a, b):
    return lax.dot_general(_bf(a), _bf(b), (((0,), (0,)), ((), ())), preferred_element_type=F32)


ROWWISE_VMEM_BYTES = 20 * 1024 * 1024


@functools.partial(jax.custom_vjp, nondiff_argnums=(1,))
def _split_lanes(x, n):
    w = x.shape[1] // n
    return tuple(x[:, i * w:(i + 1) * w] for i in range(n))


def _split_lanes_fwd(x, n):
    return _split_lanes(x, n), None


def _split_lanes_bwd(n, _, gs):
    return (jnp.concatenate(gs, axis=1),)


_split_lanes.defvjp(_split_lanes_fwd, _split_lanes_bwd)


def _rowwise_tile(t, widths):
    for tm in (768, 512, 256, 128):
        if t % tm == 0 and tm * 8 * sum(widths) <= ROWWISE_VMEM_BYTES:
            return tm
    raise ValueError((t, widths))


def _rowwise(name, fn, out_w, ctx_len):
    multi = isinstance(out_w, tuple)
    out_ws = out_w if multi else (out_w,)
    n_o = len(out_ws)

    def is_ctx(tm):
        return pl.program_id(0) * tm + lax.broadcasted_iota(jnp.int32, (tm, 1), 0) < ctx_len

    def specs(params, rows, tm):
        return ([pl.BlockSpec(p.shape, lambda i: (0, 0)) for p in params]
                + [pl.BlockSpec((tm, r.shape[1]), lambda i: (i, 0)) for r in rows])

    def fwd_call(params, rows):
        t = rows[0].shape[0]
        tm = _rowwise_tile(t, [r.shape[1] for r in rows] + list(out_ws))
        n_p = len(params)

        def body(*refs):
            vals = [r[...] for r in refs[:-n_o]]
            outs = fn(is_ctx(tm), tuple(vals[:n_p]), tuple(vals[n_p:]))
            for ref, v in zip(refs[-n_o:], outs if multi else (outs,)):
                ref[...] = v

        outs = pl.pallas_call(
            body, name=name + "_fwd", grid=(t // tm,),
            in_specs=specs(params, rows, tm),
            out_specs=[pl.BlockSpec((tm, w), lambda i: (i, 0)) for w in out_ws],
            out_shape=[jax.ShapeDtypeStruct((t, w), F32) for w in out_ws],
            compiler_params=_params("parallel"),
        )(*params, *rows)
        return tuple(outs) if multi else outs[0]

    def bwd_call(params, rows, dout):
        t = rows[0].shape[0]
        douts = tuple(dout) if multi else (dout,)
        tm = _rowwise_tile(t, [2 * r.shape[1] for r in rows] + list(out_ws))
        n_p, n_r = len(params), len(rows)
        n_in = n_p + n_r + n_o

        def body(*refs):
            vals = [r[...] for r in refs[:n_in]]
            dx_refs = refs[n_in:n_in + n_r]
            dp_refs = refs[n_in + n_r:]
            ctx_rows = is_ctx(tm)
            _, vjp = jax.vjp(lambda p, x: fn(ctx_rows, p, x), tuple(vals[:n_p]), tuple(vals[n_p:n_p + n_r]))
            dp, dx = vjp(tuple(vals[n_p + n_r:]) if multi else vals[-1])
            for ref, v in zip(dx_refs, dx):
                ref[...] = v

            @pl.when(pl.program_id(0) == 0)
            def _():
                for ref, v in zip(dp_refs, dp):
                    ref[...] = v

            @pl.when(pl.program_id(0) > 0)
            def _():
                for ref, v in zip(dp_refs, dp):
                    ref[...] += v

        row_specs = [pl.BlockSpec((tm, r.shape[1]), lambda i: (i, 0)) for r in rows]
        outs = pl.pallas_call(
            body, name=name + "_bwd", grid=(t // tm,),
            in_specs=specs(params, rows, tm) + [pl.BlockSpec((tm, w), lambda i: (i, 0)) for w in out_ws],
            out_specs=row_specs + [pl.BlockSpec(p.shape, lambda i: (0, 0)) for p in params],
            out_shape=[jax.ShapeDtypeStruct(r.shape, F32) for r in rows]
            + [jax.ShapeDtypeStruct(p.shape, F32) for p in params],
            compiler_params=_params("arbitrary"),
        )(*params, *rows, *douts)
        return tuple(outs[n_r:]), tuple(outs[:n_r])

    @jax.custom_vjp
    def op(params, rows):
        return fwd_call(params, rows)

    op.defvjp(lambda params, rows: (fwd_call(params, rows), (params, rows)),
              lambda res, g: bwd_call(res[0], res[1], g))
    return op


def _silu(x):
    return x * jax.nn.sigmoid(x)


def _ret_finish_tile(is_ctx, params, rows):
    (gn_w,), (y_f, y_b, gate) = params, rows
    heads = []
    for yh in _split_lanes(y_f + y_b, RET_HEADS):
        yc = yh - jnp.mean(yh, axis=1, keepdims=True)
        heads.append(yc * lax.rsqrt(jnp.mean(yc * yc, axis=1, keepdims=True) + NORM_EPS))
    return jnp.concatenate(heads, axis=1) * gn_w * _silu(gate)


def _ssd_finish_tile(is_ctx, params, rows):
    (d_skip, norm_w), (y_f, y_b, xs, z) = params, rows
    g = (y_f + y_b + d_skip * xs) * _silu(z)
    return g * lax.rsqrt(jnp.mean(g * g, axis=1, keepdims=True) + NORM_EPS) * norm_w


def _gate_merge_tile(is_ctx, params, rows):
    y0, y1, y2, logits = rows
    return sum(jax.nn.sigmoid(g) * y for g, y in zip(_split_lanes(logits, N_BRANCH), (y0, y1, y2)))


def _qk_prep_tile(is_ctx, params, rows):
    (q_w, k_w), (aq, ak, cos, sin) = params, rows
    out = []
    for x, norm_w, heads in ((aq, q_w, ATTN_HEADS), (ak, k_w, ATTN_KV_HEADS)):
        for xh in _split_lanes(x, heads):
            xn = xh * lax.rsqrt(jnp.mean(xh * xh, axis=1, keepdims=True) + NORM_EPS) * norm_w
            x1, x2 = _split_lanes(xn, 2)
            out += [x1 * cos - x2 * sin, x1 * sin + x2 * cos]
    return jnp.concatenate(out, axis=1)


def _ret_prep_tile(is_ctx, params, rows):
    rq, rk, cos, sin = rows
    out = []
    for x, scale in ((rq, 1.0), (rk, RET_DK ** -0.5)):
        for xh in _split_lanes(x, RET_HEADS):
            x1, x2 = _split_lanes(xh, 2)
            out += [(x1 * cos - x2 * sin) * scale, (x1 * sin + x2 * cos) * scale]
    return jnp.concatenate(out, axis=1)


def _softplus(z):
    e = jnp.exp(jnp.minimum(z, 0.0))
    series = e * (1.0 - e * (0.5 - e * (1.0 / 3.0)))
    return jnp.where(z < -5.0, series, jnp.maximum(z, 0.0) + jnp.log(1.0 + jnp.exp(-jnp.abs(z))))


def _ssd_prep_tile(is_ctx, params, rows):
    tap_prev, tap_mid, tap_next, conv_b, dt_bias, a_diag, expand_f, expand_b = params
    x, prev, nxt, dt_raw = rows
    xbc = _silu(prev * tap_prev + x * tap_mid + nxt * tap_next + conv_b)
    xs, rest = _split_lanes(xbc, 2)
    bm, cm = _split_lanes(rest, 2)
    dt = _softplus(dt_raw + dt_bias)
    hi = lax.Precision.HIGHEST
    dt_f = jnp.dot(dt, expand_f, precision=hi, preferred_element_type=F32)
    dt_b = jnp.dot(dt, expand_b, precision=hi, preferred_element_type=F32)
    return xs, bm, cm, xs * dt_f, xs * dt_b, jnp.dot(dt, a_diag, precision=hi, preferred_element_type=F32)


def _loss_rows_tile(is_ctx, params, rows):
    (norm_w,), (h, target) = params, rows
    y = h * lax.rsqrt(jnp.mean(h * h, axis=1, keepdims=True) + NORM_EPS) * norm_w
    err = jnp.mean(jnp.square(y - target), axis=1, keepdims=True)
    return jnp.broadcast_to(err, (h.shape[0], LANES))


def _gated_residual_tile(is_ctx, params, rows):
    (gate,), (h, update) = params, rows
    return h + jnp.where(is_ctx, gate[1:2, :], gate[0:1, :]) * update


NEG_BIG = -1e30


def _attn_tiles(t, ctx_len, backward=False):
    tq = _tile(ctx_len, (256, 128))
    assert t % tq == 0 and ctx_len % tq == 0
    tk = _tile(t, (2816, 1408, 768, 512, 256, 128) if backward else (4224, 2816, 1408, 768, 512, 256, 128))
    return tq, tk


def _head_scores(q_ref, k_bf, g, ki, tk, ctx_len, masked):
    q = (q_ref[0, g] * (ATTN_HEAD_DIM ** -0.5)).astype(MXU_DTYPE)
    s = _dot_nt(q, k_bf)
    if masked:
        col = ki * tk + lax.broadcasted_iota(jnp.int32, s.shape, 1)
        s = jnp.where(col < ctx_len, s, NEG_BIG)
    return q, s


def _attn_cases(qi, ki, tq, tk, ctx_len, compute):
    ctx_q = (qi + 1) * tq <= ctx_len

    @pl.when(jnp.logical_not(ctx_q))
    def _():
        compute(False)

    @pl.when(jnp.logical_and(ctx_q, ki * tk < ctx_len))
    def _():
        compute(True)


def _attn_fwd_call(q, k, v, ctx_len):
    kvh, grp, t, hd = q.shape
    tq, tk = _attn_tiles(t, ctx_len)
    nkb = t // tk

    def body(q_ref, k_ref, v_ref, o_ref, lse_ref, m_sc, l_sc, acc_sc):
        qi, ki = pl.program_id(1), pl.program_id(2)

        @pl.when(ki == 0)
        def _():
            m_sc[...] = jnp.full(m_sc.shape, NEG_BIG, F32)
            l_sc[...] = jnp.zeros(l_sc.shape, F32)
            acc_sc[...] = jnp.zeros(acc_sc.shape, F32)

        def compute(masked):
            k_bf, v_bf = _bf(k_ref[0]), _bf(v_ref[0])
            for g in range(grp):
                _, s = _head_scores(q_ref, k_bf, g, ki, tk, ctx_len, masked)
                m_prev = m_sc[g]
                m_new = jnp.maximum(m_prev, jnp.max(s, axis=1, keepdims=True))
                alpha = jnp.exp(m_prev - m_new)
                p = jnp.exp(s - m_new)
                l_sc[g] = alpha * l_sc[g] + jnp.sum(p, axis=1, keepdims=True)
                acc_sc[g] = alpha * acc_sc[g] + _dot(p, v_bf)
                m_sc[g] = m_new

        _attn_cases(qi, ki, tq, tk, ctx_len, compute)

        @pl.when(ki == nkb - 1)
        def _():
            o_ref[0] = acc_sc[...] / l_sc[...]
            lse_ref[0] = m_sc[...] + jnp.log(l_sc[...])

    return pl.pallas_call(
        body, name="attn_fwd",
        grid=(kvh, t // tq, nkb),
        in_specs=[pl.BlockSpec((1, grp, tq, hd), lambda h, i, j: (h, 0, i, 0)),
                  pl.BlockSpec((1, tk, hd), lambda h, i, j: (h, j, 0)),
                  pl.BlockSpec((1, tk, hd), lambda h, i, j: (h, j, 0))],
        out_specs=[pl.BlockSpec((1, grp, tq, hd), lambda h, i, j: (h, 0, i, 0)),
                   pl.BlockSpec((1, grp, tq, 1), lambda h, i, j: (h, 0, i, 0))],
        out_shape=[jax.ShapeDtypeStruct(q.shape, F32), jax.ShapeDtypeStruct((kvh, grp, t, 1), F32)],
        scratch_shapes=[pltpu.VMEM((grp, tq, 1), F32), pltpu.VMEM((grp, tq, 1), F32), pltpu.VMEM((grp, tq, hd), F32)],
        compiler_params=_params("parallel", "parallel", "arbitrary"),
    )(q, k, v)


def _head_probs(q_ref, k_bf, v_bf, o_ref, do_ref, lse_ref, g, ki, tk, ctx_len, masked):
    q, s = _head_scores(q_ref, k_bf, g, ki, tk, ctx_len, masked)
    do = do_ref[0, g]
    delta = jnp.sum(do * o_ref[0, g], axis=1, keepdims=True)
    p = jnp.exp(s - lse_ref[0, g])
    do = _bf(do)
    ds = p * (_dot_nt(do, v_bf) - delta)
    return q, do, p, ds


def _attn_bwd_call(q, k, v, o, lse, do, ctx_len):
    kvh, grp, t, hd = q.shape
    tq, tk = _attn_tiles(t, ctx_len, backward=True)
    nqb, nkb = t // tq, t // tk

    def body(q_ref, k_ref, v_ref, o_ref, lse_ref, do_ref, dq_hbm, dk_ref, dv_ref, dq_sc, dk_sc, dv_sc, dq_out,
             dq_sem):
        hi, ki, qi = pl.program_id(0), pl.program_id(1), pl.program_id(2)
        rows = pl.ds(pl.multiple_of(qi * tq, tq), tq)

        @pl.when(ki == 0)
        def _():
            dq_sc[:, rows, :] = jnp.zeros((grp, tq, hd), F32)

        @pl.when(qi == 0)
        def _():
            dk_sc[...] = jnp.zeros(dk_sc.shape, F32)
            dv_sc[...] = jnp.zeros(dv_sc.shape, F32)

        def compute(masked):
            k_bf, v_bf = _bf(k_ref[0]), _bf(v_ref[0])
            dk_part = jnp.zeros(dk_sc.shape, F32)
            dv_part = jnp.zeros(dv_sc.shape, F32)
            for g in range(grp):
                qs, dob, p, ds = _head_probs(q_ref, k_bf, v_bf, o_ref, do_ref, lse_ref, g, ki, tk, ctx_len, masked)
                dv_part = dv_part + _dot_tn(p, dob)
                dk_part = dk_part + _dot_tn(ds, qs)
                dq_sc[g, rows, :] += _dot(ds, k_bf)
            dk_sc[...] += dk_part
            dv_sc[...] += dv_part

        _attn_cases(qi, ki, tq, tk, ctx_len, compute)

        @pl.when(ki == nkb - 1)
        def _():
            dq_out[...] = dq_sc[:, rows, :] * (hd ** -0.5)
            done = pltpu.make_async_copy(dq_out, dq_hbm.at[hi, :, rows, :], dq_sem)
            done.start()
            done.wait()

        @pl.when(qi == nqb - 1)
        def _():
            dk_ref[0] = dk_sc[...]
            dv_ref[0] = dv_sc[...]

    qspec = pl.BlockSpec((1, grp, tq, hd), lambda h, j, i: (h, 0, i, 0))
    kspec = pl.BlockSpec((1, tk, hd), lambda h, j, i: (h, j, 0))
    return pl.pallas_call(
        body, name="attn_bwd",
        grid=(kvh, t // tk, nqb),
        in_specs=[qspec, kspec, kspec, qspec, pl.BlockSpec((1, grp, tq, 1), lambda h, j, i: (h, 0, i, 0)), qspec],
        out_specs=[pl.BlockSpec(memory_space=pl.ANY), kspec, kspec],
        out_shape=[jax.ShapeDtypeStruct(q.shape, F32), jax.ShapeDtypeStruct(k.shape, F32),
                   jax.ShapeDtypeStruct(v.shape, F32)],
        scratch_shapes=[pltpu.VMEM((grp, t, hd), F32), pltpu.VMEM((tk, hd), F32), pltpu.VMEM((tk, hd), F32),
                        pltpu.VMEM((grp, tq, hd), F32), pltpu.SemaphoreType.DMA],
        compiler_params=_params("arbitrary", "arbitrary", "arbitrary"),
    )(q, k, v, o, lse, do)


@functools.partial(jax.custom_vjp, nondiff_argnums=(3,))
def attention(q, k, v, ctx_len):
    return _attn_fwd_call(q, k, v, ctx_len)[0]


def _attention_fwd(q, k, v, ctx_len):
    o, lse = _attn_fwd_call(q, k, v, ctx_len)
    return o, (q, k, v, o, lse)


def _attention_bwd(ctx_len, res, do):
    q, k, v, o, lse = res
    return tuple(_attn_bwd_call(q, k, v, o, lse, do, ctx_len))


attention.defvjp(_attention_fwd, _attention_bwd)


def _chunk_order(step, n_chunks, n_ctx_chunks, reverse):
    if not reverse:
        return step
    return jnp.where(step < n_ctx_chunks, n_ctx_chunks - 1 - step, n_chunks + n_ctx_chunks - 1 - step)


def _scan_masks(chunk, reverse):
    row = lax.broadcasted_iota(jnp.int32, (chunk, chunk), 0)
    col = lax.broadcasted_iota(jnp.int32, (chunk, chunk), 1)
    vis = (col >= row) if reverse else (col <= row)
    vis_t = (row >= col) if reverse else (row <= col)
    return vis, vis.astype(F32), vis_t.astype(F32)


def _cum_decay(a_col, a_row, vis_f):
    hi = lax.Precision.HIGHEST
    cum_col = jnp.dot(vis_f, a_col, precision=hi, preferred_element_type=F32)
    cum_row = lax.dot_general(a_row, vis_f, (((1,), (1,)), ((), ())), precision=hi, preferred_element_type=F32)
    total = jnp.sum(a_col, axis=0, keepdims=True)
    return cum_col, cum_row, total


SCAN_CHUNKS_PER_STEP = 2


def _scan_specs(t, ctx_len, reverse, backward, widths):
    rows = SCAN_CHUNK * SCAN_CHUNKS_PER_STEP
    assert t % rows == 0 and ctx_len % rows == 0
    n_blocks, n_ctx_blocks = t // rows, ctx_len // rows

    def order(i):
        step = (n_blocks - 1 - i) if backward else i
        return _chunk_order(step, n_blocks, n_ctx_blocks, reverse)

    halves = list(range(SCAN_CHUNKS_PER_STEP))
    if reverse != backward:
        halves.reverse()
    return [pl.BlockSpec((rows, w), lambda i: (order(i), 0)) for w in widths], order, n_blocks, halves


def _scan_fwd_call(q, k, v, a_col, a_row, *, groups, per_group, dk, dv, ctx_len, reverse):
    t = q.shape[0]
    chunk, per_step = SCAN_CHUNK, SCAN_CHUNKS_PER_STEP
    heads = groups * per_group
    (q_spec, k_spec, v_spec, acol_spec), order, n_blocks, halves = _scan_specs(
        t, ctx_len, reverse, False, (groups * dk, groups * dk, heads * dv, LANES))

    def body(q_ref, k_ref, v_ref, acol_ref, arow_ref, y_ref, st_ref, s_sc):
        @pl.when(pl.program_id(0) == 0)
        def _():
            s_sc[...] = jnp.zeros(s_sc.shape, F32)

        vis, vis_f, _ = _scan_masks(chunk, reverse)
        for half in halves:
            rows = slice(half * chunk, (half + 1) * chunk)
            st_ref[half] = s_sc[...]
            cum_col, cum_row, total = _cum_decay(acol_ref[rows, :], arow_ref[:, rows], vis_f)
            for g in range(groups):
                qg = q_ref[rows, g * dk:(g + 1) * dk]
                kg = k_ref[rows, g * dk:(g + 1) * dk]
                qk = _dot_nt(qg, kg)
                for r in range(per_group):
                    h = g * per_group + r
                    ccol = cum_col[:, h:h + 1]
                    decay = jnp.exp(jnp.where(vis, ccol - cum_row[h:h + 1, :], NEG_BIG))
                    vh = v_ref[rows, h * dv:(h + 1) * dv]
                    s_in = s_sc[h]
                    y = _dot(qk * decay, vh) + jnp.exp(ccol) * _dot(qg, s_in)
                    y_ref[rows, h * dv:(h + 1) * dv] = y
                    tot = total[:, h:h + 1]
                    s_sc[h] = jnp.exp(tot) * s_in + _dot_tn(kg * jnp.exp(tot - ccol), vh)

    return pl.pallas_call(
        body, name="scan_fwd",
        grid=(n_blocks,),
        in_specs=[q_spec, k_spec, v_spec, acol_spec, pl.BlockSpec((8, chunk * per_step), lambda i: (0, order(i)))],
        out_specs=[v_spec, pl.BlockSpec((per_step, heads, dk, dv), lambda i: (order(i), 0, 0, 0))],
        out_shape=[jax.ShapeDtypeStruct(v.shape, F32),
                   jax.ShapeDtypeStruct((n_blocks * per_step, heads, dk, dv), F32)],
        scratch_shapes=[pltpu.VMEM((heads, dk, dv), F32)],
        compiler_params=_params("arbitrary"),
    )(q, k, v, a_col, a_row)


def _scan_bwd_call(q, k, v, a_col, a_row, states, dy, *, groups, per_group, dk, dv, ctx_len, reverse):
    t = q.shape[0]
    chunk, per_step = SCAN_CHUNK, SCAN_CHUNKS_PER_STEP
    heads = groups * per_group
    (q_spec, k_spec, v_spec, acol_spec), order, n_blocks, halves = _scan_specs(
        t, ctx_len, reverse, True, (groups * dk, groups * dk, heads * dv, LANES))
    arow_spec = pl.BlockSpec((8, chunk * per_step), lambda i: (0, order(i)))
    last = 0 if reverse else chunk - 1

    def body(q_ref, k_ref, v_ref, acol_ref, arow_ref, st_ref, dy_ref, dq_ref, dk_ref, dv_ref, da_ref, dat_ref,
             ds_sc):
        @pl.when(pl.program_id(0) == 0)
        def _():
            ds_sc[...] = jnp.zeros(ds_sc.shape, F32)

        vis, vis_f, vis_tf = _scan_masks(chunk, reverse)
        lane = lax.broadcasted_iota(jnp.int32, (chunk, LANES), 1)
        row = lax.broadcasted_iota(jnp.int32, (chunk, LANES), 0)
        sub = lax.broadcasted_iota(jnp.int32, (8, chunk), 0)
        for half in halves:
            rows = slice(half * chunk, (half + 1) * chunk)
            cum_col, cum_row, total = _cum_decay(acol_ref[rows, :], arow_ref[:, rows], vis_f)
            dcum = jnp.zeros((chunk, LANES), F32)
            dcum_t = jnp.zeros((8, chunk), F32)
            for g in range(groups):
                qg = q_ref[rows, g * dk:(g + 1) * dk]
                kg = k_ref[rows, g * dk:(g + 1) * dk]
                qk = _dot_nt(qg, kg)
                dq_g = jnp.zeros((chunk, dk), F32)
                dk_g = jnp.zeros((chunk, dk), F32)
                for r in range(per_group):
                    h = g * per_group + r
                    ccol = cum_col[:, h:h + 1]
                    decay = jnp.exp(jnp.where(vis, ccol - cum_row[h:h + 1, :], NEG_BIG))
                    vh = v_ref[rows, h * dv:(h + 1) * dv]
                    dyh = dy_ref[rows, h * dv:(h + 1) * dv]
                    s_in = st_ref[half, h]
                    ds_out = ds_sc[h]
                    tot = total[:, h:h + 1]
                    e_in = jnp.exp(ccol)
                    e_out = jnp.exp(tot - ccol)
                    e_tot = jnp.exp(tot)
                    k_out = kg * e_out
                    dv_ref[rows, h * dv:(h + 1) * dv] = _dot_tn(qk * decay, dyh) + _dot(k_out, ds_out)
                    dqk = _dot_nt(dyh, vh) * decay
                    dq_in = e_in * _dot_nt(dyh, s_in)
                    dk_out = e_out * _dot_nt(vh, ds_out)
                    dq_h = _dot(dqk, kg) + dq_in
                    dk_h = _dot_tn(dqk, qg) + dk_out
                    s_out = e_tot * s_in + _dot_tn(k_out, vh)
                    edge = jnp.sum(jnp.sum(s_out * ds_out, axis=1, keepdims=True), axis=0, keepdims=True)
                    w_seg = dqk * qk
                    dcum_h = (jnp.sum(w_seg, axis=1, keepdims=True) + jnp.sum(dq_in * qg, axis=1, keepdims=True)
                              - jnp.sum(dk_out * kg, axis=1, keepdims=True))
                    dcum = jnp.where(lane == h, dcum_h + jnp.where(row == last, edge, 0.0), dcum)
                    dcum_t = jnp.where(sub == h, -jnp.sum(w_seg, axis=0, keepdims=True), dcum_t)
                    ds_sc[h] = e_tot * ds_out + _dot_tn(qg, e_in * dyh)
                    dq_g = dq_g + dq_h
                    dk_g = dk_g + dk_h
                dq_ref[rows, g * dk:(g + 1) * dk] = dq_g
                dk_ref[rows, g * dk:(g + 1) * dk] = dk_g
            hi = lax.Precision.HIGHEST
            da_ref[rows, :] = jnp.dot(vis_tf, dcum, precision=hi, preferred_element_type=F32)
            dat_ref[:, rows] = jnp.dot(dcum_t, vis_f, precision=hi, preferred_element_type=F32)

    return pl.pallas_call(
        body, name="scan_bwd",
        grid=(n_blocks,),
        in_specs=[q_spec, k_spec, v_spec, acol_spec, arow_spec,
                  pl.BlockSpec((per_step, heads, dk, dv), lambda i: (order(i), 0, 0, 0)), v_spec],
        out_specs=[q_spec, k_spec, v_spec, acol_spec, arow_spec],
        out_shape=[jax.ShapeDtypeStruct(q.shape, F32), jax.ShapeDtypeStruct(k.shape, F32),
                   jax.ShapeDtypeStruct(v.shape, F32), jax.ShapeDtypeStruct((t, LANES), F32),
                   jax.ShapeDtypeStruct((8, t), F32)],
        scratch_shapes=[pltpu.VMEM((heads, dk, dv), F32)],
        compiler_params=_params("arbitrary"),
    )(q, k, v, a_col, a_row, states, dy)


def _decay_layouts(a):
    t, heads = a.shape
    a_col = jnp.pad(a, ((0, 0), (0, LANES - heads)))
    a_row = jnp.pad(a.T, ((0, 8 - heads), (0, 0)))
    return a_col, a_row


@functools.partial(jax.custom_vjp, nondiff_argnums=(4,))
def linear_scan(q, k, v, a, cfg):
    a_col, a_row = _decay_layouts(a)
    return _scan_fwd_call(q, k, v, a_col, a_row, **dict(cfg))[0]


def _linear_scan_fwd(q, k, v, a, cfg):
    a_col, a_row = _decay_layouts(a)
    y, states = _scan_fwd_call(q, k, v, a_col, a_row, **dict(cfg))
    return y, (q, k, v, a, states)


def _linear_scan_bwd(cfg, res, dy):
    q, k, v, a, states = res
    a_col, a_row = _decay_layouts(a)
    dq, dk, dv, da, da_t = _scan_bwd_call(q, k, v, a_col, a_row, states, dy, **dict(cfg))
    heads = a.shape[1]
    return dq, dk, dv, da[:, :heads] + da_t[:heads].T


linear_scan.defvjp(_linear_scan_fwd, _linear_scan_bwd)


def _scan_cfg(groups, per_group, dk, dv, ctx_len, reverse):
    return (("groups", groups), ("per_group", per_group), ("dk", dk), ("dv", dv), ("ctx_len", ctx_len),
            ("reverse", reverse))


def _axial_tables(n_lat, n_ctx):
    freqs = ATTN_HEAD_DIM // 4
    rows = n_lat // GRID_W
    row = jnp.repeat(jnp.arange(rows, dtype=F32), GRID_W)
    col = jnp.tile(jnp.arange(GRID_W, dtype=F32), rows)
    inv = ROPE_THETA ** (-jnp.arange(freqs, dtype=F32) / freqs)
    ang = jnp.concatenate([row[:, None] * inv, col[:, None] * inv], axis=-1)
    cos = jnp.concatenate([jnp.ones((n_ctx, 2 * freqs), F32), jnp.cos(ang)], axis=0)
    sin = jnp.concatenate([jnp.zeros((n_ctx, 2 * freqs), F32), jnp.sin(ang)], axis=0)
    return cos, sin


def _seq_tables(t):
    pos = jnp.arange(t, dtype=F32)
    inv = ROPE_THETA ** (-jnp.linspace(0.0, 1.0, RET_DK // 2, dtype=F32))
    ang = pos[:, None] * inv
    return jnp.cos(ang), jnp.sin(ang)


def _pad_w_in(w_in):
    d = w_in.shape[0]
    return jnp.concatenate([w_in[:, :DT_END], jnp.zeros((d, DT_PAD), w_in.dtype), w_in[:, DT_END:],
                            jnp.zeros((d, TAIL_PAD), w_in.dtype)], axis=1)


def _split_proj(p):
    widths = list(IN_SPLITS)
    widths[5] = LANES
    out, off = [], 0
    for w in widths:
        out.append(p[:, off:off + w])
        off += w
    out[5] = out[5][:, :DT_COLS]
    return out


def _mixer(u, w, wq, layer, n_ctx, tables):
    t = u.shape[0]
    attn_rope, ret_rope, seg_first, seg_last = tables
    proj = matmul(u, wq["w_in"][layer], _pad_w_in(w["w_in"][layer]))
    aq, ak, av, z, xbc_raw, dt_raw, rq, rk, rv, rg, gate_logits = _split_proj(proj)

    q_width = ATTN_HEADS * ATTN_HEAD_DIM
    qk = _rowwise("qk_prep", _qk_prep_tile, q_width + ATTN_KV_HEADS * ATTN_HEAD_DIM, n_ctx)(
        (w["attn_q_norm"][layer][None, :], w["attn_k_norm"][layer][None, :]), (aq, ak) + attn_rope)
    q4 = qk[:, :q_width].reshape(t, ATTN_KV_HEADS, ATTN_GROUP, ATTN_HEAD_DIM).transpose(1, 2, 0, 3)
    k3 = qk[:, q_width:].reshape(t, ATTN_KV_HEADS, ATTN_HEAD_DIM).transpose(1, 0, 2)
    v3 = av.reshape(t, ATTN_KV_HEADS, ATTN_HEAD_DIM).transpose(1, 0, 2)
    o4 = attention(q4, k3, v3, n_ctx)
    br_attn = o4.transpose(2, 0, 1, 3).reshape(t, ATTN_HEADS * ATTN_HEAD_DIM)

    cw, cb = w["ssd_conv_w"][layer], w["ssd_conv_b"][layer]
    zero_row = jnp.zeros((1, xbc_raw.shape[1]), F32)
    prev = jnp.concatenate([zero_row, xbc_raw[:-1]], axis=0) * (1.0 - seg_first)
    nxt = jnp.concatenate([xbc_raw[1:], zero_row], axis=0) * (1.0 - seg_last)
    n_dt = 2 * SSD_HEADS
    a_neg = -jnp.exp(w["ssd_a_log"][layer]).reshape(n_dt)
    a_diag = jnp.where(jnp.arange(n_dt)[:, None] == jnp.arange(LANES)[None, :], a_neg[:, None], 0.0)
    head_of_lane = jnp.arange(SSD_D_INNER) // SSD_HEAD_DIM
    expand = [(jnp.arange(n_dt)[:, None] == head_of_lane[None, :] + d * SSD_HEADS).astype(F32) for d in range(2)]
    xs, bm, cm, v_fwd, v_bwd, decays = _rowwise(
        "ssd_prep", _ssd_prep_tile, (SSD_D_INNER, SSD_GROUPS * SSD_STATE, SSD_GROUPS * SSD_STATE, SSD_D_INNER,
                                     SSD_D_INNER, LANES), n_ctx)(
        (cw[0:1], cw[1:2], cw[2:3], cb[None, :], w["ssd_dt_bias"][layer].reshape(1, n_dt), a_diag, expand[0],
         expand[1]), (xbc_raw, prev, nxt, dt_raw))
    y_ssd = []
    for d, (reverse, v_d) in enumerate(((False, v_fwd), (True, v_bwd))):
        cfg = _scan_cfg(SSD_GROUPS, SSD_HEADS // SSD_GROUPS, SSD_STATE, SSD_HEAD_DIM, n_ctx, reverse)
        y_ssd.append(linear_scan(cm, bm, v_d, decays[:, d * SSD_HEADS:(d + 1) * SSD_HEADS], cfg))
    d_skip = jnp.repeat(w["ssd_d"][layer], SSD_HEAD_DIM)[None, :]
    br_ssd = _rowwise("ssd_finish", _ssd_finish_tile, SSD_D_INNER, n_ctx)(
        (d_skip, w["ssd_norm_w"][layer][None, :]), (y_ssd[0], y_ssd[1], xs, z))

    lg = -jnp.exp(w["ret_log_decay"][layer])
    ret_width = RET_HEADS * RET_DK
    ret_qk = _rowwise("ret_prep", _ret_prep_tile, 2 * ret_width, n_ctx)((), (rq, rk) + ret_rope)
    rq_r, rk_r = ret_qk[:, :ret_width], ret_qk[:, ret_width:]
    y_ret = []
    for d, reverse in ((0, False), (1, True)):
        cfg = _scan_cfg(RET_HEADS, 1, RET_DK, RET_DV, n_ctx, reverse)
        y_ret.append(linear_scan(rq_r, rk_r, rv, jnp.broadcast_to(lg[d][None, :], (t, RET_HEADS)), cfg))
    br_ret = _rowwise("ret_finish", _ret_finish_tile, RET_HEADS * RET_DV, n_ctx)(
        (w["ret_gn_w"][layer][None, :],), (y_ret[0], y_ret[1], rg))

    projected = tuple(matmul(br, wq["w_branch"][layer][j], w["w_branch"][layer][j])
                      for j, br in enumerate((br_attn, br_ssd, br_ret)))
    merged = _rowwise("gate_merge", _gate_merge_tile, D_MODEL, n_ctx)((), projected + (gate_logits,))
    return matmul(merged, wq["w_out"][layer], w["w_out"][layer])


def _local_loss(w, x, c, ctx, target, wq):
    n, m = x.shape[0], ctx.shape[0]
    t = n + m
    pos = jnp.arange(t)[:, None]
    seg_first = ((pos == 0) | (pos == m)).astype(F32)
    seg_last = ((pos == m - 1) | (pos == t - 1)).astype(F32)
    tables = (_axial_tables(n, m), _seq_tables(t), seg_first, seg_last)
    h = jnp.concatenate([ctx, x], axis=0)
    cond = jax.nn.silu(jnp.stack([c, w["c_ctx"]], axis=0))
    cond8 = jnp.concatenate([cond, jnp.zeros((6, D_MODEL), F32)], axis=0)
    for layer in range(DEPTH):
        mod = (matmul(cond8, wq["w_mod"][layer], w["w_mod"][layer])[:2] + w["b_mod"][layer]).reshape(2, 6, D_MODEL)

        u = norm_mod(h, w["norm1_w"][layer], mod[:, 0], mod[:, 1], m)
        residual = _rowwise("gated_residual", _gated_residual_tile, D_MODEL, m)
        h = residual((mod[:, 2],), (h, _mixer(u, w, wq, layer, m, tables)))
        v = norm_mod(h, w["norm2_w"][layer], mod[:, 3], mod[:, 4], m)
        mlp = sq_relu_mlp(v, wq["w_mlp1"][layer], w["w_mlp1"][layer], wq["w_mlp2"][layer], w["w_mlp2"][layer])
        residual = _rowwise("gated_residual", _gated_residual_tile, D_MODEL, m)
        h = residual((mod[:, 5],), (h, mlp))
    per_token = _rowwise("loss_rows", _loss_rows_tile, LANES, 0)((w["final_norm_w"][None, :],), (h[m:], target))
    return 0.5 * jnp.sum(per_token[:, 0])


def _coords():
    return lax.axis_index("x"), lax.axis_index("y"), lax.axis_index("c")


def _all_gather(blocks, name):
    n = len(blocks)

    def body(*refs):
        x_refs, out_refs = refs[:n], refs[n:2 * n]
        send_sems, recv_sems, local_sems = refs[2 * n:]
        x, y, c = _coords()
        me, sibling = (x, y, c), (x, y, 1 - c)
        chips = [(1 - x, y), (x, 1 - y), (1 - x, 1 - y)]

        def copy(k, i, blk, to, from_input=False):
            slot = out_refs[i].at[4 * blk[0] + 2 * blk[1] + blk[2]]
            return pltpu.make_async_remote_copy(
                src_ref=x_refs[i] if from_input else slot, dst_ref=slot,
                send_sem=send_sems.at[k * n + i], recv_sem=recv_sems.at[k * n + i],
                device_id=to, device_id_type=pl.DeviceIdType.MESH)

        mine = [pltpu.make_async_copy(x_refs[i], out_refs[i].at[4 * x + 2 * y + c], local_sems.at[i])
                for i in range(n)]
        for cp in mine:
            cp.start()
        first = [copy(0, i, me, sibling, True) for i in range(n)]
        first += [copy(1 + j, i, me, (*chip, c), True) for j, chip in enumerate(chips) for i in range(n)]
        for cp in first:
            cp.start()
        passed = []
        for j, chip in enumerate(chips):
            for i in range(n):
                copy(1 + j, i, (*chip, c), me).wait_recv()
                passed.append(copy(4 + j, i, (*chip, c), sibling))
                passed[-1].start()
        for i in range(n):
            copy(0, i, sibling, me).wait_recv()
        for j, chip in enumerate(chips):
            for i in range(n):
                copy(4 + j, i, (*chip, 1 - c), me).wait_recv()
        for cp in first + passed:
            cp.wait_send()
        for cp in mine:
            cp.wait()

    return pl.pallas_call(
        body, name=name,
        out_shape=[jax.ShapeDtypeStruct((N_DEV,) + b.shape, b.dtype) for b in blocks],
        in_specs=[pl.BlockSpec(memory_space=pl.ANY)] * n,
        out_specs=[pl.BlockSpec(memory_space=pl.ANY)] * n,
        scratch_shapes=[pltpu.SemaphoreType.DMA((7 * n,)), pltpu.SemaphoreType.DMA((7 * n,)),
                        pltpu.SemaphoreType.DMA((n,))],
    )(*blocks)


def _pair_exchange(arrays, name):
    n = len(arrays)

    def body(*refs):
        g_refs, out_refs = refs[:n], refs[n:2 * n]
        send_sems, recv_sems = refs[2 * n:]
        x, y, c = _coords()
        copies = [pltpu.make_async_remote_copy(
            src_ref=g_refs[i].at[1 - c], dst_ref=out_refs[i], send_sem=send_sems.at[i], recv_sem=recv_sems.at[i],
            device_id=(x, y, 1 - c), device_id_type=pl.DeviceIdType.MESH) for i in range(n)]
        for cp in copies:
            cp.start()
        for cp in copies:
            cp.wait()

    return pl.pallas_call(
        body, name=name,
        out_shape=[jax.ShapeDtypeStruct(a.shape[1:], a.dtype) for a in arrays],
        in_specs=[pl.BlockSpec(memory_space=pl.ANY)] * n,
        out_specs=[pl.BlockSpec(memory_space=pl.ANY)] * n,
        scratch_shapes=[pltpu.SemaphoreType.DMA((n,)), pltpu.SemaphoreType.DMA((n,))],
    )(*arrays)


def _chip_exchange(arrays, name):
    n = len(arrays)

    def body(*refs):
        g_refs, out_refs = refs[:n], refs[n:2 * n]
        send_sems, recv_sems, local_sems = refs[2 * n:]
        x, y, c = _coords()
        me = 2 * x + y
        mine = [pltpu.make_async_copy(g_refs[i].at[me], out_refs[i].at[me], local_sems.at[i]) for i in range(n)]
        for cp in mine:
            cp.start()
        copies = []
        for k in range(1, 4):
            px, py = (1 - x if (k >> 1) & 1 else x), (1 - y if k & 1 else y)
            for i in range(n):
                copies.append(pltpu.make_async_remote_copy(
                    src_ref=g_refs[i].at[2 * px + py], dst_ref=out_refs[i].at[me],
                    send_sem=send_sems.at[(k - 1) * n + i], recv_sem=recv_sems.at[(k - 1) * n + i],
                    device_id=(px, py, c), device_id_type=pl.DeviceIdType.MESH))
        for cp in copies:
            cp.start()
        for cp in copies:
            cp.wait_recv()
        for cp in copies:
            cp.wait_send()
        for cp in mine:
            cp.wait()

    return pl.pallas_call(
        body, name=name,
        out_shape=[jax.ShapeDtypeStruct(a.shape, a.dtype) for a in arrays],
        in_specs=[pl.BlockSpec(memory_space=pl.ANY)] * n,
        out_specs=[pl.BlockSpec(memory_space=pl.ANY)] * n,
        scratch_shapes=[pltpu.SemaphoreType.DMA((3 * n,)), pltpu.SemaphoreType.DMA((3 * n,)),
                        pltpu.SemaphoreType.DMA((n,))],
    )(*arrays)


def _add_pair(a, b, name):
    rows, cols = a.shape
    tr = _tile(rows, [r for r in (2048, 1024, 512, 256, 128, 64, 32, 16) if r * cols <= 2 * ADAMW_BLOCK_ELEMS])

    def body(a_ref, b_ref, o_ref):
        o_ref[...] = (a_ref[...].astype(F32) + b_ref[...].astype(F32)).astype(o_ref.dtype)

    spec = pl.BlockSpec((tr, cols), lambda i: (i, 0))
    return pl.pallas_call(
        body, name=name, grid=(rows // tr,), in_specs=[spec, spec], out_specs=spec,
        out_shape=jax.ShapeDtypeStruct(a.shape, a.dtype), compiler_params=_params("parallel"),
    )(a, b)


ADAMW_BLOCK_ELEMS = 256 * 1024


def _sum_adamw(g8, w, m, v, name):
    rows, cols = w.shape
    tr = _tile(rows, [r for r in (2048, 1024, 512, 256, 128, 64, 32, 16) if r * cols <= ADAMW_BLOCK_ELEMS])

    def body(g_ref, w_ref, m_ref, v_ref, go_ref, d_ref, mo_ref, vo_ref):
        g = g_ref[0].astype(F32)
        for s in range(1, g8.shape[0]):
            g = g + g_ref[s].astype(F32)
        m_new = ADAM_B1 * m_ref[...] + (1.0 - ADAM_B1) * g
        v_new = ADAM_B2 * v_ref[...] + (1.0 - ADAM_B2) * (g * g)
        m_hat = m_new / (1.0 - ADAM_B1 ** ADAM_STEP)
        v_hat = v_new / (1.0 - ADAM_B2 ** ADAM_STEP)
        go_ref[...] = g
        d_ref[...] = -ADAM_LR * (m_hat / (jnp.sqrt(v_hat) + ADAM_EPS) + ADAM_WD * w_ref[...])
        mo_ref[...] = m_new
        vo_ref[...] = v_new

    spec = pl.BlockSpec((tr, cols), lambda i: (i, 0))
    shape = jax.ShapeDtypeStruct((rows, cols), F32)
    return pl.pallas_call(
        body, name=name,
        grid=(rows // tr,),
        in_specs=[pl.BlockSpec((g8.shape[0], tr, cols), lambda i: (0, i, 0)), spec, spec, spec],
        out_specs=[spec, spec, spec, spec],
        out_shape=[shape, shape, shape, shape],
        compiler_params=_params("parallel"),
    )(g8, w, m, v)


BIG = ("w_mod", "w_in", "w_branch", "w_out", "w_mlp1", "w_mlp2")
COL_SHARDED = ("w_mod", "w_mlp1")
ROW_SHARDED = ("w_out", "w_mlp2")
SMALL = ("c_ctx", "b_mod", "norm1_w", "norm2_w", "attn_q_norm", "attn_k_norm", "ssd_conv_b", "ssd_dt_bias",
         "ssd_a_log", "ssd_d", "ssd_norm_w", "ret_log_decay", "ret_gn_w", "final_norm_w")
CONV_AXIS = 2
ORDER = ("c_ctx", "w_mod", "b_mod", "norm1_w", "norm2_w", "w_in", "attn_q_norm", "attn_k_norm", "ssd_conv_w",
         "ssd_conv_b", "ssd_dt_bias", "ssd_a_log", "ssd_d", "ssd_norm_w", "ret_log_decay", "ret_gn_w", "w_branch",
         "w_out", "w_mlp1", "w_mlp2", "final_norm_w")


def _compute_weights(gathered):
    wq, carrier = {}, {}
    for name in BIG:
        g = gathered[name]
        per_layer = []
        for layer in range(DEPTH):
            gl = g[:, layer]
            if name in COL_SHARDED:
                per_layer.append(gl)
            elif name in ROW_SHARDED:
                per_layer.append(gl.reshape(-1, gl.shape[-1]))
            elif name == "w_in":
                per_layer.append(_pad_w_in(jnp.concatenate([gl[d] for d in range(N_DEV)], axis=-1)))
            else:
                per_layer.append([jnp.concatenate([gl[d, j] for d in range(N_DEV)], axis=-1) for j in range(N_BRANCH)])
        wq[name] = per_layer
    for name in BIG:
        if name == "w_in":
            carrier[name] = [jnp.zeros((D_MODEL, IN_DIM), F32) for _ in range(DEPTH)]
        else:
            carrier[name] = jax.tree.map(lambda a: jnp.zeros(a.shape, F32), wq[name])
    return wq, carrier


def _grad_shards(gw):
    out = {}
    for name in BIG:
        per_layer = []
        for layer in range(DEPTH):
            g = gw[name][layer]
            if name in COL_SHARDED:
                per_layer.append(g)
            elif name in ROW_SHARDED:
                per_layer.append(g.reshape(N_DEV, -1, g.shape[-1]))
            elif name == "w_in":
                size = IN_DIM // N_DEV
                per_layer.append(jnp.stack([g[:, d * size:(d + 1) * size] for d in range(N_DEV)]))
            else:
                per_layer.append(jnp.stack([gj.reshape(gj.shape[0], N_DEV, -1).transpose(1, 0, 2) for gj in g], axis=1))
        out[name] = jnp.stack(per_layer, axis=1)
    return out


def _pack(arrays, row_multiple):
    flat = jnp.concatenate(arrays, axis=-1)
    n = flat.shape[-1]
    per = LANES * row_multiple
    padded = -(-n // per) * per
    flat = jnp.pad(flat, [(0, 0)] * (flat.ndim - 1) + [(0, padded - n)])
    return flat.reshape(flat.shape[:-1] + (padded // LANES, LANES))


def _unpack(slab, shapes):
    flat = slab.reshape(slab.shape[:-2] + (-1,))
    out, off = [], 0
    for shp in shapes:
        size = math.prod(shp)
        out.append(flat[..., off:off + size].reshape(flat.shape[:-1] + tuple(shp)))
        off += size
    return out


def kernel(x, c, ctx, c_ctx, w_mod, b_mod, norm1_w, norm2_w, w_in, attn_q_norm, attn_k_norm, ssd_conv_w, ssd_conv_b, ssd_dt_bias, ssd_a_log, ssd_d, ssd_norm_w, ret_log_decay, ret_gn_w, w_branch, w_out, w_mlp1, w_mlp2, final_norm_w, loss_target, m_c_ctx, m_w_mod, m_b_mod, m_norm1_w, m_norm2_w, m_w_in, m_attn_q_norm, m_attn_k_norm, m_ssd_conv_w, m_ssd_conv_b, m_ssd_dt_bias, m_ssd_a_log, m_ssd_d, m_ssd_norm_w, m_ret_log_decay, m_ret_gn_w, m_w_branch, m_w_out, m_w_mlp1, m_w_mlp2, m_final_norm_w, v_c_ctx, v_w_mod, v_b_mod, v_norm1_w, v_norm2_w, v_w_in, v_attn_q_norm, v_attn_k_norm, v_ssd_conv_w, v_ssd_conv_b, v_ssd_dt_bias, v_ssd_a_log, v_ssd_d, v_ssd_norm_w, v_ret_log_decay, v_ret_gn_w, v_w_branch, v_w_out, v_w_mlp1, v_w_mlp2, v_final_norm_w):
    args = dict(locals())
    weights = {n: args[n] for n in ORDER}
    mom1 = {n: args["m_" + n] for n in ORDER}
    mom2 = {n: args["v_" + n] for n in ORDER}
    me = 4 * lax.axis_index("x") + 2 * lax.axis_index("y") + lax.axis_index("c")

    gathered = _all_gather([weights[n].astype(MXU_DTYPE) for n in BIG], "gather_weights")
    wq, params = _compute_weights(dict(zip(BIG, gathered)))
    conv_shape = ssd_conv_w.shape
    conv_all = _all_gather([_pack([ssd_conv_w.reshape(-1)], 8)], "gather_conv")[0]
    params["ssd_conv_w"] = jnp.concatenate(list(_unpack(conv_all, [conv_shape])[0]), axis=CONV_AXIS)
    for n in SMALL:
        params[n] = weights[n]

    loss, (gw, gx) = jax.value_and_grad(_local_loss, argnums=(0, 1))(params, x[0], c[0], ctx[0], loss_target[0], wq)
    loss = lax.psum(loss, MESH_AXES)

    g_send = _grad_shards(gw)
    by_core = [g_send[n].reshape((4, 2) + g_send[n].shape[1:]).swapaxes(0, 1).astype(jnp.bfloat16) for n in BIG]
    from_sibling = _pair_exchange(by_core, "scatter_grads_d2d")
    my_core = lax.axis_index("c")
    chip_sums = []
    for n, mine, theirs in zip(BIG, by_core, from_sibling):
        kept = lax.dynamic_index_in_dim(mine, my_core, 0, keepdims=False)
        chip_sums.append(_add_pair(kept.reshape(-1, kept.shape[-1]), theirs.reshape(-1, theirs.shape[-1]),
                                   "chip_sum_" + n).reshape(theirs.shape))
    g_recv = _chip_exchange(chip_sums, "scatter_grads_ici")
    result = {}
    for n, g8 in zip(BIG, g_recv):
        shape = weights[n].shape
        as2d = lambda a: a.reshape(-1, shape[-1])
        outs = _sum_adamw(g8.reshape(4, -1, shape[-1]), as2d(weights[n]), as2d(mom1[n]), as2d(mom2[n]), "adamw_" + n)
        for kind, arr in zip(("grad", "delta", "new_m", "new_v"), outs):
            result[kind, n] = arr.reshape(shape)

    conv_full_shape = params["ssd_conv_w"].shape
    small_shapes = [weights[n].shape for n in SMALL]
    partial = _pack([gw[n].reshape(-1) for n in SMALL] + [gw["ssd_conv_w"].reshape(-1)], 8)
    parts = _unpack(_all_gather([partial], "gather_small_grads")[0], small_shapes + [conv_full_shape])
    conv_part = lax.dynamic_slice_in_dim(parts[-1], me * conv_shape[CONV_AXIS], conv_shape[CONV_AXIS], CONV_AXIS + 1)
    small_names = list(SMALL) + ["ssd_conv_w"]
    g8_small = _pack([p.reshape(N_DEV, -1) for p in parts[:-1]] + [conv_part.reshape(N_DEV, -1)], 8)
    slabs = [_pack([d[n].reshape(-1) for n in small_names], 8) for d in (weights, mom1, mom2)]
    small_out = [_unpack(s, small_shapes + [conv_shape]) for s in _sum_adamw(g8_small, *slabs, "adamw_small")]
    for kind, small_k in zip(("grad", "delta", "new_m", "new_v"), small_out):
        for n, arr in zip(small_names, small_k):
            result[kind, n] = arr

    outs = [loss, gx[None]]
    for kind in ("grad", "delta", "new_m", "new_v"):
        outs += [result[kind, n] for n in ORDER]
    return tuple(outs)
```

```python
import functools
import math

import jax
import jax.numpy as jnp
from jax import lax
from jax.experimental import pallas as pl
from jax.experimental.pallas import tpu as pltpu

F32 = jnp.float32
MXU_DTYPE = jnp.bfloat16
VMEM_LIMIT_BYTES = 48 * 1024 * 1024
LANES = 128
N_DEV = 8
MESH_AXES = ("x", "y", "c")

D_MODEL = 1024
GRID_W = 64
NORM_EPS = 1e-6
ROPE_THETA = 10000.0
ATTN_HEADS, ATTN_KV_HEADS, ATTN_HEAD_DIM = 8, 2, 64
ATTN_GROUP = ATTN_HEADS // ATTN_KV_HEADS
SSD_HEADS, SSD_HEAD_DIM, SSD_GROUPS, SSD_STATE = 8, 64, 2, 128
SSD_D_INNER = SSD_HEADS * SSD_HEAD_DIM
RET_HEADS, RET_DK, RET_DV = 4, 128, 128
SCAN_CHUNK = 128
N_BRANCH = 3
DEPTH = 2

IN_SPLITS = (512, 128, 128, 512, 1024, 16, 512, 512, 512, 512, 3072)
IN_DIM = sum(IN_SPLITS)
DT_COLS = 16
DT_PAD = LANES - DT_COLS
TAIL_PAD = 128
IN_DIM_PADDED = IN_DIM + DT_PAD + TAIL_PAD
DT_END = sum(IN_SPLITS[:6])

ADAM_LR, ADAM_B1, ADAM_B2, ADAM_EPS, ADAM_WD, ADAM_STEP = 0.001, 0.9, 0.999, 1e-08, 0.01, 10


def _tile(dim, prefs):
    for p in prefs:
        if dim % p == 0:
            return p
    return dim


def _params(*sem):
    return pltpu.CompilerParams(dimension_semantics=sem, vmem_limit_bytes=VMEM_LIMIT_BYTES)


def _mm(a, b, *, ta=False, tb=False, out_shards=False, epilogue=None, extra=None, out_dtype=F32, name):
    if ta:
        kdim, m = a.shape
    else:
        m, kdim = a.shape
    b_shards = b.ndim == 3
    if b_shards:
        rows_b, cols_b = b.shape[1], N_DEV * b.shape[2]
    else:
        rows_b, cols_b = b.shape
    n, kdim_b = (rows_b, cols_b) if tb else (cols_b, rows_b)
    assert kdim == kdim_b, (a.shape, b.shape, ta, tb)
    tm = _tile(m, (1024, 768, 512, 256, 128))
    n_tile_of = n // N_DEV if (out_shards or (b_shards and not tb)) else n
    k_tile_of = kdim // N_DEV if (b_shards and tb) else kdim
    tn = _tile(n_tile_of, (1280, 1024, 768, 512, 384, 256, 128))
    tk = _tile(k_tile_of, (1024, 768, 512, 256, 128))
    dims = (((0 if ta else 1,), (1 if tb else 0,)), ((), ()))

    n_k = kdim // tk

    def body(a_ref, b_ref, *rest):
        o_ref = rest[-1]
        part = lax.dot_general(a_ref[...].astype(MXU_DTYPE), b_ref[...].astype(MXU_DTYPE), dims,
                               preferred_element_type=F32)

        def finish(acc):
            if epilogue == "sq_relu":
                acc = jnp.square(jnp.maximum(acc, 0.0))
            elif epilogue == "d_sq_relu":
                acc = acc * (2.0 * jnp.sqrt(rest[0][...].astype(F32)))
            return acc.astype(o_ref.dtype)

        if n_k == 1:
            o_ref[...] = finish(part)
            return

        @pl.when(pl.program_id(2) == 0)
        def _():
            o_ref[...] = part

        @pl.when(pl.program_id(2) > 0)
        def _():
            o_ref[...] += part

        if epilogue is not None:
            @pl.when(pl.program_id(2) == n_k - 1)
            def _():
                o_ref[...] = finish(o_ref[...])

    a_spec = pl.BlockSpec((tk, tm), lambda i, j, k: (k, i)) if ta else pl.BlockSpec((tm, tk), lambda i, j, k: (i, k))
    if not b_shards:
        b_spec = pl.BlockSpec((tn, tk), lambda i, j, k: (j, k)) if tb else pl.BlockSpec((tk, tn), lambda i, j, k: (k, j))
    elif tb:
        per = b.shape[2] // tk
        b_spec = pl.BlockSpec((None, tn, tk), lambda i, j, k: (k // per, j, k % per))
    else:
        per = b.shape[2] // tn
        b_spec = pl.BlockSpec((None, tk, tn), lambda i, j, k: (j // per, k, j % per))
    if out_shards:
        per_out = n // N_DEV // tn
        out_spec = pl.BlockSpec((None, tm, tn), lambda i, j, k: (j // per_out, i, j % per_out))
        out_shape = jax.ShapeDtypeStruct((N_DEV, m, n // N_DEV), out_dtype)
    else:
        out_spec = pl.BlockSpec((tm, tn), lambda i, j, k: (i, j))
        out_shape = jax.ShapeDtypeStruct((m, n), out_dtype)
    assert epilogue in (None, "sq_relu", "d_sq_relu") and (extra is not None) == (epilogue == "d_sq_relu")
    assert out_dtype == F32 or n_k == 1
    operands, in_specs = [a, b], [a_spec, b_spec]
    if extra is not None:
        assert not out_shards and extra.shape == (m, n)
        operands.append(extra)
        in_specs.append(out_spec)
    return pl.pallas_call(
        body, name=name,
        grid=(m // tm, n // tn, n_k),
        in_specs=in_specs,
        out_specs=out_spec,
        out_shape=out_shape,
        compiler_params=_params("parallel", "parallel", "arbitrary"),
    )(*operands)


@jax.custom_vjp
def matmul(a, b, b_grad):
    return _mm(a, b, name="mm_fwd")


def _matmul_fwd(a, b, b_grad):
    return _mm(a, b, name="mm_fwd"), (a, b)


def _matmul_bwd(res, g):
    a, b = res
    dw = _mm(a, g, ta=True, out_shards=b.ndim == 3, name="mm_dw")
    return _mm(g, b, tb=True, name="mm_dx"), jnp.zeros_like(b), dw


matmul.defvjp(_matmul_fwd, _matmul_bwd)


@jax.custom_vjp
def sq_relu_mlp(x, w1, w1_grad, w2, w2_grad):
    return _mm(_mm(x, w1, epilogue="sq_relu", out_dtype=MXU_DTYPE, name="mlp_up"), w2, name="mlp_down")


def _sq_relu_mlp_fwd(x, w1, w1_grad, w2, w2_grad):
    hid = _mm(x, w1, epilogue="sq_relu", out_dtype=MXU_DTYPE, name="mlp_up")
    return _mm(hid, w2, name="mlp_down"), (x, w1, w2, hid)


def _sq_relu_mlp_bwd(res, g):
    x, w1, w2, hid = res
    d_pre = _mm(g, w2, tb=True, epilogue="d_sq_relu", extra=hid, name="mlp_down_dx")
    dw2 = _mm(hid, g, ta=True, out_shards=w2.ndim == 3, name="mlp_down_dw")
    dw1 = _mm(x, d_pre, ta=True, out_shards=w1.ndim == 3, name="mlp_up_dw")
    return _mm(d_pre, w1, tb=True, name="mlp_up_dx"), jnp.zeros_like(w1), dw1, jnp.zeros_like(w2), dw2


sq_relu_mlp.defvjp(_sq_relu_mlp_fwd, _sq_relu_mlp_bwd)


NORM_TILE_PREFS = (768, 512, 256, 128)


def _norm_mod_pieces(h_ref, w_ref, shift_ref, scale_ref, tile, tm, ctx_len):
    x = h_ref[...]
    rstd = lax.rsqrt(jnp.mean(x * x, axis=1, keepdims=True) + NORM_EPS)
    xn = x * rstd
    is_ctx = tile * tm + lax.broadcasted_iota(jnp.int32, (tm, 1), 0) < ctx_len
    scale = jnp.where(is_ctx, scale_ref[1:2, :], scale_ref[0:1, :])
    shift = jnp.where(is_ctx, shift_ref[1:2, :], shift_ref[0:1, :])
    return xn, rstd, is_ctx, scale, shift


def _norm_mod_fwd_call(h, w, shift, scale, ctx_len):
    t, d = h.shape
    tm = _tile(t, NORM_TILE_PREFS)

    def body(h_ref, w_ref, shift_ref, scale_ref, u_ref):
        xn, _, _, sc, sh = _norm_mod_pieces(h_ref, w_ref, shift_ref, scale_ref, pl.program_id(0), tm, ctx_len)
        u_ref[...] = xn * w_ref[...] * (1.0 + sc) + sh

    row = pl.BlockSpec((tm, d), lambda i: (i, 0))
    return pl.pallas_call(
        body, name="norm_mod_fwd",
        grid=(t // tm,),
        in_specs=[row, pl.BlockSpec((1, d), lambda i: (0, 0)), pl.BlockSpec((2, d), lambda i: (0, 0)),
                  pl.BlockSpec((2, d), lambda i: (0, 0))],
        out_specs=row,
        out_shape=jax.ShapeDtypeStruct((t, d), F32),
        compiler_params=_params("parallel"),
    )(h, w, shift, scale)


def _norm_mod_bwd_call(h, w, shift, scale, du, ctx_len):
    t, d = h.shape
    tm = _tile(t, NORM_TILE_PREFS)

    def body(h_ref, w_ref, shift_ref, scale_ref, du_ref, dh_ref, sums_ref):
        xn, rstd, is_ctx, sc, _ = _norm_mod_pieces(h_ref, w_ref, shift_ref, scale_ref, pl.program_id(0), tm, ctx_len)
        du = du_ref[...]
        wv = w_ref[...]
        dy = du * (1.0 + sc)
        dxn = dy * wv
        dh_ref[...] = rstd * (dxn - xn * jnp.mean(dxn * xn, axis=1, keepdims=True))
        dsc = du * (xn * wv)

        def colsum(v):
            return jnp.sum(v, axis=0, keepdims=True)

        dshift_all, dshift_ctx = colsum(du), colsum(jnp.where(is_ctx, du, 0.0))
        dscale_all, dscale_ctx = colsum(dsc), colsum(jnp.where(is_ctx, dsc, 0.0))
        part = jnp.concatenate([colsum(dy * xn), dshift_all - dshift_ctx, dshift_ctx, dscale_all - dscale_ctx,
                                dscale_ctx, jnp.zeros((3, d), F32)], axis=0)

        @pl.when(pl.program_id(0) == 0)
        def _():
            sums_ref[...] = part

        @pl.when(pl.program_id(0) > 0)
        def _():
            sums_ref[...] += part

    row = pl.BlockSpec((tm, d), lambda i: (i, 0))
    return pl.pallas_call(
        body, name="norm_mod_bwd",
        grid=(t // tm,),
        in_specs=[row, pl.BlockSpec((1, d), lambda i: (0, 0)), pl.BlockSpec((2, d), lambda i: (0, 0)),
                  pl.BlockSpec((2, d), lambda i: (0, 0)), row],
        out_specs=[row, pl.BlockSpec((8, d), lambda i: (0, 0))],
        out_shape=[jax.ShapeDtypeStruct((t, d), F32), jax.ShapeDtypeStruct((8, d), F32)],
        compiler_params=_params("arbitrary"),
    )(h, w, shift, scale, du)


@functools.partial(jax.custom_vjp, nondiff_argnums=(4,))
def norm_mod(h, w, shift, scale, ctx_len):
    return _norm_mod_fwd_call(h, w[None, :], shift, scale, ctx_len)


def _norm_mod_fwd(h, w, shift, scale, ctx_len):
    return _norm_mod_fwd_call(h, w[None, :], shift, scale, ctx_len), (h, w, shift, scale)


def _norm_mod_bwd(ctx_len, res, du):
    h, w, shift, scale = res
    dh, sums = _norm_mod_bwd_call(h, w[None, :], shift, scale, du, ctx_len)
    return dh, sums[0], sums[1:3], sums[3:5]


norm_mod.defvjp(_norm_mod_fwd, _norm_mod_bwd)


def _bf(x):
    return x.astype(MXU_DTYPE)


def _dot(a, b):
    return jnp.dot(_bf(a), _bf(b), preferred_element_type=F32)


def _dot_nt(a, b):
    return lax.dot_general(_bf(a), _bf(b), (((1,), (1,)), ((), ())), preferred_element_type=F32)


def _dot_tn(a, b):
    return lax.dot_general(_bf(a), _bf(b), (((0,), (0,)), ((), ())), preferred_element_type=F32)


ROWWISE_VMEM_BYTES = 20 * 1024 * 1024


@functools.partial(jax.custom_vjp, nondiff_argnums=(1,))
def _split_lanes(x, n):
    w = x.shape[1] // n
    return tuple(x[:, i * w:(i + 1) * w] for i in range(n))


def _split_lanes_fwd(x, n):
    return _split_lanes(x, n), None


def _split_lanes_bwd(n, _, gs):
    return (jnp.concatenate(gs, axis=1),)


_split_lanes.defvjp(_split_lanes_fwd, _split_lanes_bwd)


def _rowwise_tile(t, widths):
    for tm in (768, 512, 256, 128):
        if t % tm == 0 and tm * 8 * sum(widths) <= ROWWISE_VMEM_BYTES:
            return tm
    raise ValueError((t, widths))


def _rowwise(name, fn, out_w, ctx_len):
    multi = isinstance(out_w, tuple)
    out_ws = out_w if multi else (out_w,)
    n_o = len(out_ws)

    def is_ctx(tm):
        return pl.program_id(0) * tm + lax.broadcasted_iota(jnp.int32, (tm, 1), 0) < ctx_len

    def specs(params, rows, tm):
        return ([pl.BlockSpec(p.shape, lambda i: (0, 0)) for p in params]
                + [pl.BlockSpec((tm, r.shape[1]), lambda i: (i, 0)) for r in rows])

    def fwd_call(params, rows):
        t = rows[0].shape[0]
        tm = _rowwise_tile(t, [r.shape[1] for r in rows] + list(out_ws))
        n_p = len(params)

        def body(*refs):
            vals = [r[...] for r in refs[:-n_o]]
            outs = fn(is_ctx(tm), tuple(vals[:n_p]), tuple(vals[n_p:]))
            for ref, v in zip(refs[-n_o:], outs if multi else (outs,)):
                ref[...] = v

        outs = pl.pallas_call(
            body, name=name + "_fwd", grid=(t // tm,),
            in_specs=specs(params, rows, tm),
            out_specs=[pl.BlockSpec((tm, w), lambda i: (i, 0)) for w in out_ws],
            out_shape=[jax.ShapeDtypeStruct((t, w), F32) for w in out_ws],
            compiler_params=_params("parallel"),
        )(*params, *rows)
        return tuple(outs) if multi else outs[0]

    def bwd_call(params, rows, dout):
        t = rows[0].shape[0]
        douts = tuple(dout) if multi else (dout,)
        tm = _rowwise_tile(t, [2 * r.shape[1] for r in rows] + list(out_ws))
        n_p, n_r = len(params), len(rows)
        n_in = n_p + n_r + n_o

        def body(*refs):
            vals = [r[...] for r in refs[:n_in]]
            dx_refs = refs[n_in:n_in + n_r]
            dp_refs = refs[n_in + n_r:]
            ctx_rows = is_ctx(tm)
            _, vjp = jax.vjp(lambda p, x: fn(ctx_rows, p, x), tuple(vals[:n_p]), tuple(vals[n_p:n_p + n_r]))
            dp, dx = vjp(tuple(vals[n_p + n_r:]) if multi else vals[-1])
            for ref, v in zip(dx_refs, dx):
                ref[...] = v

            @pl.when(pl.program_id(0) == 0)
            def _():
                for ref, v in zip(dp_refs, dp):
                    ref[...] = v

            @pl.when(pl.program_id(0) > 0)
            def _():
                for ref, v in zip(dp_refs, dp):
                    ref[...] += v

        row_specs = [pl.BlockSpec((tm, r.shape[1]), lambda i: (i, 0)) for r in rows]
        outs = pl.pallas_call(
            body, name=name + "_bwd", grid=(t // tm,),
            in_specs=specs(params, rows, tm) + [pl.BlockSpec((tm, w), lambda i: (i, 0)) for w in out_ws],
            out_specs=row_specs + [pl.BlockSpec(p.shape, lambda i: (0, 0)) for p in params],
            out_shape=[jax.ShapeDtypeStruct(r.shape, F32) for r in rows]
            + [jax.ShapeDtypeStruct(p.shape, F32) for p in params],
            compiler_params=_params("arbitrary"),
        )(*params, *rows, *douts)
        return tuple(outs[n_r:]), tuple(outs[:n_r])

    @jax.custom_vjp
    def op(params, rows):
        return fwd_call(params, rows)

    op.defvjp(lambda params, rows: (fwd_call(params, rows), (params, rows)),
              lambda res, g: bwd_call(res[0], res[1], g))
    return op


def _silu(x):
    return x * jax.nn.sigmoid(x)


def _ret_finish_tile(is_ctx, params, rows):
    (gn_w,), (y_f, y_b, gate) = params, rows
    heads = []
    for yh in _split_lanes(y_f + y_b, RET_HEADS):
        yc = yh - jnp.mean(yh, axis=1, keepdims=True)
        heads.append(yc * lax.rsqrt(jnp.mean(yc * yc, axis=1, keepdims=True) + NORM_EPS))
    return jnp.concatenate(heads, axis=1) * gn_w * _silu(gate)


def _ssd_finish_tile(is_ctx, params, rows):
    (d_skip, norm_w), (y_f, y_b, xs, z) = params, rows
    g = (y_f + y_b + d_skip * xs) * _silu(z)
    return g * lax.rsqrt(jnp.mean(g * g, axis=1, keepdims=True) + NORM_EPS) * norm_w


def _gate_merge_tile(is_ctx, params, rows):
    y0, y1, y2, logits = rows
    return sum(jax.nn.sigmoid(g) * y for g, y in zip(_split_lanes(logits, N_BRANCH), (y0, y1, y2)))


def _qk_prep_tile(is_ctx, params, rows):
    (q_w, k_w), (aq, ak, cos, sin) = params, rows
    cos, sin = lax.stop_gradient(cos), lax.stop_gradient(sin)
    out = []
    for x, norm_w, heads in ((aq, q_w, ATTN_HEADS), (ak, k_w, ATTN_KV_HEADS)):
        for xh in _split_lanes(x, heads):
            xn = xh * lax.rsqrt(jnp.mean(xh * xh, axis=1, keepdims=True) + NORM_EPS) * norm_w
            x1, x2 = _split_lanes(xn, 2)
            out += [x1 * cos - x2 * sin, x1 * sin + x2 * cos]
    return jnp.concatenate(out, axis=1)


def _ret_prep_tile(is_ctx, params, rows):
    rq, rk, cos, sin = rows
    cos, sin = lax.stop_gradient(cos), lax.stop_gradient(sin)
    out = []
    for x, scale in ((rq, 1.0), (rk, RET_DK ** -0.5)):
        for xh in _split_lanes(x, RET_HEADS):
            x1, x2 = _split_lanes(xh, 2)
            out += [(x1 * cos - x2 * sin) * scale, (x1 * sin + x2 * cos) * scale]
    return jnp.concatenate(out, axis=1)


def _softplus(z):
    e = jnp.exp(jnp.minimum(z, 0.0))
    series = e * (1.0 - e * (0.5 - e * (1.0 / 3.0)))
    return jnp.where(z < -5.0, series, jnp.maximum(z, 0.0) + jnp.log(1.0 + jnp.exp(-jnp.abs(z))))


def _ssd_prep_tile(is_ctx, params, rows):
    tap_prev, tap_mid, tap_next, conv_b, dt_bias, a_diag, expand_f, expand_b = params
    expand_f, expand_b = lax.stop_gradient(expand_f), lax.stop_gradient(expand_b)
    x, prev, nxt, dt_raw = rows
    xbc = _silu(prev * tap_prev + x * tap_mid + nxt * tap_next + conv_b)
    xs, rest = _split_lanes(xbc, 2)
    bm, cm = _split_lanes(rest, 2)
    dt = _softplus(dt_raw + dt_bias)
    hi = lax.Precision.HIGHEST
    dt_f = jnp.dot(dt, expand_f, precision=hi, preferred_element_type=F32)
    dt_b = jnp.dot(dt, expand_b, precision=hi, preferred_element_type=F32)
    return xs, bm, cm, xs * dt_f, xs * dt_b, jnp.dot(dt, a_diag, precision=hi, preferred_element_type=F32)


def _loss_rows_tile(is_ctx, params, rows):
    (norm_w,), (h, target) = params, rows
    y = h * lax.rsqrt(jnp.mean(h * h, axis=1, keepdims=True) + NORM_EPS) * norm_w
    err = jnp.mean(jnp.square(y - target), axis=1, keepdims=True)
    return jnp.broadcast_to(err, (h.shape[0], LANES))


def _gated_residual_tile(is_ctx, params, rows):
    (gate,), (h, update) = params, rows
    return h + jnp.where(is_ctx, gate[1:2, :], gate[0:1, :]) * update


NEG_BIG = -1e30


def _attn_tiles(t, ctx_len, backward=False):
    tq = _tile(ctx_len, (256, 128))
    assert t % tq == 0 and ctx_len % tq == 0
    tk = _tile(t, (2816, 1408, 768, 512, 256, 128) if backward else (4224, 2816, 1408, 768, 512, 256, 128))
    return tq, tk


def _head_scores(q_ref, k_bf, g, ki, tk, ctx_len, masked):
    q = (q_ref[0, g] * (ATTN_HEAD_DIM ** -0.5)).astype(MXU_DTYPE)
    s = _dot_nt(q, k_bf)
    if masked:
        col = ki * tk + lax.broadcasted_iota(jnp.int32, s.shape, 1)
        s = jnp.where(col < ctx_len, s, NEG_BIG)
    return q, s


def _attn_cases(qi, ki, tq, tk, ctx_len, compute):
    ctx_q = (qi + 1) * tq <= ctx_len

    @pl.when(jnp.logical_not(ctx_q))
    def _():
        compute(False)

    @pl.when(jnp.logical_and(ctx_q, ki * tk < ctx_len))
    def _():
        compute(True)


def _attn_fwd_call(q, k, v, ctx_len):
    kvh, grp, t, hd = q.shape
    tq, tk = _attn_tiles(t, ctx_len)
    nkb = t // tk

    def body(q_ref, k_ref, v_ref, o_ref, lse_ref, m_sc, l_sc, acc_sc):
        qi, ki = pl.program_id(1), pl.program_id(2)

        @pl.when(ki == 0)
        def _():
            m_sc[...] = jnp.full(m_sc.shape, NEG_BIG, F32)
            l_sc[...] = jnp.zeros(l_sc.shape, F32)
            acc_sc[...] = jnp.zeros(acc_sc.shape, F32)

        def compute(masked):
            k_bf, v_bf = _bf(k_ref[0]), _bf(v_ref[0])
            for g in range(grp):
                _, s = _head_scores(q_ref, k_bf, g, ki, tk, ctx_len, masked)
                m_prev = m_sc[g]
                m_new = jnp.maximum(m_prev, jnp.max(s, axis=1, keepdims=True))
                alpha = jnp.exp(m_prev - m_new)
                p = jnp.exp(s - m_new)
                l_sc[g] = alpha * l_sc[g] + jnp.sum(p, axis=1, keepdims=True)
                acc_sc[g] = alpha * acc_sc[g] + _dot(p, v_bf)
                m_sc[g] = m_new

        _attn_cases(qi, ki, tq, tk, ctx_len, compute)

        @pl.when(ki == nkb - 1)
        def _():
            o_ref[0] = acc_sc[...] / l_sc[...]
            lse_ref[0] = m_sc[...] + jnp.log(l_sc[...])

    return pl.pallas_call(
        body, name="attn_fwd",
        grid=(kvh, t // tq, nkb),
        in_specs=[pl.BlockSpec((1, grp, tq, hd), lambda h, i, j: (h, 0, i, 0)),
                  pl.BlockSpec((1, tk, hd), lambda h, i, j: (h, j, 0)),
                  pl.BlockSpec((1, tk, hd), lambda h, i, j: (h, j, 0))],
        out_specs=[pl.BlockSpec((1, grp, tq, hd), lambda h, i, j: (h, 0, i, 0)),
                   pl.BlockSpec((1, grp, tq, 1), lambda h, i, j: (h, 0, i, 0))],
        out_shape=[jax.ShapeDtypeStruct(q.shape, F32), jax.ShapeDtypeStruct((kvh, grp, t, 1), F32)],
        scratch_shapes=[pltpu.VMEM((grp, tq, 1), F32), pltpu.VMEM((grp, tq, 1), F32), pltpu.VMEM((grp, tq, hd), F32)],
        compiler_params=_params("parallel", "parallel", "arbitrary"),
    )(q, k, v)


def _head_probs(q_ref, k_bf, v_bf, o_ref, do_ref, lse_ref, g, ki, tk, ctx_len, masked):
    q, s = _head_scores(q_ref, k_bf, g, ki, tk, ctx_len, masked)
    do = do_ref[0, g]
    delta = jnp.sum(do * o_ref[0, g], axis=1, keepdims=True)
    p = jnp.exp(s - lse_ref[0, g])
    do = _bf(do)
    ds = p * (_dot_nt(do, v_bf) - delta)
    return q, do, p, ds


def _attn_bwd_call(q, k, v, o, lse, do, ctx_len):
    kvh, grp, t, hd = q.shape
    tq, tk = _attn_tiles(t, ctx_len, backward=True)
    nqb, nkb = t // tq, t // tk

    def body(q_ref, k_ref, v_ref, o_ref, lse_ref, do_ref, dq_hbm, dk_ref, dv_ref, dq_sc, dk_sc, dv_sc, dq_out,
             dq_sem):
        hi, ki, qi = pl.program_id(0), pl.program_id(1), pl.program_id(2)
        rows = pl.ds(pl.multiple_of(qi * tq, tq), tq)

        @pl.when(ki == 0)
        def _():
            dq_sc[:, rows, :] = jnp.zeros((grp, tq, hd), F32)

        @pl.when(qi == 0)
        def _():
            dk_sc[...] = jnp.zeros(dk_sc.shape, F32)
            dv_sc[...] = jnp.zeros(dv_sc.shape, F32)

        def compute(masked):
            k_bf, v_bf = _bf(k_ref[0]), _bf(v_ref[0])
            dk_part = jnp.zeros(dk_sc.shape, F32)
            dv_part = jnp.zeros(dv_sc.shape, F32)
            for g in range(grp):
                qs, dob, p, ds = _head_probs(q_ref, k_bf, v_bf, o_ref, do_ref, lse_ref, g, ki, tk, ctx_len, masked)
                dv_part = dv_part + _dot_tn(p, dob)
                dk_part = dk_part + _dot_tn(ds, qs)
                dq_sc[g, rows, :] += _dot(ds, k_bf)
            dk_sc[...] += dk_part
            dv_sc[...] += dv_part

        _attn_cases(qi, ki, tq, tk, ctx_len, compute)

        @pl.when(ki == nkb - 1)
        def _():
            dq_out[...] = dq_sc[:, rows, :] * (hd ** -0.5)
            done = pltpu.make_async_copy(dq_out, dq_hbm.at[hi, :, rows, :], dq_sem)
            done.start()
            done.wait()

        @pl.when(qi == nqb - 1)
        def _():
            dk_ref[0] = dk_sc[...]
            dv_ref[0] = dv_sc[...]

    qspec = pl.BlockSpec((1, grp, tq, hd), lambda h, j, i: (h, 0, i, 0))
    kspec = pl.BlockSpec((1, tk, hd), lambda h, j, i: (h, j, 0))
    return pl.pallas_call(
        body, name="attn_bwd",
        grid=(kvh, t // tk, nqb),
        in_specs=[qspec, kspec, kspec, qspec, pl.BlockSpec((1, grp, tq, 1), lambda h, j, i: (h, 0, i, 0)), qspec],
        out_specs=[pl.BlockSpec(memory_space=pl.ANY), kspec, kspec],
        out_shape=[jax.ShapeDtypeStruct(q.shape, F32), jax.ShapeDtypeStruct(k.shape, F32),
                   jax.ShapeDtypeStruct(v.shape, F32)],
        scratch_shapes=[pltpu.VMEM((grp, t, hd), F32), pltpu.VMEM((tk, hd), F32), pltpu.VMEM((tk, hd), F32),
                        pltpu.VMEM((grp, tq, hd), F32), pltpu.SemaphoreType.DMA],
        compiler_params=_params("arbitrary", "arbitrary", "arbitrary"),
    )(q, k, v, o, lse, do)


@functools.partial(jax.custom_vjp, nondiff_argnums=(3,))
def attention(q, k, v, ctx_len):
    return _attn_fwd_call(q, k, v, ctx_len)[0]


def _attention_fwd(q, k, v, ctx_len):
    o, lse = _attn_fwd_call(q, k, v, ctx_len)
    return o, (q, k, v, o, lse)


def _attention_bwd(ctx_len, res, do):
    q, k, v, o, lse = res
    return tuple(_attn_bwd_call(q, k, v, o, lse, do, ctx_len))


attention.defvjp(_attention_fwd, _attention_bwd)


def _chunk_order(step, n_chunks, n_ctx_chunks, reverse):
    if not reverse:
        return step
    return jnp.where(step < n_ctx_chunks, n_ctx_chunks - 1 - step, n_chunks + n_ctx_chunks - 1 - step)


def _scan_masks(chunk, reverse):
    row = lax.broadcasted_iota(jnp.int32, (chunk, chunk), 0)
    col = lax.broadcasted_iota(jnp.int32, (chunk, chunk), 1)
    vis = (col >= row) if reverse else (col <= row)
    vis_t = (row >= col) if reverse else (row <= col)
    return vis, vis.astype(F32), vis_t.astype(F32)


def _cum_decay(a_col, a_row, vis_f):
    hi = lax.Precision.HIGHEST
    cum_col = jnp.dot(vis_f, a_col, precision=hi, preferred_element_type=F32)
    cum_row = lax.dot_general(a_row, vis_f, (((1,), (1,)), ((), ())), precision=hi, preferred_element_type=F32)
    total = jnp.sum(a_col, axis=0, keepdims=True)
    return cum_col, cum_row, total


SCAN_CHUNKS_PER_STEP = 2


def _scan_specs(t, ctx_len, reverse, backward, widths):
    rows = SCAN_CHUNK * SCAN_CHUNKS_PER_STEP
    assert t % rows == 0 and ctx_len % rows == 0
    n_blocks, n_ctx_blocks = t // rows, ctx_len // rows

    def order(i):
        step = (n_blocks - 1 - i) if backward else i
        return _chunk_order(step, n_blocks, n_ctx_blocks, reverse)

    halves = list(range(SCAN_CHUNKS_PER_STEP))
    if reverse != backward:
        halves.reverse()
    return [pl.BlockSpec((rows, w), lambda i: (order(i), 0)) for w in widths], order, n_blocks, halves


def _scan_fwd_call(q, k, v, a_col, a_row, *, groups, per_group, dk, dv, ctx_len, reverse):
    t = q.shape[0]
    chunk, per_step = SCAN_CHUNK, SCAN_CHUNKS_PER_STEP
    heads = groups * per_group
    (q_spec, k_spec, v_spec, acol_spec), order, n_blocks, halves = _scan_specs(
        t, ctx_len, reverse, False, (groups * dk, groups * dk, heads * dv, LANES))

    def body(q_ref, k_ref, v_ref, acol_ref, arow_ref, y_ref, st_ref, s_sc):
        @pl.when(pl.program_id(0) == 0)
        def _():
            s_sc[...] = jnp.zeros(s_sc.shape, F32)

        vis, vis_f, _ = _scan_masks(chunk, reverse)
        for half in halves:
            rows = slice(half * chunk, (half + 1) * chunk)
            st_ref[half] = s_sc[...]
            cum_col, cum_row, total = _cum_decay(acol_ref[rows, :], arow_ref[:, rows], vis_f)
            for g in range(groups):
                qg = q_ref[rows, g * dk:(g + 1) * dk]
                kg = k_ref[rows, g * dk:(g + 1) * dk]
                qk = _dot_nt(qg, kg)
                for r in range(per_group):
                    h = g * per_group + r
                    ccol = cum_col[:, h:h + 1]
                    decay = jnp.exp(jnp.where(vis, ccol - cum_row[h:h + 1, :], NEG_BIG))
                    vh = v_ref[rows, h * dv:(h + 1) * dv]
                    s_in = s_sc[h]
                    y = _dot(qk * decay, vh) + jnp.exp(ccol) * _dot(qg, s_in)
                    y_ref[rows, h * dv:(h + 1) * dv] = y
                    tot = total[:, h:h + 1]
                    s_sc[h] = jnp.exp(tot) * s_in + _dot_tn(kg * jnp.exp(tot - ccol), vh)

    return pl.pallas_call(
        body, name="scan_fwd",
        grid=(n_blocks,),
        in_specs=[q_spec, k_spec, v_spec, acol_spec, pl.BlockSpec((8, chunk * per_step), lambda i: (0, order(i)))],
        out_specs=[v_spec, pl.BlockSpec((per_step, heads, dk, dv), lambda i: (order(i), 0, 0, 0))],
        out_shape=[jax.ShapeDtypeStruct(v.shape, F32),
                   jax.ShapeDtypeStruct((n_blocks * per_step, heads, dk, dv), F32)],
        scratch_shapes=[pltpu.VMEM((heads, dk, dv), F32)],
        compiler_params=_params("arbitrary"),
    )(q, k, v, a_col, a_row)


def _scan_bwd_call(q, k, v, a_col, a_row, states, dy, *, groups, per_group, dk, dv, ctx_len, reverse):
    t = q.shape[0]
    chunk, per_step = SCAN_CHUNK, SCAN_CHUNKS_PER_STEP
    heads = groups * per_group
    (q_spec, k_spec, v_spec, acol_spec), order, n_blocks, halves = _scan_specs(
        t, ctx_len, reverse, True, (groups * dk, groups * dk, heads * dv, LANES))
    arow_spec = pl.BlockSpec((8, chunk * per_step), lambda i: (0, order(i)))
    last = 0 if reverse else chunk - 1

    def body(q_ref, k_ref, v_ref, acol_ref, arow_ref, st_ref, dy_ref, dq_ref, dk_ref, dv_ref, da_ref, dat_ref,
             ds_sc):
        @pl.when(pl.program_id(0) == 0)
        def _():
            ds_sc[...] = jnp.zeros(ds_sc.shape, F32)

        vis, vis_f, vis_tf = _scan_masks(chunk, reverse)
        lane = lax.broadcasted_iota(jnp.int32, (chunk, LANES), 1)
        row = lax.broadcasted_iota(jnp.int32, (chunk, LANES), 0)
        sub = lax.broadcasted_iota(jnp.int32, (8, chunk), 0)
        for half in halves:
            rows = slice(half * chunk, (half + 1) * chunk)
            cum_col, cum_row, total = _cum_decay(acol_ref[rows, :], arow_ref[:, rows], vis_f)
            dcum = jnp.zeros((chunk, LANES), F32)
            dcum_t = jnp.zeros((8, chunk), F32)
            for g in range(groups):
                qg = q_ref[rows, g * dk:(g + 1) * dk]
                kg = k_ref[rows, g * dk:(g + 1) * dk]
                qk = _dot_nt(qg, kg)
                dq_g = jnp.zeros((chunk, dk), F32)
                dk_g = jnp.zeros((chunk, dk), F32)
                for r in range(per_group):
                    h = g * per_group + r
                    ccol = cum_col[:, h:h + 1]
                    decay = jnp.exp(jnp.where(vis, ccol - cum_row[h:h + 1, :], NEG_BIG))
                    vh = v_ref[rows, h * dv:(h + 1) * dv]
                    dyh = dy_ref[rows, h * dv:(h + 1) * dv]
                    s_in = st_ref[half, h]
                    ds_out = ds_sc[h]
                    tot = total[:, h:h + 1]
                    e_in = jnp.exp(ccol)
                    e_out = jnp.exp(tot - ccol)
                    e_tot = jnp.exp(tot)
                    k_out = kg * e_out
                    dv_ref[rows, h * dv:(h + 1) * dv] = _dot_tn(qk * decay, dyh) + _dot(k_out, ds_out)
                    dqk = _dot_nt(dyh, vh) * decay
                    dq_in = e_in * _dot_nt(dyh, s_in)
                    dk_out = e_out * _dot_nt(vh, ds_out)
                    dq_h = _dot(dqk, kg) + dq_in
                    dk_h = _dot_tn(dqk, qg) + dk_out
                    s_out = e_tot * s_in + _dot_tn(k_out, vh)
                    edge = jnp.sum(jnp.sum(s_out * ds_out, axis=1, keepdims=True), axis=0, keepdims=True)
                    w_seg = dqk * qk
                    dcum_h = (jnp.sum(w_seg, axis=1, keepdims=True) + jnp.sum(dq_in * qg, axis=1, keepdims=True)
                              - jnp.sum(dk_out * kg, axis=1, keepdims=True))
                    dcum = jnp.where(lane == h, dcum_h + jnp.where(row == last, edge, 0.0), dcum)
                    dcum_t = jnp.where(sub == h, -jnp.sum(w_seg, axis=0, keepdims=True), dcum_t)
                    ds_sc[h] = e_tot * ds_out + _dot_tn(qg, e_in * dyh)
                    dq_g = dq_g + dq_h
                    dk_g = dk_g + dk_h
                dq_ref[rows, g * dk:(g + 1) * dk] = dq_g
                dk_ref[rows, g * dk:(g + 1) * dk] = dk_g
            hi = lax.Precision.HIGHEST
            da_ref[rows, :] = jnp.dot(vis_tf, dcum, precision=hi, preferred_element_type=F32)
            dat_ref[:, rows] = jnp.dot(dcum_t, vis_f, precision=hi, preferred_element_type=F32)

    return pl.pallas_call(
        body, name="scan_bwd",
        grid=(n_blocks,),
        in_specs=[q_spec, k_spec, v_spec, acol_spec, arow_spec,
                  pl.BlockSpec((per_step, heads, dk, dv), lambda i: (order(i), 0, 0, 0)), v_spec],
        out_specs=[q_spec, k_spec, v_spec, acol_spec, arow_spec],
        out_shape=[jax.ShapeDtypeStruct(q.shape, F32), jax.ShapeDtypeStruct(k.shape, F32),
                   jax.ShapeDtypeStruct(v.shape, F32), jax.ShapeDtypeStruct((t, LANES), F32),
                   jax.ShapeDtypeStruct((8, t), F32)],
        scratch_shapes=[pltpu.VMEM((heads, dk, dv), F32)],
        compiler_params=_params("arbitrary"),
    )(q, k, v, a_col, a_row, states, dy)


def _decay_layouts(a):
    t, heads = a.shape
    a_col = jnp.pad(a, ((0, 0), (0, LANES - heads)))
    a_row = jnp.pad(a.T, ((0, 8 - heads), (0, 0)))
    return a_col, a_row


@functools.partial(jax.custom_vjp, nondiff_argnums=(4,))
def linear_scan(q, k, v, a, cfg):
    a_col, a_row = _decay_layouts(a)
    return _scan_fwd_call(q, k, v, a_col, a_row, **dict(cfg))[0]


def _linear_scan_fwd(q, k, v, a, cfg):
    a_col, a_row = _decay_layouts(a)
    y, states = _scan_fwd_call(q, k, v, a_col, a_row, **dict(cfg))
    return y, (q, k, v, a, states)


def _linear_scan_bwd(cfg, res, dy):
    q, k, v, a, states = res
    a_col, a_row = _decay_layouts(a)
    dq, dk, dv, da, da_t = _scan_bwd_call(q, k, v, a_col, a_row, states, dy, **dict(cfg))
    heads = a.shape[1]
    return dq, dk, dv, da[:, :heads] + da_t[:heads].T


linear_scan.defvjp(_linear_scan_fwd, _linear_scan_bwd)


def _scan_cfg(groups, per_group, dk, dv, ctx_len, reverse):
    return (("groups", groups), ("per_group", per_group), ("dk", dk), ("dv", dv), ("ctx_len", ctx_len),
            ("reverse", reverse))


def _axial_tables(n_lat, n_ctx):
    freqs = ATTN_HEAD_DIM // 4
    rows = n_lat // GRID_W
    row = jnp.repeat(jnp.arange(rows, dtype=F32), GRID_W)
    col = jnp.tile(jnp.arange(GRID_W, dtype=F32), rows)
    inv = ROPE_THETA ** (-jnp.arange(freqs, dtype=F32) / freqs)
    ang = jnp.concatenate([row[:, None] * inv, col[:, None] * inv], axis=-1)
    cos = jnp.concatenate([jnp.ones((n_ctx, 2 * freqs), F32), jnp.cos(ang)], axis=0)
    sin = jnp.concatenate([jnp.zeros((n_ctx, 2 * freqs), F32), jnp.sin(ang)], axis=0)
    return cos, sin


def _seq_tables(t):
    pos = jnp.arange(t, dtype=F32)
    inv = ROPE_THETA ** (-jnp.linspace(0.0, 1.0, RET_DK // 2, dtype=F32))
    ang = pos[:, None] * inv
    return jnp.cos(ang), jnp.sin(ang)


def _pad_w_in(w_in):
    d = w_in.shape[0]
    return jnp.concatenate([w_in[:, :DT_END], jnp.zeros((d, DT_PAD), w_in.dtype), w_in[:, DT_END:],
                            jnp.zeros((d, TAIL_PAD), w_in.dtype)], axis=1)


def _split_proj(p):
    widths = list(IN_SPLITS)
    widths[5] = LANES
    out, off = [], 0
    for w in widths:
        out.append(p[:, off:off + w])
        off += w
    out[5] = out[5][:, :DT_COLS]
    return out


def _mixer(u, w, wq, layer, n_ctx, tables):
    t = u.shape[0]
    attn_rope, ret_rope, seg_first, seg_last = tables
    proj = matmul(u, wq["w_in"][layer], _pad_w_in(w["w_in"][layer]))
    aq, ak, av, z, xbc_raw, dt_raw, rq, rk, rv, rg, gate_logits = _split_proj(proj)

    q_width = ATTN_HEADS * ATTN_HEAD_DIM
    qk = _rowwise("qk_prep", _qk_prep_tile, q_width + ATTN_KV_HEADS * ATTN_HEAD_DIM, n_ctx)(
        (w["attn_q_norm"][layer][None, :], w["attn_k_norm"][layer][None, :]), (aq, ak) + attn_rope)
    q4 = qk[:, :q_width].reshape(t, ATTN_KV_HEADS, ATTN_GROUP, ATTN_HEAD_DIM).transpose(1, 2, 0, 3)
    k3 = qk[:, q_width:].reshape(t, ATTN_KV_HEADS, ATTN_HEAD_DIM).transpose(1, 0, 2)
    v3 = av.reshape(t, ATTN_KV_HEADS, ATTN_HEAD_DIM).transpose(1, 0, 2)
    o4 = attention(q4, k3, v3, n_ctx)
    br_attn = o4.transpose(2, 0, 1, 3).reshape(t, ATTN_HEADS * ATTN_HEAD_DIM)

    cw, cb = w["ssd_conv_w"][layer], w["ssd_conv_b"][layer]
    zero_row = jnp.zeros((1, xbc_raw.shape[1]), F32)
    prev = jnp.concatenate([zero_row, xbc_raw[:-1]], axis=0) * (1.0 - seg_first)
    nxt = jnp.concatenate([xbc_raw[1:], zero_row], axis=0) * (1.0 - seg_last)
    n_dt = 2 * SSD_HEADS
    a_neg = -jnp.exp(w["ssd_a_log"][layer]).reshape(n_dt)
    a_diag = jnp.where(jnp.arange(n_dt)[:, None] == jnp.arange(LANES)[None, :], a_neg[:, None], 0.0)
    head_of_lane = jnp.arange(SSD_D_INNER) // SSD_HEAD_DIM
    expand = [(jnp.arange(n_dt)[:, None] == head_of_lane[None, :] + d * SSD_HEADS).astype(F32) for d in range(2)]
    xs, bm, cm, v_fwd, v_bwd, decays = _rowwise(
        "ssd_prep", _ssd_prep_tile, (SSD_D_INNER, SSD_GROUPS * SSD_STATE, SSD_GROUPS * SSD_STATE, SSD_D_INNER,
                                     SSD_D_INNER, LANES), n_ctx)(
        (cw[0:1], cw[1:2], cw[2:3], cb[None, :], w["ssd_dt_bias"][layer].reshape(1, n_dt), a_diag, expand[0],
         expand[1]), (xbc_raw, prev, nxt, dt_raw))
    y_ssd = []
    for d, (reverse, v_d) in enumerate(((False, v_fwd), (True, v_bwd))):
        cfg = _scan_cfg(SSD_GROUPS, SSD_HEADS // SSD_GROUPS, SSD_STATE, SSD_HEAD_DIM, n_ctx, reverse)
        y_ssd.append(linear_scan(cm, bm, v_d, decays[:, d * SSD_HEADS:(d + 1) * SSD_HEADS], cfg))
    d_skip = jnp.repeat(w["ssd_d"][layer], SSD_HEAD_DIM)[None, :]
    br_ssd = _rowwise("ssd_finish", _ssd_finish_tile, SSD_D_INNER, n_ctx)(
        (d_skip, w["ssd_norm_w"][layer][None, :]), (y_ssd[0], y_ssd[1], xs, z))

    lg = -jnp.exp(w["ret_log_decay"][layer])
    ret_width = RET_HEADS * RET_DK
    ret_qk = _rowwise("ret_prep", _ret_prep_tile, 2 * ret_width, n_ctx)((), (rq, rk) + ret_rope)
    rq_r, rk_r = ret_qk[:, :ret_width], ret_qk[:, ret_width:]
    y_ret = []
    for d, reverse in ((0, False), (1, True)):
        cfg = _scan_cfg(RET_HEADS, 1, RET_DK, RET_DV, n_ctx, reverse)
        y_ret.append(linear_scan(rq_r, rk_r, rv, jnp.broadcast_to(lg[d][None, :], (t, RET_HEADS)), cfg))
    br_ret = _rowwise("ret_finish", _ret_finish_tile, RET_HEADS * RET_DV, n_ctx)(
        (w["ret_gn_w"][layer][None, :],), (y_ret[0], y_ret[1], rg))

    projected = tuple(matmul(br, wq["w_branch"][layer][j], w["w_branch"][layer][j])
                      for j, br in enumerate((br_attn, br_ssd, br_ret)))
    merged = _rowwise("gate_merge", _gate_merge_tile, D_MODEL, n_ctx)((), projected + (gate_logits,))
    return matmul(merged, wq["w_out"][layer], w["w_out"][layer])


def _local_loss(w, x, c, ctx, target, wq):
    n, m = x.shape[0], ctx.shape[0]
    t = n + m
    pos = jnp.arange(t)[:, None]
    seg_first = ((pos == 0) | (pos == m)).astype(F32)
    seg_last = ((pos == m - 1) | (pos == t - 1)).astype(F32)
    tables = (_axial_tables(n, m), _seq_tables(t), seg_first, seg_last)
    h = jnp.concatenate([ctx, x], axis=0)
    cond = jax.nn.silu(jnp.stack([c, w["c_ctx"]], axis=0))
    cond8 = jnp.concatenate([cond, jnp.zeros((6, D_MODEL), F32)], axis=0)
    for layer in range(DEPTH):
        mod = (matmul(cond8, wq["w_mod"][layer], w["w_mod"][layer])[:2] + w["b_mod"][layer]).reshape(2, 6, D_MODEL)

        u = norm_mod(h, w["norm1_w"][layer], mod[:, 0], mod[:, 1], m)
        residual = _rowwise("gated_residual", _gated_residual_tile, D_MODEL, m)
        h = residual((mod[:, 2],), (h, _mixer(u, w, wq, layer, m, tables)))
        v = norm_mod(h, w["norm2_w"][layer], mod[:, 3], mod[:, 4], m)
        mlp = sq_relu_mlp(v, wq["w_mlp1"][layer], w["w_mlp1"][layer], wq["w_mlp2"][layer], w["w_mlp2"][layer])
        residual = _rowwise("gated_residual", _gated_residual_tile, D_MODEL, m)
        h = residual((mod[:, 5],), (h, mlp))
    per_token = _rowwise("loss_rows", _loss_rows_tile, LANES, 0)((w["final_norm_w"][None, :],), (h[m:], target))
    return 0.5 * jnp.sum(per_token[:, 0])


def _coords():
    return lax.axis_index("x"), lax.axis_index("y"), lax.axis_index("c")


def _all_gather(blocks, name):
    n = len(blocks)

    def body(*refs):
        x_refs, out_refs = refs[:n], refs[n:2 * n]
        send_sems, recv_sems, local_sems = refs[2 * n:]
        x, y, c = _coords()
        me, sibling = (x, y, c), (x, y, 1 - c)
        chips = [(1 - x, y), (x, 1 - y), (1 - x, 1 - y)]

        def copy(k, i, blk, to, from_input=False):
            slot = out_refs[i].at[4 * blk[0] + 2 * blk[1] + blk[2]]
            return pltpu.make_async_remote_copy(
                src_ref=x_refs[i] if from_input else slot, dst_ref=slot,
                send_sem=send_sems.at[k * n + i], recv_sem=recv_sems.at[k * n + i],
                device_id=to, device_id_type=pl.DeviceIdType.MESH)

        mine = [pltpu.make_async_copy(x_refs[i], out_refs[i].at[4 * x + 2 * y + c], local_sems.at[i])
                for i in range(n)]
        for cp in mine:
            cp.start()
        first = [copy(0, i, me, sibling, True) for i in range(n)]
        first += [copy(1 + j, i, me, (*chip, c), True) for j, chip in enumerate(chips) for i in range(n)]
        for cp in first:
            cp.start()
        passed = []
        for j, chip in enumerate(chips):
            for i in range(n):
                copy(1 + j, i, (*chip, c), me).wait_recv()
                passed.append(copy(4 + j, i, (*chip, c), sibling))
                passed[-1].start()
        for i in range(n):
            copy(0, i, sibling, me).wait_recv()
        for j, chip in enumerate(chips):
            for i in range(n):
                copy(4 + j, i, (*chip, 1 - c), me).wait_recv()
        for cp in first + passed:
            cp.wait_send()
        for cp in mine:
            cp.wait()

    return pl.pallas_call(
        body, name=name,
        out_shape=[jax.ShapeDtypeStruct((N_DEV,) + b.shape, b.dtype) for b in blocks],
        in_specs=[pl.BlockSpec(memory_space=pl.ANY)] * n,
        out_specs=[pl.BlockSpec(memory_space=pl.ANY)] * n,
        scratch_shapes=[pltpu.SemaphoreType.DMA((7 * n,)), pltpu.SemaphoreType.DMA((7 * n,)),
                        pltpu.SemaphoreType.DMA((n,))],
    )(*blocks)


def _pair_exchange(arrays, name):
    n = len(arrays)

    def body(*refs):
        g_refs, out_refs = refs[:n], refs[n:2 * n]
        send_sems, recv_sems = refs[2 * n:]
        x, y, c = _coords()
        copies = [pltpu.make_async_remote_copy(
            src_ref=g_refs[i].at[1 - c], dst_ref=out_refs[i], send_sem=send_sems.at[i], recv_sem=recv_sems.at[i],
            device_id=(x, y, 1 - c), device_id_type=pl.DeviceIdType.MESH) for i in range(n)]
        for cp in copies:
            cp.start()
        for cp in copies:
            cp.wait()

    return pl.pallas_call(
        body, name=name,
        out_shape=[jax.ShapeDtypeStruct(a.shape[1:], a.dtype) for a in arrays],
        in_specs=[pl.BlockSpec(memory_space=pl.ANY)] * n,
        out_specs=[pl.BlockSpec(memory_space=pl.ANY)] * n,
        scratch_shapes=[pltpu.SemaphoreType.DMA((n,)), pltpu.SemaphoreType.DMA((n,))],
    )(*arrays)


def _chip_exchange(arrays, name):
    n = len(arrays)

    def body(*refs):
        g_refs, out_refs = refs[:n], refs[n:2 * n]
        send_sems, recv_sems, local_sems = refs[2 * n:]
        x, y, c = _coords()
        me = 2 * x + y
        mine = [pltpu.make_async_copy(g_refs[i].at[me], out_refs[i].at[me], local_sems.at[i]) for i in range(n)]
        for cp in mine:
            cp.start()
        copies = []
        for k in range(1, 4):
            px, py = (1 - x if (k >> 1) & 1 else x), (1 - y if k & 1 else y)
            for i in range(n):
                copies.append(pltpu.make_async_remote_copy(
                    src_ref=g_refs[i].at[2 * px + py], dst_ref=out_refs[i].at[me],
                    send_sem=send_sems.at[(k - 1) * n + i], recv_sem=recv_sems.at[(k - 1) * n + i],
                    device_id=(px, py, c), device_id_type=pl.DeviceIdType.MESH))
        for cp in copies:
            cp.start()
        for cp in copies:
            cp.wait_recv()
        for cp in copies:
            cp.wait_send()
        for cp in mine:
            cp.wait()

    return pl.pallas_call(
        body, name=name,
        out_shape=[jax.ShapeDtypeStruct(a.shape, a.dtype) for a in arrays],
        in_specs=[pl.BlockSpec(memory_space=pl.ANY)] * n,
        out_specs=[pl.BlockSpec(memory_space=pl.ANY)] * n,
        scratch_shapes=[pltpu.SemaphoreType.DMA((3 * n,)), pltpu.SemaphoreType.DMA((3 * n,)),
                        pltpu.SemaphoreType.DMA((n,))],
    )(*arrays)


def _add_pair(a, b, name):
    rows, cols = a.shape
    tr = _tile(rows, [r for r in (2048, 1024, 512, 256, 128, 64, 32, 16) if r * cols <= 2 * ADAMW_BLOCK_ELEMS])

    def body(a_ref, b_ref, o_ref):
        o_ref[...] = (a_ref[...].astype(F32) + b_ref[...].astype(F32)).astype(o_ref.dtype)

    spec = pl.BlockSpec((tr, cols), lambda i: (i, 0))
    return pl.pallas_call(
        body, name=name, grid=(rows // tr,), in_specs=[spec, spec], out_specs=spec,
        out_shape=jax.ShapeDtypeStruct(a.shape, a.dtype), compiler_params=_params("parallel"),
    )(a, b)


ADAMW_BLOCK_ELEMS = 256 * 1024


def _sum_adamw(g8, w, m, v, name):
    rows, cols = w.shape
    tr = _tile(rows, [r for r in (2048, 1024, 512, 256, 128, 64, 32, 16) if r * cols <= ADAMW_BLOCK_ELEMS])

    def body(g_ref, w_ref, m_ref, v_ref, go_ref, d_ref, mo_ref, vo_ref):
        g = g_ref[0].astype(F32)
        for s in range(1, g8.shape[0]):
            g = g + g_ref[s].astype(F32)
        m_new = ADAM_B1 * m_ref[...] + (1.0 - ADAM_B1) * g
        v_new = ADAM_B2 * v_ref[...] + (1.0 - ADAM_B2) * (g * g)
        m_hat = m_new / (1.0 - ADAM_B1 ** ADAM_STEP)
        v_hat = v_new / (1.0 - ADAM_B2 ** ADAM_STEP)
        go_ref[...] = g
        d_ref[...] = -ADAM_LR * (m_hat / (jnp.sqrt(v_hat) + ADAM_EPS) + ADAM_WD * w_ref[...])
        mo_ref[...] = m_new
        vo_ref[...] = v_new

    spec = pl.BlockSpec((tr, cols), lambda i: (i, 0))
    shape = jax.ShapeDtypeStruct((rows, cols), F32)
    return pl.pallas_call(
        body, name=name,
        grid=(rows // tr,),
        in_specs=[pl.BlockSpec((g8.shape[0], tr, cols), lambda i: (0, i, 0)), spec, spec, spec],
        out_specs=[spec, spec, spec, spec],
        out_shape=[shape, shape, shape, shape],
        compiler_params=_params("parallel"),
    )(g8, w, m, v)


BIG = ("w_mod", "w_in", "w_branch", "w_out", "w_mlp1", "w_mlp2")
COL_SHARDED = ("w_mod", "w_mlp1")
ROW_SHARDED = ("w_out", "w_mlp2")
SMALL = ("c_ctx", "b_mod", "norm1_w", "norm2_w", "attn_q_norm", "attn_k_norm", "ssd_conv_b", "ssd_dt_bias",
         "ssd_a_log", "ssd_d", "ssd_norm_w", "ret_log_decay", "ret_gn_w", "final_norm_w")
CONV_AXIS = 2
ORDER = ("c_ctx", "w_mod", "b_mod", "norm1_w", "norm2_w", "w_in", "attn_q_norm", "attn_k_norm", "ssd_conv_w",
         "ssd_conv_b", "ssd_dt_bias", "ssd_a_log", "ssd_d", "ssd_norm_w", "ret_log_decay", "ret_gn_w", "w_branch",
         "w_out", "w_mlp1", "w_mlp2", "final_norm_w")


def _compute_weights(gathered):
    wq, carrier = {}, {}
    for name in BIG:
        g = gathered[name]
        per_layer = []
        for layer in range(DEPTH):
            gl = g[:, layer]
            if name in COL_SHARDED:
                per_layer.append(gl)
            elif name in ROW_SHARDED:
                per_layer.append(gl.reshape(-1, gl.shape[-1]))
            elif name == "w_in":
                per_layer.append(_pad_w_in(jnp.concatenate([gl[d] for d in range(N_DEV)], axis=-1)))
            else:
                per_layer.append([jnp.concatenate([gl[d, j] for d in range(N_DEV)], axis=-1) for j in range(N_BRANCH)])
        wq[name] = per_layer
    for name in BIG:
        if name == "w_in":
            carrier[name] = [jnp.zeros((D_MODEL, IN_DIM), F32) for _ in range(DEPTH)]
        else:
            carrier[name] = jax.tree.map(lambda a: jnp.zeros(a.shape, F32), wq[name])
    return wq, carrier


def _grad_shards(gw):
    out = {}
    for name in BIG:
        per_layer = []
        for layer in range(DEPTH):
            g = gw[name][layer]
            if name in COL_SHARDED:
                per_layer.append(g)
            elif name in ROW_SHARDED:
                per_layer.append(g.reshape(N_DEV, -1, g.shape[-1]))
            elif name == "w_in":
                size = IN_DIM // N_DEV
                per_layer.append(jnp.stack([g[:, d * size:(d + 1) * size] for d in range(N_DEV)]))
            else:
                per_layer.append(jnp.stack([gj.reshape(gj.shape[0], N_DEV, -1).transpose(1, 0, 2) for gj in g], axis=1))
        out[name] = jnp.stack(per_layer, axis=1)
    return out


def _pack(arrays, row_multiple):
    flat = jnp.concatenate(arrays, axis=-1)
    n = flat.shape[-1]
    per = LANES * row_multiple
    padded = -(-n // per) * per
    flat = jnp.pad(flat, [(0, 0)] * (flat.ndim - 1) + [(0, padded - n)])
    return flat.reshape(flat.shape[:-1] + (padded // LANES, LANES))


def _unpack(slab, shapes):
    flat = slab.reshape(slab.shape[:-2] + (-1,))
    out, off = [], 0
    for shp in shapes:
        size = math.prod(shp)
        out.append(flat[..., off:off + size].reshape(flat.shape[:-1] + tuple(shp)))
        off += size
    return out


def kernel(x, c, ctx, c_ctx, w_mod, b_mod, norm1_w, norm2_w, w_in, attn_q_norm, attn_k_norm, ssd_conv_w, ssd_conv_b, ssd_dt_bias, ssd_a_log, ssd_d, ssd_norm_w, ret_log_decay, ret_gn_w, w_branch, w_out, w_mlp1, w_mlp2, final_norm_w, loss_target, m_c_ctx, m_w_mod, m_b_mod, m_norm1_w, m_norm2_w, m_w_in, m_attn_q_norm, m_attn_k_norm, m_ssd_conv_w, m_ssd_conv_b, m_ssd_dt_bias, m_ssd_a_log, m_ssd_d, m_ssd_norm_w, m_ret_log_decay, m_ret_gn_w, m_w_branch, m_w_out, m_w_mlp1, m_w_mlp2, m_final_norm_w, v_c_ctx, v_w_mod, v_b_mod, v_norm1_w, v_norm2_w, v_w_in, v_attn_q_norm, v_attn_k_norm, v_ssd_conv_w, v_ssd_conv_b, v_ssd_dt_bias, v_ssd_a_log, v_ssd_d, v_ssd_norm_w, v_ret_log_decay, v_ret_gn_w, v_w_branch, v_w_out, v_w_mlp1, v_w_mlp2, v_final_norm_w):
    args = dict(locals())
    weights = {n: args[n] for n in ORDER}
    mom1 = {n: args["m_" + n] for n in ORDER}
    mom2 = {n: args["v_" + n] for n in ORDER}
    me = 4 * lax.axis_index("x") + 2 * lax.axis_index("y") + lax.axis_index("c")

    gathered = _all_gather([weights[n].astype(MXU_DTYPE) for n in BIG], "gather_weights")
    wq, params = _compute_weights(dict(zip(BIG, gathered)))
    conv_shape = ssd_conv_w.shape
    conv_all = _all_gather([_pack([ssd_conv_w.reshape(-1)], 8)], "gather_conv")[0]
    params["ssd_conv_w"] = jnp.concatenate(list(_unpack(conv_all, [conv_shape])[0]), axis=CONV_AXIS)
    for n in SMALL:
        params[n] = weights[n]

    loss, (gw, gx) = jax.value_and_grad(_local_loss, argnums=(0, 1))(params, x[0], c[0], ctx[0], loss_target[0], wq)
    loss = lax.psum(loss, MESH_AXES)

    g_send = _grad_shards(gw)
    by_core = [g_send[n].reshape((4, 2) + g_send[n].shape[1:]).swapaxes(0, 1).astype(jnp.bfloat16) for n in BIG]
    from_sibling = _pair_exchange(by_core, "scatter_grads_d2d")
    my_core = lax.axis_index("c")
    chip_sums = []
    for n, mine, theirs in zip(BIG, by_core, from_sibling):
        kept = lax.dynamic_index_in_dim(mine, my_core, 0, keepdims=False)
        chip_sums.append(_add_pair(kept.reshape(-1, kept.shape[-1]), theirs.reshape(-1, theirs.shape[-1]),
                                   "chip_sum_" + n).reshape(theirs.shape))
    g_recv = _chip_exchange(chip_sums, "scatter_grads_ici")
    result = {}
    for n, g8 in zip(BIG, g_recv):
        shape = weights[n].shape
        as2d = lambda a: a.reshape(-1, shape[-1])
        outs = _sum_adamw(g8.reshape(4, -1, shape[-1]), as2d(weights[n]), as2d(mom1[n]), as2d(mom2[n]), "adamw_" + n)
        for kind, arr in zip(("grad", "delta", "new_m", "new_v"), outs):
            result[kind, n] = arr.reshape(shape)

    conv_full_shape = params["ssd_conv_w"].shape
    small_shapes = [weights[n].shape for n in SMALL]
    partial = _pack([gw[n].reshape(-1) for n in SMALL] + [gw["ssd_conv_w"].reshape(-1)], 8)
    parts = _unpack(_all_gather([partial], "gather_small_grads")[0], small_shapes + [conv_full_shape])
    conv_part = lax.dynamic_slice_in_dim(parts[-1], me * conv_shape[CONV_AXIS], conv_shape[CONV_AXIS], CONV_AXIS + 1)
    small_names = list(SMALL) + ["ssd_conv_w"]
    g8_small = _pack([p.reshape(N_DEV, -1) for p in parts[:-1]] + [conv_part.reshape(N_DEV, -1)], 8)
    slabs = [_pack([d[n].reshape(-1) for n in small_names], 8) for d in (weights, mom1, mom2)]
    small_out = [_unpack(s, small_shapes + [conv_shape]) for s in _sum_adamw(g8_small, *slabs, "adamw_small")]
    for kind, small_k in zip(("grad", "delta", "new_m", "new_v"), small_out):
        for n, arr in zip(small_names, small_k):
            result[kind, n] = arr

    outs = [loss, gx[None]]
    for kind in ("grad", "delta", "new_m", "new_v"):
        outs += [result[kind, n] for n in ORDER]
    return tuple(outs)
```

```python
import functools
import math

import jax
import jax.numpy as jnp
from jax import lax
from jax.experimental import pallas as pl
from jax.experimental.pallas import tpu as pltpu

F32 = jnp.float32
MXU_DTYPE = jnp.bfloat16
VMEM_LIMIT_BYTES = 48 * 1024 * 1024
LANES = 128
N_DEV = 8
MESH_AXES = ("x", "y", "c")

D_MODEL = 1024
GRID_W = 64
NORM_EPS = 1e-6
ROPE_THETA = 10000.0
ATTN_HEADS, ATTN_KV_HEADS, ATTN_HEAD_DIM = 8, 2, 64
ATTN_GROUP = ATTN_HEADS // ATTN_KV_HEADS
SSD_HEADS, SSD_HEAD_DIM, SSD_GROUPS, SSD_STATE = 8, 64, 2, 128
SSD_D_INNER = SSD_HEADS * SSD_HEAD_DIM
RET_HEADS, RET_DK, RET_DV = 4, 128, 128
SCAN_CHUNK = 128
N_BRANCH = 3
DEPTH = 2

IN_SPLITS = (512, 128, 128, 512, 1024, 16, 512, 512, 512, 512, 3072)
IN_DIM = sum(IN_SPLITS)
DT_COLS = 16
DT_PAD = LANES - DT_COLS
TAIL_PAD = 128
IN_DIM_PADDED = IN_DIM + DT_PAD + TAIL_PAD
DT_END = sum(IN_SPLITS[:6])

ADAM_LR, ADAM_B1, ADAM_B2, ADAM_EPS, ADAM_WD, ADAM_STEP = 0.001, 0.9, 0.999, 1e-08, 0.01, 10


def _tile(dim, prefs):
    for p in prefs:
        if dim % p == 0:
            return p
    return dim


def _params(*sem):
    return pltpu.CompilerParams(dimension_semantics=sem, vmem_limit_bytes=VMEM_LIMIT_BYTES)


def _mm(a, b, *, ta=False, tb=False, out_shards=False, epilogue=None, extra=None, out_dtype=F32, name):
    if ta:
        kdim, m = a.shape
    else:
        m, kdim = a.shape
    b_shards = b.ndim == 3
    if b_shards:
        rows_b, cols_b = b.shape[1], N_DEV * b.shape[2]
    else:
        rows_b, cols_b = b.shape
    n, kdim_b = (rows_b, cols_b) if tb else (cols_b, rows_b)
    assert kdim == kdim_b, (a.shape, b.shape, ta, tb)
    tm = _tile(m, (1024, 768, 512, 256, 128))
    n_tile_of = n // N_DEV if (out_shards or (b_shards and not tb)) else n
    k_tile_of = kdim // N_DEV if (b_shards and tb) else kdim
    tn = _tile(n_tile_of, (1280, 1024, 768, 512, 384, 256, 128))
    tk = _tile(k_tile_of, (1024, 768, 512, 256, 128))
    dims = (((0 if ta else 1,), (1 if tb else 0,)), ((), ()))

    n_k = kdim // tk

    def body(a_ref, b_ref, *rest):
        o_ref = rest[-1]
        part = lax.dot_general(a_ref[...].astype(MXU_DTYPE), b_ref[...].astype(MXU_DTYPE), dims,
                               preferred_element_type=F32)

        def finish(acc):
            if epilogue == "sq_relu":
                acc = jnp.square(jnp.maximum(acc, 0.0))
            elif epilogue == "d_sq_relu":
                acc = acc * (2.0 * jnp.sqrt(rest[0][...].astype(F32)))
            return acc.astype(o_ref.dtype)

        if n_k == 1:
            o_ref[...] = finish(part)
            return

        @pl.when(pl.program_id(2) == 0)
        def _():
            o_ref[...] = part

        @pl.when(pl.program_id(2) > 0)
        def _():
            o_ref[...] += part

        if epilogue is not None:
            @pl.when(pl.program_id(2) == n_k - 1)
            def _():
                o_ref[...] = finish(o_ref[...])

    a_spec = pl.BlockSpec((tk, tm), lambda i, j, k: (k, i)) if ta else pl.BlockSpec((tm, tk), lambda i, j, k: (i, k))
    if not b_shards:
        b_spec = pl.BlockSpec((tn, tk), lambda i, j, k: (j, k)) if tb else pl.BlockSpec((tk, tn), lambda i, j, k: (k, j))
    elif tb:
        per = b.shape[2] // tk
        b_spec = pl.BlockSpec((None, tn, tk), lambda i, j, k: (k // per, j, k % per))
    else:
        per = b.shape[2] // tn
        b_spec = pl.BlockSpec((None, tk, tn), lambda i, j, k: (j // per, k, j % per))
    if out_shards:
        per_out = n // N_DEV // tn
        out_spec = pl.BlockSpec((None, tm, tn), lambda i, j, k: (j // per_out, i, j % per_out))
        out_shape = jax.ShapeDtypeStruct((N_DEV, m, n // N_DEV), out_dtype)
    else:
        out_spec = pl.BlockSpec((tm, tn), lambda i, j, k: (i, j))
        out_shape = jax.ShapeDtypeStruct((m, n), out_dtype)
    assert epilogue in (None, "sq_relu", "d_sq_relu") and (extra is not None) == (epilogue == "d_sq_relu")
    assert out_dtype == F32 or n_k == 1
    operands, in_specs = [a, b], [a_spec, b_spec]
    if extra is not None:
        assert not out_shards and extra.shape == (m, n)
        operands.append(extra)
        in_specs.append(out_spec)
    return pl.pallas_call(
        body, name=name,
        grid=(m // tm, n // tn, n_k),
        in_specs=in_specs,
        out_specs=out_spec,
        out_shape=out_shape,
        compiler_params=_params("parallel", "parallel", "arbitrary"),
    )(*operands)


@jax.custom_vjp
def matmul(a, b, b_grad):
    return _mm(a, b, name="mm_fwd")


def _matmul_fwd(a, b, b_grad):
    return _mm(a, b, name="mm_fwd"), (a, b)


def _matmul_bwd(res, g):
    a, b = res
    dw = _mm(a, g, ta=True, out_shards=b.ndim == 3, name="mm_dw")
    return _mm(g, b, tb=True, name="mm_dx"), jnp.zeros_like(b), dw


matmul.defvjp(_matmul_fwd, _matmul_bwd)


@jax.custom_vjp
def sq_relu_mlp(x, w1, w1_grad, w2, w2_grad):
    return _mm(_mm(x, w1, epilogue="sq_relu", out_dtype=MXU_DTYPE, name="mlp_up"), w2, name="mlp_down")


def _sq_relu_mlp_fwd(x, w1, w1_grad, w2, w2_grad):
    hid = _mm(x, w1, epilogue="sq_relu", out_dtype=MXU_DTYPE, name="mlp_up")
    return _mm(hid, w2, name="mlp_down"), (x, w1, w2, hid)


def _sq_relu_mlp_bwd(res, g):
    x, w1, w2, hid = res
    d_pre = _mm(g, w2, tb=True, epilogue="d_sq_relu", extra=hid, name="mlp_down_dx")
    dw2 = _mm(hid, g, ta=True, out_shards=w2.ndim == 3, name="mlp_down_dw")
    dw1 = _mm(x, d_pre, ta=True, out_shards=w1.ndim == 3, name="mlp_up_dw")
    return _mm(d_pre, w1, tb=True, name="mlp_up_dx"), jnp.zeros_like(w1), dw1, jnp.zeros_like(w2), dw2


sq_relu_mlp.defvjp(_sq_relu_mlp_fwd, _sq_relu_mlp_bwd)


NORM_TILE_PREFS = (768, 512, 256, 128)


def _norm_mod_pieces(h_ref, w_ref, shift_ref, scale_ref, tile, tm, ctx_len):
    x = h_ref[...]
    rstd = lax.rsqrt(jnp.mean(x * x, axis=1, keepdims=True) + NORM_EPS)
    xn = x * rstd
    is_ctx = tile * tm + lax.broadcasted_iota(jnp.int32, (tm, 1), 0) < ctx_len
    scale = jnp.where(is_ctx, scale_ref[1:2, :], scale_ref[0:1, :])
    shift = jnp.where(is_ctx, shift_ref[1:2, :], shift_ref[0:1, :])
    return xn, rstd, is_ctx, scale, shift


def _norm_mod_fwd_call(h, w, shift, scale, ctx_len):
    t, d = h.shape
    tm = _tile(t, NORM_TILE_PREFS)

    def body(h_ref, w_ref, shift_ref, scale_ref, u_ref):
        xn, _, _, sc, sh = _norm_mod_pieces(h_ref, w_ref, shift_ref, scale_ref, pl.program_id(0), tm, ctx_len)
        u_ref[...] = xn * w_ref[...] * (1.0 + sc) + sh

    row = pl.BlockSpec((tm, d), lambda i: (i, 0))
    return pl.pallas_call(
        body, name="norm_mod_fwd",
        grid=(t // tm,),
        in_specs=[row, pl.BlockSpec((1, d), lambda i: (0, 0)), pl.BlockSpec((2, d), lambda i: (0, 0)),
                  pl.BlockSpec((2, d), lambda i: (0, 0))],
        out_specs=row,
        out_shape=jax.ShapeDtypeStruct((t, d), F32),
        compiler_params=_params("parallel"),
    )(h, w, shift, scale)


def _norm_mod_bwd_call(h, w, shift, scale, du, ctx_len):
    t, d = h.shape
    tm = _tile(t, NORM_TILE_PREFS)

    def body(h_ref, w_ref, shift_ref, scale_ref, du_ref, dh_ref, sums_ref):
        xn, rstd, is_ctx, sc, _ = _norm_mod_pieces(h_ref, w_ref, shift_ref, scale_ref, pl.program_id(0), tm, ctx_len)
        du = du_ref[...]
        wv = w_ref[...]
        dy = du * (1.0 + sc)
        dxn = dy * wv
        dh_ref[...] = rstd * (dxn - xn * jnp.mean(dxn * xn, axis=1, keepdims=True))
        dsc = du * (xn * wv)

        def colsum(v):
            return jnp.sum(v, axis=0, keepdims=True)

        dshift_all, dshift_ctx = colsum(du), colsum(jnp.where(is_ctx, du, 0.0))
        dscale_all, dscale_ctx = colsum(dsc), colsum(jnp.where(is_ctx, dsc, 0.0))
        part = jnp.concatenate([colsum(dy * xn), dshift_all - dshift_ctx, dshift_ctx, dscale_all - dscale_ctx,
                                dscale_ctx, jnp.zeros((3, d), F32)], axis=0)

        @pl.when(pl.program_id(0) == 0)
        def _():
            sums_ref[...] = part

        @pl.when(pl.program_id(0) > 0)
        def _():
            sums_ref[...] += part

    row = pl.BlockSpec((tm, d), lambda i: (i, 0))
    return pl.pallas_call(
        body, name="norm_mod_bwd",
        grid=(t // tm,),
        in_specs=[row, pl.BlockSpec((1, d), lambda i: (0, 0)), pl.BlockSpec((2, d), lambda i: (0, 0)),
                  pl.BlockSpec((2, d), lambda i: (0, 0)), row],
        out_specs=[row, pl.BlockSpec((8, d), lambda i: (0, 0))],
        out_shape=[jax.ShapeDtypeStruct((t, d), F32), jax.ShapeDtypeStruct((8, d), F32)],
        compiler_params=_params("arbitrary"),
    )(h, w, shift, scale, du)


@functools.partial(jax.custom_vjp, nondiff_argnums=(4,))
def norm_mod(h, w, shift, scale, ctx_len):
    return _norm_mod_fwd_call(h, w[None, :], shift, scale, ctx_len)


def _norm_mod_fwd(h, w, shift, scale, ctx_len):
    return _norm_mod_fwd_call(h, w[None, :], shift, scale, ctx_len), (h, w, shift, scale)


def _norm_mod_bwd(ctx_len, res, du):
    h, w, shift, scale = res
    dh, sums = _norm_mod_bwd_call(h, w[None, :], shift, scale, du, ctx_len)
    return dh, sums[0], sums[1:3], sums[3:5]


norm_mod.defvjp(_norm_mod_fwd, _norm_mod_bwd)


def _bf(x):
    return x.astype(MXU_DTYPE)


def _dot(a, b):
    return jnp.dot(_bf(a), _bf(b), preferred_element_type=F32)


def _dot_nt(a, b):
    return lax.dot_general(_bf(a), _bf(b), (((1,), (1,)), ((), ())), preferred_element_type=F32)


def _dot_tn(a, b):
    return lax.dot_general(_bf(a), _bf(b), (((0,), (0,)), ((), ())), preferred_element_type=F32)


ROWWISE_VMEM_BYTES = 20 * 1024 * 1024


@functools.partial(jax.custom_vjp, nondiff_argnums=(1,))
def _split_lanes(x, n):
    w = x.shape[1] // n
    return tuple(x[:, i * w:(i + 1) * w] for i in range(n))


def _split_lanes_fwd(x, n):
    return _split_lanes(x, n), None


def _split_lanes_bwd(n, _, gs):
    return (jnp.concatenate(gs, axis=1),)


_split_lanes.defvjp(_split_lanes_fwd, _split_lanes_bwd)


def _rowwise_tile(t, widths):
    for tm in (768, 512, 256, 128):
        if t % tm == 0 and tm * 8 * sum(widths) <= ROWWISE_VMEM_BYTES:
            return tm
    raise ValueError((t, widths))


def _rowwise(name, fn, out_w, ctx_len):
    multi = isinstance(out_w, tuple)
    out_ws = out_w if multi else (out_w,)
    n_o = len(out_ws)

    def is_ctx(tm):
        return pl.program_id(0) * tm + lax.broadcasted_iota(jnp.int32, (tm, 1), 0) < ctx_len

    def specs(params, rows, tm):
        return ([pl.BlockSpec(p.shape, lambda i: (0, 0)) for p in params]
                + [pl.BlockSpec((tm, r.shape[1]), lambda i: (i, 0)) for r in rows])

    def fwd_call(params, rows):
        t = rows[0].shape[0]
        tm = _rowwise_tile(t, [r.shape[1] for r in rows] + list(out_ws))
        n_p = len(params)

        def body(*refs):
            vals = [r[...] for r in refs[:-n_o]]
            outs = fn(is_ctx(tm), tuple(vals[:n_p]), tuple(vals[n_p:]))
            for ref, v in zip(refs[-n_o:], outs if multi else (outs,)):
                ref[...] = v

        outs = pl.pallas_call(
            body, name=name + "_fwd", grid=(t // tm,),
            in_specs=specs(params, rows, tm),
            out_specs=[pl.BlockSpec((tm, w), lambda i: (i, 0)) for w in out_ws],
            out_shape=[jax.ShapeDtypeStruct((t, w), F32) for w in out_ws],
            compiler_params=_params("parallel"),
        )(*params, *rows)
        return tuple(outs) if multi else outs[0]

    def bwd_call(params, rows, dout):
        t = rows[0].shape[0]
        douts = tuple(dout) if multi else (dout,)
        tm = _rowwise_tile(t, [2 * r.shape[1] for r in rows] + list(out_ws))
        n_p, n_r = len(params), len(rows)
        n_in = n_p + n_r + n_o

        def body(*refs):
            vals = [r[...] for r in refs[:n_in]]
            dx_refs = refs[n_in:n_in + n_r]
            dp_refs = refs[n_in + n_r:]
            ctx_rows = is_ctx(tm)
            _, vjp = jax.vjp(lambda p, x: fn(ctx_rows, p, x), tuple(vals[:n_p]), tuple(vals[n_p:n_p + n_r]))
            dp, dx = vjp(tuple(vals[n_p + n_r:]) if multi else vals[-1])
            for ref, v in zip(dx_refs, dx):
                ref[...] = v

            @pl.when(pl.program_id(0) == 0)
            def _():
                for ref, v in zip(dp_refs, dp):
                    ref[...] = v

            @pl.when(pl.program_id(0) > 0)
            def _():
                for ref, v in zip(dp_refs, dp):
                    ref[...] += v

        row_specs = [pl.BlockSpec((tm, r.shape[1]), lambda i: (i, 0)) for r in rows]
        outs = pl.pallas_call(
            body, name=name + "_bwd", grid=(t // tm,),
            in_specs=specs(params, rows, tm) + [pl.BlockSpec((tm, w), lambda i: (i, 0)) for w in out_ws],
            out_specs=row_specs + [pl.BlockSpec(p.shape, lambda i: (0, 0)) for p in params],
            out_shape=[jax.ShapeDtypeStruct(r.shape, F32) for r in rows]
            + [jax.ShapeDtypeStruct(p.shape, F32) for p in params],
            compiler_params=_params("arbitrary"),
        )(*params, *rows, *douts)
        return tuple(outs[n_r:]), tuple(outs[:n_r])

    @jax.custom_vjp
    def op(params, rows):
        return fwd_call(params, rows)

    op.defvjp(lambda params, rows: (fwd_call(params, rows), (params, rows)),
              lambda res, g: bwd_call(res[0], res[1], g))
    return op


def _silu(x):
    return x * jax.nn.sigmoid(x)


def _ret_finish_tile(is_ctx, params, rows):
    (gn_w,), (y_f, y_b, gate) = params, rows
    heads = []
    for yh in _split_lanes(y_f + y_b, RET_HEADS):
        yc = yh - jnp.mean(yh, axis=1, keepdims=True)
        heads.append(yc * lax.rsqrt(jnp.mean(yc * yc, axis=1, keepdims=True) + NORM_EPS))
    return jnp.concatenate(heads, axis=1) * gn_w * _silu(gate)


def _ssd_finish_tile(is_ctx, params, rows):
    (d_skip, norm_w), (y_f, y_b, xs, z) = params, rows
    g = (y_f + y_b + d_skip * xs) * _silu(z)
    return g * lax.rsqrt(jnp.mean(g * g, axis=1, keepdims=True) + NORM_EPS) * norm_w


def _gate_merge_tile(is_ctx, params, rows):
    y0, y1, y2, logits = rows
    return sum(jax.nn.sigmoid(g) * y for g, y in zip(_split_lanes(logits, N_BRANCH), (y0, y1, y2)))


def _qk_prep_tile(is_ctx, params, rows):
    (q_w, k_w), (aq, ak, cos, sin) = params, rows
    cos, sin = lax.stop_gradient(cos), lax.stop_gradient(sin)
    out = []
    for x, norm_w, heads in ((aq, q_w, ATTN_HEADS), (ak, k_w, ATTN_KV_HEADS)):
        for xh in _split_lanes(x, heads):
            xn = xh * lax.rsqrt(jnp.mean(xh * xh, axis=1, keepdims=True) + NORM_EPS) * norm_w
            x1, x2 = _split_lanes(xn, 2)
            out += [x1 * cos - x2 * sin, x1 * sin + x2 * cos]
    return jnp.concatenate(out, axis=1)


def _ret_prep_tile(is_ctx, params, rows):
    rq, rk, cos, sin = rows
    cos, sin = lax.stop_gradient(cos), lax.stop_gradient(sin)
    out = []
    for x, scale in ((rq, 1.0), (rk, RET_DK ** -0.5)):
        for xh in _split_lanes(x, RET_HEADS):
            x1, x2 = _split_lanes(xh, 2)
            out += [(x1 * cos - x2 * sin) * scale, (x1 * sin + x2 * cos) * scale]
    return jnp.concatenate(out, axis=1)


def _softplus(z):
    e = jnp.exp(jnp.minimum(z, 0.0))
    series = e * (1.0 - e * (0.5 - e * (1.0 / 3.0)))
    return jnp.where(z < -5.0, series, jnp.maximum(z, 0.0) + jnp.log(1.0 + jnp.exp(-jnp.abs(z))))


def _ssd_prep_tile(is_ctx, params, rows):
    tap_prev, tap_mid, tap_next, conv_b, dt_bias, a_diag, expand_f, expand_b = params
    expand_f, expand_b = lax.stop_gradient(expand_f), lax.stop_gradient(expand_b)
    x, prev, nxt, dt_raw = rows
    xbc = _silu(prev * tap_prev + x * tap_mid + nxt * tap_next + conv_b)
    xs, rest = _split_lanes(xbc, 2)
    bm, cm = _split_lanes(rest, 2)
    dt = _softplus(dt_raw + dt_bias)
    hi = lax.Precision.HIGHEST
    dt_f = jnp.dot(dt, expand_f, precision=hi, preferred_element_type=F32)
    dt_b = jnp.dot(dt, expand_b, precision=hi, preferred_element_type=F32)
    return xs, bm, cm, xs * dt_f, xs * dt_b, jnp.dot(dt, a_diag, precision=hi, preferred_element_type=F32)


def _loss_rows_tile(is_ctx, params, rows):
    (norm_w,), (h, target) = params, rows
    y = h * lax.rsqrt(jnp.mean(h * h, axis=1, keepdims=True) + NORM_EPS) * norm_w
    err = jnp.mean(jnp.square(y - target), axis=1, keepdims=True)
    return jnp.broadcast_to(err, (h.shape[0], LANES))


def _residual_norm_tile(is_ctx, params, rows):
    (gate, norm_w, shift, scale), (h, update) = params, rows

    def pick(two_rows):
        return jnp.where(is_ctx, two_rows[1:2, :], two_rows[0:1, :])

    h_new = h + pick(gate) * update
    normed = h_new * lax.rsqrt(jnp.mean(h_new * h_new, axis=1, keepdims=True) + NORM_EPS) * norm_w
    return h_new, normed * (1.0 + pick(scale)) + pick(shift)


def _gated_residual_tile(is_ctx, params, rows):
    (gate,), (h, update) = params, rows
    return h + jnp.where(is_ctx, gate[1:2, :], gate[0:1, :]) * update


NEG_BIG = -1e30


def _attn_tiles(t, ctx_len, backward=False):
    tq = _tile(ctx_len, (256, 128))
    assert t % tq == 0 and ctx_len % tq == 0
    tk = _tile(t, (2816, 1408, 768, 512, 256, 128) if backward else (4224, 2816, 1408, 768, 512, 256, 128))
    return tq, tk


def _head_scores(q_ref, k_bf, g, ki, tk, ctx_len, masked):
    q = (q_ref[0, g] * (ATTN_HEAD_DIM ** -0.5)).astype(MXU_DTYPE)
    s = _dot_nt(q, k_bf)
    if masked:
        col = ki * tk + lax.broadcasted_iota(jnp.int32, s.shape, 1)
        s = jnp.where(col < ctx_len, s, NEG_BIG)
    return q, s


def _attn_cases(qi, ki, tq, tk, ctx_len, compute):
    ctx_q = (qi + 1) * tq <= ctx_len

    @pl.when(jnp.logical_not(ctx_q))
    def _():
        compute(False)

    @pl.when(jnp.logical_and(ctx_q, ki * tk < ctx_len))
    def _():
        compute(True)


def _attn_fwd_call(q, k, v, ctx_len):
    kvh, grp, t, hd = q.shape
    tq, tk = _attn_tiles(t, ctx_len)
    nkb = t // tk

    def body(q_ref, k_ref, v_ref, o_ref, lse_ref, m_sc, l_sc, acc_sc):
        qi, ki = pl.program_id(1), pl.program_id(2)

        @pl.when(ki == 0)
        def _():
            m_sc[...] = jnp.full(m_sc.shape, NEG_BIG, F32)
            l_sc[...] = jnp.zeros(l_sc.shape, F32)
            acc_sc[...] = jnp.zeros(acc_sc.shape, F32)

        def compute(masked):
            k_bf, v_bf = _bf(k_ref[0]), _bf(v_ref[0])
            for g in range(grp):
                _, s = _head_scores(q_ref, k_bf, g, ki, tk, ctx_len, masked)
                m_prev = m_sc[g]
                m_new = jnp.maximum(m_prev, jnp.max(s, axis=1, keepdims=True))
                alpha = jnp.exp(m_prev - m_new)
                p = jnp.exp(s - m_new)
                l_sc[g] = alpha * l_sc[g] + jnp.sum(p, axis=1, keepdims=True)
                acc_sc[g] = alpha * acc_sc[g] + _dot(p, v_bf)
                m_sc[g] = m_new

        _attn_cases(qi, ki, tq, tk, ctx_len, compute)

        @pl.when(ki == nkb - 1)
        def _():
            o_ref[0] = acc_sc[...] / l_sc[...]
            lse_ref[0] = m_sc[...] + jnp.log(l_sc[...])

    return pl.pallas_call(
        body, name="attn_fwd",
        grid=(kvh, t // tq, nkb),
        in_specs=[pl.BlockSpec((1, grp, tq, hd), lambda h, i, j: (h, 0, i, 0)),
                  pl.BlockSpec((1, tk, hd), lambda h, i, j: (h, j, 0)),
                  pl.BlockSpec((1, tk, hd), lambda h, i, j: (h, j, 0))],
        out_specs=[pl.BlockSpec((1, grp, tq, hd), lambda h, i, j: (h, 0, i, 0)),
                   pl.BlockSpec((1, grp, tq, 1), lambda h, i, j: (h, 0, i, 0))],
        out_shape=[jax.ShapeDtypeStruct(q.shape, F32), jax.ShapeDtypeStruct((kvh, grp, t, 1), F32)],
        scratch_shapes=[pltpu.VMEM((grp, tq, 1), F32), pltpu.VMEM((grp, tq, 1), F32), pltpu.VMEM((grp, tq, hd), F32)],
        compiler_params=_params("parallel", "parallel", "arbitrary"),
    )(q, k, v)


def _head_probs(q_ref, k_bf, v_bf, o_ref, do_ref, lse_ref, g, ki, tk, ctx_len, masked):
    q, s = _head_scores(q_ref, k_bf, g, ki, tk, ctx_len, masked)
    do = do_ref[0, g]
    delta = jnp.sum(do * o_ref[0, g], axis=1, keepdims=True)
    p = jnp.exp(s - lse_ref[0, g])
    do = _bf(do)
    ds = p * (_dot_nt(do, v_bf) - delta)
    return q, do, p, ds


def _attn_bwd_call(q, k, v, o, lse, do, ctx_len):
    kvh, grp, t, hd = q.shape
    tq, tk = _attn_tiles(t, ctx_len, backward=True)
    nqb, nkb = t // tq, t // tk

    def body(q_ref, k_ref, v_ref, o_ref, lse_ref, do_ref, dq_hbm, dk_ref, dv_ref, dq_sc, dk_sc, dv_sc, dq_out,
             dq_sem):
        hi, ki, qi = pl.program_id(0), pl.program_id(1), pl.program_id(2)
        rows = pl.ds(pl.multiple_of(qi * tq, tq), tq)

        @pl.when(ki == 0)
        def _():
            dq_sc[:, rows, :] = jnp.zeros((grp, tq, hd), F32)

        @pl.when(qi == 0)
        def _():
            dk_sc[...] = jnp.zeros(dk_sc.shape, F32)
            dv_sc[...] = jnp.zeros(dv_sc.shape, F32)

        def compute(masked):
            k_bf, v_bf = _bf(k_ref[0]), _bf(v_ref[0])
            dk_part = jnp.zeros(dk_sc.shape, F32)
            dv_part = jnp.zeros(dv_sc.shape, F32)
            for g in range(grp):
                qs, dob, p, ds = _head_probs(q_ref, k_bf, v_bf, o_ref, do_ref, lse_ref, g, ki, tk, ctx_len, masked)
                dv_part = dv_part + _dot_tn(p, dob)
                dk_part = dk_part + _dot_tn(ds, qs)
                dq_sc[g, rows, :] += _dot(ds, k_bf)
            dk_sc[...] += dk_part
            dv_sc[...] += dv_part

        _attn_cases(qi, ki, tq, tk, ctx_len, compute)

        @pl.when(ki == nkb - 1)
        def _():
            dq_out[...] = dq_sc[:, rows, :] * (hd ** -0.5)
            done = pltpu.make_async_copy(dq_out, dq_hbm.at[hi, :, rows, :], dq_sem)
            done.start()
            done.wait()

        @pl.when(qi == nqb - 1)
        def _():
            dk_ref[0] = dk_sc[...]
            dv_ref[0] = dv_sc[...]

    qspec = pl.BlockSpec((1, grp, tq, hd), lambda h, j, i: (h, 0, i, 0))
    kspec = pl.BlockSpec((1, tk, hd), lambda h, j, i: (h, j, 0))
    return pl.pallas_call(
        body, name="attn_bwd",
        grid=(kvh, t // tk, nqb),
        in_specs=[qspec, kspec, kspec, qspec, pl.BlockSpec((1, grp, tq, 1), lambda h, j, i: (h, 0, i, 0)), qspec],
        out_specs=[pl.BlockSpec(memory_space=pl.ANY), kspec, kspec],
        out_shape=[jax.ShapeDtypeStruct(q.shape, F32), jax.ShapeDtypeStruct(k.shape, F32),
                   jax.ShapeDtypeStruct(v.shape, F32)],
        scratch_shapes=[pltpu.VMEM((grp, t, hd), F32), pltpu.VMEM((tk, hd), F32), pltpu.VMEM((tk, hd), F32),
                        pltpu.VMEM((grp, tq, hd), F32), pltpu.SemaphoreType.DMA],
        compiler_params=_params("arbitrary", "arbitrary", "arbitrary"),
    )(q, k, v, o, lse, do)


@functools.partial(jax.custom_vjp, nondiff_argnums=(3,))
def attention(q, k, v, ctx_len):
    return _attn_fwd_call(q, k, v, ctx_len)[0]


def _attention_fwd(q, k, v, ctx_len):
    o, lse = _attn_fwd_call(q, k, v, ctx_len)
    return o, (q, k, v, o, lse)


def _attention_bwd(ctx_len, res, do):
    q, k, v, o, lse = res
    return tuple(_attn_bwd_call(q, k, v, o, lse, do, ctx_len))


attention.defvjp(_attention_fwd, _attention_bwd)


def _chunk_order(step, n_chunks, n_ctx_chunks, reverse):
    if not reverse:
        return step
    return jnp.where(step < n_ctx_chunks, n_ctx_chunks - 1 - step, n_chunks + n_ctx_chunks - 1 - step)


def _scan_masks(chunk, reverse):
    row = lax.broadcasted_iota(jnp.int32, (chunk, chunk), 0)
    col = lax.broadcasted_iota(jnp.int32, (chunk, chunk), 1)
    vis = (col >= row) if reverse else (col <= row)
    vis_t = (row >= col) if reverse else (row <= col)
    return vis, vis.astype(F32), vis_t.astype(F32)


def _cum_decay(a_col, a_row, vis_f):
    hi = lax.Precision.HIGHEST
    cum_col = jnp.dot(vis_f, a_col, precision=hi, preferred_element_type=F32)
    cum_row = lax.dot_general(a_row, vis_f, (((1,), (1,)), ((), ())), precision=hi, preferred_element_type=F32)
    total = jnp.sum(a_col, axis=0, keepdims=True)
    return cum_col, cum_row, total


SCAN_CHUNKS_PER_STEP = 2


def _scan_specs(t, ctx_len, reverse, backward, widths):
    rows = SCAN_CHUNK * SCAN_CHUNKS_PER_STEP
    assert t % rows == 0 and ctx_len % rows == 0
    n_blocks, n_ctx_blocks = t // rows, ctx_len // rows

    def order(i):
        step = (n_blocks - 1 - i) if backward else i
        return _chunk_order(step, n_blocks, n_ctx_blocks, reverse)

    halves = list(range(SCAN_CHUNKS_PER_STEP))
    if reverse != backward:
        halves.reverse()
    return [pl.BlockSpec((rows, w), lambda i: (order(i), 0)) for w in widths], order, n_blocks, halves


def _scan_fwd_call(q, k, v, a_col, a_row, *, groups, per_group, dk, dv, ctx_len, reverse):
    t = q.shape[0]
    chunk, per_step = SCAN_CHUNK, SCAN_CHUNKS_PER_STEP
    heads = groups * per_group
    (q_spec, k_spec, v_spec, acol_spec), order, n_blocks, halves = _scan_specs(
        t, ctx_len, reverse, False, (groups * dk, groups * dk, heads * dv, LANES))

    def body(q_ref, k_ref, v_ref, acol_ref, arow_ref, y_ref, st_ref, s_sc):
        @pl.when(pl.program_id(0) == 0)
        def _():
            s_sc[...] = jnp.zeros(s_sc.shape, F32)

        vis, vis_f, _ = _scan_masks(chunk, reverse)
        for half in halves:
            rows = slice(half * chunk, (half + 1) * chunk)
            st_ref[half] = s_sc[...]
            cum_col, cum_row, total = _cum_decay(acol_ref[rows, :], arow_ref[:, rows], vis_f)
            for g in range(groups):
                qg = q_ref[rows, g * dk:(g + 1) * dk]
                kg = k_ref[rows, g * dk:(g + 1) * dk]
                qk = _dot_nt(qg, kg)
                for r in range(per_group):
                    h = g * per_group + r
                    ccol = cum_col[:, h:h + 1]
                    decay = jnp.exp(jnp.where(vis, ccol - cum_row[h:h + 1, :], NEG_BIG))
                    vh = v_ref[rows, h * dv:(h + 1) * dv]
                    s_in = s_sc[h]
                    y = _dot(qk * decay, vh) + jnp.exp(ccol) * _dot(qg, s_in)
                    y_ref[rows, h * dv:(h + 1) * dv] = y
                    tot = total[:, h:h + 1]
                    s_sc[h] = jnp.exp(tot) * s_in + _dot_tn(kg * jnp.exp(tot - ccol), vh)

    return pl.pallas_call(
        body, name="scan_fwd",
        grid=(n_blocks,),
        in_specs=[q_spec, k_spec, v_spec, acol_spec, pl.BlockSpec((8, chunk * per_step), lambda i: (0, order(i)))],
        out_specs=[v_spec, pl.BlockSpec((per_step, heads, dk, dv), lambda i: (order(i), 0, 0, 0))],
        out_shape=[jax.ShapeDtypeStruct(v.shape, F32),
                   jax.ShapeDtypeStruct((n_blocks * per_step, heads, dk, dv), F32)],
        scratch_shapes=[pltpu.VMEM((heads, dk, dv), F32)],
        compiler_params=_params("arbitrary"),
    )(q, k, v, a_col, a_row)


def _scan_bwd_call(q, k, v, a_col, a_row, states, dy, *, groups, per_group, dk, dv, ctx_len, reverse):
    t = q.shape[0]
    chunk, per_step = SCAN_CHUNK, SCAN_CHUNKS_PER_STEP
    heads = groups * per_group
    (q_spec, k_spec, v_spec, acol_spec), order, n_blocks, halves = _scan_specs(
        t, ctx_len, reverse, True, (groups * dk, groups * dk, heads * dv, LANES))
    arow_spec = pl.BlockSpec((8, chunk * per_step), lambda i: (0, order(i)))
    last = 0 if reverse else chunk - 1

    def body(q_ref, k_ref, v_ref, acol_ref, arow_ref, st_ref, dy_ref, dq_ref, dk_ref, dv_ref, da_ref, dat_ref,
             ds_sc):
        @pl.when(pl.program_id(0) == 0)
        def _():
            ds_sc[...] = jnp.zeros(ds_sc.shape, F32)

        vis, vis_f, vis_tf = _scan_masks(chunk, reverse)
        lane = lax.broadcasted_iota(jnp.int32, (chunk, LANES), 1)
        row = lax.broadcasted_iota(jnp.int32, (chunk, LANES), 0)
        sub = lax.broadcasted_iota(jnp.int32, (8, chunk), 0)
        for half in halves:
            rows = slice(half * chunk, (half + 1) * chunk)
            cum_col, cum_row, total = _cum_decay(acol_ref[rows, :], arow_ref[:, rows], vis_f)
            dcum = jnp.zeros((chunk, LANES), F32)
            dcum_t = jnp.zeros((8, chunk), F32)
            for g in range(groups):
                qg = q_ref[rows, g * dk:(g + 1) * dk]
                kg = k_ref[rows, g * dk:(g + 1) * dk]
                qk = _dot_nt(qg, kg)
                dq_g = jnp.zeros((chunk, dk), F32)
                dk_g = jnp.zeros((chunk, dk), F32)
                for r in range(per_group):
                    h = g * per_group + r
                    ccol = cum_col[:, h:h + 1]
                    decay = jnp.exp(jnp.where(vis, ccol - cum_row[h:h + 1, :], NEG_BIG))
                    vh = v_ref[rows, h * dv:(h + 1) * dv]
                    dyh = dy_ref[rows, h * dv:(h + 1) * dv]
                    s_in = st_ref[half, h]
                    ds_out = ds_sc[h]
                    tot = total[:, h:h + 1]
                    e_in = jnp.exp(ccol)
                    e_out = jnp.exp(tot - ccol)
                    e_tot = jnp.exp(tot)
                    k_out = kg * e_out
                    dv_ref[rows, h * dv:(h + 1) * dv] = _dot_tn(qk * decay, dyh) + _dot(k_out, ds_out)
                    dqk = _dot_nt(dyh, vh) * decay
                    dq_in = e_in * _dot_nt(dyh, s_in)
                    dk_out = e_out * _dot_nt(vh, ds_out)
                    dq_h = _dot(dqk, kg) + dq_in
                    dk_h = _dot_tn(dqk, qg) + dk_out
                    s_out = e_tot * s_in + _dot_tn(k_out, vh)
                    edge = jnp.sum(jnp.sum(s_out * ds_out, axis=1, keepdims=True), axis=0, keepdims=True)
                    w_seg = dqk * qk
                    dcum_h = (jnp.sum(w_seg, axis=1, keepdims=True) + jnp.sum(dq_in * qg, axis=1, keepdims=True)
                              - jnp.sum(dk_out * kg, axis=1, keepdims=True))
                    dcum = jnp.where(lane == h, dcum_h + jnp.where(row == last, edge, 0.0), dcum)
                    dcum_t = jnp.where(sub == h, -jnp.sum(w_seg, axis=0, keepdims=True), dcum_t)
                    ds_sc[h] = e_tot * ds_out + _dot_tn(qg, e_in * dyh)
                    dq_g = dq_g + dq_h
                    dk_g = dk_g + dk_h
                dq_ref[rows, g * dk:(g + 1) * dk] = dq_g
                dk_ref[rows, g * dk:(g + 1) * dk] = dk_g
            hi = lax.Precision.HIGHEST
            da_ref[rows, :] = jnp.dot(vis_tf, dcum, precision=hi, preferred_element_type=F32)
            dat_ref[:, rows] = jnp.dot(dcum_t, vis_f, precision=hi, preferred_element_type=F32)

    return pl.pallas_call(
        body, name="scan_bwd",
        grid=(n_blocks,),
        in_specs=[q_spec, k_spec, v_spec, acol_spec, arow_spec,
                  pl.BlockSpec((per_step, heads, dk, dv), lambda i: (order(i), 0, 0, 0)), v_spec],
        out_specs=[q_spec, k_spec, v_spec, acol_spec, arow_spec],
        out_shape=[jax.ShapeDtypeStruct(q.shape, F32), jax.ShapeDtypeStruct(k.shape, F32),
                   jax.ShapeDtypeStruct(v.shape, F32), jax.ShapeDtypeStruct((t, LANES), F32),
                   jax.ShapeDtypeStruct((8, t), F32)],
        scratch_shapes=[pltpu.VMEM((heads, dk, dv), F32)],
        compiler_params=_params("arbitrary"),
    )(q, k, v, a_col, a_row, states, dy)


def _decay_layouts(a):
    t, heads = a.shape
    a_col = jnp.pad(a, ((0, 0), (0, LANES - heads)))
    a_row = jnp.pad(a.T, ((0, 8 - heads), (0, 0)))
    return a_col, a_row


@functools.partial(jax.custom_vjp, nondiff_argnums=(4,))
def linear_scan(q, k, v, a, cfg):
    a_col, a_row = _decay_layouts(a)
    return _scan_fwd_call(q, k, v, a_col, a_row, **dict(cfg))[0]


def _linear_scan_fwd(q, k, v, a, cfg):
    a_col, a_row = _decay_layouts(a)
    y, states = _scan_fwd_call(q, k, v, a_col, a_row, **dict(cfg))
    return y, (q, k, v, a, states)


def _linear_scan_bwd(cfg, res, dy):
    q, k, v, a, states = res
    a_col, a_row = _decay_layouts(a)
    dq, dk, dv, da, da_t = _scan_bwd_call(q, k, v, a_col, a_row, states, dy, **dict(cfg))
    heads = a.shape[1]
    return dq, dk, dv, da[:, :heads] + da_t[:heads].T


linear_scan.defvjp(_linear_scan_fwd, _linear_scan_bwd)


def _scan_cfg(groups, per_group, dk, dv, ctx_len, reverse):
    return (("groups", groups), ("per_group", per_group), ("dk", dk), ("dv", dv), ("ctx_len", ctx_len),
            ("reverse", reverse))


def _axial_tables(n_lat, n_ctx):
    freqs = ATTN_HEAD_DIM // 4
    rows = n_lat // GRID_W
    row = jnp.repeat(jnp.arange(rows, dtype=F32), GRID_W)
    col = jnp.tile(jnp.arange(GRID_W, dtype=F32), rows)
    inv = ROPE_THETA ** (-jnp.arange(freqs, dtype=F32) / freqs)
    ang = jnp.concatenate([row[:, None] * inv, col[:, None] * inv], axis=-1)
    cos = jnp.concatenate([jnp.ones((n_ctx, 2 * freqs), F32), jnp.cos(ang)], axis=0)
    sin = jnp.concatenate([jnp.zeros((n_ctx, 2 * freqs), F32), jnp.sin(ang)], axis=0)
    return cos, sin


def _seq_tables(t):
    pos = jnp.arange(t, dtype=F32)
    inv = ROPE_THETA ** (-jnp.linspace(0.0, 1.0, RET_DK // 2, dtype=F32))
    ang = pos[:, None] * inv
    return jnp.cos(ang), jnp.sin(ang)


def _pad_w_in(w_in):
    d = w_in.shape[0]
    return jnp.concatenate([w_in[:, :DT_END], jnp.zeros((d, DT_PAD), w_in.dtype), w_in[:, DT_END:],
                            jnp.zeros((d, TAIL_PAD), w_in.dtype)], axis=1)


def _split_proj(p):
    widths = list(IN_SPLITS)
    widths[5] = LANES
    out, off = [], 0
    for w in widths:
        out.append(p[:, off:off + w])
        off += w
    out[5] = out[5][:, :DT_COLS]
    return out


def _mixer(u, w, wq, layer, n_ctx, tables):
    t = u.shape[0]
    attn_rope, ret_rope, seg_first, seg_last = tables
    proj = matmul(u, wq["w_in"][layer], _pad_w_in(w["w_in"][layer]))
    aq, ak, av, z, xbc_raw, dt_raw, rq, rk, rv, rg, gate_logits = _split_proj(proj)

    q_width = ATTN_HEADS * ATTN_HEAD_DIM
    qk = _rowwise("qk_prep", _qk_prep_tile, q_width + ATTN_KV_HEADS * ATTN_HEAD_DIM, n_ctx)(
        (w["attn_q_norm"][layer][None, :], w["attn_k_norm"][layer][None, :]), (aq, ak) + attn_rope)
    q4 = qk[:, :q_width].reshape(t, ATTN_KV_HEADS, ATTN_GROUP, ATTN_HEAD_DIM).transpose(1, 2, 0, 3)
    k3 = qk[:, q_width:].reshape(t, ATTN_KV_HEADS, ATTN_HEAD_DIM).transpose(1, 0, 2)
    v3 = av.reshape(t, ATTN_KV_HEADS, ATTN_HEAD_DIM).transpose(1, 0, 2)
    o4 = attention(q4, k3, v3, n_ctx)
    br_attn = o4.transpose(2, 0, 1, 3).reshape(t, ATTN_HEADS * ATTN_HEAD_DIM)

    cw, cb = w["ssd_conv_w"][layer], w["ssd_conv_b"][layer]
    zero_row = jnp.zeros((1, xbc_raw.shape[1]), F32)
    prev = jnp.concatenate([zero_row, xbc_raw[:-1]], axis=0) * (1.0 - seg_first)
    nxt = jnp.concatenate([xbc_raw[1:], zero_row], axis=0) * (1.0 - seg_last)
    n_dt = 2 * SSD_HEADS
    a_neg = -jnp.exp(w["ssd_a_log"][layer]).reshape(n_dt)
    a_diag = jnp.where(jnp.arange(n_dt)[:, None] == jnp.arange(LANES)[None, :], a_neg[:, None], 0.0)
    head_of_lane = jnp.arange(SSD_D_INNER) // SSD_HEAD_DIM
    expand = [(jnp.arange(n_dt)[:, None] == head_of_lane[None, :] + d * SSD_HEADS).astype(F32) for d in range(2)]
    xs, bm, cm, v_fwd, v_bwd, decays = _rowwise(
        "ssd_prep", _ssd_prep_tile, (SSD_D_INNER, SSD_GROUPS * SSD_STATE, SSD_GROUPS * SSD_STATE, SSD_D_INNER,
                                     SSD_D_INNER, LANES), n_ctx)(
        (cw[0:1], cw[1:2], cw[2:3], cb[None, :], w["ssd_dt_bias"][layer].reshape(1, n_dt), a_diag, expand[0],
         expand[1]), (xbc_raw, prev, nxt, dt_raw))
    y_ssd = []
    for d, (reverse, v_d) in enumerate(((False, v_fwd), (True, v_bwd))):
        cfg = _scan_cfg(SSD_GROUPS, SSD_HEADS // SSD_GROUPS, SSD_STATE, SSD_HEAD_DIM, n_ctx, reverse)
        y_ssd.append(linear_scan(cm, bm, v_d, decays[:, d * SSD_HEADS:(d + 1) * SSD_HEADS], cfg))
    d_skip = jnp.repeat(w["ssd_d"][layer], SSD_HEAD_DIM)[None, :]
    br_ssd = _rowwise("ssd_finish", _ssd_finish_tile, SSD_D_INNER, n_ctx)(
        (d_skip, w["ssd_norm_w"][layer][None, :]), (y_ssd[0], y_ssd[1], xs, z))

    lg = -jnp.exp(w["ret_log_decay"][layer])
    ret_width = RET_HEADS * RET_DK
    ret_qk = _rowwise("ret_prep", _ret_prep_tile, 2 * ret_width, n_ctx)((), (rq, rk) + ret_rope)
    rq_r, rk_r = ret_qk[:, :ret_width], ret_qk[:, ret_width:]
    y_ret = []
    for d, reverse in ((0, False), (1, True)):
        cfg = _scan_cfg(RET_HEADS, 1, RET_DK, RET_DV, n_ctx, reverse)
        y_ret.append(linear_scan(rq_r, rk_r, rv, jnp.broadcast_to(lg[d][None, :], (t, RET_HEADS)), cfg))
    br_ret = _rowwise("ret_finish", _ret_finish_tile, RET_HEADS * RET_DV, n_ctx)(
        (w["ret_gn_w"][layer][None, :],), (y_ret[0], y_ret[1], rg))

    projected = tuple(matmul(br, wq["w_branch"][layer][j], w["w_branch"][layer][j])
                      for j, br in enumerate((br_attn, br_ssd, br_ret)))
    merged = _rowwise("gate_merge", _gate_merge_tile, D_MODEL, n_ctx)((), projected + (gate_logits,))
    return matmul(merged, wq["w_out"][layer], w["w_out"][layer])


def _local_loss(w, x, c, ctx, target, wq):
    n, m = x.shape[0], ctx.shape[0]
    t = n + m
    pos = jnp.arange(t)[:, None]
    seg_first = ((pos == 0) | (pos == m)).astype(F32)
    seg_last = ((pos == m - 1) | (pos == t - 1)).astype(F32)
    tables = (_axial_tables(n, m), _seq_tables(t), seg_first, seg_last)
    h = jnp.concatenate([ctx, x], axis=0)
    cond = jax.nn.silu(jnp.stack([c, w["c_ctx"]], axis=0))
    cond8 = jnp.concatenate([cond, jnp.zeros((6, D_MODEL), F32)], axis=0)
    u = None
    for layer in range(DEPTH):
        mod = (matmul(cond8, wq["w_mod"][layer], w["w_mod"][layer])[:2] + w["b_mod"][layer]).reshape(2, 6, D_MODEL)
        if u is None:
            u = norm_mod(h, w["norm1_w"][layer], mod[:, 0], mod[:, 1], m)
        mix = _mixer(u, w, wq, layer, m, tables)
        h, v = _rowwise("residual_norm", _residual_norm_tile, (D_MODEL, D_MODEL), m)(
            (mod[:, 2], w["norm2_w"][layer][None, :], mod[:, 3], mod[:, 4]), (h, mix))
        mlp = sq_relu_mlp(v, wq["w_mlp1"][layer], w["w_mlp1"][layer], wq["w_mlp2"][layer], w["w_mlp2"][layer])
        if layer + 1 < DEPTH:
            nxt = (matmul(cond8, wq["w_mod"][layer + 1], w["w_mod"][layer + 1])[:2]
                   + w["b_mod"][layer + 1]).reshape(2, 6, D_MODEL)
            h, u = _rowwise("residual_norm", _residual_norm_tile, (D_MODEL, D_MODEL), m)(
                (mod[:, 5], w["norm1_w"][layer + 1][None, :], nxt[:, 0], nxt[:, 1]), (h, mlp))
        else:
            h = _rowwise("gated_residual", _gated_residual_tile, D_MODEL, m)((mod[:, 5],), (h, mlp))
    per_token = _rowwise("loss_rows", _loss_rows_tile, LANES, 0)((w["final_norm_w"][None, :],), (h[m:], target))
    return 0.5 * jnp.sum(per_token[:, 0])


def _coords():
    return lax.axis_index("x"), lax.axis_index("y"), lax.axis_index("c")


def _all_gather(blocks, name):
    n = len(blocks)

    def body(*refs):
        x_refs, out_refs = refs[:n], refs[n:2 * n]
        send_sems, recv_sems, local_sems = refs[2 * n:]
        x, y, c = _coords()
        me, sibling = (x, y, c), (x, y, 1 - c)
        chips = [(1 - x, y), (x, 1 - y), (1 - x, 1 - y)]

        def copy(k, i, blk, to, from_input=False):
            slot = out_refs[i].at[4 * blk[0] + 2 * blk[1] + blk[2]]
            return pltpu.make_async_remote_copy(
                src_ref=x_refs[i] if from_input else slot, dst_ref=slot,
                send_sem=send_sems.at[k * n + i], recv_sem=recv_sems.at[k * n + i],
                device_id=to, device_id_type=pl.DeviceIdType.MESH)

        mine = [pltpu.make_async_copy(x_refs[i], out_refs[i].at[4 * x + 2 * y + c], local_sems.at[i])
                for i in range(n)]
        for cp in mine:
            cp.start()
        first = [copy(0, i, me, sibling, True) for i in range(n)]
        first += [copy(1 + j, i, me, (*chip, c), True) for j, chip in enumerate(chips) for i in range(n)]
        for cp in first:
            cp.start()
        passed = []
        for j, chip in enumerate(chips):
            for i in range(n):
                copy(1 + j, i, (*chip, c), me).wait_recv()
                passed.append(copy(4 + j, i, (*chip, c), sibling))
                passed[-1].start()
        for i in range(n):
            copy(0, i, sibling, me).wait_recv()
        for j, chip in enumerate(chips):
            for i in range(n):
                copy(4 + j, i, (*chip, 1 - c), me).wait_recv()
        for cp in first + passed:
            cp.wait_send()
        for cp in mine:
            cp.wait()

    return pl.pallas_call(
        body, name=name,
        out_shape=[jax.ShapeDtypeStruct((N_DEV,) + b.shape, b.dtype) for b in blocks],
        in_specs=[pl.BlockSpec(memory_space=pl.ANY)] * n,
        out_specs=[pl.BlockSpec(memory_space=pl.ANY)] * n,
        scratch_shapes=[pltpu.SemaphoreType.DMA((7 * n,)), pltpu.SemaphoreType.DMA((7 * n,)),
                        pltpu.SemaphoreType.DMA((n,))],
    )(*blocks)


def _pair_exchange(arrays, name):
    n = len(arrays)

    def body(*refs):
        g_refs, out_refs = refs[:n], refs[n:2 * n]
        send_sems, recv_sems = refs[2 * n:]
        x, y, c = _coords()
        copies = [pltpu.make_async_remote_copy(
            src_ref=g_refs[i].at[1 - c], dst_ref=out_refs[i], send_sem=send_sems.at[i], recv_sem=recv_sems.at[i],
            device_id=(x, y, 1 - c), device_id_type=pl.DeviceIdType.MESH) for i in range(n)]
        for cp in copies:
            cp.start()
        for cp in copies:
            cp.wait()

    return pl.pallas_call(
        body, name=name,
        out_shape=[jax.ShapeDtypeStruct(a.shape[1:], a.dtype) for a in arrays],
        in_specs=[pl.BlockSpec(memory_space=pl.ANY)] * n,
        out_specs=[pl.BlockSpec(memory_space=pl.ANY)] * n,
        scratch_shapes=[pltpu.SemaphoreType.DMA((n,)), pltpu.SemaphoreType.DMA((n,))],
    )(*arrays)


def _chip_exchange(arrays, name):
    n = len(arrays)

    def body(*refs):
        g_refs, out_refs = refs[:n], refs[n:2 * n]
        send_sems, recv_sems, local_sems = refs[2 * n:]
        x, y, c = _coords()
        me = 2 * x + y
        mine = [pltpu.make_async_copy(g_refs[i].at[me], out_refs[i].at[me], local_sems.at[i]) for i in range(n)]
        for cp in mine:
            cp.start()
        copies = []
        for k in range(1, 4):
            px, py = (1 - x if (k >> 1) & 1 else x), (1 - y if k & 1 else y)
            for i in range(n):
                copies.append(pltpu.make_async_remote_copy(
                    src_ref=g_refs[i].at[2 * px + py], dst_ref=out_refs[i].at[me],
                    send_sem=send_sems.at[(k - 1) * n + i], recv_sem=recv_sems.at[(k - 1) * n + i],
                    device_id=(px, py, c), device_id_type=pl.DeviceIdType.MESH))
        for cp in copies:
            cp.start()
        for cp in copies:
            cp.wait_recv()
        for cp in copies:
            cp.wait_send()
        for cp in mine:
            cp.wait()

    return pl.pallas_call(
        body, name=name,
        out_shape=[jax.ShapeDtypeStruct(a.shape, a.dtype) for a in arrays],
        in_specs=[pl.BlockSpec(memory_space=pl.ANY)] * n,
        out_specs=[pl.BlockSpec(memory_space=pl.ANY)] * n,
        scratch_shapes=[pltpu.SemaphoreType.DMA((3 * n,)), pltpu.SemaphoreType.DMA((3 * n,)),
                        pltpu.SemaphoreType.DMA((n,))],
    )(*arrays)


def _add_pair(a, b, name):
    rows, cols = a.shape
    tr = _tile(rows, [r for r in (2048, 1024, 512, 256, 128, 64, 32, 16) if r * cols <= 2 * ADAMW_BLOCK_ELEMS])

    def body(a_ref, b_ref, o_ref):
        o_ref[...] = (a_ref[...].astype(F32) + b_ref[...].astype(F32)).astype(o_ref.dtype)

    spec = pl.BlockSpec((tr, cols), lambda i: (i, 0))
    return pl.pallas_call(
        body, name=name, grid=(rows // tr,), in_specs=[spec, spec], out_specs=spec,
        out_shape=jax.ShapeDtypeStruct(a.shape, a.dtype), compiler_params=_params("parallel"),
    )(a, b)


ADAMW_BLOCK_ELEMS = 256 * 1024


def _sum_adamw(g8, w, m, v, name):
    rows, cols = w.shape
    tr = _tile(rows, [r for r in (2048, 1024, 512, 256, 128, 64, 32, 16) if r * cols <= ADAMW_BLOCK_ELEMS])

    def body(g_ref, w_ref, m_ref, v_ref, go_ref, d_ref, mo_ref, vo_ref):
        g = g_ref[0].astype(F32)
        for s in range(1, g8.shape[0]):
            g = g + g_ref[s].astype(F32)
        m_new = ADAM_B1 * m_ref[...] + (1.0 - ADAM_B1) * g
        v_new = ADAM_B2 * v_ref[...] + (1.0 - ADAM_B2) * (g * g)
        m_hat = m_new / (1.0 - ADAM_B1 ** ADAM_STEP)
        v_hat = v_new / (1.0 - ADAM_B2 ** ADAM_STEP)
        go_ref[...] = g
        d_ref[...] = -ADAM_LR * (m_hat / (jnp.sqrt(v_hat) + ADAM_EPS) + ADAM_WD * w_ref[...])
        mo_ref[...] = m_new
        vo_ref[...] = v_new

    spec = pl.BlockSpec((tr, cols), lambda i: (i, 0))
    shape = jax.ShapeDtypeStruct((rows, cols), F32)
    return pl.pallas_call(
        body, name=name,
        grid=(rows // tr,),
        in_specs=[pl.BlockSpec((g8.shape[0], tr, cols), lambda i: (0, i, 0)), spec, spec, spec],
        out_specs=[spec, spec, spec, spec],
        out_shape=[shape, shape, shape, shape],
        compiler_params=_params("parallel"),
    )(g8, w, m, v)


BIG = ("w_mod", "w_in", "w_branch", "w_out", "w_mlp1", "w_mlp2")
COL_SHARDED = ("w_mod", "w_mlp1")
ROW_SHARDED = ("w_out", "w_mlp2")
SMALL = ("c_ctx", "b_mod", "norm1_w", "norm2_w", "attn_q_norm", "attn_k_norm", "ssd_conv_b", "ssd_dt_bias",
         "ssd_a_log", "ssd_d", "ssd_norm_w", "ret_log_decay", "ret_gn_w", "final_norm_w")
CONV_AXIS = 2
ORDER = ("c_ctx", "w_mod", "b_mod", "norm1_w", "norm2_w", "w_in", "attn_q_norm", "attn_k_norm", "ssd_conv_w",
         "ssd_conv_b", "ssd_dt_bias", "ssd_a_log", "ssd_d", "ssd_norm_w", "ret_log_decay", "ret_gn_w", "w_branch",
         "w_out", "w_mlp1", "w_mlp2", "final_norm_w")


def _compute_weights(gathered):
    wq, carrier = {}, {}
    for name in BIG:
        g = gathered[name]
        per_layer = []
        for layer in range(DEPTH):
            gl = g[:, layer]
            if name in COL_SHARDED:
                per_layer.append(gl)
            elif name in ROW_SHARDED:
                per_layer.append(gl.reshape(-1, gl.shape[-1]))
            elif name == "w_in":
                per_layer.append(_pad_w_in(jnp.concatenate([gl[d] for d in range(N_DEV)], axis=-1)))
            else:
                per_layer.append([jnp.concatenate([gl[d, j] for d in range(N_DEV)], axis=-1) for j in range(N_BRANCH)])
        wq[name] = per_layer
    for name in BIG:
        if name == "w_in":
            carrier[name] = [jnp.zeros((D_MODEL, IN_DIM), F32) for _ in range(DEPTH)]
        else:
            carrier[name] = jax.tree.map(lambda a: jnp.zeros(a.shape, F32), wq[name])
    return wq, carrier


def _grad_shards(gw):
    out = {}
    for name in BIG:
        per_layer = []
        for layer in range(DEPTH):
            g = gw[name][layer]
            if name in COL_SHARDED:
                per_layer.append(g)
            elif name in ROW_SHARDED:
                per_layer.append(g.reshape(N_DEV, -1, g.shape[-1]))
            elif name == "w_in":
                size = IN_DIM // N_DEV
                per_layer.append(jnp.stack([g[:, d * size:(d + 1) * size] for d in range(N_DEV)]))
            else:
                per_layer.append(jnp.stack([gj.reshape(gj.shape[0], N_DEV, -1).transpose(1, 0, 2) for gj in g], axis=1))
        out[name] = jnp.stack(per_layer, axis=1)
    return out


def _pack(arrays, row_multiple):
    flat = jnp.concatenate(arrays, axis=-1)
    n = flat.shape[-1]
    per = LANES * row_multiple
    padded = -(-n // per) * per
    flat = jnp.pad(flat, [(0, 0)] * (flat.ndim - 1) + [(0, padded - n)])
    return flat.reshape(flat.shape[:-1] + (padded // LANES, LANES))


def _unpack(slab, shapes):
    flat = slab.reshape(slab.shape[:-2] + (-1,))
    out, off = [], 0
    for shp in shapes:
        size = math.prod(shp)
        out.append(flat[..., off:off + size].reshape(flat.shape[:-1] + tuple(shp)))
        off += size
    return out


def kernel(x, c, ctx, c_ctx, w_mod, b_mod, norm1_w, norm2_w, w_in, attn_q_norm, attn_k_norm, ssd_conv_w, ssd_conv_b, ssd_dt_bias, ssd_a_log, ssd_d, ssd_norm_w, ret_log_decay, ret_gn_w, w_branch, w_out, w_mlp1, w_mlp2, final_norm_w, loss_target, m_c_ctx, m_w_mod, m_b_mod, m_norm1_w, m_norm2_w, m_w_in, m_attn_q_norm, m_attn_k_norm, m_ssd_conv_w, m_ssd_conv_b, m_ssd_dt_bias, m_ssd_a_log, m_ssd_d, m_ssd_norm_w, m_ret_log_decay, m_ret_gn_w, m_w_branch, m_w_out, m_w_mlp1, m_w_mlp2, m_final_norm_w, v_c_ctx, v_w_mod, v_b_mod, v_norm1_w, v_norm2_w, v_w_in, v_attn_q_norm, v_attn_k_norm, v_ssd_conv_w, v_ssd_conv_b, v_ssd_dt_bias, v_ssd_a_log, v_ssd_d, v_ssd_norm_w, v_ret_log_decay, v_ret_gn_w, v_w_branch, v_w_out, v_w_mlp1, v_w_mlp2, v_final_norm_w):
    args = dict(locals())
    weights = {n: args[n] for n in ORDER}
    mom1 = {n: args["m_" + n] for n in ORDER}
    mom2 = {n: args["v_" + n] for n in ORDER}
    me = 4 * lax.axis_index("x") + 2 * lax.axis_index("y") + lax.axis_index("c")

    gathered = _all_gather([weights[n].astype(MXU_DTYPE) for n in BIG], "gather_weights")
    wq, params = _compute_weights(dict(zip(BIG, gathered)))
    conv_shape = ssd_conv_w.shape
    conv_all = _all_gather([_pack([ssd_conv_w.reshape(-1)], 8)], "gather_conv")[0]
    params["ssd_conv_w"] = jnp.concatenate(list(_unpack(conv_all, [conv_shape])[0]), axis=CONV_AXIS)
    for n in SMALL:
        params[n] = weights[n]

    loss, (gw, gx) = jax.value_and_grad(_local_loss, argnums=(0, 1))(params, x[0], c[0], ctx[0], loss_target[0], wq)
    loss = lax.psum(loss, MESH_AXES)

    g_send = _grad_shards(gw)
    by_core = [g_send[n].reshape((4, 2) + g_send[n].shape[1:]).swapaxes(0, 1).astype(jnp.bfloat16) for n in BIG]
    from_sibling = _pair_exchange(by_core, "scatter_grads_d2d")
    my_core = lax.axis_index("c")
    chip_sums = []
    for n, mine, theirs in zip(BIG, by_core, from_sibling):
        kept = lax.dynamic_index_in_dim(mine, my_core, 0, keepdims=False)
        chip_sums.append(_add_pair(kept.reshape(-1, kept.shape[-1]), theirs.reshape(-1, theirs.shape[-1]),
                                   "chip_sum_" + n).reshape(theirs.shape))
    g_recv = _chip_exchange(chip_sums, "scatter_grads_ici")
    result = {}
    for n, g8 in zip(BIG, g_recv):
        shape = weights[n].shape
        as2d = lambda a: a.reshape(-1, shape[-1])
        outs = _sum_adamw(g8.reshape(4, -1, shape[-1]), as2d(weights[n]), as2d(mom1[n]), as2d(mom2[n]), "adamw_" + n)
        for kind, arr in zip(("grad", "delta", "new_m", "new_v"), outs):
            result[kind, n] = arr.reshape(shape)

    conv_full_shape = params["ssd_conv_w"].shape
    small_shapes = [weights[n].shape for n in SMALL]
    partial = _pack([gw[n].reshape(-1) for n in SMALL] + [gw["ssd_conv_w"].reshape(-1)], 8)
    parts = _unpack(_all_gather([partial], "gather_small_grads")[0], small_shapes + [conv_full_shape])
    conv_part = lax.dynamic_slice_in_dim(parts[-1], me * conv_shape[CONV_AXIS], conv_shape[CONV_AXIS], CONV_AXIS + 1)
    small_names = list(SMALL) + ["ssd_conv_w"]
    g8_small = _pack([p.reshape(N_DEV, -1) for p in parts[:-1]] + [conv_part.reshape(N_DEV, -1)], 8)
    slabs = [_pack([d[n].reshape(-1) for n in small_names], 8) for d in (weights, mom1, mom2)]
    small_out = [_unpack(s, small_shapes + [conv_shape]) for s in _sum_adamw(g8_small, *slabs, "adamw_small")]
    for kind, small_k in zip(("grad", "delta", "new_m", "new_v"), small_out):
        for n, arr in zip(small_names, small_k):
            result[kind, n] = arr

    outs = [loss, gx[None]]
    for kind in ("grad", "delta", "new_m", "new_v"):
        outs += [result[kind, n] for n in ORDER]
    return tuple(outs)
```
